```python
import functools
import jax, jax.numpy as jnp
from jax import lax
import numpy as np

D_MODEL = 2048
BATCH = 2
SEQ = 4096
DEPTH = 1
DEC_BATCH = 128
DEC_SEQ = 1
PAST_LEN = 2048
PAGE_SIZE = 128

M_HEADS = 4
M_DK = 128
M_DV = 256
F_HEADS = 8
F_DH = 128
D_FF = 5632
CONV_W = 3
CHUNK = 128
Q_BLOCK = 128
NORM_EPS = 1e-6

M_QK = M_HEADS * M_DK
M_V = M_HEADS * M_DV
F_W = F_HEADS * F_DH
SPLIT_SIZES = (M_QK, M_QK, M_V, M_V, M_HEADS, M_HEADS, F_W, F_W, F_W, F_HEADS, D_MODEL, D_MODEL)
SPLIT_POINTS = tuple(sum(SPLIT_SIZES[:i + 1]) for i in range(len(SPLIT_SIZES) - 1))
P_IN = sum(SPLIT_SIZES)
IDX_M_FORGET = 5
IDX_F_FORGET = 9

kernel_name = 'hybrid_mlstm_fox_convffn_adaln_step'


def rmsnorm(x, w):
    xf = x.astype(jnp.float32)
    y = xf * lax.rsqrt(jnp.mean(xf * xf, axis=-1, keepdims=True) + NORM_EPS)
    return (y * w.astype(jnp.float32)).astype(x.dtype)


def ada_mod(c, w_ada, b_ada):
    m = jax.nn.silu(c) @ w_ada + b_ada
    return [t[:, None, :] for t in jnp.split(m, 6, axis=-1)]


def modulate(h, shift, scale):
    return h * (1.0 + scale) + shift


def split_heads(t, n):
    return t.reshape(t.shape[0], t.shape[1], n, -1)


def mlstm_chunk(carry, inputs):
    C0, n0, m0 = carry
    q, k, v, ig, lf = inputs
    L = q.shape[2]
    b = jnp.cumsum(lf, axis=-1)
    m = b + jnp.maximum(m0[..., None], lax.cummax(ig - b, axis=2))
    causal = jnp.tril(jnp.ones((L, L), dtype=bool))
    log_d = b[..., :, None] - b[..., None, :] + ig[..., None, :] - m[..., :, None]
    dmat = jnp.exp(jnp.where(causal, log_d, -jnp.inf))
    inter = jnp.exp(b + m0[..., None] - m)
    s = jnp.einsum('bhtd,bhsd->bhts', q, k) * dmat
    num = jnp.einsum('bhts,bhsv->bhtv', s, v) + inter[..., None] * jnp.einsum('bhvd,bhtd->bhtv', C0, q)
    den = jnp.sum(s, axis=-1) + inter * jnp.einsum('bhd,bhtd->bht', n0, q)
    h = num / jnp.maximum(jnp.abs(den), jnp.exp(-m))[..., None]
    m_new = m[..., -1]
    w_end = jnp.exp(ig + b[..., -1:] - b - m_new[..., None])
    decay = jnp.exp(b[..., -1] + m0 - m_new)
    C_new = decay[..., None, None] * C0 + jnp.einsum('bhs,bhsv,bhsd->bhvd', w_end, v, k)
    n_new = decay[..., None] * n0 + jnp.einsum('bhs,bhsd->bhd', w_end, k)
    return (C_new, n_new, m_new), h


def mlstm_prompt(q, k, v, ig, lf):
    B, H, T, _ = q.shape
    nc = T // CHUNK

    def to_chunks(a):
        return jnp.moveaxis(a.reshape((B, H, nc, CHUNK) + a.shape[3:]), 2, 0)

    xs = tuple(to_chunks(a) for a in (q, k, v, ig, lf))
    carry0 = (jnp.zeros((B, H, M_DV, M_DK), jnp.float32),
              jnp.zeros((B, H, M_DK), jnp.float32),
              jnp.zeros((B, H), jnp.float32))
    state, hs = lax.scan(mlstm_chunk, carry0, xs)
    h = jnp.moveaxis(hs, 0, 2).reshape(B, H, T, M_DV)
    return h, state


def mlstm_from_state(C0, n0, m0, q, k, v, ig, lf):
    carry0 = (C0.astype(jnp.float32), n0.astype(jnp.float32), m0.astype(jnp.float32))
    state, h = mlstm_chunk(carry0, (q, k, v, ig, lf))
    return h, state


def mlstm_output(h, mo, hnorm_w):
    B, H, T, _ = h.shape
    hn = h * lax.rsqrt(jnp.mean(h * h, axis=-1, keepdims=True) + NORM_EPS)
    hn = hn * hnorm_w.reshape(M_HEADS, 1, M_DV).astype(jnp.float32)
    hn = hn.transpose(0, 2, 1, 3).reshape(B, T, M_V)
    return (jax.nn.sigmoid(mo.astype(jnp.float32)) * hn).astype(mo.dtype)


def fox_prompt(q, k, v, lf):
    B, T, H, Dh = q.shape
    nb = T // Q_BLOCK
    scale = Dh ** -0.5
    F = jnp.cumsum(lf, axis=1)
    FkT = F.transpose(0, 2, 1)
    qb = q.reshape(B, nb, Q_BLOCK, H, Dh).swapaxes(0, 1)
    Fb = F.reshape(B, nb, Q_BLOCK, H).swapaxes(0, 1)
    kpos = jnp.arange(T)

    def block(args):
        qi, Fi, i = args
        qpos = i * Q_BLOCK + jnp.arange(Q_BLOCK)
        s = jnp.einsum('bqhd,bkhd->bhqk', qi, k).astype(jnp.float32) * scale
        s = s + (Fi.transpose(0, 2, 1)[..., :, None] - FkT[:, :, None, :])
        s = jnp.where(qpos[:, None] >= kpos[None, :], s, -jnp.inf)
        p = jax.nn.softmax(s, axis=-1).astype(v.dtype)
        return jnp.einsum('bhqk,bkhd->bqhd', p, v)

    out = lax.map(block, (qb, Fb, jnp.arange(nb)))
    return out.swapaxes(0, 1).reshape(B, T, H * Dh)


def fox_sample(cache_k, cache_v, cache_logf, page_table, layer, q, k, v, lf):
    DB, S, H, Dh = q.shape
    n_pages = page_table.shape[1]
    P = n_pages * cache_k.shape[2]
    scale = Dh ** -0.5
    pk = cache_k[layer, page_table].reshape(DB, P, H, Dh).astype(k.dtype)
    pv = cache_v[layer, page_table].reshape(DB, P, H, Dh).astype(v.dtype)
    plf = cache_logf[layer, page_table].reshape(DB, P, H).astype(jnp.float32)
    k_all = jnp.concatenate([pk, k], axis=1)
    v_all = jnp.concatenate([pv, v], axis=1)
    F = jnp.cumsum(jnp.concatenate([plf, lf], axis=1), axis=1)
    Fq = F[:, P:]
    s = jnp.einsum('bqhd,bkhd->bhqk', q, k_all).astype(jnp.float32) * scale
    s = s + (Fq.transpose(0, 2, 1)[..., :, None] - F.transpose(0, 2, 1)[:, :, None, :])
    qpos = P + jnp.arange(S)
    kpos = jnp.arange(P + S)
    s = jnp.where(qpos[:, None] >= kpos[None, :], s, -jnp.inf)
    p = jax.nn.softmax(s, axis=-1).astype(v.dtype)
    out = jnp.einsum('bhqk,bkhd->bqhd', p, v_all)
    return out.reshape(DB, S, H * Dh)


def conv_ffn(h, conv_state, w_ffn_in, conv_w, conv_b, w_ffn_out):
    T = h.shape[1]
    a, g = jnp.split(h @ w_ffn_in, 2, axis=-1)
    ap = jnp.concatenate([conv_state.astype(a.dtype), a], axis=1)
    ac = sum(conv_w[j] * ap[:, j:j + T] for j in range(CONV_W)) + conv_b
    out = (jax.nn.gelu(ac) * g) @ w_ffn_out
    return out, ap[:, ap.shape[1] - (CONV_W - 1):]


def hybrid_layer(x, c, mlstm_fn, fox_fn, conv_state,
                 w_ada, b_ada, norm1_w, w_in, b_in, m_hnorm_w, f_qnorm_w, f_knorm_w,
                 w_proj_a, w_proj_b, w_out, norm2_w, w_ffn_in, conv_w, conv_b, w_ffn_out):
    B, T, _ = x.shape
    sh1, sc1, g1, sh2, sc2, g2 = ada_mod(c, w_ada, b_ada)
    h = modulate(rmsnorm(x, norm1_w), sh1, sc1)
    mq, mk, mv, mo, mi, mf, fq, fk, fv, ff, ga, gb = jnp.split(h @ w_in + b_in, SPLIT_POINTS, axis=-1)
    q_m = split_heads(mq, M_HEADS).transpose(0, 2, 1, 3).astype(jnp.float32)
    k_m = split_heads(mk, M_HEADS).transpose(0, 2, 1, 3).astype(jnp.float32) * (M_DK ** -0.5)
    v_m = split_heads(mv, M_HEADS).transpose(0, 2, 1, 3).astype(jnp.float32)
    ig = mi.astype(jnp.float32).transpose(0, 2, 1)
    lf_m = jax.nn.log_sigmoid(mf.astype(jnp.float32)).transpose(0, 2, 1)
    h_m, m_state = mlstm_fn(q_m, k_m, v_m, ig, lf_m)
    y_m = mlstm_output(h_m, mo, m_hnorm_w)
    q_f = rmsnorm(split_heads(fq, F_HEADS), f_qnorm_w)
    k_f = rmsnorm(split_heads(fk, F_HEADS), f_knorm_w)
    v_f = split_heads(fv, F_HEADS)
    lf_f = jax.nn.log_sigmoid(ff.astype(jnp.float32))
    y_f = fox_fn(q_f, k_f, v_f, lf_f)
    merged = jax.nn.sigmoid(ga) * (y_m @ w_proj_a) + jax.nn.sigmoid(gb) * (y_f @ w_proj_b)
    x = x + g1 * (merged @ w_out)
    h2 = modulate(rmsnorm(x, norm2_w), sh2, sc2)
    f_out, conv_new = conv_ffn(h2, conv_state, w_ffn_in, conv_w, conv_b, w_ffn_out)
    x = x + g2 * f_out
    return x, m_state, (k_f, v_f, lf_f.astype(x.dtype)), conv_new


def setup_inputs(seed: int = 0) -> dict:
    key = jax.random.key(seed)
    ks = jax.random.split(key, 32)
    n_pages = PAST_LEN // PAGE_SIZE
    n_pool = (DEC_BATCH * n_pages * 5) // 4

    def nrm(k, shape, scale):
        return scale * jax.random.normal(k, shape, jnp.float32)

    forget_offsets = [jnp.zeros((s,), jnp.float32) for s in SPLIT_SIZES]
    forget_offsets[IDX_M_FORGET] = jnp.linspace(3.0, 6.0, M_HEADS)
    forget_offsets[IDX_F_FORGET] = jnp.linspace(1.0, 4.0, F_HEADS)
    b_in = nrm(ks[17], (DEPTH, P_IN), 0.02) + jnp.concatenate(forget_offsets)[None, :]

    return {
        'x_prompt': nrm(ks[0], (BATCH, SEQ, D_MODEL), 1.0),
        'x_sample': nrm(ks[1], (DEC_BATCH, DEC_SEQ, D_MODEL), 1.0),
        'c_prompt': nrm(ks[2], (BATCH, D_MODEL), 1.0),
        'c_sample': nrm(ks[3], (DEC_BATCH, D_MODEL), 1.0),
        'cache_k': nrm(ks[4], (DEPTH, n_pool, PAGE_SIZE, F_HEADS, F_DH), 1.0),
        'cache_v': nrm(ks[5], (DEPTH, n_pool, PAGE_SIZE, F_HEADS, F_DH), 1.0),
        'cache_logf': jax.nn.log_sigmoid(2.0 + nrm(ks[6], (DEPTH, n_pool, PAGE_SIZE, F_HEADS), 1.0)),
        'page_table': jax.random.permutation(ks[7], n_pool)[:DEC_BATCH * n_pages].reshape(DEC_BATCH, n_pages).astype(jnp.int32),
        'state_C': nrm(ks[8], (DEPTH, DEC_BATCH, M_HEADS, M_DV, M_DK), 0.1),
        'state_n': nrm(ks[9], (DEPTH, DEC_BATCH, M_HEADS, M_DK), 0.1),
        'state_m': nrm(ks[10], (DEPTH, DEC_BATCH, M_HEADS), 1.0),
        'state_conv': nrm(ks[11], (DEPTH, DEC_BATCH, CONV_W - 1, D_FF), 1.0),
        'w_ada': nrm(ks[12], (DEPTH, D_MODEL, 6 * D_MODEL), 0.5 * D_MODEL ** -0.5),
        'b_ada': nrm(ks[13], (DEPTH, 6 * D_MODEL), 0.02),
        'norm1_w': 1.0 + nrm(ks[14], (DEPTH, D_MODEL), 0.02),
        'w_in': nrm(ks[15], (DEPTH, D_MODEL, P_IN), D_MODEL ** -0.5),
        'b_in': b_in,
        'm_hnorm_w': 1.0 + nrm(ks[18], (DEPTH, M_V), 0.02),
        'f_qnorm_w': 1.0 + nrm(ks[19], (DEPTH, F_DH), 0.02),
        'f_knorm_w': 1.0 + nrm(ks[20], (DEPTH, F_DH), 0.02),
        'w_proj_a': nrm(ks[21], (DEPTH, M_V, D_MODEL), M_V ** -0.5),
        'w_proj_b': nrm(ks[22], (DEPTH, F_W, D_MODEL), F_W ** -0.5),
        'w_out': nrm(ks[23], (DEPTH, D_MODEL, D_MODEL), D_MODEL ** -0.5),
        'norm2_w': 1.0 + nrm(ks[24], (DEPTH, D_MODEL), 0.02),
        'w_ffn_in': nrm(ks[25], (DEPTH, D_MODEL, 2 * D_FF), D_MODEL ** -0.5),
        'conv_w': nrm(ks[26], (DEPTH, CONV_W, D_FF), CONV_W ** -0.5),
        'conv_b': nrm(ks[27], (DEPTH, D_FF), 0.02),
        'w_ffn_out': nrm(ks[28], (DEPTH, D_FF, D_MODEL), D_FF ** -0.5),
    }


def reference(x_prompt, x_sample, c_prompt, c_sample, cache_k, cache_v, cache_logf, page_table,
              state_C, state_n, state_m, state_conv,
              w_ada, b_ada, norm1_w, w_in, b_in, m_hnorm_w, f_qnorm_w, f_knorm_w,
              w_proj_a, w_proj_b, w_out, norm2_w, w_ffn_in, conv_w, conv_b, w_ffn_out):
    yp, ys = x_prompt, x_sample
    outs_p = [[] for _ in range(7)]
    outs_s = [[] for _ in range(7)]
    for l in range(DEPTH):
        lw = (w_ada[l], b_ada[l], norm1_w[l], w_in[l], b_in[l], m_hnorm_w[l], f_qnorm_w[l], f_knorm_w[l],
              w_proj_a[l], w_proj_b[l], w_out[l], norm2_w[l], w_ffn_in[l], conv_w[l], conv_b[l], w_ffn_out[l])
        conv0 = jnp.zeros((yp.shape[0], CONV_W - 1, D_FF), yp.dtype)
        yp, (Cp, nP, mP), (kp, vp, lfp), convp = hybrid_layer(
            yp, c_prompt, mlstm_prompt, fox_prompt, conv0, *lw)
        ys, (Cs, nS, mS), (kS, vS, lfS), convs = hybrid_layer(
            ys, c_sample,
            functools.partial(mlstm_from_state, state_C[l], state_n[l], state_m[l]),
            functools.partial(fox_sample, cache_k, cache_v, cache_logf, page_table, l),
            state_conv[l], *lw)
        for lst, val in zip(outs_p, (kp, vp, lfp, Cp, nP, mP, convp)):
            lst.append(val)
        for lst, val in zip(outs_s, (kS, vS, lfS, Cs, nS, mS, convs)):
            lst.append(val)
    k_prompt, v_prompt, logf_prompt, C_prompt, n_prompt, m_prompt, conv_prompt = [jnp.stack(a) for a in outs_p]
    k_sample, v_sample, logf_sample, C_sample, n_sample, m_sample, conv_sample = [jnp.stack(a) for a in outs_s]
    return (yp, ys, k_prompt, v_prompt, logf_prompt, C_prompt, n_prompt, m_prompt, conv_prompt,
            k_sample, v_sample, logf_sample, C_sample, n_sample, m_sample, conv_sample)
```

```python
import functools

import jax
import jax.numpy as jnp
from jax import lax
from jax.experimental import pallas as pl
from jax.experimental.pallas import tpu as pltpu

F32 = jnp.float32
BF16 = jnp.bfloat16

NORM_EPS = 1e-6
M_HEADS, M_DK, M_DV = 4, 128, 256
F_HEADS, F_DH = 8, 128
M_QK = M_HEADS * M_DK
M_V = M_HEADS * M_DV
F_W = F_HEADS * F_DH
CONV_W = 3
CHUNK = 128
LANES = 128
SUBLANES = 8
SMALL_ROWS = 16
VMEM_LIMIT = 56 * 1024 * 1024

IG0, LFM0, LFF0 = 0, M_HEADS, 2 * M_HEADS


def _dot(a, b):
    return jnp.dot(a, b, preferred_element_type=F32)


def _dot_nt(a, b):
    return lax.dot_general(a, b, (((1,), (1,)), ((), ())), preferred_element_type=F32)


def _dot_tn(a, b):
    return lax.dot_general(a, b, (((0,), (0,)), ((), ())), preferred_element_type=F32)


def _split3(x):
    hi = x.astype(BF16)
    r1 = x - hi.astype(F32)
    mid = r1.astype(BF16)
    lo = (r1 - mid.astype(F32)).astype(BF16)
    return hi, mid, lo


def _dot_f32_lhs(x, mat01):
    hi, mid, lo = _split3(x)
    return _dot(hi, mat01) + _dot(mid, mat01) + _dot(lo, mat01)


def _dot_f32_rhs(mat01, x):
    hi, mid, lo = _split3(x)
    return _dot(mat01, hi) + _dot(mat01, mid) + _dot(mat01, lo)


def _log_sigmoid(x):
    return jnp.minimum(x, 0.0) - jnp.log1p(jnp.exp(-jnp.abs(x)))


def _tri01(n, kind):
    r = lax.broadcasted_iota(jnp.int32, (n, n), 0)
    c = lax.broadcasted_iota(jnp.int32, (n, n), 1)
    if kind == "lower":
        m = c <= r
    elif kind == "upper":
        m = r <= c
    else:
        m = r > c
    return jnp.where(m, 1.0, 0.0).astype(BF16)


def _params(*sem):
    return pltpu.CompilerParams(dimension_semantics=sem, vmem_limit_bytes=VMEM_LIMIT)


def _ada_kernel(c_ref, w_ref, b_ref, o_ref):
    c = c_ref[...]
    s = (c * jax.nn.sigmoid(c)).astype(BF16)
    o_ref[...] = _dot(s, w_ref[...].astype(BF16)) + b_ref[...]


def _ada(c_all, w_ada, b_ada, tn=1024):
    rows, d = c_all.shape
    n = w_ada.shape[1]
    return pl.pallas_call(
        _ada_kernel,
        grid=(n // tn,),
        in_specs=[pl.BlockSpec((rows, d), lambda j: (0, 0)),
                  pl.BlockSpec((d, tn), lambda j: (0, j)),
                  pl.BlockSpec((1, tn), lambda j: (0, j))],
        out_specs=pl.BlockSpec((rows, tn), lambda j: (0, j)),
        out_shape=jax.ShapeDtypeStruct((rows, n), F32),
        compiler_params=_params("arbitrary"),
        name="ada",
    )(c_all, w_ada, b_ada.reshape(1, n))


INPROJ_TN = 512
_J_QM, _J_KM, _J_VM, _J_OG, _J_QF, _J_KF, _J_VF, _J_GA, _J_GB, _J_END = 0, 1, 2, 4, 6, 8, 10, 12, 16, 20


def _head_rmsnorm(acc, w):
    outs = []
    for h in range(acc.shape[1] // F_DH):
        sl = acc[:, h * F_DH:(h + 1) * F_DH]
        ms = jnp.mean(sl * sl, axis=-1, keepdims=True)
        outs.append(sl * lax.rsqrt(ms + NORM_EPS) * w)
    return outs


def _inproj_kernel(x_ref, sc_ref, sh_ref, n1w_ref, wbig_ref, bbig_ref, wsm_ref, bsm_ref, wsmt_ref, bsmt_ref,
                   qnw_ref, knw_ref,
                   qm_ref, km_ref, vm_ref, og_ref, qf_ref, kf_ref, kfb_ref, vf_ref, vfb_ref, sga_ref, sgb_ref,
                   sm_ref, smt_ref, h_ref):
    j = pl.program_id(1)

    @pl.when(j == 0)
    def _():
        x = x_ref[...]
        rstd = lax.rsqrt(jnp.mean(x * x, axis=-1, keepdims=True) + NORM_EPS)
        h = (x * rstd * n1w_ref[...]) * (1.0 + sc_ref[...]) + sh_ref[...]
        hb = h.astype(BF16)
        h_ref[...] = hb
        sm = _dot(hb, wsm_ref[...]) + bsm_ref[...]
        lane = lax.broadcasted_iota(jnp.int32, sm.shape, 1)
        sm_ref[...] = jnp.where(lane < LFM0, sm, _log_sigmoid(sm))
        smt = _dot_nt(wsmt_ref[...], hb) + bsmt_ref[...]
        row = lax.broadcasted_iota(jnp.int32, smt.shape, 0)
        smt_ref[...] = jnp.where(row < LFM0, smt, _log_sigmoid(smt))

    acc = _dot(h_ref[...], wbig_ref[...]) + bbig_ref[...]

    @pl.when(j == _J_QM)
    def _():
        qm_ref[...] = acc.astype(BF16)

    @pl.when(j == _J_KM)
    def _():
        km_ref[...] = (acc * (M_DK ** -0.5)).astype(BF16)

    @pl.when(jnp.logical_and(j >= _J_VM, j < _J_OG))
    def _():
        vm_ref[...] = acc.astype(BF16)

    @pl.when(jnp.logical_and(j >= _J_OG, j < _J_QF))
    def _():
        og_ref[...] = jax.nn.sigmoid(acc)

    @pl.when(jnp.logical_and(j >= _J_QF, j < _J_KF))
    def _():
        for h, y in enumerate(_head_rmsnorm(acc, qnw_ref[...])):
            qf_ref[:, h * F_DH:(h + 1) * F_DH] = (y * (F_DH ** -0.5)).astype(BF16)

    @pl.when(jnp.logical_and(j >= _J_KF, j < _J_VF))
    def _():
        for h, y in enumerate(_head_rmsnorm(acc, knw_ref[...])):
            kf_ref[:, h * F_DH:(h + 1) * F_DH] = y
            kfb_ref[:, h * F_DH:(h + 1) * F_DH] = y.astype(BF16)

    @pl.when(jnp.logical_and(j >= _J_VF, j < _J_GA))
    def _():
        vf_ref[...] = acc
        vfb_ref[...] = acc.astype(BF16)

    @pl.when(jnp.logical_and(j >= _J_GA, j < _J_GB))
    def _():
        sga_ref[...] = jax.nn.sigmoid(acc)

    @pl.when(j >= _J_GB)
    def _():
        sgb_ref[...] = jax.nn.sigmoid(acc)


def _inproj(x2, sc, sh, n1w, wts, tm):
    m, d = x2.shape
    groups, mod_rows, _ = sc.shape
    tiles_per_group = (m // tm) // groups
    tn = INPROJ_TN

    def col(j0, ntiles):
        return lambda i, j: (i, jnp.clip(j - j0, 0, ntiles - 1))

    mod_spec = pl.BlockSpec((None, mod_rows, d), lambda i, j: (i // tiles_per_group, 0, 0))
    in_specs = [
        pl.BlockSpec((tm, d), lambda i, j: (i, 0)),
        mod_spec, mod_spec,
        pl.BlockSpec((1, d), lambda i, j: (0, 0)),
        pl.BlockSpec((d, tn), lambda i, j: (0, j)),
        pl.BlockSpec((1, tn), lambda i, j: (0, j)),
        pl.BlockSpec((d, LANES), lambda i, j: (0, 0)),
        pl.BlockSpec((1, LANES), lambda i, j: (0, 0)),
        pl.BlockSpec((SMALL_ROWS, d), lambda i, j: (0, 0)),
        pl.BlockSpec((SMALL_ROWS, 1), lambda i, j: (0, 0)),
        pl.BlockSpec((1, F_DH), lambda i, j: (0, 0)),
        pl.BlockSpec((1, F_DH), lambda i, j: (0, 0)),
    ]
    out_shapes = [
        jax.ShapeDtypeStruct((m, M_QK), BF16),
        jax.ShapeDtypeStruct((m, M_QK), BF16),
        jax.ShapeDtypeStruct((m, M_V), BF16),
        jax.ShapeDtypeStruct((m, M_V), F32),
        jax.ShapeDtypeStruct((m, F_W), BF16),
        jax.ShapeDtypeStruct((m, F_W), F32),
        jax.ShapeDtypeStruct((m, F_W), BF16),
        jax.ShapeDtypeStruct((m, F_W), F32),
        jax.ShapeDtypeStruct((m, F_W), BF16),
        jax.ShapeDtypeStruct((m, d), F32),
        jax.ShapeDtypeStruct((m, d), F32),
        jax.ShapeDtypeStruct((m, LANES), F32),
        jax.ShapeDtypeStruct((SMALL_ROWS, m), F32),
    ]
    out_specs = [
        pl.BlockSpec((tm, tn), col(_J_QM, 1)),
        pl.BlockSpec((tm, tn), col(_J_KM, 1)),
        pl.BlockSpec((tm, tn), col(_J_VM, 2)),
        pl.BlockSpec((tm, tn), col(_J_OG, 2)),
        pl.BlockSpec((tm, tn), col(_J_QF, 2)),
        pl.BlockSpec((tm, tn), col(_J_KF, 2)),
        pl.BlockSpec((tm, tn), col(_J_KF, 2)),
        pl.BlockSpec((tm, tn), col(_J_VF, 2)),
        pl.BlockSpec((tm, tn), col(_J_VF, 2)),
        pl.BlockSpec((tm, tn), col(_J_GA, 4)),
        pl.BlockSpec((tm, tn), col(_J_GB, 4)),
        pl.BlockSpec((tm, LANES), lambda i, j: (i, 0)),
        pl.BlockSpec((SMALL_ROWS, tm), lambda i, j: (0, i)),
    ]
    return pl.pallas_call(
        _inproj_kernel,
        grid=(m // tm, _J_END),
        in_specs=in_specs,
        out_specs=out_specs,
        out_shape=out_shapes,
        scratch_shapes=[pltpu.VMEM((tm, d), BF16)],
        compiler_params=_params("arbitrary", "arbitrary"),
        name="inproj",
    )(x2, sc, sh, n1w, wts["wbig"], wts["bbig"], wts["wsm"], wts["bsm"], wts["wsmt"], wts["bsmt"],
      wts["qnw"], wts["knw"])


def _cumsum_kernel(sm_ref, smt_ref, fcol_ref, frow_ref, ccol_ref, crow_ref):
    c = pl.program_id(1)
    tc = sm_ref.shape[0]

    @pl.when(c == 0)
    def _():
        ccol_ref[...] = jnp.zeros_like(ccol_ref)
        crow_ref[...] = jnp.zeros_like(crow_ref)

    fc = _dot_f32_rhs(_tri01(tc, "lower"), sm_ref[...]) + ccol_ref[0:1, :]
    fcol_ref[...] = fc
    ccol_ref[...] = jnp.broadcast_to(fc[tc - 1:tc, :], ccol_ref.shape)
    fr = _dot_f32_lhs(smt_ref[...], _tri01(tc, "upper")) + crow_ref[:, 0:1]
    frow_ref[...] = fr
    crow_ref[...] = jnp.broadcast_to(fr[:, tc - 1:tc], crow_ref.shape)


def _cumsum(sm, smt, batch, tc=512):
    m = sm.shape[0]
    nt = (m // batch) // tc
    return pl.pallas_call(
        _cumsum_kernel,
        grid=(batch, nt),
        in_specs=[pl.BlockSpec((tc, LANES), lambda b, c: (b * nt + c, 0)),
                  pl.BlockSpec((SMALL_ROWS, tc), lambda b, c: (0, b * nt + c))],
        out_specs=[pl.BlockSpec((tc, LANES), lambda b, c: (b * nt + c, 0)),
                   pl.BlockSpec((SMALL_ROWS, tc), lambda b, c: (0, b * nt + c))],
        out_shape=[jax.ShapeDtypeStruct((m, LANES), F32), jax.ShapeDtypeStruct((SMALL_ROWS, m), F32)],
        scratch_shapes=[pltpu.VMEM((SUBLANES, LANES), F32), pltpu.VMEM((SMALL_ROWS, LANES), F32)],
        compiler_params=_params("arbitrary", "arbitrary"),
        name="cumsum",
    )(sm, smt)


def _mlstm_kernel(q_ref, k_ref, v_ref, sm_ref, smt_ref, og_ref, hw_ref,
                  ym_ref, cout_ref, nm_ref, c_ref, n_ref, m_ref):
    c = pl.program_id(1)
    L = q_ref.shape[0]

    @pl.when(c == 0)
    def _():
        c_ref[...] = jnp.zeros_like(c_ref)
        n_ref[...] = jnp.zeros_like(n_ref)
        m_ref[...] = jnp.zeros_like(m_ref)

    r = lax.broadcasted_iota(jnp.int32, (L, L), 0)
    s_idx = lax.broadcasted_iota(jnp.int32, (L, L), 1)
    causal = s_idx <= r
    sm = sm_ref[...]
    smt = smt_ref[...]
    bcol_all = _dot_f32_rhs(_tri01(L, "lower"), sm)
    brow_all = _dot_f32_lhs(smt, _tri01(L, "upper"))

    for h in range(M_HEADS):
        ig_col = sm[:, IG0 + h:IG0 + h + 1]
        b_col = bcol_all[:, LFM0 + h:LFM0 + h + 1]
        ig_row = smt[IG0 + h:IG0 + h + 1, :]
        b_row = brow_all[LFM0 + h:LFM0 + h + 1, :]
        m0 = m_ref[h:h + 1, 0:1]
        a_row = ig_row - b_row
        cm_col = jnp.max(jnp.where(causal, a_row, -jnp.inf), axis=1, keepdims=True)
        mx_col = jnp.maximum(m0, cm_col)
        m_col = b_col + mx_col
        dmat = jnp.exp(jnp.where(causal, a_row - mx_col, -jnp.inf))
        inter = jnp.exp(m0 - mx_col)
        q = q_ref[:, h * M_DK:(h + 1) * M_DK]
        k = k_ref[:, h * M_DK:(h + 1) * M_DK]
        v = v_ref[:, h * M_DV:(h + 1) * M_DV]
        c0 = c_ref[h]
        n0 = n_ref[h:h + 1, :]
        s = _dot_nt(q, k) * dmat
        num = _dot(s.astype(BF16), v) + inter * _dot_nt(q, c0.astype(BF16))
        den = jnp.sum(s, axis=1, keepdims=True) + inter * jnp.sum(q.astype(F32) * n0, axis=1, keepdims=True)
        hh = num / jnp.maximum(jnp.abs(den), jnp.exp(-m_col))
        hn = hh * lax.rsqrt(jnp.mean(hh * hh, axis=1, keepdims=True) + NORM_EPS) * hw_ref[:, h * M_DV:(h + 1) * M_DV]
        ym_ref[:, h * M_DV:(h + 1) * M_DV] = (og_ref[:, h * M_DV:(h + 1) * M_DV] * hn).astype(BF16)
        m_new = m_col[L - 1:L, :]
        b_last = b_col[L - 1:L, :]
        w_end = jnp.exp(ig_col + b_last - b_col - m_new)
        decay = jnp.exp(b_last + m0 - m_new)
        vw = (w_end * v.astype(F32)).astype(BF16)
        c_ref[h] = decay * c0 + _dot_tn(vw, k)
        n_ref[h:h + 1, :] = decay * n0 + jnp.sum(w_end * k.astype(F32), axis=0, keepdims=True)
        m_ref[h:h + 1, :] = jnp.broadcast_to(m_new, (1, LANES))

    @pl.when(c == pl.num_programs(1) - 1)
    def _():
        cout_ref[...] = c_ref[...]
        nm_ref[0:M_HEADS, :] = n_ref[0:M_HEADS, :]
        nm_ref[M_HEADS:2 * M_HEADS, :] = m_ref[0:M_HEADS, :]


def _mlstm_prompt(qm, km, vm, sm, smt, og, hw, batch):
    m = qm.shape[0]
    nc = (m // batch) // CHUNK
    row = lambda b, c: (b * nc + c, 0)
    return pl.pallas_call(
        _mlstm_kernel,
        grid=(batch, nc),
        in_specs=[pl.BlockSpec((CHUNK, M_QK), row),
                  pl.BlockSpec((CHUNK, M_QK), row),
                  pl.BlockSpec((CHUNK, M_V), row),
                  pl.BlockSpec((CHUNK, LANES), row),
                  pl.BlockSpec((SMALL_ROWS, CHUNK), lambda b, c: (0, b * nc + c)),
                  pl.BlockSpec((CHUNK, M_V), row),
                  pl.BlockSpec((1, M_V), lambda b, c: (0, 0))],
        out_specs=[pl.BlockSpec((CHUNK, M_V), row),
                   pl.BlockSpec((None, M_HEADS, M_DV, M_DK), lambda b, c: (b, 0, 0, 0)),
                   pl.BlockSpec((None, 2 * M_HEADS, M_DK), lambda b, c: (b, 0, 0))],
        out_shape=[jax.ShapeDtypeStruct((m, M_V), BF16),
                   jax.ShapeDtypeStruct((batch, M_HEADS, M_DV, M_DK), F32),
                   jax.ShapeDtypeStruct((batch, 2 * M_HEADS, M_DK), F32)],
        scratch_shapes=[pltpu.VMEM((M_HEADS, M_DV, M_DK), F32),
                        pltpu.VMEM((SUBLANES, M_DK), F32),
                        pltpu.VMEM((SUBLANES, LANES), F32)],
        compiler_params=_params("arbitrary", "arbitrary"),
        name="mlstm",
    )(qm, km, vm, sm, smt, og, hw)


def _fox_kernel(q_ref, k_ref, v_ref, fcol_ref, frow_ref, o_ref, m_ref, l_ref, acc_ref):
    h = pl.program_id(1)
    qi = pl.program_id(2)
    tq = q_ref.shape[0]
    q = q_ref[...]
    lane = lax.broadcasted_iota(jnp.int32, (tq, LANES), 1)
    fq = jnp.sum(jnp.where(lane == LFF0 + h, fcol_ref[...], 0.0), axis=1, keepdims=True)
    m_ref[...] = jnp.full_like(m_ref, -jnp.inf)
    l_ref[...] = jnp.zeros_like(l_ref)
    acc_ref[...] = jnp.zeros_like(acc_ref)

    def block(kj, masked):
        off = pl.multiple_of(kj * tq, tq)
        kb = k_ref[pl.ds(off, tq), :]
        vb = v_ref[pl.ds(off, tq), :]
        fk = frow_ref[:, pl.ds(off, tq)]
        s = _dot_nt(q, kb) + (fq - fk)
        if masked:
            r = lax.broadcasted_iota(jnp.int32, (tq, tq), 0)
            c = lax.broadcasted_iota(jnp.int32, (tq, tq), 1)
            s = jnp.where(r >= c, s, -jnp.inf)
        m_old = m_ref[...]
        m_new = jnp.maximum(m_old, jnp.max(s, axis=1, keepdims=True))
        alpha = jnp.exp(m_old - m_new)
        p = jnp.exp(s - m_new)
        l_ref[...] = alpha * l_ref[...] + jnp.sum(p, axis=1, keepdims=True)
        acc_ref[...] = alpha * acc_ref[...] + _dot(p.astype(BF16), vb)
        m_ref[...] = m_new

    def body(kj, carry):
        block(kj, False)
        return carry

    lax.fori_loop(0, qi, body, 0)
    block(qi, True)
    o_ref[...] = (acc_ref[...] / l_ref[...]).astype(BF16)


def _fox_prompt(qf, kfb, vfb, fcol, frow3, batch, tq=512):
    m = qf.shape[0]
    t = m // batch
    nq = t // tq
    return pl.pallas_call(
        _fox_kernel,
        grid=(batch, F_HEADS, nq),
        in_specs=[pl.BlockSpec((tq, F_DH), lambda b, h, i: (b * nq + i, h)),
                  pl.BlockSpec((t, F_DH), lambda b, h, i: (b, h)),
                  pl.BlockSpec((t, F_DH), lambda b, h, i: (b, h)),
                  pl.BlockSpec((tq, LANES), lambda b, h, i: (b * nq + i, 0)),
                  pl.BlockSpec((None, 1, t), lambda b, h, i: (LFF0 + h, 0, b))],
        out_specs=pl.BlockSpec((tq, F_DH), lambda b, h, i: (b * nq + i, h)),
        out_shape=jax.ShapeDtypeStruct((m, F_W), BF16),
        scratch_shapes=[pltpu.VMEM((tq, 1), F32), pltpu.VMEM((tq, 1), F32), pltpu.VMEM((tq, F_DH), F32)],
        compiler_params=_params("arbitrary", "arbitrary", "arbitrary"),
        name="fox",
    )(qf, kfb, vfb, fcol, frow3)


MSTEP_ROWS = 8


def _mstep_kernel(q_ref, k_ref, v_ref, sm_ref, og_ref, hw_ref, c0_ref, n0_ref, m0_ref,
                  ym_ref, cout_ref, nout_ref, mout_ref, cq_ref):
    rows = q_ref.shape[0]
    sm = sm_ref[...]
    ig = sm[:, IG0:IG0 + M_HEADS]
    lf = sm[:, LFM0:LFM0 + M_HEADS]
    m0 = m0_ref[...]
    m_new = lf + jnp.maximum(m0, ig - lf)
    w_end = jnp.exp(ig - m_new)
    decay = jnp.exp(lf + m0 - m_new)
    mout_ref[...] = m_new
    row_id = lax.broadcasted_iota(jnp.int32, (SUBLANES, M_DV), 0)

    for h in range(M_HEADS):
        qh = q_ref[:, h * M_DK:(h + 1) * M_DK]
        kh = k_ref[:, h * M_DK:(h + 1) * M_DK]
        vh = v_ref[:, h * M_DV:(h + 1) * M_DV]
        qf = qh.astype(F32)
        kf = kh.astype(F32)
        vf = vh.astype(F32)
        wh = w_end[:, h:h + 1]
        dh = decay[:, h:h + 1]
        for r in range(rows):
            c0 = c0_ref[r, h]
            q8 = jnp.broadcast_to(qh[r:r + 1, :], (SUBLANES, M_DK))
            cq_ref[r:r + 1, :] = _dot_nt(q8, c0.astype(BF16))[0:1, :]
            vw8 = jnp.where(row_id == 0, jnp.broadcast_to(wh[r:r + 1, :] * vf[r:r + 1, :], (SUBLANES, M_DV)), 0.0)
            k8 = jnp.broadcast_to(kh[r:r + 1, :], (SUBLANES, M_DK))
            cout_ref[r, h] = dh[r:r + 1, :] * c0 + _dot_tn(vw8.astype(BF16), k8)
        n0 = n0_ref[:, h, :]
        s = jnp.sum(qf * kf, axis=1, keepdims=True) * wh
        num = s * vf + dh * cq_ref[...]
        den = s + dh * jnp.sum(n0 * qf, axis=1, keepdims=True)
        hh = num / jnp.maximum(jnp.abs(den), jnp.exp(-m_new[:, h:h + 1]))
        hn = hh * lax.rsqrt(jnp.mean(hh * hh, axis=1, keepdims=True) + NORM_EPS) * hw_ref[:, h * M_DV:(h + 1) * M_DV]
        ym_ref[:, h * M_DV:(h + 1) * M_DV] = (og_ref[:, h * M_DV:(h + 1) * M_DV] * hn).astype(BF16)
        nout_ref[:, h, :] = dh * n0 + wh * kf


def _mlstm_step(qm, km, vm, sm, og, hw, c0, n0, m0):
    db = qm.shape[0]
    rows = MSTEP_ROWS
    row = lambda i: (i, 0)
    return pl.pallas_call(
        _mstep_kernel,
        grid=(db // rows,),
        in_specs=[pl.BlockSpec((rows, M_QK), row),
                  pl.BlockSpec((rows, M_QK), row),
                  pl.BlockSpec((rows, M_V), row),
                  pl.BlockSpec((rows, LANES), row),
                  pl.BlockSpec((rows, M_V), row),
                  pl.BlockSpec((1, M_V), lambda i: (0, 0)),
                  pl.BlockSpec((rows, M_HEADS, M_DV, M_DK), lambda i: (i, 0, 0, 0)),
                  pl.BlockSpec((rows, M_HEADS, M_DK), lambda i: (i, 0, 0)),
                  pl.BlockSpec((rows, M_HEADS), row)],
        out_specs=[pl.BlockSpec((rows, M_V), row),
                   pl.BlockSpec((rows, M_HEADS, M_DV, M_DK), lambda i: (i, 0, 0, 0)),
                   pl.BlockSpec((rows, M_HEADS, M_DK), lambda i: (i, 0, 0)),
                   pl.BlockSpec((rows, M_HEADS), row)],
        out_shape=[jax.ShapeDtypeStruct((db, M_V), BF16),
                   jax.ShapeDtypeStruct(c0.shape, F32),
                   jax.ShapeDtypeStruct(n0.shape, F32),
                   jax.ShapeDtypeStruct(m0.shape, F32)],
        scratch_shapes=[pltpu.VMEM((rows, M_DV), F32)],
        compiler_params=_params("arbitrary"),
        name="mstep",
    )(qm, km, vm, sm, og, hw, c0, n0, m0)


def _decode_kernel(pt_ref, ck_ref, cv_ref, lf_ref, q_ref, k_ref, v_ref, sm_ref, o_ref,
                   qrows_ref, m_ref, l_ref, acc_ref, carry_ref):
    del pt_ref
    j = pl.program_id(1)
    page = ck_ref.shape[0]
    sub = lax.broadcasted_iota(jnp.int32, (F_HEADS, F_W), 0)
    lane = lax.broadcasted_iota(jnp.int32, (F_HEADS, F_W), 1)
    own = (lane // F_DH) == sub

    @pl.when(j == 0)
    def _():
        q = q_ref[...]
        qb = jnp.broadcast_to(q.astype(F32), (F_HEADS, F_W))
        qrows_ref[...] = jnp.where(own, qb, 0.0).astype(BF16)
        s0 = jnp.sum(jnp.where(own, q.astype(F32) * k_ref[...], 0.0), axis=1, keepdims=True)
        m_ref[...] = jnp.broadcast_to(s0, m_ref.shape)
        l_ref[...] = jnp.ones_like(l_ref)
        acc_ref[...] = jnp.where(own, jnp.broadcast_to(v_ref[...], (F_HEADS, F_W)), 0.0)
        sub8 = lax.broadcasted_iota(jnp.int32, (F_HEADS, LANES), 0)
        lane8 = lax.broadcasted_iota(jnp.int32, (F_HEADS, LANES), 1)
        lf_new = jnp.sum(jnp.where(lane8 == sub8 + LFF0, sm_ref[...], 0.0), axis=1, keepdims=True)
        carry_ref[...] = jnp.broadcast_to(lf_new, carry_ref.shape)

    kb = ck_ref[...].astype(BF16)
    vb = cv_ref[...].astype(BF16)
    plf = lf_ref[...]
    sfx = _dot_f32_lhs(plf, _tri01(page, "strict_lower"))
    tot = _dot_f32_lhs(plf, jnp.ones((page, LANES), BF16))
    s = _dot_nt(qrows_ref[...], kb) + carry_ref[...] + sfx
    m_old = m_ref[...]
    m_new = jnp.maximum(m_old, jnp.max(s, axis=1, keepdims=True))
    alpha = jnp.exp(m_old - m_new)
    p = jnp.exp(s - m_new)
    l_ref[...] = alpha * l_ref[...] + jnp.sum(p, axis=1, keepdims=True)
    acc_ref[...] = alpha[:, 0:1] * acc_ref[...] + _dot(p.astype(BF16), vb)
    m_ref[...] = m_new
    carry_ref[...] = carry_ref[...] + tot

    @pl.when(j == pl.num_programs(1) - 1)
    def _():
        out = jnp.where(own, acc_ref[...] / l_ref[:, 0:1], 0.0)
        o_ref[...] = jnp.sum(out, axis=0, keepdims=True).astype(BF16)


def _fox_decode(page_table, ck, cv, clf_t, q3, k3, v3, sm3):
    db, n_pages = page_table.shape
    page = ck.shape[1]
    seq = lambda b, j, pt: (b, 0, 0)
    pg = lambda b, j, pt: (pt[b, n_pages - 1 - j], 0, 0)
    grid_spec = pltpu.PrefetchScalarGridSpec(
        num_scalar_prefetch=1,
        grid=(db, n_pages),
        in_specs=[pl.BlockSpec((None, page, F_W), pg),
                  pl.BlockSpec((None, page, F_W), pg),
                  pl.BlockSpec((None, F_HEADS, page), pg),
                  pl.BlockSpec((None, 1, F_W), seq),
                  pl.BlockSpec((None, 1, F_W), seq),
                  pl.BlockSpec((None, 1, F_W), seq),
                  pl.BlockSpec((None, 1, LANES), seq)],
        out_specs=pl.BlockSpec((None, 1, F_W), seq),
        scratch_shapes=[pltpu.VMEM((F_HEADS, F_W), BF16),
                        pltpu.VMEM((F_HEADS, LANES), F32),
                        pltpu.VMEM((F_HEADS, LANES), F32),
                        pltpu.VMEM((F_HEADS, F_W), F32),
                        pltpu.VMEM((F_HEADS, LANES), F32)],
    )
    return pl.pallas_call(
        _decode_kernel,
        grid_spec=grid_spec,
        out_shape=jax.ShapeDtypeStruct((db, 1, F_W), BF16),
        compiler_params=_params("arbitrary", "arbitrary"),
        name="decode",
    )(page_table, ck, cv, clf_t, q3, k3, v3, sm3)


def _merge_kernel(ym_ref, yf_ref, sga_ref, sgb_ref, x_ref, g1_ref, sc_ref, sh_ref, n2w_ref,
                  wa_ref, wb_ref, wo_ref, x1_ref, h2_ref):
    merged = sga_ref[...] * _dot(ym_ref[...], wa_ref[...]) + sgb_ref[...] * _dot(yf_ref[...], wb_ref[...])
    x1 = x_ref[...] + g1_ref[...] * _dot(merged.astype(BF16), wo_ref[...])
    x1_ref[...] = x1
    rstd = lax.rsqrt(jnp.mean(x1 * x1, axis=-1, keepdims=True) + NORM_EPS)
    h2_ref[...] = ((x1 * rstd * n2w_ref[...]) * (1.0 + sc_ref[...]) + sh_ref[...]).astype(BF16)


def _merge(ym, yf, sga, sgb, x2, g1, sc2, sh2, n2w, wts, tm):
    m, d = x2.shape
    groups, mod_rows, _ = g1.shape
    tiles_per_group = (m // tm) // groups
    row = lambda i: (i, 0)
    const = lambda i: (0, 0)
    mod_spec = pl.BlockSpec((None, mod_rows, d), lambda i: (i // tiles_per_group, 0, 0))
    resident = dict(pipeline_mode=pl.Buffered(1))
    return pl.pallas_call(
        _merge_kernel,
        grid=(m // tm,),
        in_specs=[pl.BlockSpec((tm, M_V), row),
                  pl.BlockSpec((tm, F_W), row),
                  pl.BlockSpec((tm, d), row),
                  pl.BlockSpec((tm, d), row),
                  pl.BlockSpec((tm, d), row),
                  mod_spec, mod_spec, mod_spec,
                  pl.BlockSpec((1, d), const),
                  pl.BlockSpec((M_V, d), const, **resident),
                  pl.BlockSpec((F_W, d), const, **resident),
                  pl.BlockSpec((d, d), const, **resident)],
        out_specs=[pl.BlockSpec((tm, d), row), pl.BlockSpec((tm, d), row)],
        out_shape=[jax.ShapeDtypeStruct((m, d), F32), jax.ShapeDtypeStruct((m, d), BF16)],
        compiler_params=_params("arbitrary"),
        name="merge",
    )(ym, yf, sga, sgb, x2, g1, sc2, sh2, n2w, wts["wa"], wts["wb"], wts["wo"])


FFN_TF = 512


def _ffn_kernel(h2_ref, x1_ref, g2_ref, p_ref, wa_ref, wg_ref, cw_ref, cb_ref, w2_ref,
                y_ref, a_out_ref, acc_ref, carry_ref, *, seq):
    f = pl.program_id(2)
    tm = h2_ref.shape[0]
    h2 = h2_ref[...]
    a = _dot(h2, wa_ref[...])
    g = _dot(h2, wg_ref[...])
    if seq:
        i = pl.program_id(1)

        @pl.when(i == 0)
        def _():
            carry_ref[f] = p_ref[...]

        prev = carry_ref[f]
        p2, p1 = prev[SUBLANES - 2:SUBLANES - 1, :], prev[SUBLANES - 1:SUBLANES, :]
        r = lax.broadcasted_iota(jnp.int32, a.shape, 0)
        a1 = jnp.where(r == 0, p1, pltpu.roll(a, 1, axis=0))
        a2 = jnp.where(r == 0, p2, jnp.where(r == 1, p1, pltpu.roll(a, 2, axis=0)))
        carry_ref[f] = a[tm - SUBLANES:tm, :]
        a_out_ref[...] = a[tm - SUBLANES:tm, :]
    else:
        a2, a1 = p_ref[0], p_ref[1]
        a_out_ref[...] = a
    ac = cw_ref[0:1, :] * a2 + cw_ref[1:2, :] * a1 + cw_ref[2:3, :] * a + cb_ref[...]
    u = (jax.nn.gelu(ac, approximate=True) * g).astype(BF16)
    contrib = _dot(u, w2_ref[...])

    @pl.when(f == 0)
    def _():
        acc_ref[...] = contrib

    @pl.when(f > 0)
    def _():
        acc_ref[...] += contrib

    @pl.when(f == pl.num_programs(2) - 1)
    def _():
        y_ref[...] = x1_ref[...] + g2_ref[...] * acc_ref[...]


def _ffn(h2, x1, g2, prev, wts, tm, seq):
    m, d = x1.shape
    groups, mod_rows, _ = g2.shape
    tiles = (m // tm) // groups
    dff = wts["w2"].shape[0]
    tf = FFN_TF
    nf = dff // tf
    row = lambda b, i, f: (b * tiles + i, 0)
    if seq:
        prev_spec = pl.BlockSpec((None, SUBLANES, tf), lambda b, i, f: (b, 0, f))
        a_spec = pl.BlockSpec((None, None, SUBLANES, tf), lambda b, i, f: (b, i, 0, f))
        a_shape = jax.ShapeDtypeStruct((groups, tiles, SUBLANES, dff), F32)
    else:
        prev_spec = pl.BlockSpec((CONV_W - 1, tm, tf), lambda b, i, f: (0, i, f))
        a_spec = pl.BlockSpec((tm, tf), lambda b, i, f: (i, f))
        a_shape = jax.ShapeDtypeStruct((m, dff), F32)
    return pl.pallas_call(
        functools.partial(_ffn_kernel, seq=seq),
        grid=(groups, tiles, nf),
        in_specs=[pl.BlockSpec((tm, d), row),
                  pl.BlockSpec((tm, d), row),
                  pl.BlockSpec((None, mod_rows, d), lambda b, i, f: (b, 0, 0)),
                  prev_spec,
                  pl.BlockSpec((d, tf), lambda b, i, f: (0, f)),
                  pl.BlockSpec((d, tf), lambda b, i, f: (0, f + nf)),
                  pl.BlockSpec((CONV_W, tf), lambda b, i, f: (0, f)),
                  pl.BlockSpec((1, tf), lambda b, i, f: (0, f)),
                  pl.BlockSpec((tf, d), lambda b, i, f: (f, 0))],
        out_specs=[pl.BlockSpec((tm, d), row), a_spec],
        out_shape=[jax.ShapeDtypeStruct((m, d), F32), a_shape],
        scratch_shapes=[pltpu.VMEM((tm, d), F32), pltpu.VMEM((nf, SUBLANES, tf), F32)],
        compiler_params=_params("arbitrary", "arbitrary", "arbitrary"),
        name="ffn_seq" if seq else "ffn_row",
    )(h2, x1, g2, prev, wts["w1"], wts["w1"], wts["cw"], wts["cb"], wts["w2"])


def _prep_weights(w_in, b_in, f_qnorm_w, f_knorm_w, w_proj_a, w_proj_b, w_out, w_ffn_in, conv_w, conv_b, w_ffn_out):
    sizes = (M_QK, M_QK, M_V, M_V, M_HEADS, M_HEADS, F_W, F_W, F_W, F_HEADS)
    o = [0]
    for s in sizes:
        o.append(o[-1] + s)
    big = lambda a: jnp.concatenate([a[..., :o[4]], a[..., o[6]:o[9]], a[..., o[10]:]], axis=-1)
    small = lambda a: jnp.concatenate([a[..., o[4]:o[6]], a[..., o[9]:o[10]]], axis=-1)
    wsm = small(w_in)
    bsm = small(b_in)
    return {
        "wbig": big(w_in).astype(BF16),
        "bbig": big(b_in).reshape(1, -1),
        "wsm": jnp.pad(wsm, ((0, 0), (0, LANES - SMALL_ROWS))).astype(BF16),
        "bsm": jnp.pad(bsm, (0, LANES - SMALL_ROWS)).reshape(1, LANES),
        "wsmt": wsm.T.astype(BF16),
        "bsmt": bsm.reshape(SMALL_ROWS, 1),
        "qnw": f_qnorm_w.reshape(1, F_DH),
        "knw": f_knorm_w.reshape(1, F_DH),
        "wa": w_proj_a.astype(BF16),
        "wb": w_proj_b.astype(BF16),
        "wo": w_out.astype(BF16),
        "w1": w_ffn_in.astype(BF16),
        "cw": conv_w,
        "cb": conv_b.reshape(1, -1),
        "w2": w_ffn_out.astype(BF16),
    }


def kernel(x_prompt, x_sample, c_prompt, c_sample, cache_k, cache_v, cache_logf, page_table, state_C, state_n, state_m, state_conv, w_ada, b_ada, norm1_w, w_in, b_in, m_hnorm_w, f_qnorm_w, f_knorm_w, w_proj_a, w_proj_b, w_out, norm2_w, w_ffn_in, conv_w, conv_b, w_ffn_out):
    depth = w_ada.shape[0]
    assert depth == 1, "single-layer step"
    batch, seq_len, d = x_prompt.shape
    db = x_sample.shape[0]
    assert x_sample.shape[1] == 1
    dff = w_ffn_out.shape[1]
    layer = 0

    wts = _prep_weights(w_in[layer], b_in[layer], f_qnorm_w[layer], f_knorm_w[layer], w_proj_a[layer],
                        w_proj_b[layer], w_out[layer], w_ffn_in[layer], conv_w[layer], conv_b[layer],
                        w_ffn_out[layer])
    n1w = norm1_w[layer].reshape(1, d)
    n2w = norm2_w[layer].reshape(1, d)
    hw = m_hnorm_w[layer].reshape(1, M_V)

    mod = _ada(jnp.concatenate([c_prompt, c_sample], axis=0), w_ada[layer], b_ada[layer])
    mod_p = [t.reshape(batch, 1, d) for t in jnp.split(mod[:batch], 6, axis=-1)]
    mod_s = [t.reshape(1, db, d) for t in jnp.split(mod[batch:], 6, axis=-1)]

    tm_p = 512
    xp = x_prompt.reshape(batch * seq_len, d)
    (qm, km, vm, og, qf, kf, kfb, vf, vfb, sga, sgb, sm, smt) = _inproj(xp, mod_p[1], mod_p[0], n1w, wts, tm_p)
    fcol, frow = _cumsum(sm, smt, batch)
    ym, c_p, nm_p = _mlstm_prompt(qm, km, vm, sm, smt, og, hw, batch)
    yf = _fox_prompt(qf, kfb, vfb, fcol, frow.reshape(SMALL_ROWS, 1, batch * seq_len), batch)
    x1, h2 = _merge(ym, yf, sga, sgb, xp, mod_p[2], mod_p[4], mod_p[3], n2w, wts, 256)
    conv0 = jnp.zeros((batch, SUBLANES, dff), F32)
    yp, a_last = _ffn(h2, x1, mod_p[5], conv0, wts, tm_p, seq=True)

    y_prompt = yp.reshape(batch, seq_len, d)
    k_prompt = kf.reshape(depth, batch, seq_len, F_HEADS, F_DH)
    v_prompt = vf.reshape(depth, batch, seq_len, F_HEADS, F_DH)
    logf_prompt = sm[:, LFF0:LFF0 + F_HEADS].reshape(depth, batch, seq_len, F_HEADS)
    c_prompt_out = c_p[None]
    n_prompt = nm_p[:, :M_HEADS, :][None]
    m_prompt = nm_p[:, M_HEADS:, 0][None]
    conv_prompt = a_last[:, -1, SUBLANES - (CONV_W - 1):, :][None]

    xs = x_sample.reshape(db, d)
    (qm_s, km_s, vm_s, og_s, qf_s, kf_s, _, vf_s, _, sga_s, sgb_s, sm_s, _) = _inproj(
        xs, mod_s[1], mod_s[0], n1w, wts, db)
    ym_s, c_s, n_s, m_s = _mlstm_step(qm_s, km_s, vm_s, sm_s, og_s, hw,
                                      state_C[layer], state_n[layer], state_m[layer])
    n_pool, page = cache_k.shape[1], cache_k.shape[2]
    yf_s = _fox_decode(page_table,
                       cache_k[layer].reshape(n_pool, page, F_W),
                       cache_v[layer].reshape(n_pool, page, F_W),
                       cache_logf[layer].transpose(0, 2, 1),
                       qf_s.reshape(db, 1, F_W), kf_s.reshape(db, 1, F_W), vf_s.reshape(db, 1, F_W),
                       sm_s.reshape(db, 1, LANES)).reshape(db, F_W)
    x1_s, h2_s = _merge(ym_s, yf_s, sga_s, sgb_s, xs, mod_s[2], mod_s[4], mod_s[3], n2w, wts, db)
    prev_s = state_conv[layer].transpose(1, 0, 2)
    ys, a_s = _ffn(h2_s, x1_s, mod_s[5], prev_s, wts, db, seq=False)

    y_sample = ys.reshape(db, 1, d)
    k_sample = kf_s.reshape(depth, db, 1, F_HEADS, F_DH)
    v_sample = vf_s.reshape(depth, db, 1, F_HEADS, F_DH)
    logf_sample = sm_s[:, LFF0:LFF0 + F_HEADS].reshape(depth, db, 1, F_HEADS)
    conv_sample = jnp.stack([state_conv[layer][:, CONV_W - 2, :], a_s], axis=1)[None]

    return (y_prompt, y_sample, k_prompt, v_prompt, logf_prompt, c_prompt_out, n_prompt, m_prompt, conv_prompt,
            k_sample, v_sample, logf_sample, c_s[None], n_s[None], m_s[None], conv_sample)
```

```python
import functools

import jax
import jax.numpy as jnp
from jax import lax
from jax.experimental import pallas as pl
from jax.experimental.pallas import tpu as pltpu

F32 = jnp.float32
BF16 = jnp.bfloat16

NORM_EPS = 1e-6
M_HEADS, M_DK, M_DV = 4, 128, 256
F_HEADS, F_DH = 8, 128
M_QK = M_HEADS * M_DK
M_V = M_HEADS * M_DV
F_W = F_HEADS * F_DH
CONV_W = 3
CHUNK = 128
LANES = 128
SUBLANES = 8
SMALL_ROWS = 16
VMEM_LIMIT = 56 * 1024 * 1024

IG0, LFM0, LFF0 = 0, M_HEADS, 2 * M_HEADS

LOG2E = 1.4426950408889634
FOX_Q_SCALE = F_DH ** -0.5 * LOG2E


def _dot(a, b):
    return jnp.dot(a, b, preferred_element_type=F32)


def _dot_nt(a, b):
    return lax.dot_general(a, b, (((1,), (1,)), ((), ())), preferred_element_type=F32)


def _dot_tn(a, b):
    return lax.dot_general(a, b, (((0,), (0,)), ((), ())), preferred_element_type=F32)


def _split3(x):
    hi = x.astype(BF16)
    r1 = x - hi.astype(F32)
    mid = r1.astype(BF16)
    lo = (r1 - mid.astype(F32)).astype(BF16)
    return hi, mid, lo


def _dot_f32_lhs(x, mat01):
    hi, mid, lo = _split3(x)
    return _dot(hi, mat01) + _dot(mid, mat01) + _dot(lo, mat01)


def _dot_f32_rhs(mat01, x):
    hi, mid, lo = _split3(x)
    return _dot(mat01, hi) + _dot(mat01, mid) + _dot(mat01, lo)


def _log_sigmoid(x):
    return jnp.minimum(x, 0.0) - jnp.log1p(jnp.exp(-jnp.abs(x)))


def _tri01(n, kind):
    r = lax.broadcasted_iota(jnp.int32, (n, n), 0)
    c = lax.broadcasted_iota(jnp.int32, (n, n), 1)
    if kind == "lower":
        m = c <= r
    elif kind == "upper":
        m = r <= c
    else:
        m = r > c
    return jnp.where(m, 1.0, 0.0).astype(BF16)


def _params(*sem):
    return pltpu.CompilerParams(dimension_semantics=sem, vmem_limit_bytes=VMEM_LIMIT)


def _ada_kernel(c_ref, w_ref, b_ref, o_ref):
    c = c_ref[...]
    s = (c * jax.nn.sigmoid(c)).astype(BF16)
    o_ref[...] = _dot(s, w_ref[...].astype(BF16)) + b_ref[...]


def _ada(c_all, w_ada, b_ada, tn=1024):
    rows, d = c_all.shape
    n = w_ada.shape[1]
    return pl.pallas_call(
        _ada_kernel,
        grid=(n // tn,),
        in_specs=[pl.BlockSpec((rows, d), lambda j: (0, 0)),
                  pl.BlockSpec((d, tn), lambda j: (0, j)),
                  pl.BlockSpec((1, tn), lambda j: (0, j))],
        out_specs=pl.BlockSpec((rows, tn), lambda j: (0, j)),
        out_shape=jax.ShapeDtypeStruct((rows, n), F32),
        compiler_params=_params("arbitrary"),
        name="ada",
    )(c_all, w_ada, b_ada.reshape(1, n))


INPROJ_TN = 512
_J_QM, _J_KM, _J_VM, _J_OG, _J_QF, _J_KF, _J_VF, _J_GA, _J_GB, _J_END = 0, 1, 2, 4, 6, 8, 10, 12, 16, 20


def _head_rmsnorm(acc, w):
    outs = []
    for h in range(acc.shape[1] // F_DH):
        sl = acc[:, h * F_DH:(h + 1) * F_DH]
        ms = jnp.mean(sl * sl, axis=-1, keepdims=True)
        outs.append(sl * lax.rsqrt(ms + NORM_EPS) * w)
    return outs


def _inproj_kernel(x_ref, sc_ref, sh_ref, n1w_ref, wbig_ref, bbig_ref, wsm_ref, bsm_ref, wsmt_ref, bsmt_ref,
                   qnw_ref, knw_ref,
                   qm_ref, km_ref, vm_ref, og_ref, qf_ref, kf_ref, kfb_ref, vf_ref, vfb_ref, sga_ref, sgb_ref,
                   sm_ref, smt_ref, h_ref):
    j = pl.program_id(1)

    @pl.when(j == 0)
    def _():
        x = x_ref[...]
        rstd = lax.rsqrt(jnp.mean(x * x, axis=-1, keepdims=True) + NORM_EPS)
        h = (x * rstd * n1w_ref[...]) * (1.0 + sc_ref[...]) + sh_ref[...]
        hb = h.astype(BF16)
        h_ref[...] = hb
        sm = _dot(hb, wsm_ref[...]) + bsm_ref[...]
        lane = lax.broadcasted_iota(jnp.int32, sm.shape, 1)
        sm_ref[...] = jnp.where(lane < LFM0, sm, _log_sigmoid(sm))
        smt = _dot_nt(wsmt_ref[...], hb) + bsmt_ref[...]
        row = lax.broadcasted_iota(jnp.int32, smt.shape, 0)
        smt_ref[...] = jnp.where(row < LFM0, smt, _log_sigmoid(smt))

    acc = _dot(h_ref[...], wbig_ref[...]) + bbig_ref[...]

    @pl.when(j == _J_QM)
    def _():
        qm_ref[...] = acc.astype(BF16)

    @pl.when(j == _J_KM)
    def _():
        km_ref[...] = (acc * (M_DK ** -0.5)).astype(BF16)

    @pl.when(jnp.logical_and(j >= _J_VM, j < _J_OG))
    def _():
        vm_ref[...] = acc.astype(BF16)

    @pl.when(jnp.logical_and(j >= _J_OG, j < _J_QF))
    def _():
        og_ref[...] = jax.nn.sigmoid(acc).astype(BF16)

    @pl.when(jnp.logical_and(j >= _J_QF, j < _J_KF))
    def _():
        for h, y in enumerate(_head_rmsnorm(acc, qnw_ref[...])):
            qf_ref[:, h * F_DH:(h + 1) * F_DH] = (y * FOX_Q_SCALE).astype(BF16)

    @pl.when(jnp.logical_and(j >= _J_KF, j < _J_VF))
    def _():
        for h, y in enumerate(_head_rmsnorm(acc, knw_ref[...])):
            kf_ref[:, h * F_DH:(h + 1) * F_DH] = y
            kfb_ref[:, h * F_DH:(h + 1) * F_DH] = y.astype(BF16)

    @pl.when(jnp.logical_and(j >= _J_VF, j < _J_GA))
    def _():
        vf_ref[...] = acc
        vfb_ref[...] = acc.astype(BF16)

    @pl.when(jnp.logical_and(j >= _J_GA, j < _J_GB))
    def _():
        sga_ref[...] = jax.nn.sigmoid(acc).astype(BF16)

    @pl.when(j >= _J_GB)
    def _():
        sgb_ref[...] = jax.nn.sigmoid(acc).astype(BF16)


def _inproj(x2, sc, sh, n1w, wts, tm):
    m, d = x2.shape
    groups, mod_rows, _ = sc.shape
    tiles_per_group = (m // tm) // groups
    tn = INPROJ_TN

    def col(j0, ntiles):
        return lambda i, j: (i, jnp.clip(j - j0, 0, ntiles - 1))

    mod_spec = pl.BlockSpec((None, mod_rows, d), lambda i, j: (i // tiles_per_group, 0, 0))
    in_specs = [
        pl.BlockSpec((tm, d), lambda i, j: (i, 0), pipeline_mode=pl.Buffered(1)),
        mod_spec, mod_spec,
        pl.BlockSpec((1, d), lambda i, j: (0, 0)),
        pl.BlockSpec((d, tn), lambda i, j: (0, j)),
        pl.BlockSpec((1, tn), lambda i, j: (0, j)),
        pl.BlockSpec((d, LANES), lambda i, j: (0, 0)),
        pl.BlockSpec((1, LANES), lambda i, j: (0, 0)),
        pl.BlockSpec((SMALL_ROWS, d), lambda i, j: (0, 0)),
        pl.BlockSpec((SMALL_ROWS, 1), lambda i, j: (0, 0)),
        pl.BlockSpec((1, F_DH), lambda i, j: (0, 0)),
        pl.BlockSpec((1, F_DH), lambda i, j: (0, 0)),
    ]
    out_shapes = [
        jax.ShapeDtypeStruct((m, M_QK), BF16),
        jax.ShapeDtypeStruct((m, M_QK), BF16),
        jax.ShapeDtypeStruct((m, M_V), BF16),
        jax.ShapeDtypeStruct((m, M_V), BF16),
        jax.ShapeDtypeStruct((m, F_W), BF16),
        jax.ShapeDtypeStruct((m, F_W), F32),
        jax.ShapeDtypeStruct((m, F_W), BF16),
        jax.ShapeDtypeStruct((m, F_W), F32),
        jax.ShapeDtypeStruct((m, F_W), BF16),
        jax.ShapeDtypeStruct((m, d), BF16),
        jax.ShapeDtypeStruct((m, d), BF16),
        jax.ShapeDtypeStruct((m, LANES), F32),
        jax.ShapeDtypeStruct((SMALL_ROWS, m), F32),
    ]
    out_specs = [
        pl.BlockSpec((tm, tn), col(_J_QM, 1)),
        pl.BlockSpec((tm, tn), col(_J_KM, 1)),
        pl.BlockSpec((tm, tn), col(_J_VM, 2)),
        pl.BlockSpec((tm, tn), col(_J_OG, 2)),
        pl.BlockSpec((tm, tn), col(_J_QF, 2)),
        pl.BlockSpec((tm, tn), col(_J_KF, 2)),
        pl.BlockSpec((tm, tn), col(_J_KF, 2)),
        pl.BlockSpec((tm, tn), col(_J_VF, 2)),
        pl.BlockSpec((tm, tn), col(_J_VF, 2)),
        pl.BlockSpec((tm, tn), col(_J_GA, 4)),
        pl.BlockSpec((tm, tn), col(_J_GB, 4)),
        pl.BlockSpec((tm, LANES), lambda i, j: (i, 0)),
        pl.BlockSpec((SMALL_ROWS, tm), lambda i, j: (0, i)),
    ]
    return pl.pallas_call(
        _inproj_kernel,
        grid=(m // tm, _J_END),
        in_specs=in_specs,
        out_specs=out_specs,
        out_shape=out_shapes,
        scratch_shapes=[pltpu.VMEM((tm, d), BF16)],
        compiler_params=_params("arbitrary", "arbitrary"),
        name="inproj",
    )(x2, sc, sh, n1w, wts["wbig"], wts["bbig"], wts["wsm"], wts["bsm"], wts["wsmt"], wts["bsmt"],
      wts["qnw"], wts["knw"])


def _cumsum_kernel(sm_ref, fcol_ref, carry_ref):
    c = pl.program_id(1)
    tc = sm_ref.shape[0]

    @pl.when(c == 0)
    def _():
        carry_ref[...] = jnp.zeros_like(carry_ref)

    fc = _dot_f32_rhs(_tri01(tc, "lower"), sm_ref[...]) + carry_ref[0:1, :]
    fcol_ref[...] = fc
    carry_ref[...] = jnp.broadcast_to(fc[tc - 1:tc, :], carry_ref.shape)


def _cumsum(sm, batch, tc=512):
    m = sm.shape[0]
    nt = (m // batch) // tc
    return pl.pallas_call(
        _cumsum_kernel,
        grid=(batch, nt),
        in_specs=[pl.BlockSpec((tc, LANES), lambda b, c: (b * nt + c, 0))],
        out_specs=pl.BlockSpec((tc, LANES), lambda b, c: (b * nt + c, 0)),
        out_shape=jax.ShapeDtypeStruct((m, LANES), F32),
        scratch_shapes=[pltpu.VMEM((SUBLANES, LANES), F32)],
        compiler_params=_params("arbitrary", "arbitrary"),
        name="cumsum",
    )(sm)


def _mlstm_kernel(q_ref, k_ref, v_ref, sm_ref, smt_ref, og_ref, hw_ref,
                  ym_ref, cout_ref, nm_ref, c_ref, n_ref, m_ref):
    c = pl.program_id(1)
    L = q_ref.shape[0]

    @pl.when(c == 0)
    def _():
        c_ref[...] = jnp.zeros_like(c_ref)
        n_ref[...] = jnp.zeros_like(n_ref)
        m_ref[...] = jnp.zeros_like(m_ref)

    r = lax.broadcasted_iota(jnp.int32, (L, L), 0)
    s_idx = lax.broadcasted_iota(jnp.int32, (L, L), 1)
    causal = s_idx <= r
    sm = sm_ref[...]
    smt = smt_ref[...]
    bcol_all = _dot_f32_rhs(_tri01(L, "lower"), sm)
    brow_all = _dot_f32_lhs(smt, _tri01(L, "upper"))

    for h in range(M_HEADS):
        ig_col = sm[:, IG0 + h:IG0 + h + 1]
        b_col = bcol_all[:, LFM0 + h:LFM0 + h + 1]
        ig_row = smt[IG0 + h:IG0 + h + 1, :]
        b_row = brow_all[LFM0 + h:LFM0 + h + 1, :]
        m0 = m_ref[h:h + 1, 0:1]
        a_row = ig_row - b_row
        cm_col = jnp.max(jnp.where(causal, a_row, -jnp.inf), axis=1, keepdims=True)
        mx_col = jnp.maximum(m0, cm_col)
        m_col = b_col + mx_col
        dmat = jnp.exp(jnp.where(causal, a_row - mx_col, -jnp.inf))
        inter = jnp.exp(m0 - mx_col)
        q = q_ref[:, h * M_DK:(h + 1) * M_DK]
        k = k_ref[:, h * M_DK:(h + 1) * M_DK]
        v = v_ref[:, h * M_DV:(h + 1) * M_DV]
        c0 = c_ref[h]
        n0 = n_ref[h:h + 1, :]
        s = _dot_nt(q, k) * dmat
        num = _dot(s.astype(BF16), v) + inter * _dot_nt(q, c0.astype(BF16))
        den = jnp.sum(s, axis=1, keepdims=True) + inter * jnp.sum(q.astype(F32) * n0, axis=1, keepdims=True)
        hh = num / jnp.maximum(jnp.abs(den), jnp.exp(-m_col))
        hn = hh * lax.rsqrt(jnp.mean(hh * hh, axis=1, keepdims=True) + NORM_EPS) * hw_ref[:, h * M_DV:(h + 1) * M_DV]
        ym_ref[:, h * M_DV:(h + 1) * M_DV] = (og_ref[:, h * M_DV:(h + 1) * M_DV] * hn).astype(BF16)
        m_new = m_col[L - 1:L, :]
        b_last = b_col[L - 1:L, :]
        w_end = jnp.exp(ig_col + b_last - b_col - m_new)
        decay = jnp.exp(b_last + m0 - m_new)
        vw = (w_end * v.astype(F32)).astype(BF16)
        c_ref[h] = decay * c0 + _dot_tn(vw, k)
        n_ref[h:h + 1, :] = decay * n0 + jnp.sum(w_end * k.astype(F32), axis=0, keepdims=True)
        m_ref[h:h + 1, :] = jnp.broadcast_to(m_new, (1, LANES))

    @pl.when(c == pl.num_programs(1) - 1)
    def _():
        cout_ref[...] = c_ref[...]
        nm_ref[0:M_HEADS, :] = n_ref[0:M_HEADS, :]
        nm_ref[M_HEADS:2 * M_HEADS, :] = m_ref[0:M_HEADS, :]


def _mlstm_prompt(qm, km, vm, sm, smt, og, hw, batch):
    m = qm.shape[0]
    nc = (m // batch) // CHUNK
    row = lambda b, c: (b * nc + c, 0)
    return pl.pallas_call(
        _mlstm_kernel,
        grid=(batch, nc),
        in_specs=[pl.BlockSpec((CHUNK, M_QK), row),
                  pl.BlockSpec((CHUNK, M_QK), row),
                  pl.BlockSpec((CHUNK, M_V), row),
                  pl.BlockSpec((CHUNK, LANES), row),
                  pl.BlockSpec((SMALL_ROWS, CHUNK), lambda b, c: (0, b * nc + c)),
                  pl.BlockSpec((CHUNK, M_V), row),
                  pl.BlockSpec((1, M_V), lambda b, c: (0, 0))],
        out_specs=[pl.BlockSpec((CHUNK, M_V), row),
                   pl.BlockSpec((None, M_HEADS, M_DV, M_DK), lambda b, c: (b, 0, 0, 0)),
                   pl.BlockSpec((None, 2 * M_HEADS, M_DK), lambda b, c: (b, 0, 0))],
        out_shape=[jax.ShapeDtypeStruct((m, M_V), BF16),
                   jax.ShapeDtypeStruct((batch, M_HEADS, M_DV, M_DK), F32),
                   jax.ShapeDtypeStruct((batch, 2 * M_HEADS, M_DK), F32)],
        scratch_shapes=[pltpu.VMEM((M_HEADS, M_DV, M_DK), F32),
                        pltpu.VMEM((SUBLANES, M_DK), F32),
                        pltpu.VMEM((SUBLANES, LANES), F32)],
        compiler_params=_params("arbitrary", "arbitrary"),
        name="mlstm",
    )(qm, km, vm, sm, smt, og, hw)


FOX_HEADS_PER_STEP = 2


def _forget_columns(f, first):
    hi, mid, lo = _split3(f)
    lane = lax.broadcasted_iota(jnp.int32, (f.shape[0], LANES), 1)
    ones = jnp.where(jnp.abs(lane - (4 - first)) <= 1, 1.0, 0.0)
    ext = jnp.where(lane == first, hi.astype(F32),
                    jnp.where(lane == first + 1, mid.astype(F32),
                              jnp.where(lane == first + 2, lo.astype(F32), ones)))
    return ext.astype(BF16)


def _fox_kernel(q_ref, k_ref, v_ref, fcol_ref, o_ref, qaug_ref, kaug_ref, m_ref, l_ref, acc_ref):
    hp = pl.program_id(1)
    qi = pl.program_id(2)
    tq = q_ref.shape[0]
    t = k_ref.shape[0]
    nh = FOX_HEADS_PER_STEP

    def forget_sum(fc, h2):
        lane = lax.broadcasted_iota(jnp.int32, fc.shape, 1)
        sel = jnp.where(lane == LFF0 + nh * hp + h2, fc, 0.0)
        return jnp.sum(sel, axis=1, keepdims=True) * LOG2E

    @pl.when(qi == 0)
    def _():
        fc = fcol_ref[...]
        for h2 in range(nh):
            kaug_ref[h2, :, 0:F_DH] = k_ref[:, h2 * F_DH:(h2 + 1) * F_DH]
            kaug_ref[h2, :, F_DH:2 * F_DH] = _forget_columns(-forget_sum(fc, h2), 3)

    fq = fcol_ref[pl.ds(pl.multiple_of(qi * tq, tq), tq), :]
    for h2 in range(nh):
        qaug_ref[h2, :, 0:F_DH] = q_ref[:, h2 * F_DH:(h2 + 1) * F_DH]
        qaug_ref[h2, :, F_DH:2 * F_DH] = _forget_columns(forget_sum(fq, h2), 0)
    m_ref[...] = jnp.full_like(m_ref, -jnp.inf)
    l_ref[...] = jnp.zeros_like(l_ref)
    acc_ref[...] = jnp.zeros_like(acc_ref)

    def block(kj, masked):
        off = pl.multiple_of(kj * tq, tq)
        for h2 in range(nh):
            s = _dot_nt(qaug_ref[h2], kaug_ref[h2, pl.ds(off, tq), :])
            if masked:
                r = lax.broadcasted_iota(jnp.int32, (tq, tq), 0)
                c = lax.broadcasted_iota(jnp.int32, (tq, tq), 1)
                s = jnp.where(r >= c, s, -jnp.inf)
            m_old = m_ref[h2]
            m_new = jnp.maximum(m_old, jnp.max(s, axis=1, keepdims=True))
            alpha = jnp.exp2(m_old - m_new)
            p = jnp.exp2(s - m_new)
            l_ref[h2] = alpha * l_ref[h2] + jnp.sum(p, axis=1, keepdims=True)
            vb = v_ref[pl.ds(off, tq), h2 * F_DH:(h2 + 1) * F_DH]
            acc_ref[h2] = alpha * acc_ref[h2] + _dot(p.astype(BF16), vb)
            m_ref[h2] = m_new

    def body(kj, carry):
        block(kj, False)
        return carry

    lax.fori_loop(0, qi, body, 0)
    block(qi, True)
    for h2 in range(nh):
        o_ref[:, h2 * F_DH:(h2 + 1) * F_DH] = (acc_ref[h2] / l_ref[h2]).astype(BF16)


def _fox_prompt(qf, kfb, vfb, fcol, batch, tq=512):
    m = qf.shape[0]
    t = m // batch
    nq = t // tq
    nh = FOX_HEADS_PER_STEP
    wide = nh * F_DH
    return pl.pallas_call(
        _fox_kernel,
        grid=(batch, F_HEADS // nh, nq),
        in_specs=[pl.BlockSpec((tq, wide), lambda b, h, i: (b * nq + i, h)),
                  pl.BlockSpec((t, wide), lambda b, h, i: (b, h)),
                  pl.BlockSpec((t, wide), lambda b, h, i: (b, h)),
                  pl.BlockSpec((t, LANES), lambda b, h, i: (b, 0))],
        out_specs=pl.BlockSpec((tq, wide), lambda b, h, i: (b * nq + i, h)),
        out_shape=jax.ShapeDtypeStruct((m, F_W), BF16),
        scratch_shapes=[pltpu.VMEM((nh, tq, 2 * F_DH), BF16),
                        pltpu.VMEM((nh, t, 2 * F_DH), BF16),
                        pltpu.VMEM((nh, tq, 1), F32),
                        pltpu.VMEM((nh, tq, 1), F32),
                        pltpu.VMEM((nh, tq, F_DH), F32)],
        compiler_params=_params("arbitrary", "arbitrary", "arbitrary"),
        name="fox",
    )(qf, kfb, vfb, fcol)


MSTEP_ROWS = 8


def _mstep_kernel(q_ref, k_ref, v_ref, sm_ref, og_ref, hw_ref, c0_ref, n0_ref, m0_ref,
                  ym_ref, cout_ref, nout_ref, mout_ref, cq_ref):
    rows = q_ref.shape[0]
    sm = sm_ref[...]
    ig = sm[:, IG0:IG0 + M_HEADS]
    lf = sm[:, LFM0:LFM0 + M_HEADS]
    m0 = m0_ref[...]
    m_new = lf + jnp.maximum(m0, ig - lf)
    w_end = jnp.exp(ig - m_new)
    decay = jnp.exp(lf + m0 - m_new)
    mout_ref[...] = m_new
    row_id = lax.broadcasted_iota(jnp.int32, (SUBLANES, M_DV), 0)

    for h in range(M_HEADS):
        qh = q_ref[:, h * M_DK:(h + 1) * M_DK]
        kh = k_ref[:, h * M_DK:(h + 1) * M_DK]
        vh = v_ref[:, h * M_DV:(h + 1) * M_DV]
        qf = qh.astype(F32)
        kf = kh.astype(F32)
        vf = vh.astype(F32)
        wh = w_end[:, h:h + 1]
        dh = decay[:, h:h + 1]
        for r in range(rows):
            c0 = c0_ref[r, h]
            q8 = jnp.broadcast_to(qh[r:r + 1, :], (SUBLANES, M_DK))
            cq_ref[r:r + 1, :] = _dot_nt(q8, c0.astype(BF16))[0:1, :]
            vw8 = jnp.where(row_id == 0, jnp.broadcast_to(wh[r:r + 1, :] * vf[r:r + 1, :], (SUBLANES, M_DV)), 0.0)
            k8 = jnp.broadcast_to(kh[r:r + 1, :], (SUBLANES, M_DK))
            cout_ref[r, h] = dh[r:r + 1, :] * c0 + _dot_tn(vw8.astype(BF16), k8)
        n0 = n0_ref[:, h, :]
        s = jnp.sum(qf * kf, axis=1, keepdims=True) * wh
        num = s * vf + dh * cq_ref[...]
        den = s + dh * jnp.sum(n0 * qf, axis=1, keepdims=True)
        hh = num / jnp.maximum(jnp.abs(den), jnp.exp(-m_new[:, h:h + 1]))
        hn = hh * lax.rsqrt(jnp.mean(hh * hh, axis=1, keepdims=True) + NORM_EPS) * hw_ref[:, h * M_DV:(h + 1) * M_DV]
        ym_ref[:, h * M_DV:(h + 1) * M_DV] = (og_ref[:, h * M_DV:(h + 1) * M_DV] * hn).astype(BF16)
        nout_ref[:, h, :] = dh * n0 + wh * kf


def _mlstm_step(qm, km, vm, sm, og, hw, c0, n0, m0):
    db = qm.shape[0]
    rows = MSTEP_ROWS
    row = lambda i: (i, 0)
    return pl.pallas_call(
        _mstep_kernel,
        grid=(db // rows,),
        in_specs=[pl.BlockSpec((rows, M_QK), row),
                  pl.BlockSpec((rows, M_QK), row),
                  pl.BlockSpec((rows, M_V), row),
                  pl.BlockSpec((rows, LANES), row),
                  pl.BlockSpec((rows, M_V), row),
                  pl.BlockSpec((1, M_V), lambda i: (0, 0)),
                  pl.BlockSpec((rows, M_HEADS, M_DV, M_DK), lambda i: (i, 0, 0, 0)),
                  pl.BlockSpec((rows, M_HEADS, M_DK), lambda i: (i, 0, 0)),
                  pl.BlockSpec((rows, M_HEADS), row)],
        out_specs=[pl.BlockSpec((rows, M_V), row),
                   pl.BlockSpec((rows, M_HEADS, M_DV, M_DK), lambda i: (i, 0, 0, 0)),
                   pl.BlockSpec((rows, M_HEADS, M_DK), lambda i: (i, 0, 0)),
                   pl.BlockSpec((rows, M_HEADS), row)],
        out_shape=[jax.ShapeDtypeStruct((db, M_V), BF16),
                   jax.ShapeDtypeStruct(c0.shape, F32),
                   jax.ShapeDtypeStruct(n0.shape, F32),
                   jax.ShapeDtypeStruct(m0.shape, F32)],
        scratch_shapes=[pltpu.VMEM((rows, M_DV), F32)],
        compiler_params=_params("arbitrary"),
        name="mstep",
    )(qm, km, vm, sm, og, hw, c0, n0, m0)


def _decode_kernel(*refs, n_pages):
    k_refs = refs[1:1 + n_pages]
    v_refs = refs[1 + n_pages:1 + 2 * n_pages]
    lf_refs = refs[1 + 2 * n_pages:1 + 3 * n_pages]
    q_ref, kn_ref, vn_ref, sm_ref, o_ref, s_ref, x_ref = refs[1 + 3 * n_pages:]
    pw = x_ref.shape[1]

    for p in range(n_pages):
        x_ref[p:p + 1, :] = lf_refs[p][...]
    x = x_ref[...] * LOG2E
    lane = lax.broadcasted_iota(jnp.int32, x.shape, 1)
    y, z = x, x
    step = F_HEADS
    while step < pw:
        y = y + jnp.where(lane < pw - step, pltpu.roll(y, pw - step, axis=1), 0.0)
        z = z + pltpu.roll(z, step, axis=1)
        step *= 2
    after = y - x

    r = lax.broadcasted_iota(jnp.int32, (LANES, pw), 0)
    c = lax.broadcasted_iota(jnp.int32, (LANES, pw), 1)
    spread = jnp.where(r == LFF0 + (c & (F_HEADS - 1)), 1.0, 0.0).astype(BF16)
    run = _dot_f32_lhs(jnp.broadcast_to(sm_ref[...] * LOG2E, (SUBLANES, LANES)), spread)[0:1, :]

    q = q_ref[...]
    sub = lax.broadcasted_iota(jnp.int32, (F_HEADS, pw), 0)
    lane_h = lax.broadcasted_iota(jnp.int32, (F_HEADS, pw), 1) & (F_HEADS - 1)
    valid = lane_h == sub
    s_self = jnp.sum(q.astype(F32) * kn_ref[...], axis=1, keepdims=True)
    m_run = s_self
    for p in reversed(range(n_pages)):
        kb = k_refs[p][...].astype(BF16)
        st = _dot_nt(q, kb) + (run + after[p:p + 1, :])
        st = jnp.where(valid, st, -jnp.inf)
        s_ref[p] = st
        m_run = jnp.maximum(m_run, jnp.max(st, axis=1, keepdims=True))
        run = run + z[p:p + 1, :]
    w_self = jnp.exp2(s_self - m_run)
    l = w_self
    acc = w_self * vn_ref[...]
    for p in range(n_pages):
        pe = jnp.exp2(s_ref[p] - m_run)
        l = l + jnp.sum(pe, axis=1, keepdims=True)
        acc = acc + _dot(pe.astype(BF16), v_refs[p][...].astype(BF16))
    o_ref[...] = (acc / l).astype(BF16)


def _fox_decode(page_table, ck, cv, clf, layer, q3, k3, v3, sm3):
    db, n_pages = page_table.shape
    pw = ck.shape[2]
    seq = lambda b, pt: (b, 0, 0)

    def pg(p):
        return lambda b, pt: (layer, pt[b, p], 0, 0)

    in_specs = ([pl.BlockSpec((None, None, pw, F_DH), pg(p)) for p in range(n_pages)]
                + [pl.BlockSpec((None, None, pw, F_DH), pg(p)) for p in range(n_pages)]
                + [pl.BlockSpec((None, None, 1, pw), pg(p)) for p in range(n_pages)]
                + [pl.BlockSpec((None, F_HEADS, F_DH), seq),
                   pl.BlockSpec((None, F_HEADS, F_DH), seq),
                   pl.BlockSpec((None, F_HEADS, F_DH), seq),
                   pl.BlockSpec((None, 1, LANES), seq)])
    grid_spec = pltpu.PrefetchScalarGridSpec(
        num_scalar_prefetch=1,
        grid=(db,),
        in_specs=in_specs,
        out_specs=pl.BlockSpec((None, F_HEADS, F_DH), seq),
        scratch_shapes=[pltpu.VMEM((n_pages, F_HEADS, pw), F32),
                        pltpu.VMEM((n_pages, pw), F32)],
    )
    return pl.pallas_call(
        functools.partial(_decode_kernel, n_pages=n_pages),
        grid_spec=grid_spec,
        out_shape=jax.ShapeDtypeStruct((db, F_HEADS, F_DH), BF16),
        compiler_params=_params("arbitrary"),
        name="decode",
    )(page_table, *([ck] * n_pages), *([cv] * n_pages), *([clf] * n_pages), q3, k3, v3, sm3)


def _merge_kernel(ym_ref, yf_ref, sga_ref, sgb_ref, x_ref, g1_ref, sc_ref, sh_ref, n2w_ref,
                  wa_ref, wb_ref, wo_ref, x1_ref, h2_ref):
    merged = sga_ref[...] * _dot(ym_ref[...], wa_ref[...]) + sgb_ref[...] * _dot(yf_ref[...], wb_ref[...])
    x1 = x_ref[...] + g1_ref[...] * _dot(merged.astype(BF16), wo_ref[...])
    x1_ref[...] = x1
    rstd = lax.rsqrt(jnp.mean(x1 * x1, axis=-1, keepdims=True) + NORM_EPS)
    h2_ref[...] = ((x1 * rstd * n2w_ref[...]) * (1.0 + sc_ref[...]) + sh_ref[...]).astype(BF16)


def _merge(ym, yf, sga, sgb, x2, g1, sc2, sh2, n2w, wts, tm):
    m, d = x2.shape
    groups, mod_rows, _ = g1.shape
    tiles_per_group = (m // tm) // groups
    row = lambda i: (i, 0)
    const = lambda i: (0, 0)
    mod_spec = pl.BlockSpec((None, mod_rows, d), lambda i: (i // tiles_per_group, 0, 0))
    resident = dict(pipeline_mode=pl.Buffered(1))
    return pl.pallas_call(
        _merge_kernel,
        grid=(m // tm,),
        in_specs=[pl.BlockSpec((tm, M_V), row),
                  pl.BlockSpec((tm, F_W), row),
                  pl.BlockSpec((tm, d), row),
                  pl.BlockSpec((tm, d), row),
                  pl.BlockSpec((tm, d), row),
                  mod_spec, mod_spec, mod_spec,
                  pl.BlockSpec((1, d), const),
                  pl.BlockSpec((M_V, d), const, **resident),
                  pl.BlockSpec((F_W, d), const, **resident),
                  pl.BlockSpec((d, d), const, **resident)],
        out_specs=[pl.BlockSpec((tm, d), row), pl.BlockSpec((tm, d), row)],
        out_shape=[jax.ShapeDtypeStruct((m, d), F32), jax.ShapeDtypeStruct((m, d), BF16)],
        compiler_params=_params("arbitrary"),
        name="merge",
    )(ym, yf, sga, sgb, x2, g1, sc2, sh2, n2w, wts["wa"], wts["wb"], wts["wo"])


FFN_TF = 512


def _ffn_kernel(h2_ref, x1_ref, g2_ref, p_ref, wa_ref, wg_ref, cw_ref, cb_ref, w2_ref,
                y_ref, a_out_ref, carry_ref, *, seq):
    f = pl.program_id(2)
    tm = h2_ref.shape[0]
    h2 = h2_ref[...]
    a = _dot(h2, wa_ref[...])
    g = _dot(h2, wg_ref[...])
    if seq:
        i = pl.program_id(1)

        @pl.when(i == 0)
        def _():
            carry_ref[f] = p_ref[...]

        prev = carry_ref[f]
        p2, p1 = prev[SUBLANES - 2:SUBLANES - 1, :], prev[SUBLANES - 1:SUBLANES, :]
        r = lax.broadcasted_iota(jnp.int32, a.shape, 0)
        a1 = jnp.where(r == 0, p1, pltpu.roll(a, 1, axis=0))
        a2 = jnp.where(r == 0, p2, jnp.where(r == 1, p1, pltpu.roll(a, 2, axis=0)))
        carry_ref[f] = a[tm - SUBLANES:tm, :]
        a_out_ref[...] = a[tm - SUBLANES:tm, :]
    else:
        a2, a1 = p_ref[0], p_ref[1]
        a_out_ref[...] = a
    ac = cw_ref[0:1, :] * a2 + cw_ref[1:2, :] * a1 + cw_ref[2:3, :] * a + cb_ref[...]
    u = (jax.nn.gelu(ac, approximate=True) * g).astype(BF16)
    contrib = _dot(u, w2_ref[...])

    @pl.when(f == 0)
    def _():
        y_ref[...] = contrib

    @pl.when(f > 0)
    def _():
        y_ref[...] += contrib

    @pl.when(f == pl.num_programs(2) - 1)
    def _():
        y_ref[...] = x1_ref[...] + g2_ref[...] * y_ref[...]


def _ffn(h2, x1, g2, prev, wts, tm, seq):
    m, d = x1.shape
    groups, mod_rows, _ = g2.shape
    tiles = (m // tm) // groups
    dff = wts["w2"].shape[0]
    tf = FFN_TF
    nf = dff // tf
    row = lambda b, i, f: (b * tiles + i, 0)
    if seq:
        prev_spec = pl.BlockSpec((None, SUBLANES, tf), lambda b, i, f: (b, 0, f))
        a_spec = pl.BlockSpec((None, None, SUBLANES, tf), lambda b, i, f: (b, i, 0, f))
        a_shape = jax.ShapeDtypeStruct((groups, tiles, SUBLANES, dff), F32)
    else:
        prev_spec = pl.BlockSpec((CONV_W - 1, tm, tf), lambda b, i, f: (0, i, f))
        a_spec = pl.BlockSpec((tm, tf), lambda b, i, f: (i, f))
        a_shape = jax.ShapeDtypeStruct((m, dff), F32)
    return pl.pallas_call(
        functools.partial(_ffn_kernel, seq=seq),
        grid=(groups, tiles, nf),
        in_specs=[pl.BlockSpec((tm, d), row),
                  pl.BlockSpec((tm, d), row, pipeline_mode=pl.Buffered(1)),
                  pl.BlockSpec((None, mod_rows, d), lambda b, i, f: (b, 0, 0)),
                  prev_spec,
                  pl.BlockSpec((d, tf), lambda b, i, f: (0, f)),
                  pl.BlockSpec((d, tf), lambda b, i, f: (0, f + nf)),
                  pl.BlockSpec((CONV_W, tf), lambda b, i, f: (0, f)),
                  pl.BlockSpec((1, tf), lambda b, i, f: (0, f)),
                  pl.BlockSpec((tf, d), lambda b, i, f: (f, 0))],
        out_specs=[pl.BlockSpec((tm, d), row), a_spec],
        out_shape=[jax.ShapeDtypeStruct((m, d), F32), a_shape],
        scratch_shapes=[pltpu.VMEM((nf, SUBLANES, tf), F32)],
        compiler_params=_params("arbitrary", "arbitrary", "arbitrary"),
        name="ffn_seq" if seq else "ffn_row",
    )(h2, x1, g2, prev, wts["w1"], wts["w1"], wts["cw"], wts["cb"], wts["w2"])


def _prep_weights(w_in, b_in, f_qnorm_w, f_knorm_w, w_proj_a, w_proj_b, w_out, w_ffn_in, conv_w, conv_b, w_ffn_out):
    sizes = (M_QK, M_QK, M_V, M_V, M_HEADS, M_HEADS, F_W, F_W, F_W, F_HEADS)
    o = [0]
    for s in sizes:
        o.append(o[-1] + s)
    big = lambda a: jnp.concatenate([a[..., :o[4]], a[..., o[6]:o[9]], a[..., o[10]:]], axis=-1)
    small = lambda a: jnp.concatenate([a[..., o[4]:o[6]], a[..., o[9]:o[10]]], axis=-1)
    wsm = small(w_in)
    bsm = small(b_in)
    return {
        "wbig": big(w_in).astype(BF16),
        "bbig": big(b_in).reshape(1, -1),
        "wsm": jnp.pad(wsm, ((0, 0), (0, LANES - SMALL_ROWS))).astype(BF16),
        "bsm": jnp.pad(bsm, (0, LANES - SMALL_ROWS)).reshape(1, LANES),
        "wsmt": wsm.T.astype(BF16),
        "bsmt": bsm.reshape(SMALL_ROWS, 1),
        "qnw": f_qnorm_w.reshape(1, F_DH),
        "knw": f_knorm_w.reshape(1, F_DH),
        "wa": w_proj_a.astype(BF16),
        "wb": w_proj_b.astype(BF16),
        "wo": w_out.astype(BF16),
        "w1": w_ffn_in.astype(BF16),
        "cw": conv_w,
        "cb": conv_b.reshape(1, -1),
        "w2": w_ffn_out.astype(BF16),
    }


def kernel(x_prompt, x_sample, c_prompt, c_sample, cache_k, cache_v, cache_logf, page_table, state_C, state_n, state_m, state_conv, w_ada, b_ada, norm1_w, w_in, b_in, m_hnorm_w, f_qnorm_w, f_knorm_w, w_proj_a, w_proj_b, w_out, norm2_w, w_ffn_in, conv_w, conv_b, w_ffn_out):
    depth = w_ada.shape[0]
    assert depth == 1, "single-layer step"
    batch, seq_len, d = x_prompt.shape
    db = x_sample.shape[0]
    assert x_sample.shape[1] == 1
    dff = w_ffn_out.shape[1]
    layer = 0

    wts = _prep_weights(w_in[layer], b_in[layer], f_qnorm_w[layer], f_knorm_w[layer], w_proj_a[layer],
                        w_proj_b[layer], w_out[layer], w_ffn_in[layer], conv_w[layer], conv_b[layer],
                        w_ffn_out[layer])
    n1w = norm1_w[layer].reshape(1, d)
    n2w = norm2_w[layer].reshape(1, d)
    hw = m_hnorm_w[layer].reshape(1, M_V)

    mod = _ada(jnp.concatenate([c_prompt, c_sample], axis=0), w_ada[layer], b_ada[layer])
    mod_p = [t.reshape(batch, 1, d) for t in jnp.split(mod[:batch], 6, axis=-1)]
    mod_s = [t.reshape(1, db, d) for t in jnp.split(mod[batch:], 6, axis=-1)]

    tm_p = min(1024, seq_len)
    xp = x_prompt.reshape(batch * seq_len, d)
    (qm, km, vm, og, qf, kf, kfb, vf, vfb, sga, sgb, sm, smt) = _inproj(xp, mod_p[1], mod_p[0], n1w, wts, tm_p)
    fcol = _cumsum(sm, batch)
    ym, c_p, nm_p = _mlstm_prompt(qm, km, vm, sm, smt, og, hw, batch)
    yf = _fox_prompt(qf, kfb, vfb, fcol, batch)
    x1, h2 = _merge(ym, yf, sga, sgb, xp, mod_p[2], mod_p[4], mod_p[3], n2w, wts, 512)
    conv0 = jnp.zeros((batch, SUBLANES, dff), F32)
    yp, a_last = _ffn(h2, x1, mod_p[5], conv0, wts, tm_p, seq=True)

    y_prompt = yp.reshape(batch, seq_len, d)
    k_prompt = kf.reshape(depth, batch, seq_len, F_HEADS, F_DH)
    v_prompt = vf.reshape(depth, batch, seq_len, F_HEADS, F_DH)
    logf_prompt = sm[:, LFF0:LFF0 + F_HEADS].reshape(depth, batch, seq_len, F_HEADS)
    c_prompt_out = c_p[None]
    n_prompt = nm_p[:, :M_HEADS, :][None]
    m_prompt = nm_p[:, M_HEADS:, 0][None]
    conv_prompt = a_last[:, -1, SUBLANES - (CONV_W - 1):, :][None]

    xs = x_sample.reshape(db, d)
    (qm_s, km_s, vm_s, og_s, qf_s, kf_s, _, vf_s, _, sga_s, sgb_s, sm_s, _) = _inproj(
        xs, mod_s[1], mod_s[0], n1w, wts, db)
    ym_s, c_s, n_s, m_s = _mlstm_step(qm_s, km_s, vm_s, sm_s, og_s, hw,
                                      state_C[layer], state_n[layer], state_m[layer])
    n_pool, page = cache_k.shape[1], cache_k.shape[2]
    yf_s = _fox_decode(page_table,
                       cache_k.reshape(depth, n_pool, page * F_HEADS, F_DH),
                       cache_v.reshape(depth, n_pool, page * F_HEADS, F_DH),
                       cache_logf.reshape(depth, n_pool, 1, page * F_HEADS),
                       layer,
                       qf_s.reshape(db, F_HEADS, F_DH), kf_s.reshape(db, F_HEADS, F_DH),
                       vf_s.reshape(db, F_HEADS, F_DH), sm_s.reshape(db, 1, LANES)).reshape(db, F_W)
    x1_s, h2_s = _merge(ym_s, yf_s, sga_s, sgb_s, xs, mod_s[2], mod_s[4], mod_s[3], n2w, wts, db)
    prev_s = state_conv[layer].transpose(1, 0, 2)
    ys, a_s = _ffn(h2_s, x1_s, mod_s[5], prev_s, wts, db, seq=False)

    y_sample = ys.reshape(db, 1, d)
    k_sample = kf_s.reshape(depth, db, 1, F_HEADS, F_DH)
    v_sample = vf_s.reshape(depth, db, 1, F_HEADS, F_DH)
    logf_sample = sm_s[:, LFF0:LFF0 + F_HEADS].reshape(depth, db, 1, F_HEADS)
    conv_sample = jnp.stack([state_conv[layer][:, CONV_W - 2, :], a_s], axis=1)[None]

    return (y_prompt, y_sample, k_prompt, v_prompt, logf_prompt, c_prompt_out, n_prompt, m_prompt, conv_prompt,
            k_sample, v_sample, logf_sample, c_s[None], n_s[None], m_s[None], conv_sample)
```

```python
import functools

import jax
import jax.numpy as jnp
from jax import lax
from jax.experimental import pallas as pl
from jax.experimental.pallas import tpu as pltpu

F32 = jnp.float32
BF16 = jnp.bfloat16

NORM_EPS = 1e-6
M_HEADS, M_DK, M_DV = 4, 128, 256
F_HEADS, F_DH = 8, 128
M_QK = M_HEADS * M_DK
M_V = M_HEADS * M_DV
F_W = F_HEADS * F_DH
CONV_W = 3
CHUNK = 128
LANES = 128
SUBLANES = 8
SMALL_ROWS = 16
VMEM_LIMIT = 56 * 1024 * 1024

IG0, LFM0, LFF0 = 0, M_HEADS, 2 * M_HEADS

LOG2E = 1.4426950408889634
FOX_Q_SCALE = F_DH ** -0.5 * LOG2E


def _dot(a, b):
    return jnp.dot(a, b, preferred_element_type=F32)


def _dot_nt(a, b):
    return lax.dot_general(a, b, (((1,), (1,)), ((), ())), preferred_element_type=F32)


def _dot_tn(a, b):
    return lax.dot_general(a, b, (((0,), (0,)), ((), ())), preferred_element_type=F32)


def _split3(x):
    hi = x.astype(BF16)
    r1 = x - hi.astype(F32)
    mid = r1.astype(BF16)
    lo = (r1 - mid.astype(F32)).astype(BF16)
    return hi, mid, lo


def _dot_f32_lhs(x, mat01):
    hi, mid, lo = _split3(x)
    return _dot(hi, mat01) + _dot(mid, mat01) + _dot(lo, mat01)


def _dot_f32_rhs(mat01, x):
    hi, mid, lo = _split3(x)
    return _dot(mat01, hi) + _dot(mat01, mid) + _dot(mat01, lo)


def _log_sigmoid(x):
    return jnp.minimum(x, 0.0) - jnp.log1p(jnp.exp(-jnp.abs(x)))


def _tri01(n, kind):
    r = lax.broadcasted_iota(jnp.int32, (n, n), 0)
    c = lax.broadcasted_iota(jnp.int32, (n, n), 1)
    if kind == "lower":
        m = c <= r
    elif kind == "upper":
        m = r <= c
    else:
        m = r > c
    return jnp.where(m, 1.0, 0.0).astype(BF16)


def _params(*sem):
    return pltpu.CompilerParams(dimension_semantics=sem, vmem_limit_bytes=VMEM_LIMIT)


def _ada_kernel(c_ref, w_ref, b_ref, o_ref):
    c = c_ref[...]
    s = (c * jax.nn.sigmoid(c)).astype(BF16)
    o_ref[...] = _dot(s, w_ref[...].astype(BF16)) + b_ref[...]


def _ada(c_all, w_ada, b_ada, tn=1024):
    rows, d = c_all.shape
    n = w_ada.shape[1]
    return pl.pallas_call(
        _ada_kernel,
        grid=(n // tn,),
        in_specs=[pl.BlockSpec((rows, d), lambda j: (0, 0)),
                  pl.BlockSpec((d, tn), lambda j: (0, j)),
                  pl.BlockSpec((1, tn), lambda j: (0, j))],
        out_specs=pl.BlockSpec((rows, tn), lambda j: (0, j)),
        out_shape=jax.ShapeDtypeStruct((rows, n), F32),
        compiler_params=_params("arbitrary"),
        name="ada",
    )(c_all, w_ada, b_ada.reshape(1, n))


INPROJ_TN = 512
_J_QM, _J_KM, _J_VM, _J_OG, _J_QF, _J_KF, _J_VF, _J_GA, _J_GB, _J_END = 0, 1, 2, 4, 6, 8, 10, 12, 16, 20


def _head_rmsnorm(acc, w):
    outs = []
    for h in range(acc.shape[1] // F_DH):
        sl = acc[:, h * F_DH:(h + 1) * F_DH]
        ms = jnp.mean(sl * sl, axis=-1, keepdims=True)
        outs.append(sl * lax.rsqrt(ms + NORM_EPS) * w)
    return outs


def _inproj_kernel(x_ref, sc_ref, sh_ref, n1w_ref, wbig_ref, bbig_ref, wsm_ref, bsm_ref, qnw_ref, knw_ref,
                   qm_ref, km_ref, vm_ref, og_ref, qf_ref, kf_ref, kfb_ref, vf_ref, vfb_ref, sga_ref, sgb_ref,
                   sm_ref, smt_ref, h_ref):
    j = pl.program_id(1)

    @pl.when(j == 0)
    def _():
        x = x_ref[...]
        rstd = lax.rsqrt(jnp.mean(x * x, axis=-1, keepdims=True) + NORM_EPS)
        h = (x * rstd * n1w_ref[...]) * (1.0 + sc_ref[...]) + sh_ref[...]
        hb = h.astype(BF16)
        h_ref[...] = hb
        sm = _dot(hb, wsm_ref[...]) + bsm_ref[...]
        lane = lax.broadcasted_iota(jnp.int32, sm.shape, 1)
        sm = jnp.where(lane < LFM0, sm, _log_sigmoid(sm))
        sm_ref[...] = sm
        smt_ref[...] = sm.T[0:SMALL_ROWS, :]

    acc = _dot(h_ref[...], wbig_ref[...]) + bbig_ref[...]

    @pl.when(j == _J_QM)
    def _():
        qm_ref[...] = acc.astype(BF16)

    @pl.when(j == _J_KM)
    def _():
        km_ref[...] = (acc * (M_DK ** -0.5)).astype(BF16)

    @pl.when(jnp.logical_and(j >= _J_VM, j < _J_OG))
    def _():
        vm_ref[...] = acc.astype(BF16)

    @pl.when(jnp.logical_and(j >= _J_OG, j < _J_QF))
    def _():
        og_ref[...] = jax.nn.sigmoid(acc).astype(BF16)

    @pl.when(jnp.logical_and(j >= _J_QF, j < _J_KF))
    def _():
        for h, y in enumerate(_head_rmsnorm(acc, qnw_ref[...])):
            qf_ref[:, h * F_DH:(h + 1) * F_DH] = (y * FOX_Q_SCALE).astype(BF16)

    @pl.when(jnp.logical_and(j >= _J_KF, j < _J_VF))
    def _():
        for h, y in enumerate(_head_rmsnorm(acc, knw_ref[...])):
            kf_ref[:, h * F_DH:(h + 1) * F_DH] = y
            kfb_ref[:, h * F_DH:(h + 1) * F_DH] = y.astype(BF16)

    @pl.when(jnp.logical_and(j >= _J_VF, j < _J_GA))
    def _():
        vf_ref[...] = acc
        vfb_ref[...] = acc.astype(BF16)

    @pl.when(jnp.logical_and(j >= _J_GA, j < _J_GB))
    def _():
        sga_ref[...] = jax.nn.sigmoid(acc).astype(BF16)

    @pl.when(j >= _J_GB)
    def _():
        sgb_ref[...] = jax.nn.sigmoid(acc).astype(BF16)


def _inproj(x2, sc, sh, n1w, wts, tm):
    m, d = x2.shape
    groups, mod_rows, _ = sc.shape
    tiles_per_group = (m // tm) // groups
    tn = INPROJ_TN

    def col(j0, ntiles):
        return lambda i, j: (i, jnp.clip(j - j0, 0, ntiles - 1))

    mod_spec = pl.BlockSpec((None, mod_rows, d), lambda i, j: (i // tiles_per_group, 0, 0))
    in_specs = [
        pl.BlockSpec((tm, d), lambda i, j: (i, 0), pipeline_mode=pl.Buffered(1)),
        mod_spec, mod_spec,
        pl.BlockSpec((1, d), lambda i, j: (0, 0)),
        pl.BlockSpec((d, tn), lambda i, j: (0, j)),
        pl.BlockSpec((1, tn), lambda i, j: (0, j)),
        pl.BlockSpec((d, LANES), lambda i, j: (0, 0)),
        pl.BlockSpec((1, LANES), lambda i, j: (0, 0)),
        pl.BlockSpec((1, F_DH), lambda i, j: (0, 0)),
        pl.BlockSpec((1, F_DH), lambda i, j: (0, 0)),
    ]
    out_shapes = [
        jax.ShapeDtypeStruct((m, M_QK), BF16),
        jax.ShapeDtypeStruct((m, M_QK), BF16),
        jax.ShapeDtypeStruct((m, M_V), BF16),
        jax.ShapeDtypeStruct((m, M_V), BF16),
        jax.ShapeDtypeStruct((m, F_W), BF16),
        jax.ShapeDtypeStruct((m, F_W), F32),
        jax.ShapeDtypeStruct((m, F_W), BF16),
        jax.ShapeDtypeStruct((m, F_W), F32),
        jax.ShapeDtypeStruct((m, F_W), BF16),
        jax.ShapeDtypeStruct((m, d), BF16),
        jax.ShapeDtypeStruct((m, d), BF16),
        jax.ShapeDtypeStruct((m, LANES), F32),
        jax.ShapeDtypeStruct((SMALL_ROWS, m), F32),
    ]
    out_specs = [
        pl.BlockSpec((tm, tn), col(_J_QM, 1)),
        pl.BlockSpec((tm, tn), col(_J_KM, 1)),
        pl.BlockSpec((tm, tn), col(_J_VM, 2)),
        pl.BlockSpec((tm, tn), col(_J_OG, 2)),
        pl.BlockSpec((tm, tn), col(_J_QF, 2)),
        pl.BlockSpec((tm, tn), col(_J_KF, 2)),
        pl.BlockSpec((tm, tn), col(_J_KF, 2)),
        pl.BlockSpec((tm, tn), col(_J_VF, 2)),
        pl.BlockSpec((tm, tn), col(_J_VF, 2)),
        pl.BlockSpec((tm, tn), col(_J_GA, 4)),
        pl.BlockSpec((tm, tn), col(_J_GB, 4)),
        pl.BlockSpec((tm, LANES), lambda i, j: (i, 0)),
        pl.BlockSpec((SMALL_ROWS, tm), lambda i, j: (0, i)),
    ]
    return pl.pallas_call(
        _inproj_kernel,
        grid=(m // tm, _J_END),
        in_specs=in_specs,
        out_specs=out_specs,
        out_shape=out_shapes,
        scratch_shapes=[pltpu.VMEM((tm, d), BF16)],
        compiler_params=_params("arbitrary", "arbitrary"),
        name="inproj",
    )(x2, sc, sh, n1w, wts["wbig"], wts["bbig"], wts["wsm"], wts["bsm"], wts["qnw"], wts["knw"])


def _cumsum_kernel(sm_ref, fcol_ref, carry_ref):
    c = pl.program_id(1)
    tc = sm_ref.shape[0]

    @pl.when(c == 0)
    def _():
        carry_ref[...] = jnp.zeros_like(carry_ref)

    fc = _dot_f32_rhs(_tri01(tc, "lower"), sm_ref[...]) + carry_ref[0:1, :]
    fcol_ref[...] = fc
    carry_ref[...] = jnp.broadcast_to(fc[tc - 1:tc, :], carry_ref.shape)


def _cumsum(sm, batch, tc=512):
    m = sm.shape[0]
    nt = (m // batch) // tc
    return pl.pallas_call(
        _cumsum_kernel,
        grid=(batch, nt),
        in_specs=[pl.BlockSpec((tc, LANES), lambda b, c: (b * nt + c, 0))],
        out_specs=pl.BlockSpec((tc, LANES), lambda b, c: (b * nt + c, 0)),
        out_shape=jax.ShapeDtypeStruct((m, LANES), F32),
        scratch_shapes=[pltpu.VMEM((SUBLANES, LANES), F32)],
        compiler_params=_params("arbitrary", "arbitrary"),
        name="cumsum",
    )(sm)


def _mlstm_kernel(q_ref, k_ref, v_ref, sm_ref, smt_ref, og_ref, hw_ref,
                  ym_ref, cout_ref, nm_ref, c_ref, n_ref, m_ref):
    c = pl.program_id(1)
    L = q_ref.shape[0]

    @pl.when(c == 0)
    def _():
        c_ref[...] = jnp.zeros_like(c_ref)
        n_ref[...] = jnp.zeros_like(n_ref)
        m_ref[...] = jnp.zeros_like(m_ref)

    r = lax.broadcasted_iota(jnp.int32, (L, L), 0)
    s_idx = lax.broadcasted_iota(jnp.int32, (L, L), 1)
    causal = s_idx <= r
    sm = sm_ref[...]
    smt = smt_ref[...]
    bcol_all = _dot_f32_rhs(_tri01(L, "lower"), sm)
    brow_all = _dot_f32_lhs(smt, _tri01(L, "upper"))

    for h in range(M_HEADS):
        ig_col = sm[:, IG0 + h:IG0 + h + 1]
        b_col = bcol_all[:, LFM0 + h:LFM0 + h + 1]
        ig_row = smt[IG0 + h:IG0 + h + 1, :]
        b_row = brow_all[LFM0 + h:LFM0 + h + 1, :]
        m0 = m_ref[h:h + 1, 0:1]
        a_row = ig_row - b_row
        cm_col = jnp.max(jnp.where(causal, a_row, -jnp.inf), axis=1, keepdims=True)
        mx_col = jnp.maximum(m0, cm_col)
        m_col = b_col + mx_col
        dmat = jnp.exp(jnp.where(causal, a_row - mx_col, -jnp.inf))
        inter = jnp.exp(m0 - mx_col)
        q = q_ref[:, h * M_DK:(h + 1) * M_DK]
        k = k_ref[:, h * M_DK:(h + 1) * M_DK]
        v = v_ref[:, h * M_DV:(h + 1) * M_DV]
        c0 = c_ref[h]
        n0 = n_ref[h:h + 1, :]
        s = _dot_nt(q, k) * dmat
        num = _dot(s.astype(BF16), v) + inter * _dot_nt(q, c0.astype(BF16))
        den = jnp.sum(s, axis=1, keepdims=True) + inter * jnp.sum(q.astype(F32) * n0, axis=1, keepdims=True)
        hh = num / jnp.maximum(jnp.abs(den), jnp.exp(-m_col))
        hn = hh * lax.rsqrt(jnp.mean(hh * hh, axis=1, keepdims=True) + NORM_EPS) * hw_ref[:, h * M_DV:(h + 1) * M_DV]
        ym_ref[:, h * M_DV:(h + 1) * M_DV] = (og_ref[:, h * M_DV:(h + 1) * M_DV] * hn).astype(BF16)
        m_new = m_col[L - 1:L, :]
        b_last = b_col[L - 1:L, :]
        w_end = jnp.exp(ig_col + b_last - b_col - m_new)
        decay = jnp.exp(b_last + m0 - m_new)
        vw = (w_end * v.astype(F32)).astype(BF16)
        c_ref[h] = decay * c0 + _dot_tn(vw, k)
        n_ref[h:h + 1, :] = decay * n0 + jnp.sum(w_end * k.astype(F32), axis=0, keepdims=True)
        m_ref[h:h + 1, :] = jnp.broadcast_to(m_new, (1, LANES))

    @pl.when(c == pl.num_programs(1) - 1)
    def _():
        cout_ref[...] = c_ref[...]
        nm_ref[0:M_HEADS, :] = n_ref[0:M_HEADS, :]
        nm_ref[M_HEADS:2 * M_HEADS, :] = m_ref[0:M_HEADS, :]


def _mlstm_prompt(qm, km, vm, sm, smt, og, hw, batch):
    m = qm.shape[0]
    nc = (m // batch) // CHUNK
    row = lambda b, c: (b * nc + c, 0)
    return pl.pallas_call(
        _mlstm_kernel,
        grid=(batch, nc),
        in_specs=[pl.BlockSpec((CHUNK, M_QK), row),
                  pl.BlockSpec((CHUNK, M_QK), row),
                  pl.BlockSpec((CHUNK, M_V), row),
                  pl.BlockSpec((CHUNK, LANES), row),
                  pl.BlockSpec((SMALL_ROWS, CHUNK), lambda b, c: (0, b * nc + c)),
                  pl.BlockSpec((CHUNK, M_V), row),
                  pl.BlockSpec((1, M_V), lambda b, c: (0, 0))],
        out_specs=[pl.BlockSpec((CHUNK, M_V), row),
                   pl.BlockSpec((None, M_HEADS, M_DV, M_DK), lambda b, c: (b, 0, 0, 0)),
                   pl.BlockSpec((None, 2 * M_HEADS, M_DK), lambda b, c: (b, 0, 0))],
        out_shape=[jax.ShapeDtypeStruct((m, M_V), BF16),
                   jax.ShapeDtypeStruct((batch, M_HEADS, M_DV, M_DK), F32),
                   jax.ShapeDtypeStruct((batch, 2 * M_HEADS, M_DK), F32)],
        scratch_shapes=[pltpu.VMEM((M_HEADS, M_DV, M_DK), F32),
                        pltpu.VMEM((SUBLANES, M_DK), F32),
                        pltpu.VMEM((SUBLANES, LANES), F32)],
        compiler_params=_params("arbitrary", "arbitrary"),
        name="mlstm",
    )(qm, km, vm, sm, smt, og, hw)


FOX_HEADS_PER_STEP = 2
FOX_ROW_CHUNK = 32


def _forget_columns(f, first):
    hi, mid, lo = _split3(f)
    lane = lax.broadcasted_iota(jnp.int32, (f.shape[0], LANES), 1)
    ones = jnp.where(jnp.abs(lane - (4 - first)) <= 1, 1.0, 0.0)
    ext = jnp.where(lane == first, hi.astype(F32),
                    jnp.where(lane == first + 1, mid.astype(F32),
                              jnp.where(lane == first + 2, lo.astype(F32), ones)))
    return ext.astype(BF16)


def _fox_kernel(q_ref, k_ref, v_ref, fcol_ref, o_ref, qaug_ref, kaug_ref, s_ref, p_ref, m_ref, l_ref, acc_ref):
    hp = pl.program_id(1)
    qi = pl.program_id(2)
    tq = q_ref.shape[0]
    t = k_ref.shape[0]
    nh = FOX_HEADS_PER_STEP

    def forget_sum(fc, h2):
        lane = lax.broadcasted_iota(jnp.int32, fc.shape, 1)
        sel = jnp.where(lane == LFF0 + nh * hp + h2, fc, 0.0)
        return jnp.sum(sel, axis=1, keepdims=True) * LOG2E

    @pl.when(qi == 0)
    def _():
        fc = fcol_ref[...]
        for h2 in range(nh):
            kaug_ref[h2, :, 0:F_DH] = k_ref[:, h2 * F_DH:(h2 + 1) * F_DH]
            kaug_ref[h2, :, F_DH:2 * F_DH] = _forget_columns(-forget_sum(fc, h2), 3)

    fq = fcol_ref[pl.ds(pl.multiple_of(qi * tq, tq), tq), :]
    for h2 in range(nh):
        qaug_ref[h2, :, 0:F_DH] = q_ref[:, h2 * F_DH:(h2 + 1) * F_DH]
        qaug_ref[h2, :, F_DH:2 * F_DH] = _forget_columns(forget_sum(fq, h2), 0)
    m_ref[...] = jnp.full_like(m_ref, -jnp.inf)
    l_ref[...] = jnp.zeros_like(l_ref)
    acc_ref[...] = jnp.zeros_like(acc_ref)
    rc = FOX_ROW_CHUNK
    n_lane_tiles = tq // LANES

    def softmax_rows(h2, c, masked):
        rows = slice(c * rc, (c + 1) * rc)
        s = s_ref[h2, rows, :]
        if masked:
            r = lax.broadcasted_iota(jnp.int32, (rc, tq), 0) + c * rc
            col = lax.broadcasted_iota(jnp.int32, (rc, tq), 1)
            s = jnp.where(r >= col, s, -jnp.inf)
        tiles = [s[:, k * LANES:(k + 1) * LANES] for k in range(n_lane_tiles)]
        m_old = m_ref[h2, rows, :]
        mx = functools.reduce(jnp.maximum, tiles)
        m_new = jnp.maximum(m_old, jnp.max(mx, axis=1, keepdims=True))
        alpha = jnp.exp2(m_old - m_new)
        ps = [jnp.exp2(tile - m_new) for tile in tiles]
        row_sum = jnp.sum(functools.reduce(jnp.add, ps), axis=1, keepdims=True)
        l_ref[h2, rows, :] = alpha * l_ref[h2, rows, :] + row_sum
        for k in range(n_lane_tiles):
            p_ref[h2, rows, k * LANES:(k + 1) * LANES] = ps[k].astype(BF16)
        acc_ref[h2, rows, :] = alpha * acc_ref[h2, rows, :]
        m_ref[h2, rows, :] = m_new

    def block(kj, masked):
        off = pl.multiple_of(kj * tq, tq)
        for h2 in range(nh):
            s_ref[h2] = _dot_nt(qaug_ref[h2], kaug_ref[h2, pl.ds(off, tq), :])
        for h2 in range(nh):
            for c in range(tq // rc):
                softmax_rows(h2, c, masked)
            acc_ref[h2] += _dot(p_ref[h2], v_ref[pl.ds(off, tq), h2 * F_DH:(h2 + 1) * F_DH])

    def body(kj, carry):
        block(kj, False)
        return carry

    lax.fori_loop(0, qi, body, 0)
    block(qi, True)
    for h2 in range(nh):
        o_ref[:, h2 * F_DH:(h2 + 1) * F_DH] = (acc_ref[h2] / l_ref[h2]).astype(BF16)


def _fox_prompt(qf, kfb, vfb, fcol, batch, tq=512):
    m = qf.shape[0]
    t = m // batch
    nq = t // tq
    nh = FOX_HEADS_PER_STEP
    wide = nh * F_DH
    return pl.pallas_call(
        _fox_kernel,
        grid=(batch, F_HEADS // nh, nq),
        in_specs=[pl.BlockSpec((tq, wide), lambda b, h, i: (b * nq + i, h)),
                  pl.BlockSpec((t, wide), lambda b, h, i: (b, h)),
                  pl.BlockSpec((t, wide), lambda b, h, i: (b, h)),
                  pl.BlockSpec((t, LANES), lambda b, h, i: (b, 0))],
        out_specs=pl.BlockSpec((tq, wide), lambda b, h, i: (b * nq + i, h)),
        out_shape=jax.ShapeDtypeStruct((m, F_W), BF16),
        scratch_shapes=[pltpu.VMEM((nh, tq, 2 * F_DH), BF16),
                        pltpu.VMEM((nh, t, 2 * F_DH), BF16),
                        pltpu.VMEM((nh, tq, tq), F32),
                        pltpu.VMEM((nh, tq, tq), BF16),
                        pltpu.VMEM((nh, tq, LANES), F32),
                        pltpu.VMEM((nh, tq, LANES), F32),
                        pltpu.VMEM((nh, tq, F_DH), F32)],
        compiler_params=_params("arbitrary", "arbitrary", "arbitrary"),
        name="fox",
    )(qf, kfb, vfb, fcol)


MSTEP_ROWS = 8


def _mstep_kernel(q_ref, k_ref, v_ref, sm_ref, og_ref, hw_ref, c0_ref, n0_ref, m0_ref,
                  ym_ref, cout_ref, nout_ref, mout_ref, cq_ref):
    rows = q_ref.shape[0]
    sm = sm_ref[...]
    ig = sm[:, IG0:IG0 + M_HEADS]
    lf = sm[:, LFM0:LFM0 + M_HEADS]
    m0 = m0_ref[...]
    m_new = lf + jnp.maximum(m0, ig - lf)
    w_end = jnp.exp(ig - m_new)
    decay = jnp.exp(lf + m0 - m_new)
    mout_ref[...] = m_new
    row_id = lax.broadcasted_iota(jnp.int32, (SUBLANES, M_DV), 0)

    for h in range(M_HEADS):
        qh = q_ref[:, h * M_DK:(h + 1) * M_DK]
        kh = k_ref[:, h * M_DK:(h + 1) * M_DK]
        vh = v_ref[:, h * M_DV:(h + 1) * M_DV]
        qf = qh.astype(F32)
        kf = kh.astype(F32)
        vf = vh.astype(F32)
        wh = w_end[:, h:h + 1]
        dh = decay[:, h:h + 1]
        for r in range(rows):
            c0 = c0_ref[r, h]
            q8 = jnp.broadcast_to(qh[r:r + 1, :], (SUBLANES, M_DK))
            cq_ref[r:r + 1, :] = _dot_nt(q8, c0.astype(BF16))[0:1, :]
            vw8 = jnp.where(row_id == 0, jnp.broadcast_to(wh[r:r + 1, :] * vf[r:r + 1, :], (SUBLANES, M_DV)), 0.0)
            k8 = jnp.broadcast_to(kh[r:r + 1, :], (SUBLANES, M_DK))
            cout_ref[r, h] = dh[r:r + 1, :] * c0 + _dot_tn(vw8.astype(BF16), k8)
        n0 = n0_ref[:, h, :]
        s = jnp.sum(qf * kf, axis=1, keepdims=True) * wh
        num = s * vf + dh * cq_ref[...]
        den = s + dh * jnp.sum(n0 * qf, axis=1, keepdims=True)
        hh = num / jnp.maximum(jnp.abs(den), jnp.exp(-m_new[:, h:h + 1]))
        hn = hh * lax.rsqrt(jnp.mean(hh * hh, axis=1, keepdims=True) + NORM_EPS) * hw_ref[:, h * M_DV:(h + 1) * M_DV]
        ym_ref[:, h * M_DV:(h + 1) * M_DV] = (og_ref[:, h * M_DV:(h + 1) * M_DV] * hn).astype(BF16)
        nout_ref[:, h, :] = dh * n0 + wh * kf


def _mlstm_step(qm, km, vm, sm, og, hw, c0, n0, m0):
    db = qm.shape[0]
    rows = MSTEP_ROWS
    row = lambda i: (i, 0)
    return pl.pallas_call(
        _mstep_kernel,
        grid=(db // rows,),
        in_specs=[pl.BlockSpec((rows, M_QK), row),
                  pl.BlockSpec((rows, M_QK), row),
                  pl.BlockSpec((rows, M_V), row),
                  pl.BlockSpec((rows, LANES), row),
                  pl.BlockSpec((rows, M_V), row),
                  pl.BlockSpec((1, M_V), lambda i: (0, 0)),
                  pl.BlockSpec((rows, M_HEADS, M_DV, M_DK), lambda i: (i, 0, 0, 0)),
                  pl.BlockSpec((rows, M_HEADS, M_DK), lambda i: (i, 0, 0)),
                  pl.BlockSpec((rows, M_HEADS), row)],
        out_specs=[pl.BlockSpec((rows, M_V), row),
                   pl.BlockSpec((rows, M_HEADS, M_DV, M_DK), lambda i: (i, 0, 0, 0)),
                   pl.BlockSpec((rows, M_HEADS, M_DK), lambda i: (i, 0, 0)),
                   pl.BlockSpec((rows, M_HEADS), row)],
        out_shape=[jax.ShapeDtypeStruct((db, M_V), BF16),
                   jax.ShapeDtypeStruct(c0.shape, F32),
                   jax.ShapeDtypeStruct(n0.shape, F32),
                   jax.ShapeDtypeStruct(m0.shape, F32)],
        scratch_shapes=[pltpu.VMEM((rows, M_DV), F32)],
        compiler_params=_params("arbitrary"),
        name="mstep",
    )(qm, km, vm, sm, og, hw, c0, n0, m0)


def _decode_kernel(*refs, n_pages):
    k_refs = refs[1:1 + n_pages]
    v_refs = refs[1 + n_pages:1 + 2 * n_pages]
    lf_refs = refs[1 + 2 * n_pages:1 + 3 * n_pages]
    q_ref, kn_ref, vn_ref, sm_ref, o_ref, s_ref, xn_ref = refs[1 + 3 * n_pages:]
    page = xn_ref.shape[1]
    pw = page * F_HEADS

    for p in range(n_pages):
        xn_ref[p * F_HEADS:(p + 1) * F_HEADS, :] = lf_refs[p][...]
    xn = xn_ref[...] * LOG2E
    t_id = lax.broadcasted_iota(jnp.int32, (page, pw), 0)
    l_id = lax.broadcasted_iota(jnp.int32, (page, pw), 1)
    widen = jnp.where((l_id // F_HEADS) == t_id, 1.0, 0.0).astype(BF16)
    wide = _dot_f32_lhs(xn, widen)
    r_id = lax.broadcasted_iota(jnp.int32, wide.shape, 0)
    c_id = lax.broadcasted_iota(jnp.int32, wide.shape, 1)
    own = jnp.where((r_id & (F_HEADS - 1)) == (c_id & (F_HEADS - 1)), wide, 0.0)
    g_p = lax.broadcasted_iota(jnp.int32, (n_pages, n_pages * F_HEADS), 0)
    g_r = lax.broadcasted_iota(jnp.int32, (n_pages, n_pages * F_HEADS), 1)
    gather = jnp.where((g_r // F_HEADS) == g_p, 1.0, 0.0).astype(BF16)
    x = _dot_f32_rhs(gather, own)
    lane = lax.broadcasted_iota(jnp.int32, x.shape, 1)
    y, z = x, x
    step = F_HEADS
    while step < pw:
        y = y + jnp.where(lane < pw - step, pltpu.roll(y, pw - step, axis=1), 0.0)
        z = z + pltpu.roll(z, step, axis=1)
        step *= 2
    after = y - x

    r = lax.broadcasted_iota(jnp.int32, (LANES, pw), 0)
    c = lax.broadcasted_iota(jnp.int32, (LANES, pw), 1)
    spread = jnp.where(r == LFF0 + (c & (F_HEADS - 1)), 1.0, 0.0).astype(BF16)
    run = _dot_f32_lhs(jnp.broadcast_to(sm_ref[...] * LOG2E, (SUBLANES, LANES)), spread)[0:1, :]

    q = q_ref[...]
    sub = lax.broadcasted_iota(jnp.int32, (F_HEADS, pw), 0)
    lane_h = lax.broadcasted_iota(jnp.int32, (F_HEADS, pw), 1) & (F_HEADS - 1)
    valid = lane_h == sub
    s_self = jnp.sum(q.astype(F32) * kn_ref[...], axis=1, keepdims=True)
    m_run = s_self
    for p in reversed(range(n_pages)):
        kb = k_refs[p][...].astype(BF16)
        st = _dot_nt(q, kb) + (run + after[p:p + 1, :])
        st = jnp.where(valid, st, -jnp.inf)
        s_ref[p] = st
        m_run = jnp.maximum(m_run, jnp.max(st, axis=1, keepdims=True))
        run = run + z[p:p + 1, :]
    w_self = jnp.exp2(s_self - m_run)
    l = w_self
    acc = w_self * vn_ref[...]
    for p in range(n_pages):
        pe = jnp.exp2(s_ref[p] - m_run)
        l = l + jnp.sum(pe, axis=1, keepdims=True)
        acc = acc + _dot(pe.astype(BF16), v_refs[p][...].astype(BF16))
    o_ref[...] = (acc / l).astype(BF16)


def _fox_decode(page_table, ck, cv, clf, layer, q3, k3, v3, sm3):
    db, n_pages = page_table.shape
    pw = ck.shape[2]
    page = clf.shape[3]
    seq = lambda b, pt: (b, 0, 0)

    def pg(p):
        return lambda b, pt: (layer, pt[b, p], 0, 0)

    in_specs = ([pl.BlockSpec((None, None, pw, F_DH), pg(p)) for p in range(n_pages)]
                + [pl.BlockSpec((None, None, pw, F_DH), pg(p)) for p in range(n_pages)]
                + [pl.BlockSpec((None, None, F_HEADS, page), pg(p)) for p in range(n_pages)]
                + [pl.BlockSpec((None, F_HEADS, F_DH), seq),
                   pl.BlockSpec((None, F_HEADS, F_DH), seq),
                   pl.BlockSpec((None, F_HEADS, F_DH), seq),
                   pl.BlockSpec((None, 1, LANES), seq)])
    grid_spec = pltpu.PrefetchScalarGridSpec(
        num_scalar_prefetch=1,
        grid=(db,),
        in_specs=in_specs,
        out_specs=pl.BlockSpec((None, F_HEADS, F_DH), seq),
        scratch_shapes=[pltpu.VMEM((n_pages, F_HEADS, pw), F32),
                        pltpu.VMEM((n_pages * F_HEADS, page), F32)],
    )
    return pl.pallas_call(
        functools.partial(_decode_kernel, n_pages=n_pages),
        grid_spec=grid_spec,
        out_shape=jax.ShapeDtypeStruct((db, F_HEADS, F_DH), BF16),
        compiler_params=_params("arbitrary"),
        name="decode",
    )(page_table, *([ck] * n_pages), *([cv] * n_pages), *([clf] * n_pages), q3, k3, v3, sm3)


def _merge_kernel(ym_ref, yf_ref, sga_ref, sgb_ref, x_ref, g1_ref, sc_ref, sh_ref, n2w_ref,
                  wa_ref, wb_ref, wo_ref, x1_ref, h2_ref):
    merged = sga_ref[...] * _dot(ym_ref[...], wa_ref[...]) + sgb_ref[...] * _dot(yf_ref[...], wb_ref[...])
    x1 = x_ref[...] + g1_ref[...] * _dot(merged.astype(BF16), wo_ref[...])
    x1_ref[...] = x1
    rstd = lax.rsqrt(jnp.mean(x1 * x1, axis=-1, keepdims=True) + NORM_EPS)
    h2_ref[...] = ((x1 * rstd * n2w_ref[...]) * (1.0 + sc_ref[...]) + sh_ref[...]).astype(BF16)


def _merge(ym, yf, sga, sgb, x2, g1, sc2, sh2, n2w, wts, tm):
    m, d = x2.shape
    groups, mod_rows, _ = g1.shape
    tiles_per_group = (m // tm) // groups
    row = lambda i: (i, 0)
    const = lambda i: (0, 0)
    mod_spec = pl.BlockSpec((None, mod_rows, d), lambda i: (i // tiles_per_group, 0, 0))
    resident = dict(pipeline_mode=pl.Buffered(1))
    return pl.pallas_call(
        _merge_kernel,
        grid=(m // tm,),
        in_specs=[pl.BlockSpec((tm, M_V), row),
                  pl.BlockSpec((tm, F_W), row),
                  pl.BlockSpec((tm, d), row),
                  pl.BlockSpec((tm, d), row),
                  pl.BlockSpec((tm, d), row),
                  mod_spec, mod_spec, mod_spec,
                  pl.BlockSpec((1, d), const),
                  pl.BlockSpec((M_V, d), const, **resident),
                  pl.BlockSpec((F_W, d), const, **resident),
                  pl.BlockSpec((d, d), const, **resident)],
        out_specs=[pl.BlockSpec((tm, d), row), pl.BlockSpec((tm, d), row)],
        out_shape=[jax.ShapeDtypeStruct((m, d), F32), jax.ShapeDtypeStruct((m, d), BF16)],
        compiler_params=_params("arbitrary"),
        name="merge",
    )(ym, yf, sga, sgb, x2, g1, sc2, sh2, n2w, wts["wa"], wts["wb"], wts["wo"])


FFN_TF = 512


def _ffn_kernel(h2_ref, x1_ref, g2_ref, p_ref, wa_ref, wg_ref, cw_ref, cb_ref, w2_ref,
                y_ref, a_out_ref, carry_ref, *, seq):
    f = pl.program_id(2)
    tm = h2_ref.shape[0]

    @pl.when(f == 0)
    def _():
        y_ref[...] = jnp.zeros_like(y_ref)

    if seq:
        @pl.when(pl.program_id(1) == 0)
        def _():
            carry_ref[f] = p_ref[...]

    h2 = h2_ref[...]
    a = _dot(h2, wa_ref[...])
    g = _dot(h2, wg_ref[...])
    if seq:
        prev = carry_ref[f]
        p2, p1 = prev[SUBLANES - 2:SUBLANES - 1, :], prev[SUBLANES - 1:SUBLANES, :]
        r = lax.broadcasted_iota(jnp.int32, a.shape, 0)
        a1 = jnp.where(r == 0, p1, pltpu.roll(a, 1, axis=0))
        a2 = jnp.where(r == 0, p2, jnp.where(r == 1, p1, pltpu.roll(a, 2, axis=0)))
        carry_ref[f] = a[tm - SUBLANES:tm, :]
        a_out_ref[...] = a[tm - SUBLANES:tm, :]
    else:
        a2, a1 = p_ref[0], p_ref[1]
        a_out_ref[...] = a
    ac = cw_ref[0:1, :] * a2 + cw_ref[1:2, :] * a1 + cw_ref[2:3, :] * a + cb_ref[...]
    u = (jax.nn.gelu(ac, approximate=True) * g).astype(BF16)
    y_ref[...] += _dot(u, w2_ref[...])

    @pl.when(f == pl.num_programs(2) - 1)
    def _():
        y_ref[...] = x1_ref[...] + g2_ref[...] * y_ref[...]


def _ffn(h2, x1, g2, prev, wts, tm, seq):
    m, d = x1.shape
    groups, mod_rows, _ = g2.shape
    tiles = (m // tm) // groups
    dff = wts["w2"].shape[0]
    tf = FFN_TF
    nf = dff // tf
    row = lambda b, i, f: (b * tiles + i, 0)
    if seq:
        prev_spec = pl.BlockSpec((None, SUBLANES, tf), lambda b, i, f: (b, 0, f))
        a_spec = pl.BlockSpec((None, None, SUBLANES, tf), lambda b, i, f: (b, i, 0, f))
        a_shape = jax.ShapeDtypeStruct((groups, tiles, SUBLANES, dff), F32)
    else:
        prev_spec = pl.BlockSpec((CONV_W - 1, tm, tf), lambda b, i, f: (0, i, f))
        a_spec = pl.BlockSpec((tm, tf), lambda b, i, f: (i, f))
        a_shape = jax.ShapeDtypeStruct((m, dff), F32)
    return pl.pallas_call(
        functools.partial(_ffn_kernel, seq=seq),
        grid=(groups, tiles, nf),
        in_specs=[pl.BlockSpec((tm, d), row),
                  pl.BlockSpec((tm, d), row, pipeline_mode=pl.Buffered(1)),
                  pl.BlockSpec((None, mod_rows, d), lambda b, i, f: (b, 0, 0)),
                  prev_spec,
                  pl.BlockSpec((d, tf), lambda b, i, f: (0, f)),
                  pl.BlockSpec((d, tf), lambda b, i, f: (0, f + nf)),
                  pl.BlockSpec((CONV_W, tf), lambda b, i, f: (0, f)),
                  pl.BlockSpec((1, tf), lambda b, i, f: (0, f)),
                  pl.BlockSpec((tf, d), lambda b, i, f: (f, 0))],
        out_specs=[pl.BlockSpec((tm, d), row), a_spec],
        out_shape=[jax.ShapeDtypeStruct((m, d), F32), a_shape],
        scratch_shapes=[pltpu.VMEM((nf, SUBLANES, tf), F32)],
        compiler_params=_params("arbitrary", "arbitrary", "arbitrary"),
        name="ffn_seq" if seq else "ffn_row",
    )(h2, x1, g2, prev, wts["w1"], wts["w1"], wts["cw"], wts["cb"], wts["w2"])


def _prep_weights(w_in, b_in, f_qnorm_w, f_knorm_w, w_proj_a, w_proj_b, w_out, w_ffn_in, conv_w, conv_b, w_ffn_out):
    sizes = (M_QK, M_QK, M_V, M_V, M_HEADS, M_HEADS, F_W, F_W, F_W, F_HEADS)
    o = [0]
    for s in sizes:
        o.append(o[-1] + s)
    big = lambda a: jnp.concatenate([a[..., :o[4]], a[..., o[6]:o[9]], a[..., o[10]:]], axis=-1)
    small = lambda a: jnp.concatenate([a[..., o[4]:o[6]], a[..., o[9]:o[10]]], axis=-1)
    wsm = small(w_in)
    bsm = small(b_in)
    return {
        "wbig": big(w_in).astype(BF16),
        "bbig": big(b_in).reshape(1, -1),
        "wsm": jnp.pad(wsm, ((0, 0), (0, LANES - SMALL_ROWS))).astype(BF16),
        "bsm": jnp.pad(bsm, (0, LANES - SMALL_ROWS)).reshape(1, LANES),
        "qnw": f_qnorm_w.reshape(1, F_DH),
        "knw": f_knorm_w.reshape(1, F_DH),
        "wa": w_proj_a.astype(BF16),
        "wb": w_proj_b.astype(BF16),
        "wo": w_out.astype(BF16),
        "w1": w_ffn_in.astype(BF16),
        "cw": conv_w,
        "cb": conv_b.reshape(1, -1),
        "w2": w_ffn_out.astype(BF16),
    }


def kernel(x_prompt, x_sample, c_prompt, c_sample, cache_k, cache_v, cache_logf, page_table, state_C, state_n, state_m, state_conv, w_ada, b_ada, norm1_w, w_in, b_in, m_hnorm_w, f_qnorm_w, f_knorm_w, w_proj_a, w_proj_b, w_out, norm2_w, w_ffn_in, conv_w, conv_b, w_ffn_out):
    depth = w_ada.shape[0]
    assert depth == 1, "single-layer step"
    batch, seq_len, d = x_prompt.shape
    db = x_sample.shape[0]
    assert x_sample.shape[1] == 1
    dff = w_ffn_out.shape[1]
    layer = 0

    wts = _prep_weights(w_in[layer], b_in[layer], f_qnorm_w[layer], f_knorm_w[layer], w_proj_a[layer],
                        w_proj_b[layer], w_out[layer], w_ffn_in[layer], conv_w[layer], conv_b[layer],
                        w_ffn_out[layer])
    n1w = norm1_w[layer].reshape(1, d)
    n2w = norm2_w[layer].reshape(1, d)
    hw = m_hnorm_w[layer].reshape(1, M_V)

    mod = _ada(jnp.concatenate([c_prompt, c_sample], axis=0), w_ada[layer], b_ada[layer])
    mod_p = [t.reshape(batch, 1, d) for t in jnp.split(mod[:batch], 6, axis=-1)]
    mod_s = [t.reshape(1, db, d) for t in jnp.split(mod[batch:], 6, axis=-1)]

    tm_p = min(1024, seq_len)
    xp = x_prompt.reshape(batch * seq_len, d)
    (qm, km, vm, og, qf, kf, kfb, vf, vfb, sga, sgb, sm, smt) = _inproj(xp, mod_p[1], mod_p[0], n1w, wts, tm_p)
    fcol = _cumsum(sm, batch)
    ym, c_p, nm_p = _mlstm_prompt(qm, km, vm, sm, smt, og, hw, batch)
    yf = _fox_prompt(qf, kfb, vfb, fcol, batch)
    x1, h2 = _merge(ym, yf, sga, sgb, xp, mod_p[2], mod_p[4], mod_p[3], n2w, wts, 512)
    conv0 = jnp.zeros((batch, SUBLANES, dff), F32)
    yp, a_last = _ffn(h2, x1, mod_p[5], conv0, wts, tm_p, seq=True)

    y_prompt = yp.reshape(batch, seq_len, d)
    k_prompt = kf.reshape(depth, batch, seq_len, F_HEADS, F_DH)
    v_prompt = vf.reshape(depth, batch, seq_len, F_HEADS, F_DH)
    logf_prompt = sm[:, LFF0:LFF0 + F_HEADS].reshape(depth, batch, seq_len, F_HEADS)
    c_prompt_out = c_p[None]
    n_prompt = nm_p[:, :M_HEADS, :][None]
    m_prompt = nm_p[:, M_HEADS:, 0][None]
    conv_prompt = a_last[:, -1, SUBLANES - (CONV_W - 1):, :][None]

    xs = x_sample.reshape(db, d)
    (qm_s, km_s, vm_s, og_s, qf_s, kf_s, _, vf_s, _, sga_s, sgb_s, sm_s, _) = _inproj(
        xs, mod_s[1], mod_s[0], n1w, wts, db)
    ym_s, c_s, n_s, m_s = _mlstm_step(qm_s, km_s, vm_s, sm_s, og_s, hw,
                                      state_C[layer], state_n[layer], state_m[layer])
    n_pool, page = cache_k.shape[1], cache_k.shape[2]
    yf_s = _fox_decode(page_table,
                       cache_k.reshape(depth, n_pool, page * F_HEADS, F_DH),
                       cache_v.reshape(depth, n_pool, page * F_HEADS, F_DH),
                       cache_logf.transpose(0, 1, 3, 2),
                       layer,
                       qf_s.reshape(db, F_HEADS, F_DH), kf_s.reshape(db, F_HEADS, F_DH),
                       vf_s.reshape(db, F_HEADS, F_DH), sm_s.reshape(db, 1, LANES)).reshape(db, F_W)
    x1_s, h2_s = _merge(ym_s, yf_s, sga_s, sgb_s, xs, mod_s[2], mod_s[4], mod_s[3], n2w, wts, db)
    prev_s = state_conv[layer].transpose(1, 0, 2)
    ys, a_s = _ffn(h2_s, x1_s, mod_s[5], prev_s, wts, db, seq=False)

    y_sample = ys.reshape(db, 1, d)
    k_sample = kf_s.reshape(depth, db, 1, F_HEADS, F_DH)
    v_sample = vf_s.reshape(depth, db, 1, F_HEADS, F_DH)
    logf_sample = sm_s[:, LFF0:LFF0 + F_HEADS].reshape(depth, db, 1, F_HEADS)
    conv_sample = jnp.stack([state_conv[layer][:, CONV_W - 2, :], a_s], axis=1)[None]

    return (y_prompt, y_sample, k_prompt, v_prompt, logf_prompt, c_prompt_out, n_prompt, m_prompt, conv_prompt,
            k_sample, v_sample, logf_sample, c_s[None], n_s[None], m_s[None], conv_sample)
```

```python
import functools

import jax
import jax.numpy as jnp
from jax import lax
from jax.experimental import pallas as pl
from jax.experimental.pallas import tpu as pltpu

F32 = jnp.float32
BF16 = jnp.bfloat16

NORM_EPS = 1e-6
M_HEADS, M_DK, M_DV = 4, 128, 256
F_HEADS, F_DH = 8, 128
M_QK = M_HEADS * M_DK
M_V = M_HEADS * M_DV
F_W = F_HEADS * F_DH
CONV_W = 3
CHUNK = 128
LANES = 128
SUBLANES = 8
SMALL_ROWS = 16
VMEM_LIMIT = 56 * 1024 * 1024

IG0, LFM0, LFF0 = 0, M_HEADS, 2 * M_HEADS

LOG2E = 1.4426950408889634
FOX_Q_SCALE = F_DH ** -0.5 * LOG2E


def _dot(a, b):
    return jnp.dot(a, b, preferred_element_type=F32)


def _dot_nt(a, b):
    return lax.dot_general(a, b, (((1,), (1,)), ((), ())), preferred_element_type=F32)


def _dot_tn(a, b):
    return lax.dot_general(a, b, (((0,), (0,)), ((), ())), preferred_element_type=F32)


def _split3(x):
    hi = x.astype(BF16)
    r1 = x - hi.astype(F32)
    mid = r1.astype(BF16)
    lo = (r1 - mid.astype(F32)).astype(BF16)
    return hi, mid, lo


def _dot_f32_lhs(x, mat01):
    hi, mid, lo = _split3(x)
    return _dot(hi, mat01) + _dot(mid, mat01) + _dot(lo, mat01)


def _dot_f32_rhs(mat01, x):
    hi, mid, lo = _split3(x)
    return _dot(mat01, hi) + _dot(mat01, mid) + _dot(mat01, lo)


def _log_sigmoid(x):
    return jnp.minimum(x, 0.0) - jnp.log1p(jnp.exp(-jnp.abs(x)))


def _tri01(n, kind):
    r = lax.broadcasted_iota(jnp.int32, (n, n), 0)
    c = lax.broadcasted_iota(jnp.int32, (n, n), 1)
    if kind == "lower":
        m = c <= r
    elif kind == "upper":
        m = r <= c
    else:
        m = r > c
    return jnp.where(m, 1.0, 0.0).astype(BF16)


def _params(*sem):
    return pltpu.CompilerParams(dimension_semantics=sem, vmem_limit_bytes=VMEM_LIMIT)


def _ada_kernel(cp_ref, cs_ref, w_ref, b_ref, op_ref, os_ref):
    w = w_ref[...].astype(BF16)
    for c_ref, o_ref in ((cp_ref, op_ref), (cs_ref, os_ref)):
        c = c_ref[...]
        s = (c * jax.nn.sigmoid(c)).astype(BF16)
        o_ref[...] = _dot(s, w) + b_ref[...]


def _ada(c_p, c_s, w_ada, b_ada, tn=1024):
    d = c_p.shape[1]
    n = w_ada.shape[1]
    return pl.pallas_call(
        _ada_kernel,
        grid=(n // tn,),
        in_specs=[pl.BlockSpec(c_p.shape, lambda j: (0, 0)),
                  pl.BlockSpec(c_s.shape, lambda j: (0, 0)),
                  pl.BlockSpec((d, tn), lambda j: (0, j)),
                  pl.BlockSpec((1, tn), lambda j: (0, j))],
        out_specs=[pl.BlockSpec((c_p.shape[0], tn), lambda j: (0, j)),
                   pl.BlockSpec((c_s.shape[0], tn), lambda j: (0, j))],
        out_shape=[jax.ShapeDtypeStruct((c_p.shape[0], n), F32), jax.ShapeDtypeStruct((c_s.shape[0], n), F32)],
        compiler_params=_params("arbitrary"),
        name="ada",
    )(c_p, c_s, w_ada, b_ada.reshape(1, n))


MOD_SH1, MOD_SC1, MOD_G1, MOD_SH2, MOD_SC2, MOD_G2 = range(6)


PROJ_TN = 512
_J_MIX, _J_GATE, _J_FQ, _J_FK, _J_FV, _J_END = 0, 4, 14, 16, 18, 20
_J_KM = 1


def _inproj_first_kernel(x_ref, sc_ref, sh_ref, n1w_ref, w_ref, b_ref, wsm_ref, bsm_ref,
                         h_ref, qkv_ref, sm_ref, smt_ref):
    j = pl.program_id(1)

    @pl.when(j == 0)
    def _():
        x = x_ref[...]
        rstd = lax.rsqrt(jnp.mean(x * x, axis=-1, keepdims=True) + NORM_EPS)
        h = (x * rstd * n1w_ref[...]) * (1.0 + sc_ref[...]) + sh_ref[...]
        hb = h.astype(BF16)
        h_ref[...] = hb
        sm = _dot(hb, wsm_ref[...]) + bsm_ref[...]
        lane = lax.broadcasted_iota(jnp.int32, sm.shape, 1)
        sm = jnp.where(lane < LFM0, sm, _log_sigmoid(sm))
        sm_ref[...] = sm
        smt_ref[...] = sm.T[0:SMALL_ROWS, :]

    acc = _dot(h_ref[...], w_ref[...]) + b_ref[...]
    scale = jnp.where(j == _J_KM, M_DK ** -0.5, 1.0)
    qkv_ref[...] = (acc * scale).astype(BF16)


def _proj_kernel(h_ref, w_ref, b_ref, *rest, kind, scale, emit_f32):
    acc = _dot(h_ref[...], w_ref[...]) + b_ref[...]
    if kind == "sigmoid":
        (ob_ref,) = rest
        ob_ref[...] = jax.nn.sigmoid(acc).astype(BF16)
    elif kind == "headnorm":
        nw_ref, *outs = rest
        for h in range(acc.shape[1] // F_DH):
            sl = acc[:, h * F_DH:(h + 1) * F_DH]
            ms = jnp.mean(sl * sl, axis=-1, keepdims=True)
            y = sl * lax.rsqrt(ms + NORM_EPS) * nw_ref[...]
            if emit_f32:
                outs[0][:, h * F_DH:(h + 1) * F_DH] = y
            outs[-1][:, h * F_DH:(h + 1) * F_DH] = (y * scale).astype(BF16)
    else:
        of_ref, ob_ref = rest
        of_ref[...] = acc
        ob_ref[...] = acc.astype(BF16)


def _mod_spec(mod, d, chunk, group_of):
    return pl.BlockSpec((None, mod.shape[1], d), lambda *g: (group_of(*g), 0, chunk))


def _inproj_first(x2, mod, n1w, wts, tm):
    m, d = x2.shape
    tiles_per_group = (m // tm) // mod.shape[0]
    group_of = lambda i, j: i // tiles_per_group
    tn = PROJ_TN
    n_tiles = _J_GATE - _J_MIX
    return pl.pallas_call(
        _inproj_first_kernel,
        grid=(m // tm, n_tiles),
        in_specs=[pl.BlockSpec((tm, d), lambda i, j: (i, 0), pipeline_mode=pl.Buffered(1)),
                  _mod_spec(mod, d, MOD_SC1, group_of),
                  _mod_spec(mod, d, MOD_SH1, group_of),
                  pl.BlockSpec((1, d), lambda i, j: (0, 0)),
                  pl.BlockSpec((d, tn), lambda i, j: (0, _J_MIX + j)),
                  pl.BlockSpec((1, tn), lambda i, j: (0, _J_MIX + j)),
                  pl.BlockSpec((d, LANES), lambda i, j: (0, 0)),
                  pl.BlockSpec((1, LANES), lambda i, j: (0, 0))],
        out_specs=[pl.BlockSpec((tm, d), lambda i, j: (i, 0)),
                   pl.BlockSpec((tm, tn), lambda i, j: (i, j)),
                   pl.BlockSpec((tm, LANES), lambda i, j: (i, 0)),
                   pl.BlockSpec((SMALL_ROWS, tm), lambda i, j: (0, i))],
        out_shape=[jax.ShapeDtypeStruct((m, d), BF16),
                   jax.ShapeDtypeStruct((m, n_tiles * tn), BF16),
                   jax.ShapeDtypeStruct((m, LANES), F32),
                   jax.ShapeDtypeStruct((SMALL_ROWS, m), F32)],
        compiler_params=_params("arbitrary", "arbitrary"),
        name="inproj",
    )(x2, mod, mod, n1w, wts["wbig"], wts["bbig"], wts["wsm"], wts["bsm"])


def _proj(h, wts, j0, j1, kind, tm, nw=None, scale=1.0, emit_f32=False):
    m, d = h.shape
    tn = PROJ_TN
    width = (j1 - j0) * tn
    tile = pl.BlockSpec((tm, tn), lambda i, j: (i, j))
    in_specs = [pl.BlockSpec((tm, d), lambda i, j: (i, 0)),
                pl.BlockSpec((d, tn), lambda i, j: (0, j0 + j)),
                pl.BlockSpec((1, tn), lambda i, j: (0, j0 + j))]
    args = [h, wts["wbig"], wts["bbig"]]
    if kind == "headnorm":
        in_specs.append(pl.BlockSpec((1, F_DH), lambda i, j: (0, 0)))
        args.append(nw)
    two = emit_f32 or kind == "copy"
    out_shape = ([jax.ShapeDtypeStruct((m, width), F32)] if two else []) + [jax.ShapeDtypeStruct((m, width), BF16)]
    return pl.pallas_call(
        functools.partial(_proj_kernel, kind=kind, scale=scale, emit_f32=emit_f32),
        grid=(m // tm, j1 - j0),
        in_specs=in_specs,
        out_specs=[tile] * len(out_shape),
        out_shape=out_shape,
        compiler_params=_params("arbitrary", "arbitrary"),
        name="proj_" + kind,
    )(*args)


def _cumsum_kernel(sm_ref, fcol_ref, carry_ref):
    c = pl.program_id(1)
    tc = sm_ref.shape[0]

    @pl.when(c == 0)
    def _():
        carry_ref[...] = jnp.zeros_like(carry_ref)

    fc = _dot_f32_rhs(_tri01(tc, "lower"), sm_ref[...]) + carry_ref[0:1, :]
    fcol_ref[...] = fc
    carry_ref[...] = jnp.broadcast_to(fc[tc - 1:tc, :], carry_ref.shape)


def _cumsum(sm, batch, tc=512):
    m = sm.shape[0]
    nt = (m // batch) // tc
    return pl.pallas_call(
        _cumsum_kernel,
        grid=(batch, nt),
        in_specs=[pl.BlockSpec((tc, LANES), lambda b, c: (b * nt + c, 0))],
        out_specs=pl.BlockSpec((tc, LANES), lambda b, c: (b * nt + c, 0)),
        out_shape=jax.ShapeDtypeStruct((m, LANES), F32),
        scratch_shapes=[pltpu.VMEM((SUBLANES, LANES), F32)],
        compiler_params=_params("arbitrary", "arbitrary"),
        name="cumsum",
    )(sm)


def _mlstm_kernel(q_ref, k_ref, v_ref, sm_ref, smt_ref, og_ref, hw_ref,
                  ym_ref, cout_ref, nm_ref, c_ref, n_ref, m_ref):
    c = pl.program_id(1)
    L = q_ref.shape[0]

    @pl.when(c == 0)
    def _():
        c_ref[...] = jnp.zeros_like(c_ref)
        n_ref[...] = jnp.zeros_like(n_ref)
        m_ref[...] = jnp.zeros_like(m_ref)

    r = lax.broadcasted_iota(jnp.int32, (L, L), 0)
    s_idx = lax.broadcasted_iota(jnp.int32, (L, L), 1)
    causal = s_idx <= r
    sm = sm_ref[...]
    smt = smt_ref[...]
    bcol_all = _dot_f32_rhs(_tri01(L, "lower"), sm)
    brow_all = _dot_f32_lhs(smt, _tri01(L, "upper"))

    for h in range(M_HEADS):
        ig_col = sm[:, IG0 + h:IG0 + h + 1]
        b_col = bcol_all[:, LFM0 + h:LFM0 + h + 1]
        ig_row = smt[IG0 + h:IG0 + h + 1, :]
        b_row = brow_all[LFM0 + h:LFM0 + h + 1, :]
        m0 = m_ref[h:h + 1, 0:1]
        a_row = ig_row - b_row
        cm_col = jnp.max(jnp.where(causal, a_row, -jnp.inf), axis=1, keepdims=True)
        mx_col = jnp.maximum(m0, cm_col)
        m_col = b_col + mx_col
        dmat = jnp.exp(jnp.where(causal, a_row - mx_col, -jnp.inf))
        inter = jnp.exp(m0 - mx_col)
        q = q_ref[:, h * M_DK:(h + 1) * M_DK]
        k = k_ref[:, h * M_DK:(h + 1) * M_DK]
        v = v_ref[:, h * M_DV:(h + 1) * M_DV]
        c0 = c_ref[h]
        n0 = n_ref[h:h + 1, :]
        s = _dot_nt(q, k) * dmat
        num = _dot(s.astype(BF16), v) + inter * _dot_nt(q, c0.astype(BF16))
        den = jnp.sum(s, axis=1, keepdims=True) + inter * jnp.sum(q.astype(F32) * n0, axis=1, keepdims=True)
        hh = num / jnp.maximum(jnp.abs(den), jnp.exp(-m_col))
        hn = hh * lax.rsqrt(jnp.mean(hh * hh, axis=1, keepdims=True) + NORM_EPS) * hw_ref[:, h * M_DV:(h + 1) * M_DV]
        ym_ref[:, h * M_DV:(h + 1) * M_DV] = (og_ref[:, h * M_DV:(h + 1) * M_DV] * hn).astype(BF16)
        m_new = m_col[L - 1:L, :]
        b_last = b_col[L - 1:L, :]
        w_end = jnp.exp(ig_col + b_last - b_col - m_new)
        decay = jnp.exp(b_last + m0 - m_new)
        vw = (w_end * v.astype(F32)).astype(BF16)
        c_ref[h] = decay * c0 + _dot_tn(vw, k)
        n_ref[h:h + 1, :] = decay * n0 + jnp.sum(w_end * k.astype(F32), axis=0, keepdims=True)
        m_ref[h:h + 1, :] = jnp.broadcast_to(m_new, (1, LANES))

    @pl.when(c == pl.num_programs(1) - 1)
    def _():
        cout_ref[...] = c_ref[...]
        nm_ref[0:M_HEADS, :] = n_ref[0:M_HEADS, :]
        nm_ref[M_HEADS:2 * M_HEADS, :] = m_ref[0:M_HEADS, :]


def _mlstm_prompt(qkv, sm, smt, sig, hw, batch):
    m = qkv.shape[0]
    og_block = sig.shape[1] // M_V - 1
    nc = (m // batch) // CHUNK
    row = lambda b, c: (b * nc + c, 0)
    return pl.pallas_call(
        _mlstm_kernel,
        grid=(batch, nc),
        in_specs=[pl.BlockSpec((CHUNK, M_QK), row),
                  pl.BlockSpec((CHUNK, M_QK), lambda b, c: (b * nc + c, 1)),
                  pl.BlockSpec((CHUNK, M_V), lambda b, c: (b * nc + c, 2 * M_QK // M_V)),
                  pl.BlockSpec((CHUNK, LANES), row),
                  pl.BlockSpec((SMALL_ROWS, CHUNK), lambda b, c: (0, b * nc + c)),
                  pl.BlockSpec((CHUNK, M_V), lambda b, c: (b * nc + c, og_block)),
                  pl.BlockSpec((1, M_V), lambda b, c: (0, 0))],
        out_specs=[pl.BlockSpec((CHUNK, M_V), row),
                   pl.BlockSpec((None, M_HEADS, M_DV, M_DK), lambda b, c: (b, 0, 0, 0)),
                   pl.BlockSpec((None, 2 * M_HEADS, M_DK), lambda b, c: (b, 0, 0))],
        out_shape=[jax.ShapeDtypeStruct((m, M_V), BF16),
                   jax.ShapeDtypeStruct((batch, M_HEADS, M_DV, M_DK), F32),
                   jax.ShapeDtypeStruct((batch, 2 * M_HEADS, M_DK), F32)],
        scratch_shapes=[pltpu.VMEM((M_HEADS, M_DV, M_DK), F32),
                        pltpu.VMEM((SUBLANES, M_DK), F32),
                        pltpu.VMEM((SUBLANES, LANES), F32)],
        compiler_params=_params("arbitrary", "arbitrary"),
        name="mlstm",
    )(qkv, qkv, qkv, sm, smt, sig, hw)


FOX_HEADS_PER_STEP = 2
FOX_ROW_CHUNK = 32


def _forget_columns(f, first):
    hi, mid, lo = _split3(f)
    lane = lax.broadcasted_iota(jnp.int32, (f.shape[0], LANES), 1)
    ones = jnp.where(jnp.abs(lane - (4 - first)) <= 1, 1.0, 0.0)
    ext = jnp.where(lane == first, hi.astype(F32),
                    jnp.where(lane == first + 1, mid.astype(F32),
                              jnp.where(lane == first + 2, lo.astype(F32), ones)))
    return ext.astype(BF16)


def _fox_kernel(q_ref, k_ref, v_ref, fcol_ref, o_ref, qaug_ref, kaug_ref, s_ref, p_ref, m_ref, l_ref, acc_ref):
    hp = pl.program_id(1)
    qi = pl.program_id(2)
    tq = q_ref.shape[0]
    t = k_ref.shape[0]
    nh = FOX_HEADS_PER_STEP

    def forget_sum(fc, h2):
        lane = lax.broadcasted_iota(jnp.int32, fc.shape, 1)
        sel = jnp.where(lane == LFF0 + nh * hp + h2, fc, 0.0)
        return jnp.sum(sel, axis=1, keepdims=True) * LOG2E

    @pl.when(qi == 0)
    def _():
        fc = fcol_ref[...]
        for h2 in range(nh):
            kaug_ref[h2, :, 0:F_DH] = k_ref[:, h2 * F_DH:(h2 + 1) * F_DH]
            kaug_ref[h2, :, F_DH:2 * F_DH] = _forget_columns(-forget_sum(fc, h2), 3)

    fq = fcol_ref[pl.ds(pl.multiple_of(qi * tq, tq), tq), :]
    for h2 in range(nh):
        qaug_ref[h2, :, 0:F_DH] = q_ref[:, h2 * F_DH:(h2 + 1) * F_DH]
        qaug_ref[h2, :, F_DH:2 * F_DH] = _forget_columns(forget_sum(fq, h2), 0)
    m_ref[...] = jnp.full_like(m_ref, -jnp.inf)
    l_ref[...] = jnp.zeros_like(l_ref)
    acc_ref[...] = jnp.zeros_like(acc_ref)
    rc = FOX_ROW_CHUNK
    n_lane_tiles = tq // LANES

    def softmax_rows(h2, c, masked):
        rows = slice(c * rc, (c + 1) * rc)
        s = s_ref[h2, rows, :]
        if masked:
            r = lax.broadcasted_iota(jnp.int32, (rc, tq), 0) + c * rc
            col = lax.broadcasted_iota(jnp.int32, (rc, tq), 1)
            s = jnp.where(r >= col, s, -jnp.inf)
        tiles = [s[:, k * LANES:(k + 1) * LANES] for k in range(n_lane_tiles)]
        m_old = m_ref[h2, rows, :]
        mx = functools.reduce(jnp.maximum, tiles)
        m_new = jnp.maximum(m_old, jnp.max(mx, axis=1, keepdims=True))
        alpha = jnp.exp2(m_old - m_new)
        ps = [jnp.exp2(tile - m_new) for tile in tiles]
        row_sum = jnp.sum(functools.reduce(jnp.add, ps), axis=1, keepdims=True)
        l_ref[h2, rows, :] = alpha * l_ref[h2, rows, :] + row_sum
        for k in range(n_lane_tiles):
            p_ref[h2, rows, k * LANES:(k + 1) * LANES] = ps[k].astype(BF16)
        acc_ref[h2, rows, :] = alpha * acc_ref[h2, rows, :]
        m_ref[h2, rows, :] = m_new

    def block(kj, masked):
        off = pl.multiple_of(kj * tq, tq)
        for h2 in range(nh):
            s_ref[h2] = _dot_nt(qaug_ref[h2], kaug_ref[h2, pl.ds(off, tq), :])
        for h2 in range(nh):
            for c in range(tq // rc):
                softmax_rows(h2, c, masked)
            acc_ref[h2] += _dot(p_ref[h2], v_ref[pl.ds(off, tq), h2 * F_DH:(h2 + 1) * F_DH])

    def body(kj, carry):
        block(kj, False)
        return carry

    lax.fori_loop(0, qi, body, 0)
    block(qi, True)
    for h2 in range(nh):
        o_ref[:, h2 * F_DH:(h2 + 1) * F_DH] = (acc_ref[h2] / l_ref[h2]).astype(BF16)


def _fox_prompt(qf, kfb, vfb, fcol, batch, tq=512):
    m = qf.shape[0]
    t = m // batch
    nq = t // tq
    nh = FOX_HEADS_PER_STEP
    wide = nh * F_DH
    return pl.pallas_call(
        _fox_kernel,
        grid=(batch, F_HEADS // nh, nq),
        in_specs=[pl.BlockSpec((tq, wide), lambda b, h, i: (b * nq + i, h)),
                  pl.BlockSpec((t, wide), lambda b, h, i: (b, h)),
                  pl.BlockSpec((t, wide), lambda b, h, i: (b, h)),
                  pl.BlockSpec((t, LANES), lambda b, h, i: (b, 0))],
        out_specs=pl.BlockSpec((tq, wide), lambda b, h, i: (b * nq + i, h)),
        out_shape=jax.ShapeDtypeStruct((m, F_W), BF16),
        scratch_shapes=[pltpu.VMEM((nh, tq, 2 * F_DH), BF16),
                        pltpu.VMEM((nh, t, 2 * F_DH), BF16),
                        pltpu.VMEM((nh, tq, tq), F32),
                        pltpu.VMEM((nh, tq, tq), BF16),
                        pltpu.VMEM((nh, tq, LANES), F32),
                        pltpu.VMEM((nh, tq, LANES), F32),
                        pltpu.VMEM((nh, tq, F_DH), F32)],
        compiler_params=_params("arbitrary", "arbitrary", "arbitrary"),
        name="fox",
    )(qf, kfb, vfb, fcol)


MSTEP_ROWS = 8


def _mstep_kernel(q_ref, k_ref, v_ref, sm_ref, og_ref, hw_ref, c0_ref, n0_ref, m0_ref,
                  ym_ref, cout_ref, nout_ref, mout_ref, cq_ref):
    rows = q_ref.shape[0]
    sm = sm_ref[...]
    ig = sm[:, IG0:IG0 + M_HEADS]
    lf = sm[:, LFM0:LFM0 + M_HEADS]
    m0 = m0_ref[...]
    m_new = lf + jnp.maximum(m0, ig - lf)
    w_end = jnp.exp(ig - m_new)
    decay = jnp.exp(lf + m0 - m_new)
    mout_ref[...] = m_new
    row_id = lax.broadcasted_iota(jnp.int32, (SUBLANES, M_DV), 0)

    for h in range(M_HEADS):
        qh = q_ref[:, h * M_DK:(h + 1) * M_DK]
        kh = k_ref[:, h * M_DK:(h + 1) * M_DK]
        vh = v_ref[:, h * M_DV:(h + 1) * M_DV]
        qf = qh.astype(F32)
        kf = kh.astype(F32)
        vf = vh.astype(F32)
        wh = w_end[:, h:h + 1]
        dh = decay[:, h:h + 1]
        for r in range(rows):
            c0 = c0_ref[r, h]
            q8 = jnp.broadcast_to(qh[r:r + 1, :], (SUBLANES, M_DK))
            cq_ref[r:r + 1, :] = _dot_nt(q8, c0.astype(BF16))[0:1, :]
            vw8 = jnp.where(row_id == 0, jnp.broadcast_to(wh[r:r + 1, :] * vf[r:r + 1, :], (SUBLANES, M_DV)), 0.0)
            k8 = jnp.broadcast_to(kh[r:r + 1, :], (SUBLANES, M_DK))
            cout_ref[r, h] = dh[r:r + 1, :] * c0 + _dot_tn(vw8.astype(BF16), k8)
        n0 = n0_ref[:, h, :]
        s = jnp.sum(qf * kf, axis=1, keepdims=True) * wh
        num = s * vf + dh * cq_ref[...]
        den = s + dh * jnp.sum(n0 * qf, axis=1, keepdims=True)
        hh = num / jnp.maximum(jnp.abs(den), jnp.exp(-m_new[:, h:h + 1]))
        hn = hh * lax.rsqrt(jnp.mean(hh * hh, axis=1, keepdims=True) + NORM_EPS) * hw_ref[:, h * M_DV:(h + 1) * M_DV]
        ym_ref[:, h * M_DV:(h + 1) * M_DV] = (og_ref[:, h * M_DV:(h + 1) * M_DV] * hn).astype(BF16)
        nout_ref[:, h, :] = dh * n0 + wh * kf


def _mlstm_step(qkv, sm, sig, hw, c0, n0, m0):
    db = qkv.shape[0]
    og_block = sig.shape[1] // M_V - 1
    rows = MSTEP_ROWS
    row = lambda i: (i, 0)
    return pl.pallas_call(
        _mstep_kernel,
        grid=(db // rows,),
        in_specs=[pl.BlockSpec((rows, M_QK), row),
                  pl.BlockSpec((rows, M_QK), lambda i: (i, 1)),
                  pl.BlockSpec((rows, M_V), lambda i: (i, 2 * M_QK // M_V)),
                  pl.BlockSpec((rows, LANES), row),
                  pl.BlockSpec((rows, M_V), lambda i: (i, og_block)),
                  pl.BlockSpec((1, M_V), lambda i: (0, 0)),
                  pl.BlockSpec((rows, M_HEADS, M_DV, M_DK), lambda i: (i, 0, 0, 0)),
                  pl.BlockSpec((rows, M_HEADS, M_DK), lambda i: (i, 0, 0)),
                  pl.BlockSpec((rows, M_HEADS), row)],
        out_specs=[pl.BlockSpec((rows, M_V), row),
                   pl.BlockSpec((rows, M_HEADS, M_DV, M_DK), lambda i: (i, 0, 0, 0)),
                   pl.BlockSpec((rows, M_HEADS, M_DK), lambda i: (i, 0, 0)),
                   pl.BlockSpec((rows, M_HEADS), row)],
        out_shape=[jax.ShapeDtypeStruct((db, M_V), BF16),
                   jax.ShapeDtypeStruct(c0.shape, F32),
                   jax.ShapeDtypeStruct(n0.shape, F32),
                   jax.ShapeDtypeStruct(m0.shape, F32)],
        scratch_shapes=[pltpu.VMEM((rows, M_DV), F32)],
        compiler_params=_params("arbitrary"),
        name="mstep",
    )(qkv, qkv, qkv, sm, sig, hw, c0, n0, m0)


def _decode_kernel(*refs, n_pages):
    k_refs = refs[1:1 + n_pages]
    v_refs = refs[1 + n_pages:1 + 2 * n_pages]
    lf_refs = refs[1 + 2 * n_pages:1 + 3 * n_pages]
    q_ref, kn_ref, vn_ref, sm_ref, o_ref, s_ref, xn_ref = refs[1 + 3 * n_pages:]
    page = xn_ref.shape[1]
    pw = page * F_HEADS

    for p in range(n_pages):
        xn_ref[p * F_HEADS:(p + 1) * F_HEADS, :] = lf_refs[p][...]
    xn = xn_ref[...] * LOG2E
    t_id = lax.broadcasted_iota(jnp.int32, (page, pw), 0)
    l_id = lax.broadcasted_iota(jnp.int32, (page, pw), 1)
    widen = jnp.where((l_id // F_HEADS) == t_id, 1.0, 0.0).astype(BF16)
    wide = _dot_f32_lhs(xn, widen)
    r_id = lax.broadcasted_iota(jnp.int32, wide.shape, 0)
    c_id = lax.broadcasted_iota(jnp.int32, wide.shape, 1)
    own = jnp.where((r_id & (F_HEADS - 1)) == (c_id & (F_HEADS - 1)), wide, 0.0)
    g_p = lax.broadcasted_iota(jnp.int32, (n_pages, n_pages * F_HEADS), 0)
    g_r = lax.broadcasted_iota(jnp.int32, (n_pages, n_pages * F_HEADS), 1)
    gather = jnp.where((g_r // F_HEADS) == g_p, 1.0, 0.0).astype(BF16)
    x = _dot_f32_rhs(gather, own)
    lane = lax.broadcasted_iota(jnp.int32, x.shape, 1)
    y, z = x, x
    step = F_HEADS
    while step < pw:
        y = y + jnp.where(lane < pw - step, pltpu.roll(y, pw - step, axis=1), 0.0)
        z = z + pltpu.roll(z, step, axis=1)
        step *= 2
    after = y - x

    r = lax.broadcasted_iota(jnp.int32, (LANES, pw), 0)
    c = lax.broadcasted_iota(jnp.int32, (LANES, pw), 1)
    spread = jnp.where(r == LFF0 + (c & (F_HEADS - 1)), 1.0, 0.0).astype(BF16)
    run = _dot_f32_lhs(jnp.broadcast_to(sm_ref[...] * LOG2E, (SUBLANES, LANES)), spread)[0:1, :]

    q = q_ref[...]
    sub = lax.broadcasted_iota(jnp.int32, (F_HEADS, pw), 0)
    lane_h = lax.broadcasted_iota(jnp.int32, (F_HEADS, pw), 1) & (F_HEADS - 1)
    valid = lane_h == sub
    s_self = jnp.sum(q.astype(F32) * kn_ref[...], axis=1, keepdims=True)
    m_run = s_self
    for p in reversed(range(n_pages)):
        kb = k_refs[p][...].astype(BF16)
        st = _dot_nt(q, kb) + (run + after[p:p + 1, :])
        st = jnp.where(valid, st, -jnp.inf)
        s_ref[p] = st
        m_run = jnp.maximum(m_run, jnp.max(st, axis=1, keepdims=True))
        run = run + z[p:p + 1, :]
    w_self = jnp.exp2(s_self - m_run)
    l = w_self
    acc = w_self * vn_ref[...]
    for p in range(n_pages):
        pe = jnp.exp2(s_ref[p] - m_run)
        l = l + jnp.sum(pe, axis=1, keepdims=True)
        acc = acc + _dot(pe.astype(BF16), v_refs[p][...].astype(BF16))
    o_ref[...] = (acc / l).astype(BF16)


def _fox_decode(page_table, ck, cv, clf, layer, q3, k3, v3, sm3):
    db, n_pages = page_table.shape
    pw = ck.shape[2]
    page = clf.shape[3]
    seq = lambda b, pt: (b, 0, 0)

    def pg(p):
        return lambda b, pt: (layer, pt[b, p], 0, 0)

    in_specs = ([pl.BlockSpec((None, None, pw, F_DH), pg(p)) for p in range(n_pages)]
                + [pl.BlockSpec((None, None, pw, F_DH), pg(p)) for p in range(n_pages)]
                + [pl.BlockSpec((None, None, F_HEADS, page), pg(p)) for p in range(n_pages)]
                + [pl.BlockSpec((None, F_HEADS, F_DH), seq),
                   pl.BlockSpec((None, F_HEADS, F_DH), seq),
                   pl.BlockSpec((None, F_HEADS, F_DH), seq),
                   pl.BlockSpec((None, 1, LANES), seq)])
    grid_spec = pltpu.PrefetchScalarGridSpec(
        num_scalar_prefetch=1,
        grid=(db,),
        in_specs=in_specs,
        out_specs=pl.BlockSpec((None, F_HEADS, F_DH), seq),
        scratch_shapes=[pltpu.VMEM((n_pages, F_HEADS, pw), F32),
                        pltpu.VMEM((n_pages * F_HEADS, page), F32)],
    )
    return pl.pallas_call(
        functools.partial(_decode_kernel, n_pages=n_pages),
        grid_spec=grid_spec,
        out_shape=jax.ShapeDtypeStruct((db, F_HEADS, F_DH), BF16),
        compiler_params=_params("arbitrary"),
        name="decode",
    )(page_table, *([ck] * n_pages), *([cv] * n_pages), *([clf] * n_pages), q3, k3, v3, sm3)


def _merge_kernel(ym_ref, yf_ref, sga_ref, sgb_ref, x_ref, g1_ref, sc_ref, sh_ref, n2w_ref,
                  wa_ref, wb_ref, wo_ref, x1_ref, h2_ref):
    merged = sga_ref[...] * _dot(ym_ref[...], wa_ref[...]) + sgb_ref[...] * _dot(yf_ref[...], wb_ref[...])
    x1 = x_ref[...] + g1_ref[...] * _dot(merged.astype(BF16), wo_ref[...])
    x1_ref[...] = x1
    rstd = lax.rsqrt(jnp.mean(x1 * x1, axis=-1, keepdims=True) + NORM_EPS)
    h2_ref[...] = ((x1 * rstd * n2w_ref[...]) * (1.0 + sc_ref[...]) + sh_ref[...]).astype(BF16)


def _merge(ym, yf, sig, x2, mod, n2w, wts, tm):
    m, d = x2.shape
    tiles_per_group = (m // tm) // mod.shape[0]
    group_of = lambda i: i // tiles_per_group
    row = lambda i: (i, 0)
    const = lambda i: (0, 0)
    resident = dict(pipeline_mode=pl.Buffered(1))
    return pl.pallas_call(
        _merge_kernel,
        grid=(m // tm,),
        in_specs=[pl.BlockSpec((tm, M_V), row),
                  pl.BlockSpec((tm, F_W), row),
                  pl.BlockSpec((tm, d), row),
                  pl.BlockSpec((tm, d), lambda i: (i, 1)),
                  pl.BlockSpec((tm, d), row),
                  _mod_spec(mod, d, MOD_G1, group_of),
                  _mod_spec(mod, d, MOD_SC2, group_of),
                  _mod_spec(mod, d, MOD_SH2, group_of),
                  pl.BlockSpec((1, d), const),
                  pl.BlockSpec((M_V, d), const, **resident),
                  pl.BlockSpec((F_W, d), const, **resident),
                  pl.BlockSpec((d, d), const, **resident)],
        out_specs=[pl.BlockSpec((tm, d), row), pl.BlockSpec((tm, d), row)],
        out_shape=[jax.ShapeDtypeStruct((m, d), F32), jax.ShapeDtypeStruct((m, d), BF16)],
        compiler_params=_params("arbitrary"),
        name="merge",
    )(ym, yf, sig, sig, x2, mod, mod, mod, n2w, wts["wa"], wts["wb"], wts["wo"])


FFN_TF = 512


def _ffn_kernel(h2_ref, x1_ref, g2_ref, p_ref, wa_ref, wg_ref, cw_ref, cb_ref, w2_ref,
                y_ref, a_out_ref, carry_ref, *, seq):
    f = pl.program_id(2)
    tm = h2_ref.shape[0]

    @pl.when(f == 0)
    def _():
        y_ref[...] = jnp.zeros_like(y_ref)

    if seq:
        @pl.when(pl.program_id(1) == 0)
        def _():
            carry_ref[f] = p_ref[...]

    h2 = h2_ref[...]
    a = _dot(h2, wa_ref[...])
    g = _dot(h2, wg_ref[...])
    if seq:
        prev = carry_ref[f]
        p2, p1 = prev[SUBLANES - 2:SUBLANES - 1, :], prev[SUBLANES - 1:SUBLANES, :]
        r = lax.broadcasted_iota(jnp.int32, a.shape, 0)
        a1 = jnp.where(r == 0, p1, pltpu.roll(a, 1, axis=0))
        a2 = jnp.where(r == 0, p2, jnp.where(r == 1, p1, pltpu.roll(a, 2, axis=0)))
        carry_ref[f] = a[tm - SUBLANES:tm, :]
        a_out_ref[...] = a[tm - SUBLANES:tm, :]
    else:
        a2, a1 = p_ref[0], p_ref[1]
        a_out_ref[...] = a
    ac = cw_ref[0:1, :] * a2 + cw_ref[1:2, :] * a1 + cw_ref[2:3, :] * a + cb_ref[...]
    u = (jax.nn.gelu(ac, approximate=True) * g).astype(BF16)
    y_ref[...] += _dot(u, w2_ref[...])

    @pl.when(f == pl.num_programs(2) - 1)
    def _():
        y_ref[...] = x1_ref[...] + g2_ref[...] * y_ref[...]


def _ffn(h2, x1, mod, prev, wts, tm, seq):
    m, d = x1.shape
    groups = mod.shape[0]
    tiles = (m // tm) // groups
    dff = wts["w2"].shape[0]
    tf = FFN_TF
    nf = dff // tf
    row = lambda b, i, f: (b * tiles + i, 0)
    if seq:
        prev_spec = pl.BlockSpec((None, SUBLANES, tf), lambda b, i, f: (b, 0, f))
        a_spec = pl.BlockSpec((None, None, SUBLANES, tf), lambda b, i, f: (b, i, 0, f))
        a_shape = jax.ShapeDtypeStruct((groups, tiles, SUBLANES, dff), F32)
    else:
        prev_spec = pl.BlockSpec((CONV_W - 1, tm, tf), lambda b, i, f: (0, i, f))
        a_spec = pl.BlockSpec((tm, tf), lambda b, i, f: (i, f))
        a_shape = jax.ShapeDtypeStruct((m, dff), F32)
    return pl.pallas_call(
        functools.partial(_ffn_kernel, seq=seq),
        grid=(groups, tiles, nf),
        in_specs=[pl.BlockSpec((tm, d), row),
                  pl.BlockSpec((tm, d), row, pipeline_mode=pl.Buffered(1)),
                  _mod_spec(mod, d, MOD_G2, lambda b, i, f: b),
                  prev_spec,
                  pl.BlockSpec((d, tf), lambda b, i, f: (0, f)),
                  pl.BlockSpec((d, tf), lambda b, i, f: (0, f + nf)),
                  pl.BlockSpec((CONV_W, tf), lambda b, i, f: (0, f)),
                  pl.BlockSpec((1, tf), lambda b, i, f: (0, f)),
                  pl.BlockSpec((tf, d), lambda b, i, f: (f, 0))],
        out_specs=[pl.BlockSpec((tm, d), row), a_spec],
        out_shape=[jax.ShapeDtypeStruct((m, d), F32), a_shape],
        scratch_shapes=[pltpu.VMEM((nf, SUBLANES, tf), F32)],
        compiler_params=_params("arbitrary", "arbitrary", "arbitrary"),
        name="ffn_seq" if seq else "ffn_row",
    )(h2, x1, mod, prev, wts["w1"], wts["w1"], wts["cw"], wts["cb"], wts["w2"])


def _prep_weights(w_in, b_in, f_qnorm_w, f_knorm_w, w_proj_a, w_proj_b, w_out, w_ffn_in, conv_w, conv_b, w_ffn_out):
    sizes = (M_QK, M_QK, M_V, M_V, M_HEADS, M_HEADS, F_W, F_W, F_W, F_HEADS)
    o = [0]
    for s in sizes:
        o.append(o[-1] + s)
    big = lambda a: jnp.concatenate([a[:o[3]], a[o[10]:], a[o[3]:o[4]], a[o[6]:o[9]]], axis=0)
    small = lambda a: jnp.concatenate([a[o[4]:o[6]], a[o[9]:o[10]]], axis=0)
    w_in_t = w_in.T
    wsm = small(w_in_t).T
    bsm = small(b_in)
    return {
        "wbig": big(w_in_t).astype(BF16).T,
        "bbig": big(b_in).reshape(1, -1),
        "wsm": jnp.pad(wsm, ((0, 0), (0, LANES - SMALL_ROWS))).astype(BF16),
        "bsm": jnp.pad(bsm, (0, LANES - SMALL_ROWS)).reshape(1, LANES),
        "qnw": f_qnorm_w.reshape(1, F_DH),
        "knw": f_knorm_w.reshape(1, F_DH),
        "wa": w_proj_a.astype(BF16),
        "wb": w_proj_b.astype(BF16),
        "wo": w_out.astype(BF16),
        "w1": w_ffn_in.astype(BF16),
        "cw": conv_w,
        "cb": conv_b.reshape(1, -1),
        "w2": w_ffn_out.astype(BF16),
    }


def kernel(x_prompt, x_sample, c_prompt, c_sample, cache_k, cache_v, cache_logf, page_table, state_C, state_n, state_m, state_conv, w_ada, b_ada, norm1_w, w_in, b_in, m_hnorm_w, f_qnorm_w, f_knorm_w, w_proj_a, w_proj_b, w_out, norm2_w, w_ffn_in, conv_w, conv_b, w_ffn_out):
    depth = w_ada.shape[0]
    assert depth == 1, "single-layer step"
    batch, seq_len, d = x_prompt.shape
    db = x_sample.shape[0]
    assert x_sample.shape[1] == 1
    dff = w_ffn_out.shape[1]
    layer = 0

    wts = _prep_weights(w_in[layer], b_in[layer], f_qnorm_w[layer], f_knorm_w[layer], w_proj_a[layer],
                        w_proj_b[layer], w_out[layer], w_ffn_in[layer], conv_w[layer], conv_b[layer],
                        w_ffn_out[layer])
    n1w = norm1_w[layer].reshape(1, d)
    n2w = norm2_w[layer].reshape(1, d)
    hw = m_hnorm_w[layer].reshape(1, M_V)

    mod_p, mod_s = _ada(c_prompt, c_sample, w_ada[layer], b_ada[layer])
    mod_p = mod_p.reshape(batch, 1, 6 * d)
    mod_s = mod_s.reshape(1, db, 6 * d)

    def in_projection(x2, mod, tm):
        h, qkv, sm, smt = _inproj_first(x2, mod, n1w, wts, tm)
        (sig,) = _proj(h, wts, _J_GATE, _J_FQ, "sigmoid", tm)
        (qf,) = _proj(h, wts, _J_FQ, _J_FK, "headnorm", tm, nw=wts["qnw"], scale=FOX_Q_SCALE)
        kf, kfb = _proj(h, wts, _J_FK, _J_FV, "headnorm", tm, nw=wts["knw"], emit_f32=True)
        vf, vfb = _proj(h, wts, _J_FV, _J_END, "copy", tm)
        return qkv, sig, qf, kf, kfb, vf, vfb, sm, smt

    tm_p = min(1024, seq_len)
    xp = x_prompt.reshape(batch * seq_len, d)
    qkv, sig, qf, kf, kfb, vf, vfb, sm, smt = in_projection(xp, mod_p, tm_p)
    fcol = _cumsum(sm, batch)
    ym, c_p, nm_p = _mlstm_prompt(qkv, sm, smt, sig, hw, batch)
    yf = _fox_prompt(qf, kfb, vfb, fcol, batch)
    x1, h2 = _merge(ym, yf, sig, xp, mod_p, n2w, wts, min(512, seq_len))
    conv0 = jnp.zeros((batch, SUBLANES, dff), F32)
    yp, a_last = _ffn(h2, x1, mod_p, conv0, wts, tm_p, seq=True)

    y_prompt = yp.reshape(batch, seq_len, d)
    k_prompt = kf.reshape(depth, batch, seq_len, F_HEADS, F_DH)
    v_prompt = vf.reshape(depth, batch, seq_len, F_HEADS, F_DH)
    logf_prompt = sm[:, LFF0:LFF0 + F_HEADS].reshape(depth, batch, seq_len, F_HEADS)
    c_prompt_out = c_p[None]
    n_prompt = nm_p[:, :M_HEADS, :][None]
    m_prompt = nm_p[:, M_HEADS:, 0][None]
    conv_prompt = a_last[:, -1, SUBLANES - (CONV_W - 1):, :][None]

    xs = x_sample.reshape(db, d)
    qkv_s, sig_s, qf_s, kf_s, _, vf_s, _, sm_s, _ = in_projection(xs, mod_s, db)
    ym_s, c_s, n_s, m_s = _mlstm_step(qkv_s, sm_s, sig_s, hw, state_C[layer], state_n[layer], state_m[layer])
    n_pool, page = cache_k.shape[1], cache_k.shape[2]
    yf_s = _fox_decode(page_table,
                       cache_k.reshape(depth, n_pool, page * F_HEADS, F_DH),
                       cache_v.reshape(depth, n_pool, page * F_HEADS, F_DH),
                       cache_logf.transpose(0, 1, 3, 2),
                       layer,
                       qf_s.reshape(db, F_HEADS, F_DH), kf_s.reshape(db, F_HEADS, F_DH),
                       vf_s.reshape(db, F_HEADS, F_DH), sm_s.reshape(db, 1, LANES)).reshape(db, F_W)
    x1_s, h2_s = _merge(ym_s, yf_s, sig_s, xs, mod_s, n2w, wts, db)
    prev_s = state_conv[layer].transpose(1, 0, 2)
    ys, a_s = _ffn(h2_s, x1_s, mod_s, prev_s, wts, db, seq=False)

    y_sample = ys.reshape(db, 1, d)
    k_sample = kf_s.reshape(depth, db, 1, F_HEADS, F_DH)
    v_sample = vf_s.reshape(depth, db, 1, F_HEADS, F_DH)
    logf_sample = sm_s[:, LFF0:LFF0 + F_HEADS].reshape(depth, db, 1, F_HEADS)
    conv_sample = jnp.stack([state_conv[layer][:, CONV_W - 2, :], a_s], axis=1)[None]

    return (y_prompt, y_sample, k_prompt, v_prompt, logf_prompt, c_prompt_out, n_prompt, m_prompt, conv_prompt,
            k_sample, v_sample, logf_sample, c_s[None], n_s[None], m_s[None], conv_sample)
```

```python
import functools

import jax
import jax.numpy as jnp
from jax import lax
from jax.experimental import pallas as pl
from jax.experimental.pallas import tpu as pltpu

F32 = jnp.float32
BF16 = jnp.bfloat16

NORM_EPS = 1e-6
M_HEADS, M_DK, M_DV = 4, 128, 256
F_HEADS, F_DH = 8, 128
M_QK = M_HEADS * M_DK
M_V = M_HEADS * M_DV
F_W = F_HEADS * F_DH
CONV_W = 3
CHUNK = 128
LANES = 128
SUBLANES = 8
SMALL_ROWS = 16
VMEM_LIMIT = 56 * 1024 * 1024

IG0, LFM0, LFF0 = 0, M_HEADS, 2 * M_HEADS

LOG2E = 1.4426950408889634
FOX_Q_SCALE = F_DH ** -0.5 * LOG2E


def _dot(a, b):
    return jnp.dot(a, b, preferred_element_type=F32)


def _dot_nt(a, b):
    return lax.dot_general(a, b, (((1,), (1,)), ((), ())), preferred_element_type=F32)


def _dot_tn(a, b):
    return lax.dot_general(a, b, (((0,), (0,)), ((), ())), preferred_element_type=F32)


def _split3(x):
    hi = x.astype(BF16)
    r1 = x - hi.astype(F32)
    mid = r1.astype(BF16)
    lo = (r1 - mid.astype(F32)).astype(BF16)
    return hi, mid, lo


def _dot_f32_lhs(x, mat01):
    hi, mid, lo = _split3(x)
    return _dot(hi, mat01) + _dot(mid, mat01) + _dot(lo, mat01)


def _dot_f32_rhs(mat01, x):
    hi, mid, lo = _split3(x)
    return _dot(mat01, hi) + _dot(mat01, mid) + _dot(mat01, lo)


def _log_sigmoid(x):
    return jnp.minimum(x, 0.0) - jnp.log1p(jnp.exp(-jnp.abs(x)))


def _tri01(n, kind):
    r = lax.broadcasted_iota(jnp.int32, (n, n), 0)
    c = lax.broadcasted_iota(jnp.int32, (n, n), 1)
    if kind == "lower":
        m = c <= r
    elif kind == "upper":
        m = r <= c
    else:
        m = r > c
    return jnp.where(m, 1.0, 0.0).astype(BF16)


def _params(*sem):
    return pltpu.CompilerParams(dimension_semantics=sem, vmem_limit_bytes=VMEM_LIMIT)


def _ada_kernel(cp_ref, cs_ref, w_ref, b_ref, op_ref, os_ref):
    w = w_ref[...].astype(BF16)
    for c_ref, o_ref in ((cp_ref, op_ref), (cs_ref, os_ref)):
        c = c_ref[...]
        s = (c * jax.nn.sigmoid(c)).astype(BF16)
        o_ref[...] = _dot(s, w) + b_ref[...]


def _ada(c_p, c_s, w_ada, b_ada, tn=1024):
    d = c_p.shape[1]
    n = w_ada.shape[1]
    return pl.pallas_call(
        _ada_kernel,
        grid=(n // tn,),
        in_specs=[pl.BlockSpec(c_p.shape, lambda j: (0, 0)),
                  pl.BlockSpec(c_s.shape, lambda j: (0, 0)),
                  pl.BlockSpec((d, tn), lambda j: (0, j)),
                  pl.BlockSpec((1, tn), lambda j: (0, j))],
        out_specs=[pl.BlockSpec((c_p.shape[0], tn), lambda j: (0, j)),
                   pl.BlockSpec((c_s.shape[0], tn), lambda j: (0, j))],
        out_shape=[jax.ShapeDtypeStruct((c_p.shape[0], n), F32), jax.ShapeDtypeStruct((c_s.shape[0], n), F32)],
        compiler_params=_params("arbitrary"),
        name="ada",
    )(c_p, c_s, w_ada, b_ada.reshape(1, n))


MOD_SH1, MOD_SC1, MOD_G1, MOD_SH2, MOD_SC2, MOD_G2 = range(6)


PROJ_TN = 512
_J_MIX, _J_GATE, _J_FQ, _J_FK, _J_FV, _J_END = 0, 4, 14, 16, 18, 20
_J_KM = 1
_SPLIT_SIZES = (M_QK, M_QK, M_V, M_V, M_HEADS, M_HEADS, F_W, F_W, F_W, F_HEADS)
_SPLIT_OFFS = tuple(sum(_SPLIT_SIZES[:i]) for i in range(len(_SPLIT_SIZES) + 1))
_ROW_MO, _ROW_SMALL_M, _ROW_FQ, _ROW_SMALL_F, _ROW_GATES = (_SPLIT_OFFS[3], _SPLIT_OFFS[4], _SPLIT_OFFS[6],
                                                            _SPLIT_OFFS[9], _SPLIT_OFFS[10])
_TILE_RUNS = ((_J_MIX, 0), (_J_GATE, _ROW_GATES), (_J_GATE + 8, _ROW_MO), (_J_FQ, _ROW_FQ))


def _tile_row(jj):
    row = _TILE_RUNS[0][1] + PROJ_TN * (jj - _TILE_RUNS[0][0])
    for t0, r0 in _TILE_RUNS[1:]:
        row = jnp.where(jj >= t0, r0 + PROJ_TN * (jj - t0), row)
    return row


def _inproj_first_kernel(x_ref, sc_ref, sh_ref, n1w_ref, w_ref, b_ref, wsm_m_ref, wsm_f_ref, bsm_ref,
                         h_ref, qkv_ref, sm_ref, smt_ref):
    j = pl.program_id(1)

    @pl.when(j == 0)
    def _():
        x = x_ref[...]
        rstd = lax.rsqrt(jnp.mean(x * x, axis=-1, keepdims=True) + NORM_EPS)
        h = (x * rstd * n1w_ref[...]) * (1.0 + sc_ref[...]) + sh_ref[...]
        hb = h.astype(BF16)
        h_ref[...] = hb
        wsm = jnp.concatenate([wsm_m_ref[...], wsm_f_ref[...]], axis=0).astype(BF16)
        smt = _dot_nt(wsm, hb) + bsm_ref[...]
        row = lax.broadcasted_iota(jnp.int32, smt.shape, 0)
        smt = jnp.where(row < LFM0, smt, _log_sigmoid(smt))
        smt_ref[...] = smt
        pad = jnp.zeros((LANES - SMALL_ROWS, smt.shape[1]), F32)
        sm_ref[...] = jnp.concatenate([smt, pad], axis=0).T

    acc = _dot_nt(h_ref[...], w_ref[...].astype(BF16)) + b_ref[...]
    scale = jnp.where(j == _J_KM, M_DK ** -0.5, 1.0)
    qkv_ref[...] = (acc * scale).astype(BF16)


def _proj_kernel(h_ref, w_ref, b_ref, *rest, kind, scale, emit_f32):
    acc = _dot_nt(h_ref[...], w_ref[...].astype(BF16)) + b_ref[...]
    if kind == "sigmoid":
        (ob_ref,) = rest
        ob_ref[...] = jax.nn.sigmoid(acc).astype(BF16)
    elif kind == "headnorm":
        nw_ref, *outs = rest
        for h in range(acc.shape[1] // F_DH):
            sl = acc[:, h * F_DH:(h + 1) * F_DH]
            ms = jnp.mean(sl * sl, axis=-1, keepdims=True)
            y = sl * lax.rsqrt(ms + NORM_EPS) * nw_ref[...]
            if emit_f32:
                outs[0][:, h * F_DH:(h + 1) * F_DH] = y
            outs[-1][:, h * F_DH:(h + 1) * F_DH] = (y * scale).astype(BF16)
    else:
        of_ref, ob_ref = rest
        of_ref[...] = acc
        ob_ref[...] = acc.astype(BF16)


def _feature_rows_spec(rows, d, first_row):
    def index_map(*g):
        return pl.multiple_of(jnp.asarray(first_row(*g), jnp.int32), SUBLANES), 0

    return pl.BlockSpec((pl.Element(rows), pl.Element(d)), index_map)


def _mod_spec(mod, d, chunk, group_of):
    return pl.BlockSpec((None, mod.shape[1], d), lambda *g: (group_of(*g), 0, chunk))


def _inproj_first(x2, mod, n1w, wts, tm):
    m, d = x2.shape
    tiles_per_group = (m // tm) // mod.shape[0]
    group_of = lambda i, j: i // tiles_per_group
    tn = PROJ_TN
    n_tiles = _J_GATE - _J_MIX
    return pl.pallas_call(
        _inproj_first_kernel,
        grid=(m // tm, n_tiles),
        in_specs=[pl.BlockSpec((tm, d), lambda i, j: (i, 0), pipeline_mode=pl.Buffered(1)),
                  _mod_spec(mod, d, MOD_SC1, group_of),
                  _mod_spec(mod, d, MOD_SH1, group_of),
                  pl.BlockSpec((1, d), lambda i, j: (0, 0)),
                  _feature_rows_spec(tn, d, lambda i, j: _tile_row(_J_MIX + j)),
                  pl.BlockSpec((1, tn), lambda i, j: (0, _J_MIX + j)),
                  _feature_rows_spec(SUBLANES, d, lambda i, j: _ROW_SMALL_M),
                  _feature_rows_spec(SUBLANES, d, lambda i, j: _ROW_SMALL_F),
                  pl.BlockSpec((SMALL_ROWS, 1), lambda i, j: (0, 0))],
        out_specs=[pl.BlockSpec((tm, d), lambda i, j: (i, 0)),
                   pl.BlockSpec((tm, tn), lambda i, j: (i, j)),
                   pl.BlockSpec((tm, LANES), lambda i, j: (i, 0)),
                   pl.BlockSpec((SMALL_ROWS, tm), lambda i, j: (0, i))],
        out_shape=[jax.ShapeDtypeStruct((m, d), BF16),
                   jax.ShapeDtypeStruct((m, n_tiles * tn), BF16),
                   jax.ShapeDtypeStruct((m, LANES), F32),
                   jax.ShapeDtypeStruct((SMALL_ROWS, m), F32)],
        compiler_params=_params("arbitrary", "arbitrary"),
        name="inproj",
    )(x2, mod, mod, n1w, wts["w_in_t"], wts["bbig"], wts["w_in_t"], wts["w_in_t"], wts["bsm"])


def _proj(h, wts, j0, j1, kind, tm, nw=None, scale=1.0, emit_f32=False):
    m, d = h.shape
    tn = PROJ_TN
    width = (j1 - j0) * tn
    tile = pl.BlockSpec((tm, tn), lambda i, j: (i, j))
    in_specs = [pl.BlockSpec((tm, d), lambda i, j: (i, 0)),
                _feature_rows_spec(tn, d, lambda i, j: _tile_row(j0 + j)),
                pl.BlockSpec((1, tn), lambda i, j: (0, j0 + j))]
    args = [h, wts["w_in_t"], wts["bbig"]]
    if kind == "headnorm":
        in_specs.append(pl.BlockSpec((1, F_DH), lambda i, j: (0, 0)))
        args.append(nw)
    two = emit_f32 or kind == "copy"
    out_shape = ([jax.ShapeDtypeStruct((m, width), F32)] if two else []) + [jax.ShapeDtypeStruct((m, width), BF16)]
    return pl.pallas_call(
        functools.partial(_proj_kernel, kind=kind, scale=scale, emit_f32=emit_f32),
        grid=(m // tm, j1 - j0),
        in_specs=in_specs,
        out_specs=[tile] * len(out_shape),
        out_shape=out_shape,
        compiler_params=_params("arbitrary", "arbitrary"),
        name="proj_" + kind,
    )(*args)


def _cumsum_kernel(sm_ref, fcol_ref, carry_ref):
    c = pl.program_id(1)
    tc = sm_ref.shape[0]

    @pl.when(c == 0)
    def _():
        carry_ref[...] = jnp.zeros_like(carry_ref)

    fc = _dot_f32_rhs(_tri01(tc, "lower"), sm_ref[...]) + carry_ref[0:1, :]
    fcol_ref[...] = fc
    carry_ref[...] = jnp.broadcast_to(fc[tc - 1:tc, :], carry_ref.shape)


def _cumsum(sm, batch, tc=512):
    m = sm.shape[0]
    nt = (m // batch) // tc
    return pl.pallas_call(
        _cumsum_kernel,
        grid=(batch, nt),
        in_specs=[pl.BlockSpec((tc, LANES), lambda b, c: (b * nt + c, 0))],
        out_specs=pl.BlockSpec((tc, LANES), lambda b, c: (b * nt + c, 0)),
        out_shape=jax.ShapeDtypeStruct((m, LANES), F32),
        scratch_shapes=[pltpu.VMEM((SUBLANES, LANES), F32)],
        compiler_params=_params("arbitrary", "arbitrary"),
        name="cumsum",
    )(sm)


def _mlstm_kernel(q_ref, k_ref, v_ref, sm_ref, smt_ref, og_ref, hw_ref,
                  ym_ref, cout_ref, nm_ref, c_ref, n_ref, m_ref):
    c = pl.program_id(1)
    L = q_ref.shape[0]

    @pl.when(c == 0)
    def _():
        c_ref[...] = jnp.zeros_like(c_ref)
        n_ref[...] = jnp.zeros_like(n_ref)
        m_ref[...] = jnp.zeros_like(m_ref)

    r = lax.broadcasted_iota(jnp.int32, (L, L), 0)
    s_idx = lax.broadcasted_iota(jnp.int32, (L, L), 1)
    causal = s_idx <= r
    sm = sm_ref[...]
    smt = smt_ref[...]
    bcol_all = _dot_f32_rhs(_tri01(L, "lower"), sm)
    brow_all = _dot_f32_lhs(smt, _tri01(L, "upper"))

    for h in range(M_HEADS):
        ig_col = sm[:, IG0 + h:IG0 + h + 1]
        b_col = bcol_all[:, LFM0 + h:LFM0 + h + 1]
        ig_row = smt[IG0 + h:IG0 + h + 1, :]
        b_row = brow_all[LFM0 + h:LFM0 + h + 1, :]
        m0 = m_ref[h:h + 1, 0:1]
        a_row = ig_row - b_row
        cm_col = jnp.max(jnp.where(causal, a_row, -jnp.inf), axis=1, keepdims=True)
        mx_col = jnp.maximum(m0, cm_col)
        m_col = b_col + mx_col
        dmat = jnp.exp(jnp.where(causal, a_row - mx_col, -jnp.inf))
        inter = jnp.exp(m0 - mx_col)
        q = q_ref[:, h * M_DK:(h + 1) * M_DK]
        k = k_ref[:, h * M_DK:(h + 1) * M_DK]
        v = v_ref[:, h * M_DV:(h + 1) * M_DV]
        c0 = c_ref[h]
        n0 = n_ref[h:h + 1, :]
        s = _dot_nt(q, k) * dmat
        num = _dot(s.astype(BF16), v) + inter * _dot_nt(q, c0.astype(BF16))
        den = jnp.sum(s, axis=1, keepdims=True) + inter * jnp.sum(q.astype(F32) * n0, axis=1, keepdims=True)
        hh = num / jnp.maximum(jnp.abs(den), jnp.exp(-m_col))
        hn = hh * lax.rsqrt(jnp.mean(hh * hh, axis=1, keepdims=True) + NORM_EPS) * hw_ref[:, h * M_DV:(h + 1) * M_DV]
        ym_ref[:, h * M_DV:(h + 1) * M_DV] = (og_ref[:, h * M_DV:(h + 1) * M_DV] * hn).astype(BF16)
        m_new = m_col[L - 1:L, :]
        b_last = b_col[L - 1:L, :]
        w_end = jnp.exp(ig_col + b_last - b_col - m_new)
        decay = jnp.exp(b_last + m0 - m_new)
        vw = (w_end * v.astype(F32)).astype(BF16)
        c_ref[h] = decay * c0 + _dot_tn(vw, k)
        n_ref[h:h + 1, :] = decay * n0 + jnp.sum(w_end * k.astype(F32), axis=0, keepdims=True)
        m_ref[h:h + 1, :] = jnp.broadcast_to(m_new, (1, LANES))

    @pl.when(c == pl.num_programs(1) - 1)
    def _():
        cout_ref[...] = c_ref[...]
        nm_ref[0:M_HEADS, :] = n_ref[0:M_HEADS, :]
        nm_ref[M_HEADS:2 * M_HEADS, :] = m_ref[0:M_HEADS, :]


def _mlstm_prompt(qkv, sm, smt, sig, hw, batch):
    m = qkv.shape[0]
    og_block = sig.shape[1] // M_V - 1
    nc = (m // batch) // CHUNK
    row = lambda b, c: (b * nc + c, 0)
    return pl.pallas_call(
        _mlstm_kernel,
        grid=(batch, nc),
        in_specs=[pl.BlockSpec((CHUNK, M_QK), row),
                  pl.BlockSpec((CHUNK, M_QK), lambda b, c: (b * nc + c, 1)),
                  pl.BlockSpec((CHUNK, M_V), lambda b, c: (b * nc + c, 2 * M_QK // M_V)),
                  pl.BlockSpec((CHUNK, LANES), row),
                  pl.BlockSpec((SMALL_ROWS, CHUNK), lambda b, c: (0, b * nc + c)),
                  pl.BlockSpec((CHUNK, M_V), lambda b, c: (b * nc + c, og_block)),
                  pl.BlockSpec((1, M_V), lambda b, c: (0, 0))],
        out_specs=[pl.BlockSpec((CHUNK, M_V), row),
                   pl.BlockSpec((None, M_HEADS, M_DV, M_DK), lambda b, c: (b, 0, 0, 0)),
                   pl.BlockSpec((None, 2 * M_HEADS, M_DK), lambda b, c: (b, 0, 0))],
        out_shape=[jax.ShapeDtypeStruct((m, M_V), BF16),
                   jax.ShapeDtypeStruct((batch, M_HEADS, M_DV, M_DK), F32),
                   jax.ShapeDtypeStruct((batch, 2 * M_HEADS, M_DK), F32)],
        scratch_shapes=[pltpu.VMEM((M_HEADS, M_DV, M_DK), F32),
                        pltpu.VMEM((SUBLANES, M_DK), F32),
                        pltpu.VMEM((SUBLANES, LANES), F32)],
        compiler_params=_params("arbitrary", "arbitrary"),
        name="mlstm",
    )(qkv, qkv, qkv, sm, smt, sig, hw)


FOX_HEADS_PER_STEP = 2
FOX_ROW_CHUNK = 32


def _forget_columns(f, first):
    hi, mid, lo = _split3(f)
    lane = lax.broadcasted_iota(jnp.int32, (f.shape[0], LANES), 1)
    ones = jnp.where(jnp.abs(lane - (4 - first)) <= 1, 1.0, 0.0)
    ext = jnp.where(lane == first, hi.astype(F32),
                    jnp.where(lane == first + 1, mid.astype(F32),
                              jnp.where(lane == first + 2, lo.astype(F32), ones)))
    return ext.astype(BF16)


def _fox_kernel(q_ref, k_ref, v_ref, fcol_ref, o_ref, qaug_ref, kaug_ref, s_ref, p_ref, m_ref, l_ref, acc_ref):
    hp = pl.program_id(1)
    qi = pl.program_id(2)
    tq = q_ref.shape[0]
    t = k_ref.shape[0]
    nh = FOX_HEADS_PER_STEP

    def forget_sum(fc, h2):
        lane = lax.broadcasted_iota(jnp.int32, fc.shape, 1)
        sel = jnp.where(lane == LFF0 + nh * hp + h2, fc, 0.0)
        return jnp.sum(sel, axis=1, keepdims=True) * LOG2E

    @pl.when(qi == 0)
    def _():
        fc = fcol_ref[...]
        for h2 in range(nh):
            kaug_ref[h2, :, 0:F_DH] = k_ref[:, h2 * F_DH:(h2 + 1) * F_DH]
            kaug_ref[h2, :, F_DH:2 * F_DH] = _forget_columns(-forget_sum(fc, h2), 3)

    fq = fcol_ref[pl.ds(pl.multiple_of(qi * tq, tq), tq), :]
    for h2 in range(nh):
        qaug_ref[h2, :, 0:F_DH] = q_ref[:, h2 * F_DH:(h2 + 1) * F_DH]
        qaug_ref[h2, :, F_DH:2 * F_DH] = _forget_columns(forget_sum(fq, h2), 0)
    m_ref[...] = jnp.full_like(m_ref, -jnp.inf)
    l_ref[...] = jnp.zeros_like(l_ref)
    acc_ref[...] = jnp.zeros_like(acc_ref)
    rc = FOX_ROW_CHUNK
    n_lane_tiles = tq // LANES

    def softmax_rows(h2, c, masked):
        rows = slice(c * rc, (c + 1) * rc)
        s = s_ref[h2, rows, :]
        if masked:
            r = lax.broadcasted_iota(jnp.int32, (rc, tq), 0) + c * rc
            col = lax.broadcasted_iota(jnp.int32, (rc, tq), 1)
            s = jnp.where(r >= col, s, -jnp.inf)
        tiles = [s[:, k * LANES:(k + 1) * LANES] for k in range(n_lane_tiles)]
        m_old = m_ref[h2, rows, :]
        mx = functools.reduce(jnp.maximum, tiles)
        m_new = jnp.maximum(m_old, jnp.max(mx, axis=1, keepdims=True))
        alpha = jnp.exp2(m_old - m_new)
        ps = [jnp.exp2(tile - m_new) for tile in tiles]
        row_sum = jnp.sum(functools.reduce(jnp.add, ps), axis=1, keepdims=True)
        l_ref[h2, rows, :] = alpha * l_ref[h2, rows, :] + row_sum
        for k in range(n_lane_tiles):
            p_ref[h2, rows, k * LANES:(k + 1) * LANES] = ps[k].astype(BF16)
        acc_ref[h2, rows, :] = alpha * acc_ref[h2, rows, :]
        m_ref[h2, rows, :] = m_new

    def block(kj, masked):
        off = pl.multiple_of(kj * tq, tq)
        for h2 in range(nh):
            s_ref[h2] = _dot_nt(qaug_ref[h2], kaug_ref[h2, pl.ds(off, tq), :])
        for h2 in range(nh):
            for c in range(tq // rc):
                softmax_rows(h2, c, masked)
            acc_ref[h2] += _dot(p_ref[h2], v_ref[pl.ds(off, tq), h2 * F_DH:(h2 + 1) * F_DH])

    def body(kj, carry):
        block(kj, False)
        return carry

    lax.fori_loop(0, qi, body, 0)
    block(qi, True)
    for h2 in range(nh):
        o_ref[:, h2 * F_DH:(h2 + 1) * F_DH] = (acc_ref[h2] / l_ref[h2]).astype(BF16)


def _fox_prompt(qf, kfb, vfb, fcol, batch, tq=512):
    m = qf.shape[0]
    t = m // batch
    nq = t // tq
    nh = FOX_HEADS_PER_STEP
    wide = nh * F_DH
    return pl.pallas_call(
        _fox_kernel,
        grid=(batch, F_HEADS // nh, nq),
        in_specs=[pl.BlockSpec((tq, wide), lambda b, h, i: (b * nq + i, h)),
                  pl.BlockSpec((t, wide), lambda b, h, i: (b, h)),
                  pl.BlockSpec((t, wide), lambda b, h, i: (b, h)),
                  pl.BlockSpec((t, LANES), lambda b, h, i: (b, 0))],
        out_specs=pl.BlockSpec((tq, wide), lambda b, h, i: (b * nq + i, h)),
        out_shape=jax.ShapeDtypeStruct((m, F_W), BF16),
        scratch_shapes=[pltpu.VMEM((nh, tq, 2 * F_DH), BF16),
                        pltpu.VMEM((nh, t, 2 * F_DH), BF16),
                        pltpu.VMEM((nh, tq, tq), F32),
                        pltpu.VMEM((nh, tq, tq), BF16),
                        pltpu.VMEM((nh, tq, LANES), F32),
                        pltpu.VMEM((nh, tq, LANES), F32),
                        pltpu.VMEM((nh, tq, F_DH), F32)],
        compiler_params=_params("arbitrary", "arbitrary", "arbitrary"),
        name="fox",
    )(qf, kfb, vfb, fcol)


MSTEP_ROWS = 8


def _mstep_kernel(q_ref, k_ref, v_ref, sm_ref, og_ref, hw_ref, c0_ref, n0_ref, m0_ref,
                  ym_ref, cout_ref, nout_ref, mout_ref, cq_ref):
    rows = q_ref.shape[0]
    sm = sm_ref[...]
    ig = sm[:, IG0:IG0 + M_HEADS]
    lf = sm[:, LFM0:LFM0 + M_HEADS]
    m0 = m0_ref[...]
    m_new = lf + jnp.maximum(m0, ig - lf)
    w_end = jnp.exp(ig - m_new)
    decay = jnp.exp(lf + m0 - m_new)
    mout_ref[...] = m_new
    row_id = lax.broadcasted_iota(jnp.int32, (SUBLANES, M_DV), 0)

    for h in range(M_HEADS):
        qh = q_ref[:, h * M_DK:(h + 1) * M_DK]
        kh = k_ref[:, h * M_DK:(h + 1) * M_DK]
        vh = v_ref[:, h * M_DV:(h + 1) * M_DV]
        qf = qh.astype(F32)
        kf = kh.astype(F32)
        vf = vh.astype(F32)
        wh = w_end[:, h:h + 1]
        dh = decay[:, h:h + 1]
        for r in range(rows):
            c0 = c0_ref[r, h]
            q8 = jnp.broadcast_to(qh[r:r + 1, :], (SUBLANES, M_DK))
            cq_ref[r:r + 1, :] = _dot_nt(q8, c0.astype(BF16))[0:1, :]
            vw8 = jnp.where(row_id == 0, jnp.broadcast_to(wh[r:r + 1, :] * vf[r:r + 1, :], (SUBLANES, M_DV)), 0.0)
            k8 = jnp.broadcast_to(kh[r:r + 1, :], (SUBLANES, M_DK))
            cout_ref[r, h] = dh[r:r + 1, :] * c0 + _dot_tn(vw8.astype(BF16), k8)
        n0 = n0_ref[:, h, :]
        s = jnp.sum(qf * kf, axis=1, keepdims=True) * wh
        num = s * vf + dh * cq_ref[...]
        den = s + dh * jnp.sum(n0 * qf, axis=1, keepdims=True)
        hh = num / jnp.maximum(jnp.abs(den), jnp.exp(-m_new[:, h:h + 1]))
        hn = hh * lax.rsqrt(jnp.mean(hh * hh, axis=1, keepdims=True) + NORM_EPS) * hw_ref[:, h * M_DV:(h + 1) * M_DV]
        ym_ref[:, h * M_DV:(h + 1) * M_DV] = (og_ref[:, h * M_DV:(h + 1) * M_DV] * hn).astype(BF16)
        nout_ref[:, h, :] = dh * n0 + wh * kf


def _mlstm_step(qkv, sm, sig, hw, c0, n0, m0):
    db = qkv.shape[0]
    og_block = sig.shape[1] // M_V - 1
    rows = MSTEP_ROWS
    row = lambda i: (i, 0)
    return pl.pallas_call(
        _mstep_kernel,
        grid=(db // rows,),
        in_specs=[pl.BlockSpec((rows, M_QK), row),
                  pl.BlockSpec((rows, M_QK), lambda i: (i, 1)),
                  pl.BlockSpec((rows, M_V), lambda i: (i, 2 * M_QK // M_V)),
                  pl.BlockSpec((rows, LANES), row),
                  pl.BlockSpec((rows, M_V), lambda i: (i, og_block)),
                  pl.BlockSpec((1, M_V), lambda i: (0, 0)),
                  pl.BlockSpec((rows, M_HEADS, M_DV, M_DK), lambda i: (i, 0, 0, 0)),
                  pl.BlockSpec((rows, M_HEADS, M_DK), lambda i: (i, 0, 0)),
                  pl.BlockSpec((rows, M_HEADS), row)],
        out_specs=[pl.BlockSpec((rows, M_V), row),
                   pl.BlockSpec((rows, M_HEADS, M_DV, M_DK), lambda i: (i, 0, 0, 0)),
                   pl.BlockSpec((rows, M_HEADS, M_DK), lambda i: (i, 0, 0)),
                   pl.BlockSpec((rows, M_HEADS), row)],
        out_shape=[jax.ShapeDtypeStruct((db, M_V), BF16),
                   jax.ShapeDtypeStruct(c0.shape, F32),
                   jax.ShapeDtypeStruct(n0.shape, F32),
                   jax.ShapeDtypeStruct(m0.shape, F32)],
        scratch_shapes=[pltpu.VMEM((rows, M_DV), F32)],
        compiler_params=_params("arbitrary"),
        name="mstep",
    )(qkv, qkv, qkv, sm, sig, hw, c0, n0, m0)


def _decode_kernel(*refs, n_pages):
    k_refs = refs[1:1 + n_pages]
    v_refs = refs[1 + n_pages:1 + 2 * n_pages]
    lf_refs = refs[1 + 2 * n_pages:1 + 3 * n_pages]
    q_ref, kn_ref, vn_ref, sm_ref, o_ref, s_ref, xn_ref = refs[1 + 3 * n_pages:]
    page = xn_ref.shape[1]
    pw = page * F_HEADS

    for p in range(n_pages):
        xn_ref[p * F_HEADS:(p + 1) * F_HEADS, :] = lf_refs[p][...]
    xn = xn_ref[...] * LOG2E
    t_id = lax.broadcasted_iota(jnp.int32, (page, pw), 0)
    l_id = lax.broadcasted_iota(jnp.int32, (page, pw), 1)
    widen = jnp.where((l_id // F_HEADS) == t_id, 1.0, 0.0).astype(BF16)
    wide = _dot_f32_lhs(xn, widen)
    r_id = lax.broadcasted_iota(jnp.int32, wide.shape, 0)
    c_id = lax.broadcasted_iota(jnp.int32, wide.shape, 1)
    own = jnp.where((r_id & (F_HEADS - 1)) == (c_id & (F_HEADS - 1)), wide, 0.0)
    g_p = lax.broadcasted_iota(jnp.int32, (n_pages, n_pages * F_HEADS), 0)
    g_r = lax.broadcasted_iota(jnp.int32, (n_pages, n_pages * F_HEADS), 1)
    gather = jnp.where((g_r // F_HEADS) == g_p, 1.0, 0.0).astype(BF16)
    x = _dot_f32_rhs(gather, own)
    lane = lax.broadcasted_iota(jnp.int32, x.shape, 1)
    y, z = x, x
    step = F_HEADS
    while step < pw:
        y = y + jnp.where(lane < pw - step, pltpu.roll(y, pw - step, axis=1), 0.0)
        z = z + pltpu.roll(z, step, axis=1)
        step *= 2
    after = y - x

    r = lax.broadcasted_iota(jnp.int32, (LANES, pw), 0)
    c = lax.broadcasted_iota(jnp.int32, (LANES, pw), 1)
    spread = jnp.where(r == LFF0 + (c & (F_HEADS - 1)), 1.0, 0.0).astype(BF16)
    run = _dot_f32_lhs(jnp.broadcast_to(sm_ref[...] * LOG2E, (SUBLANES, LANES)), spread)[0:1, :]

    q = q_ref[...]
    sub = lax.broadcasted_iota(jnp.int32, (F_HEADS, pw), 0)
    lane_h = lax.broadcasted_iota(jnp.int32, (F_HEADS, pw), 1) & (F_HEADS - 1)
    valid = lane_h == sub
    s_self = jnp.sum(q.astype(F32) * kn_ref[...], axis=1, keepdims=True)
    m_run = s_self
    for p in reversed(range(n_pages)):
        kb = k_refs[p][...].astype(BF16)
        st = _dot_nt(q, kb) + (run + after[p:p + 1, :])
        st = jnp.where(valid, st, -jnp.inf)
        s_ref[p] = st
        m_run = jnp.maximum(m_run, jnp.max(st, axis=1, keepdims=True))
        run = run + z[p:p + 1, :]
    w_self = jnp.exp2(s_self - m_run)
    l = w_self
    acc = w_self * vn_ref[...]
    for p in range(n_pages):
        pe = jnp.exp2(s_ref[p] - m_run)
        l = l + jnp.sum(pe, axis=1, keepdims=True)
        acc = acc + _dot(pe.astype(BF16), v_refs[p][...].astype(BF16))
    o_ref[...] = (acc / l).astype(BF16)


def _fox_decode(page_table, ck, cv, clf, layer, q3, k3, v3, sm3):
    db, n_pages = page_table.shape
    pw = ck.shape[2]
    page = clf.shape[3]
    seq = lambda b, pt: (b, 0, 0)

    def pg(p):
        return lambda b, pt: (layer, pt[b, p], 0, 0)

    in_specs = ([pl.BlockSpec((None, None, pw, F_DH), pg(p)) for p in range(n_pages)]
                + [pl.BlockSpec((None, None, pw, F_DH), pg(p)) for p in range(n_pages)]
                + [pl.BlockSpec((None, None, F_HEADS, page), pg(p)) for p in range(n_pages)]
                + [pl.BlockSpec((None, F_HEADS, F_DH), seq),
                   pl.BlockSpec((None, F_HEADS, F_DH), seq),
                   pl.BlockSpec((None, F_HEADS, F_DH), seq),
                   pl.BlockSpec((None, 1, LANES), seq)])
    grid_spec = pltpu.PrefetchScalarGridSpec(
        num_scalar_prefetch=1,
        grid=(db,),
        in_specs=in_specs,
        out_specs=pl.BlockSpec((None, F_HEADS, F_DH), seq),
        scratch_shapes=[pltpu.VMEM((n_pages, F_HEADS, pw), F32),
                        pltpu.VMEM((n_pages * F_HEADS, page), F32)],
    )
    return pl.pallas_call(
        functools.partial(_decode_kernel, n_pages=n_pages),
        grid_spec=grid_spec,
        out_shape=jax.ShapeDtypeStruct((db, F_HEADS, F_DH), BF16),
        compiler_params=_params("arbitrary"),
        name="decode",
    )(page_table, *([ck] * n_pages), *([cv] * n_pages), *([clf] * n_pages), q3, k3, v3, sm3)


def _merge_kernel(ym_ref, yf_ref, sga_ref, sgb_ref, x_ref, g1_ref, sc_ref, sh_ref, n2w_ref,
                  wa_ref, wb_ref, wo_ref, x1_ref, h2_ref):
    merged = sga_ref[...] * _dot(ym_ref[...], wa_ref[...]) + sgb_ref[...] * _dot(yf_ref[...], wb_ref[...])
    x1 = x_ref[...] + g1_ref[...] * _dot(merged.astype(BF16), wo_ref[...])
    x1_ref[...] = x1
    rstd = lax.rsqrt(jnp.mean(x1 * x1, axis=-1, keepdims=True) + NORM_EPS)
    h2_ref[...] = ((x1 * rstd * n2w_ref[...]) * (1.0 + sc_ref[...]) + sh_ref[...]).astype(BF16)


def _merge(ym, yf, sig, x2, mod, n2w, wts, tm):
    m, d = x2.shape
    tiles_per_group = (m // tm) // mod.shape[0]
    group_of = lambda i: i // tiles_per_group
    row = lambda i: (i, 0)
    const = lambda i: (0, 0)
    resident = dict(pipeline_mode=pl.Buffered(1))
    return pl.pallas_call(
        _merge_kernel,
        grid=(m // tm,),
        in_specs=[pl.BlockSpec((tm, M_V), row),
                  pl.BlockSpec((tm, F_W), row),
                  pl.BlockSpec((tm, d), row),
                  pl.BlockSpec((tm, d), lambda i: (i, 1)),
                  pl.BlockSpec((tm, d), row),
                  _mod_spec(mod, d, MOD_G1, group_of),
                  _mod_spec(mod, d, MOD_SC2, group_of),
                  _mod_spec(mod, d, MOD_SH2, group_of),
                  pl.BlockSpec((1, d), const),
                  pl.BlockSpec((M_V, d), const, **resident),
                  pl.BlockSpec((F_W, d), const, **resident),
                  pl.BlockSpec((d, d), const, **resident)],
        out_specs=[pl.BlockSpec((tm, d), row), pl.BlockSpec((tm, d), row)],
        out_shape=[jax.ShapeDtypeStruct((m, d), F32), jax.ShapeDtypeStruct((m, d), BF16)],
        compiler_params=_params("arbitrary"),
        name="merge",
    )(ym, yf, sig, sig, x2, mod, mod, mod, n2w, wts["wa"], wts["wb"], wts["wo"])


FFN_TF = 512


def _ffn_kernel(h2_ref, x1_ref, g2_ref, p_ref, wa_ref, wg_ref, cw_ref, cb_ref, w2_ref,
                y_ref, a_out_ref, carry_ref, *, seq):
    f = pl.program_id(2)
    tm = h2_ref.shape[0]

    @pl.when(f == 0)
    def _():
        y_ref[...] = jnp.zeros_like(y_ref)

    if seq:
        @pl.when(pl.program_id(1) == 0)
        def _():
            carry_ref[f] = p_ref[...]

    h2 = h2_ref[...]
    a = _dot(h2, wa_ref[...])
    g = _dot(h2, wg_ref[...])
    if seq:
        prev = carry_ref[f]
        p2, p1 = prev[SUBLANES - 2:SUBLANES - 1, :], prev[SUBLANES - 1:SUBLANES, :]
        r = lax.broadcasted_iota(jnp.int32, a.shape, 0)
        a1 = jnp.where(r == 0, p1, pltpu.roll(a, 1, axis=0))
        a2 = jnp.where(r == 0, p2, jnp.where(r == 1, p1, pltpu.roll(a, 2, axis=0)))
        carry_ref[f] = a[tm - SUBLANES:tm, :]
        a_out_ref[...] = a[tm - SUBLANES:tm, :]
    else:
        a2, a1 = p_ref[0], p_ref[1]
        a_out_ref[...] = a
    ac = cw_ref[0:1, :] * a2 + cw_ref[1:2, :] * a1 + cw_ref[2:3, :] * a + cb_ref[...]
    u = (jax.nn.gelu(ac, approximate=True) * g).astype(BF16)
    y_ref[...] += _dot(u, w2_ref[...])

    @pl.when(f == pl.num_programs(2) - 1)
    def _():
        y_ref[...] = x1_ref[...] + g2_ref[...] * y_ref[...]


def _ffn(h2, x1, mod, prev, wts, tm, seq):
    m, d = x1.shape
    groups = mod.shape[0]
    tiles = (m // tm) // groups
    dff = wts["w2"].shape[0]
    tf = FFN_TF
    nf = dff // tf
    row = lambda b, i, f: (b * tiles + i, 0)
    if seq:
        prev_spec = pl.BlockSpec((None, SUBLANES, tf), lambda b, i, f: (b, 0, f))
        a_spec = pl.BlockSpec((None, None, SUBLANES, tf), lambda b, i, f: (b, i, 0, f))
        a_shape = jax.ShapeDtypeStruct((groups, tiles, SUBLANES, dff), F32)
    else:
        prev_spec = pl.BlockSpec((CONV_W - 1, tm, tf), lambda b, i, f: (0, i, f))
        a_spec = pl.BlockSpec((tm, tf), lambda b, i, f: (i, f))
        a_shape = jax.ShapeDtypeStruct((m, dff), F32)
    return pl.pallas_call(
        functools.partial(_ffn_kernel, seq=seq),
        grid=(groups, tiles, nf),
        in_specs=[pl.BlockSpec((tm, d), row),
                  pl.BlockSpec((tm, d), row, pipeline_mode=pl.Buffered(1)),
                  _mod_spec(mod, d, MOD_G2, lambda b, i, f: b),
                  prev_spec,
                  pl.BlockSpec((d, tf), lambda b, i, f: (0, f)),
                  pl.BlockSpec((d, tf), lambda b, i, f: (0, f + nf)),
                  pl.BlockSpec((CONV_W, tf), lambda b, i, f: (0, f)),
                  pl.BlockSpec((1, tf), lambda b, i, f: (0, f)),
                  pl.BlockSpec((tf, d), lambda b, i, f: (f, 0))],
        out_specs=[pl.BlockSpec((tm, d), row), a_spec],
        out_shape=[jax.ShapeDtypeStruct((m, d), F32), a_shape],
        scratch_shapes=[pltpu.VMEM((nf, SUBLANES, tf), F32)],
        compiler_params=_params("arbitrary", "arbitrary", "arbitrary"),
        name="ffn_seq" if seq else "ffn_row",
    )(h2, x1, mod, prev, wts["w1"], wts["w1"], wts["cw"], wts["cb"], wts["w2"])


def _prep_weights(w_in, b_in, f_qnorm_w, f_knorm_w, w_proj_a, w_proj_b, w_out, w_ffn_in, conv_w, conv_b, w_ffn_out):
    o = _SPLIT_OFFS
    bbig = jnp.concatenate([b_in[:o[3]], b_in[o[10]:], b_in[o[3]:o[4]], b_in[o[6]:o[9]]])
    bsm = jnp.concatenate([b_in[o[4]:o[6]], b_in[o[9]:o[10]]])
    return {
        "w_in_t": w_in.T,
        "bbig": bbig.reshape(1, -1),
        "bsm": bsm.reshape(SMALL_ROWS, 1),
        "qnw": f_qnorm_w.reshape(1, F_DH),
        "knw": f_knorm_w.reshape(1, F_DH),
        "wa": w_proj_a.astype(BF16),
        "wb": w_proj_b.astype(BF16),
        "wo": w_out.astype(BF16),
        "w1": w_ffn_in.astype(BF16),
        "cw": conv_w,
        "cb": conv_b.reshape(1, -1),
        "w2": w_ffn_out.astype(BF16),
    }


def kernel(x_prompt, x_sample, c_prompt, c_sample, cache_k, cache_v, cache_logf, page_table, state_C, state_n, state_m, state_conv, w_ada, b_ada, norm1_w, w_in, b_in, m_hnorm_w, f_qnorm_w, f_knorm_w, w_proj_a, w_proj_b, w_out, norm2_w, w_ffn_in, conv_w, conv_b, w_ffn_out):
    depth = w_ada.shape[0]
    assert depth == 1, "single-layer step"
    batch, seq_len, d = x_prompt.shape
    db = x_sample.shape[0]
    assert x_sample.shape[1] == 1
    dff = w_ffn_out.shape[1]
    layer = 0

    wts = _prep_weights(w_in[layer], b_in[layer], f_qnorm_w[layer], f_knorm_w[layer], w_proj_a[layer],
                        w_proj_b[layer], w_out[layer], w_ffn_in[layer], conv_w[layer], conv_b[layer],
                        w_ffn_out[layer])
    n1w = norm1_w[layer].reshape(1, d)
    n2w = norm2_w[layer].reshape(1, d)
    hw = m_hnorm_w[layer].reshape(1, M_V)

    mod_p, mod_s = _ada(c_prompt, c_sample, w_ada[layer], b_ada[layer])
    mod_p = mod_p.reshape(batch, 1, 6 * d)
    mod_s = mod_s.reshape(1, db, 6 * d)

    def in_projection(x2, mod, tm):
        h, qkv, sm, smt = _inproj_first(x2, mod, n1w, wts, tm)
        (sig,) = _proj(h, wts, _J_GATE, _J_FQ, "sigmoid", tm)
        (qf,) = _proj(h, wts, _J_FQ, _J_FK, "headnorm", tm, nw=wts["qnw"], scale=FOX_Q_SCALE)
        kf, kfb = _proj(h, wts, _J_FK, _J_FV, "headnorm", tm, nw=wts["knw"], emit_f32=True)
        vf, vfb = _proj(h, wts, _J_FV, _J_END, "copy", tm)
        return qkv, sig, qf, kf, kfb, vf, vfb, sm, smt

    tm_p = min(1024, seq_len)
    xp = x_prompt.reshape(batch * seq_len, d)
    qkv, sig, qf, kf, kfb, vf, vfb, sm, smt = in_projection(xp, mod_p, tm_p)
    fcol = _cumsum(sm, batch)
    ym, c_p, nm_p = _mlstm_prompt(qkv, sm, smt, sig, hw, batch)
    yf = _fox_prompt(qf, kfb, vfb, fcol, batch)
    x1, h2 = _merge(ym, yf, sig, xp, mod_p, n2w, wts, min(512, seq_len))
    conv0 = jnp.zeros((batch, SUBLANES, dff), F32)
    yp, a_last = _ffn(h2, x1, mod_p, conv0, wts, tm_p, seq=True)

    y_prompt = yp.reshape(batch, seq_len, d)
    k_prompt = kf.reshape(depth, batch, seq_len, F_HEADS, F_DH)
    v_prompt = vf.reshape(depth, batch, seq_len, F_HEADS, F_DH)
    logf_prompt = sm[:, LFF0:LFF0 + F_HEADS].reshape(depth, batch, seq_len, F_HEADS)
    c_prompt_out = c_p[None]
    n_prompt = nm_p[:, :M_HEADS, :][None]
    m_prompt = nm_p[:, M_HEADS:, 0][None]
    conv_prompt = a_last[:, -1, SUBLANES - (CONV_W - 1):, :][None]

    xs = x_sample.reshape(db, d)
    qkv_s, sig_s, qf_s, kf_s, _, vf_s, _, sm_s, _ = in_projection(xs, mod_s, db)
    ym_s, c_s, n_s, m_s = _mlstm_step(qkv_s, sm_s, sig_s, hw, state_C[layer], state_n[layer], state_m[layer])
    n_pool, page = cache_k.shape[1], cache_k.shape[2]
    yf_s = _fox_decode(page_table,
                       cache_k.reshape(depth, n_pool, page * F_HEADS, F_DH),
                       cache_v.reshape(depth, n_pool, page * F_HEADS, F_DH),
                       cache_logf.transpose(0, 1, 3, 2),
                       layer,
                       qf_s.reshape(db, F_HEADS, F_DH), kf_s.reshape(db, F_HEADS, F_DH),
                       vf_s.reshape(db, F_HEADS, F_DH), sm_s.reshape(db, 1, LANES)).reshape(db, F_W)
    x1_s, h2_s = _merge(ym_s, yf_s, sig_s, xs, mod_s, n2w, wts, db)
    prev_s = state_conv[layer].transpose(1, 0, 2)
    ys, a_s = _ffn(h2_s, x1_s, mod_s, prev_s, wts, db, seq=False)

    y_sample = ys.reshape(db, 1, d)
    k_sample = kf_s.reshape(depth, db, 1, F_HEADS, F_DH)
    v_sample = vf_s.reshape(depth, db, 1, F_HEADS, F_DH)
    logf_sample = sm_s[:, LFF0:LFF0 + F_HEADS].reshape(depth, db, 1, F_HEADS)
    conv_sample = jnp.stack([state_conv[layer][:, CONV_W - 2, :], a_s], axis=1)[None]

    return (y_prompt, y_sample, k_prompt, v_prompt, logf_prompt, c_prompt_out, n_prompt, m_prompt, conv_prompt,
            k_sample, v_sample, logf_sample, c_s[None], n_s[None], m_s[None], conv_sample)
```

```python
import functools

import jax
import jax.numpy as jnp
from jax import lax
from jax.experimental import pallas as pl
from jax.experimental.pallas import tpu as pltpu

F32 = jnp.float32
BF16 = jnp.bfloat16

NORM_EPS = 1e-6
M_HEADS, M_DK, M_DV = 4, 128, 256
F_HEADS, F_DH = 8, 128
M_QK = M_HEADS * M_DK
M_V = M_HEADS * M_DV
F_W = F_HEADS * F_DH
CONV_W = 3
CHUNK = 128
LANES = 128
SUBLANES = 8
SMALL_ROWS = 16
VMEM_LIMIT = 56 * 1024 * 1024

IG0, LFM0, LFF0 = 0, M_HEADS, 2 * M_HEADS

LOG2E = 1.4426950408889634
FOX_Q_SCALE = F_DH ** -0.5 * LOG2E


def _dot(a, b):
    return jnp.dot(a, b, preferred_element_type=F32)


def _dot_nt(a, b):
    return lax.dot_general(a, b, (((1,), (1,)), ((), ())), preferred_element_type=F32)


def _dot_tn(a, b):
    return lax.dot_general(a, b, (((0,), (0,)), ((), ())), preferred_element_type=F32)


def _split3(x):
    hi = x.astype(BF16)
    r1 = x - hi.astype(F32)
    mid = r1.astype(BF16)
    lo = (r1 - mid.astype(F32)).astype(BF16)
    return hi, mid, lo


def _dot_f32_lhs(x, mat01):
    hi, mid, lo = _split3(x)
    return _dot(hi, mat01) + _dot(mid, mat01) + _dot(lo, mat01)


def _dot_f32_rhs(mat01, x):
    hi, mid, lo = _split3(x)
    return _dot(mat01, hi) + _dot(mat01, mid) + _dot(mat01, lo)


def _log_sigmoid(x):
    return jnp.minimum(x, 0.0) - jnp.log1p(jnp.exp(-jnp.abs(x)))


def _tri01(n, kind):
    r = lax.broadcasted_iota(jnp.int32, (n, n), 0)
    c = lax.broadcasted_iota(jnp.int32, (n, n), 1)
    if kind == "lower":
        m = c <= r
    elif kind == "upper":
        m = r <= c
    else:
        m = r > c
    return jnp.where(m, 1.0, 0.0).astype(BF16)


def _params(*sem):
    return pltpu.CompilerParams(dimension_semantics=sem, vmem_limit_bytes=VMEM_LIMIT)


def _ada_kernel(cp_ref, cs_ref, w_ref, b_ref, op_ref, os_ref):
    w = w_ref[...].astype(BF16)
    for c_ref, o_ref in ((cp_ref, op_ref), (cs_ref, os_ref)):
        c = c_ref[...]
        s = (c * jax.nn.sigmoid(c)).astype(BF16)
        o_ref[...] = _dot(s, w) + b_ref[...]


def _ada(c_p, c_s, w_ada, b_ada, tn=1024):
    d = c_p.shape[1]
    n = w_ada.shape[1]
    return pl.pallas_call(
        _ada_kernel,
        grid=(n // tn,),
        in_specs=[pl.BlockSpec(c_p.shape, lambda j: (0, 0)),
                  pl.BlockSpec(c_s.shape, lambda j: (0, 0)),
                  pl.BlockSpec((d, tn), lambda j: (0, j)),
                  pl.BlockSpec((1, tn), lambda j: (0, j))],
        out_specs=[pl.BlockSpec((c_p.shape[0], tn), lambda j: (0, j)),
                   pl.BlockSpec((c_s.shape[0], tn), lambda j: (0, j))],
        out_shape=[jax.ShapeDtypeStruct((c_p.shape[0], n), F32), jax.ShapeDtypeStruct((c_s.shape[0], n), F32)],
        compiler_params=_params("arbitrary"),
        name="ada",
    )(c_p, c_s, w_ada, b_ada.reshape(1, n))


MOD_SH1, MOD_SC1, MOD_G1, MOD_SH2, MOD_SC2, MOD_G2 = range(6)


PROJ_TN = 1024
_J_MIX, _J_GATE, _J_FQ, _J_FK, _J_FV, _J_END = 0, 2, 7, 8, 9, 10
_J_MO = _J_GATE + 4
_SPLIT_SIZES = (M_QK, M_QK, M_V, M_V, M_HEADS, M_HEADS, F_W, F_W, F_W, F_HEADS)
_SPLIT_OFFS = tuple(sum(_SPLIT_SIZES[:i]) for i in range(len(_SPLIT_SIZES) + 1))
_ROW_MO, _ROW_SMALL_M, _ROW_FQ, _ROW_SMALL_F, _ROW_GATES = (_SPLIT_OFFS[3], _SPLIT_OFFS[4], _SPLIT_OFFS[6],
                                                            _SPLIT_OFFS[9], _SPLIT_OFFS[10])
_TILE_RUNS = ((_J_MIX, 0), (_J_GATE, _ROW_GATES), (_J_MO, _ROW_MO), (_J_FQ, _ROW_FQ))


def _tile_row(jj):
    row = _TILE_RUNS[0][1] + PROJ_TN * (jj - _TILE_RUNS[0][0])
    for t0, r0 in _TILE_RUNS[1:]:
        row = jnp.where(jj >= t0, r0 + PROJ_TN * (jj - t0), row)
    return row


def _inproj_first_kernel(x_ref, sc_ref, sh_ref, n1w_ref, w_ref, b_ref, wsm_m_ref, wsm_f_ref, bsm_ref,
                         h_ref, qkv_ref, sm_ref, smt_ref):
    j = pl.program_id(1)

    @pl.when(j == 0)
    def _():
        x = x_ref[...]
        rstd = lax.rsqrt(jnp.mean(x * x, axis=-1, keepdims=True) + NORM_EPS)
        h = (x * rstd * n1w_ref[...]) * (1.0 + sc_ref[...]) + sh_ref[...]
        hb = h.astype(BF16)
        h_ref[...] = hb
        wsm = jnp.concatenate([wsm_m_ref[...], wsm_f_ref[...]], axis=0).astype(BF16)
        smt = _dot_nt(wsm, hb) + bsm_ref[...]
        row = lax.broadcasted_iota(jnp.int32, smt.shape, 0)
        smt = jnp.where(row < LFM0, smt, _log_sigmoid(smt))
        smt_ref[...] = smt
        pad = jnp.zeros((LANES - SMALL_ROWS, smt.shape[1]), F32)
        sm_ref[...] = jnp.concatenate([smt, pad], axis=0).T

    acc = _dot_nt(h_ref[...], w_ref[...].astype(BF16)) + b_ref[...]
    lane = lax.broadcasted_iota(jnp.int32, (1, acc.shape[1]), 1)
    is_mk = jnp.logical_and(j == 0, lane >= M_QK)
    qkv_ref[...] = (acc * jnp.where(is_mk, M_DK ** -0.5, 1.0)).astype(BF16)


def _proj_kernel(h_ref, w_ref, b_ref, *rest, kind, scale, emit_f32):
    *rest, wb_ref = rest

    @pl.when(pl.program_id(1) == 0)
    def _():
        wb_ref[...] = w_ref[...].astype(BF16)

    acc = _dot_nt(h_ref[...], wb_ref[...]) + b_ref[...]
    if kind == "sigmoid":
        (ob_ref,) = rest
        ob_ref[...] = jax.nn.sigmoid(acc).astype(BF16)
    elif kind == "headnorm":
        nw_ref, *outs = rest
        for h in range(acc.shape[1] // F_DH):
            sl = acc[:, h * F_DH:(h + 1) * F_DH]
            ms = jnp.mean(sl * sl, axis=-1, keepdims=True)
            y = sl * lax.rsqrt(ms + NORM_EPS) * nw_ref[...]
            if emit_f32:
                outs[0][:, h * F_DH:(h + 1) * F_DH] = y
            outs[-1][:, h * F_DH:(h + 1) * F_DH] = (y * scale).astype(BF16)
    else:
        of_ref, ob_ref = rest
        of_ref[...] = acc
        ob_ref[...] = acc.astype(BF16)


def _feature_rows_spec(rows, d, first_row):
    def index_map(*g):
        return pl.multiple_of(jnp.asarray(first_row(*g), jnp.int32), SUBLANES), 0

    return pl.BlockSpec((pl.Element(rows), pl.Element(d)), index_map)


def _mod_spec(mod, d, chunk, group_of):
    return pl.BlockSpec((None, mod.shape[1], d), lambda *g: (group_of(*g), 0, chunk))


def _inproj_first(x2, mod, n1w, wts, tm):
    m, d = x2.shape
    tiles_per_group = (m // tm) // mod.shape[0]
    group_of = lambda i, j: i // tiles_per_group
    tn = PROJ_TN
    n_tiles = _J_GATE - _J_MIX
    return pl.pallas_call(
        _inproj_first_kernel,
        grid=(m // tm, n_tiles),
        in_specs=[pl.BlockSpec((tm, d), lambda i, j: (i, 0), pipeline_mode=pl.Buffered(1)),
                  _mod_spec(mod, d, MOD_SC1, group_of),
                  _mod_spec(mod, d, MOD_SH1, group_of),
                  pl.BlockSpec((1, d), lambda i, j: (0, 0)),
                  _feature_rows_spec(tn, d, lambda i, j: _tile_row(_J_MIX + j)),
                  pl.BlockSpec((1, tn), lambda i, j: (0, _J_MIX + j)),
                  _feature_rows_spec(SUBLANES, d, lambda i, j: _ROW_SMALL_M),
                  _feature_rows_spec(SUBLANES, d, lambda i, j: _ROW_SMALL_F),
                  pl.BlockSpec((SMALL_ROWS, 1), lambda i, j: (0, 0))],
        out_specs=[pl.BlockSpec((tm, d), lambda i, j: (i, 0)),
                   pl.BlockSpec((tm, tn), lambda i, j: (i, j)),
                   pl.BlockSpec((tm, LANES), lambda i, j: (i, 0)),
                   pl.BlockSpec((SMALL_ROWS, tm), lambda i, j: (0, i))],
        out_shape=[jax.ShapeDtypeStruct((m, d), BF16),
                   jax.ShapeDtypeStruct((m, n_tiles * tn), BF16),
                   jax.ShapeDtypeStruct((m, LANES), F32),
                   jax.ShapeDtypeStruct((SMALL_ROWS, m), F32)],
        compiler_params=_params("arbitrary", "arbitrary"),
        name="inproj",
    )(x2, mod, mod, n1w, wts["w_in_t"], wts["bbig"], wts["w_in_t"], wts["w_in_t"], wts["bsm"])


def _proj(h, wts, j0, j1, kind, tm, nw=None, scale=1.0, emit_f32=False):
    m, d = h.shape
    tn = PROJ_TN
    width = (j1 - j0) * tn
    tile = pl.BlockSpec((tm, tn), lambda j, i: (i, j))
    in_specs = [pl.BlockSpec((tm, d), lambda j, i: (i, 0)),
                _feature_rows_spec(tn, d, lambda j, i: _tile_row(j0 + j)),
                pl.BlockSpec((1, tn), lambda j, i: (0, j0 + j))]
    args = [h, wts["w_in_t"], wts["bbig"]]
    if kind == "headnorm":
        in_specs.append(pl.BlockSpec((1, F_DH), lambda j, i: (0, 0)))
        args.append(nw)
    two = emit_f32 or kind == "copy"
    out_shape = ([jax.ShapeDtypeStruct((m, width), F32)] if two else []) + [jax.ShapeDtypeStruct((m, width), BF16)]
    return pl.pallas_call(
        functools.partial(_proj_kernel, kind=kind, scale=scale, emit_f32=emit_f32),
        grid=(j1 - j0, m // tm),
        in_specs=in_specs,
        out_specs=[tile] * len(out_shape),
        out_shape=out_shape,
        scratch_shapes=[pltpu.VMEM((tn, d), BF16)],
        compiler_params=_params("arbitrary", "arbitrary"),
        name="proj_" + kind,
    )(*args)


def _cumsum_kernel(sm_ref, fcol_ref, carry_ref):
    c = pl.program_id(1)
    tc = sm_ref.shape[0]

    @pl.when(c == 0)
    def _():
        carry_ref[...] = jnp.zeros_like(carry_ref)

    fc = _dot_f32_rhs(_tri01(tc, "lower"), sm_ref[...]) + carry_ref[0:1, :]
    fcol_ref[...] = fc
    carry_ref[...] = jnp.broadcast_to(fc[tc - 1:tc, :], carry_ref.shape)


def _cumsum(sm, batch, tc=512):
    m = sm.shape[0]
    nt = (m // batch) // tc
    return pl.pallas_call(
        _cumsum_kernel,
        grid=(batch, nt),
        in_specs=[pl.BlockSpec((tc, LANES), lambda b, c: (b * nt + c, 0))],
        out_specs=pl.BlockSpec((tc, LANES), lambda b, c: (b * nt + c, 0)),
        out_shape=jax.ShapeDtypeStruct((m, LANES), F32),
        scratch_shapes=[pltpu.VMEM((SUBLANES, LANES), F32)],
        compiler_params=_params("arbitrary", "arbitrary"),
        name="cumsum",
    )(sm)


def _mlstm_kernel(q_ref, k_ref, v_ref, sm_ref, smt_ref, og_ref, hw_ref,
                  ym_ref, cout_ref, nm_ref, c_ref, n_ref, m_ref):
    c = pl.program_id(1)
    L = q_ref.shape[0]

    @pl.when(c == 0)
    def _():
        c_ref[...] = jnp.zeros_like(c_ref)
        n_ref[...] = jnp.zeros_like(n_ref)
        m_ref[...] = jnp.zeros_like(m_ref)

    r = lax.broadcasted_iota(jnp.int32, (L, L), 0)
    s_idx = lax.broadcasted_iota(jnp.int32, (L, L), 1)
    causal = s_idx <= r
    sm = sm_ref[...]
    smt = smt_ref[...]
    bcol_all = _dot_f32_rhs(_tri01(L, "lower"), sm)
    brow_all = _dot_f32_lhs(smt, _tri01(L, "upper"))

    for h in range(M_HEADS):
        ig_col = sm[:, IG0 + h:IG0 + h + 1]
        b_col = bcol_all[:, LFM0 + h:LFM0 + h + 1]
        ig_row = smt[IG0 + h:IG0 + h + 1, :]
        b_row = brow_all[LFM0 + h:LFM0 + h + 1, :]
        m0 = m_ref[h:h + 1, 0:1]
        a_row = ig_row - b_row
        cm_col = jnp.max(jnp.where(causal, a_row, -jnp.inf), axis=1, keepdims=True)
        mx_col = jnp.maximum(m0, cm_col)
        m_col = b_col + mx_col
        dmat = jnp.exp(jnp.where(causal, a_row - mx_col, -jnp.inf))
        inter = jnp.exp(m0 - mx_col)
        q = q_ref[:, h * M_DK:(h + 1) * M_DK]
        k = k_ref[:, h * M_DK:(h + 1) * M_DK]
        v = v_ref[:, h * M_DV:(h + 1) * M_DV]
        c0 = c_ref[h]
        n0 = n_ref[h:h + 1, :]
        s = _dot_nt(q, k) * dmat
        num = _dot(s.astype(BF16), v) + inter * _dot_nt(q, c0.astype(BF16))
        den = jnp.sum(s, axis=1, keepdims=True) + inter * jnp.sum(q.astype(F32) * n0, axis=1, keepdims=True)
        hh = num / jnp.maximum(jnp.abs(den), jnp.exp(-m_col))
        hn = hh * lax.rsqrt(jnp.mean(hh * hh, axis=1, keepdims=True) + NORM_EPS) * hw_ref[:, h * M_DV:(h + 1) * M_DV]
        ym_ref[:, h * M_DV:(h + 1) * M_DV] = (og_ref[:, h * M_DV:(h + 1) * M_DV] * hn).astype(BF16)
        m_new = m_col[L - 1:L, :]
        b_last = b_col[L - 1:L, :]
        w_end = jnp.exp(ig_col + b_last - b_col - m_new)
        decay = jnp.exp(b_last + m0 - m_new)
        vw = (w_end * v.astype(F32)).astype(BF16)
        c_ref[h] = decay * c0 + _dot_tn(vw, k)
        n_ref[h:h + 1, :] = decay * n0 + jnp.sum(w_end * k.astype(F32), axis=0, keepdims=True)
        m_ref[h:h + 1, :] = jnp.broadcast_to(m_new, (1, LANES))

    @pl.when(c == pl.num_programs(1) - 1)
    def _():
        cout_ref[...] = c_ref[...]
        nm_ref[0:M_HEADS, :] = n_ref[0:M_HEADS, :]
        nm_ref[M_HEADS:2 * M_HEADS, :] = m_ref[0:M_HEADS, :]


def _mlstm_prompt(qkv, sm, smt, sig, hw, batch):
    m = qkv.shape[0]
    og_block = sig.shape[1] // M_V - 1
    nc = (m // batch) // CHUNK
    row = lambda b, c: (b * nc + c, 0)
    return pl.pallas_call(
        _mlstm_kernel,
        grid=(batch, nc),
        in_specs=[pl.BlockSpec((CHUNK, M_QK), row),
                  pl.BlockSpec((CHUNK, M_QK), lambda b, c: (b * nc + c, 1)),
                  pl.BlockSpec((CHUNK, M_V), lambda b, c: (b * nc + c, 2 * M_QK // M_V)),
                  pl.BlockSpec((CHUNK, LANES), row),
                  pl.BlockSpec((SMALL_ROWS, CHUNK), lambda b, c: (0, b * nc + c)),
                  pl.BlockSpec((CHUNK, M_V), lambda b, c: (b * nc + c, og_block)),
                  pl.BlockSpec((1, M_V), lambda b, c: (0, 0))],
        out_specs=[pl.BlockSpec((CHUNK, M_V), row),
                   pl.BlockSpec((None, M_HEADS, M_DV, M_DK), lambda b, c: (b, 0, 0, 0)),
                   pl.BlockSpec((None, 2 * M_HEADS, M_DK), lambda b, c: (b, 0, 0))],
        out_shape=[jax.ShapeDtypeStruct((m, M_V), BF16),
                   jax.ShapeDtypeStruct((batch, M_HEADS, M_DV, M_DK), F32),
                   jax.ShapeDtypeStruct((batch, 2 * M_HEADS, M_DK), F32)],
        scratch_shapes=[pltpu.VMEM((M_HEADS, M_DV, M_DK), F32),
                        pltpu.VMEM((SUBLANES, M_DK), F32),
                        pltpu.VMEM((SUBLANES, LANES), F32)],
        compiler_params=_params("arbitrary", "arbitrary"),
        name="mlstm",
    )(qkv, qkv, qkv, sm, smt, sig, hw)


FOX_HEADS_PER_STEP = 2
FOX_ROW_CHUNK = 32


def _forget_columns(f, first):
    hi, mid, lo = _split3(f)
    lane = lax.broadcasted_iota(jnp.int32, (f.shape[0], LANES), 1)
    ones = jnp.where(jnp.abs(lane - (4 - first)) <= 1, 1.0, 0.0)
    ext = jnp.where(lane == first, hi.astype(F32),
                    jnp.where(lane == first + 1, mid.astype(F32),
                              jnp.where(lane == first + 2, lo.astype(F32), ones)))
    return ext.astype(BF16)


def _fox_kernel(q_ref, k_ref, v_ref, fcol_ref, o_ref, qaug_ref, kaug_ref, s_ref, p_ref, m_ref, l_ref, acc_ref):
    hp = pl.program_id(1)
    qi = pl.program_id(2)
    tq = q_ref.shape[0]
    t = k_ref.shape[0]
    nh = FOX_HEADS_PER_STEP

    def forget_sum(fc, h2):
        lane = lax.broadcasted_iota(jnp.int32, fc.shape, 1)
        sel = jnp.where(lane == LFF0 + nh * hp + h2, fc, 0.0)
        return jnp.sum(sel, axis=1, keepdims=True) * LOG2E

    @pl.when(qi == 0)
    def _():
        fc = fcol_ref[...]
        for h2 in range(nh):
            kaug_ref[h2, :, 0:F_DH] = k_ref[:, h2 * F_DH:(h2 + 1) * F_DH]
            kaug_ref[h2, :, F_DH:2 * F_DH] = _forget_columns(-forget_sum(fc, h2), 3)

    fq = fcol_ref[pl.ds(pl.multiple_of(qi * tq, tq), tq), :]
    for h2 in range(nh):
        qaug_ref[h2, :, 0:F_DH] = q_ref[:, h2 * F_DH:(h2 + 1) * F_DH]
        qaug_ref[h2, :, F_DH:2 * F_DH] = _forget_columns(forget_sum(fq, h2), 0)
    m_ref[...] = jnp.full_like(m_ref, -jnp.inf)
    l_ref[...] = jnp.zeros_like(l_ref)
    acc_ref[...] = jnp.zeros_like(acc_ref)
    rc = FOX_ROW_CHUNK
    n_lane_tiles = tq // LANES

    def softmax_rows(h2, c, masked):
        rows = slice(c * rc, (c + 1) * rc)
        s = s_ref[h2, rows, :]
        if masked:
            r = lax.broadcasted_iota(jnp.int32, (rc, tq), 0) + c * rc
            col = lax.broadcasted_iota(jnp.int32, (rc, tq), 1)
            s = jnp.where(r >= col, s, -jnp.inf)
        tiles = [s[:, k * LANES:(k + 1) * LANES] for k in range(n_lane_tiles)]
        m_old = m_ref[h2, rows, :]
        mx = functools.reduce(jnp.maximum, tiles)
        m_new = jnp.maximum(m_old, jnp.max(mx, axis=1, keepdims=True))
        alpha = jnp.exp2(m_old - m_new)
        ps = [jnp.exp2(tile - m_new) for tile in tiles]
        row_sum = jnp.sum(functools.reduce(jnp.add, ps), axis=1, keepdims=True)
        l_ref[h2, rows, :] = alpha * l_ref[h2, rows, :] + row_sum
        for k in range(n_lane_tiles):
            p_ref[h2, rows, k * LANES:(k + 1) * LANES] = ps[k].astype(BF16)
        acc_ref[h2, rows, :] = alpha * acc_ref[h2, rows, :]
        m_ref[h2, rows, :] = m_new

    def block(kj, masked):
        off = pl.multiple_of(kj * tq, tq)
        for h2 in range(nh):
            s_ref[h2] = _dot_nt(qaug_ref[h2], kaug_ref[h2, pl.ds(off, tq), :])
        for h2 in range(nh):
            for c in range(tq // rc):
                softmax_rows(h2, c, masked)
            acc_ref[h2] += _dot(p_ref[h2], v_ref[pl.ds(off, tq), h2 * F_DH:(h2 + 1) * F_DH])

    def body(kj, carry):
        block(kj, False)
        return carry

    lax.fori_loop(0, qi, body, 0)
    block(qi, True)
    for h2 in range(nh):
        o_ref[:, h2 * F_DH:(h2 + 1) * F_DH] = (acc_ref[h2] / l_ref[h2]).astype(BF16)


def _fox_prompt(qf, kfb, vfb, fcol, batch, tq=512):
    m = qf.shape[0]
    t = m // batch
    nq = t // tq
    nh = FOX_HEADS_PER_STEP
    wide = nh * F_DH
    return pl.pallas_call(
        _fox_kernel,
        grid=(batch, F_HEADS // nh, nq),
        in_specs=[pl.BlockSpec((tq, wide), lambda b, h, i: (b * nq + i, h)),
                  pl.BlockSpec((t, wide), lambda b, h, i: (b, h)),
                  pl.BlockSpec((t, wide), lambda b, h, i: (b, h)),
                  pl.BlockSpec((t, LANES), lambda b, h, i: (b, 0))],
        out_specs=pl.BlockSpec((tq, wide), lambda b, h, i: (b * nq + i, h)),
        out_shape=jax.ShapeDtypeStruct((m, F_W), BF16),
        scratch_shapes=[pltpu.VMEM((nh, tq, 2 * F_DH), BF16),
                        pltpu.VMEM((nh, t, 2 * F_DH), BF16),
                        pltpu.VMEM((nh, tq, tq), F32),
                        pltpu.VMEM((nh, tq, tq), BF16),
                        pltpu.VMEM((nh, tq, LANES), F32),
                        pltpu.VMEM((nh, tq, LANES), F32),
                        pltpu.VMEM((nh, tq, F_DH), F32)],
        compiler_params=_params("arbitrary", "arbitrary", "arbitrary"),
        name="fox",
    )(qf, kfb, vfb, fcol)


MSTEP_ROWS = 8


def _mstep_kernel(q_ref, k_ref, v_ref, sm_ref, og_ref, hw_ref, c0_ref, n0_ref, m0_ref,
                  ym_ref, cout_ref, nout_ref, mout_ref, cq_ref):
    rows = q_ref.shape[0]
    sm = sm_ref[...]
    ig = sm[:, IG0:IG0 + M_HEADS]
    lf = sm[:, LFM0:LFM0 + M_HEADS]
    m0 = m0_ref[...]
    m_new = lf + jnp.maximum(m0, ig - lf)
    w_end = jnp.exp(ig - m_new)
    decay = jnp.exp(lf + m0 - m_new)
    mout_ref[...] = m_new
    row_id = lax.broadcasted_iota(jnp.int32, (SUBLANES, M_DV), 0)

    for h in range(M_HEADS):
        qh = q_ref[:, h * M_DK:(h + 1) * M_DK]
        kh = k_ref[:, h * M_DK:(h + 1) * M_DK]
        vh = v_ref[:, h * M_DV:(h + 1) * M_DV]
        qf = qh.astype(F32)
        kf = kh.astype(F32)
        vf = vh.astype(F32)
        wh = w_end[:, h:h + 1]
        dh = decay[:, h:h + 1]
        for r in range(rows):
            c0 = c0_ref[r, h]
            q8 = jnp.broadcast_to(qh[r:r + 1, :], (SUBLANES, M_DK))
            cq_ref[r:r + 1, :] = _dot_nt(q8, c0.astype(BF16))[0:1, :]
            vw8 = jnp.where(row_id == 0, jnp.broadcast_to(wh[r:r + 1, :] * vf[r:r + 1, :], (SUBLANES, M_DV)), 0.0)
            k8 = jnp.broadcast_to(kh[r:r + 1, :], (SUBLANES, M_DK))
            cout_ref[r, h] = dh[r:r + 1, :] * c0 + _dot_tn(vw8.astype(BF16), k8)
        n0 = n0_ref[:, h, :]
        s = jnp.sum(qf * kf, axis=1, keepdims=True) * wh
        num = s * vf + dh * cq_ref[...]
        den = s + dh * jnp.sum(n0 * qf, axis=1, keepdims=True)
        hh = num / jnp.maximum(jnp.abs(den), jnp.exp(-m_new[:, h:h + 1]))
        hn = hh * lax.rsqrt(jnp.mean(hh * hh, axis=1, keepdims=True) + NORM_EPS) * hw_ref[:, h * M_DV:(h + 1) * M_DV]
        ym_ref[:, h * M_DV:(h + 1) * M_DV] = (og_ref[:, h * M_DV:(h + 1) * M_DV] * hn).astype(BF16)
        nout_ref[:, h, :] = dh * n0 + wh * kf


def _mlstm_step(qkv, sm, sig, hw, c0, n0, m0):
    db = qkv.shape[0]
    og_block = sig.shape[1] // M_V - 1
    rows = MSTEP_ROWS
    row = lambda i: (i, 0)
    return pl.pallas_call(
        _mstep_kernel,
        grid=(db // rows,),
        in_specs=[pl.BlockSpec((rows, M_QK), row),
                  pl.BlockSpec((rows, M_QK), lambda i: (i, 1)),
                  pl.BlockSpec((rows, M_V), lambda i: (i, 2 * M_QK // M_V)),
                  pl.BlockSpec((rows, LANES), row),
                  pl.BlockSpec((rows, M_V), lambda i: (i, og_block)),
                  pl.BlockSpec((1, M_V), lambda i: (0, 0)),
                  pl.BlockSpec((rows, M_HEADS, M_DV, M_DK), lambda i: (i, 0, 0, 0)),
                  pl.BlockSpec((rows, M_HEADS, M_DK), lambda i: (i, 0, 0)),
                  pl.BlockSpec((rows, M_HEADS), row)],
        out_specs=[pl.BlockSpec((rows, M_V), row),
                   pl.BlockSpec((rows, M_HEADS, M_DV, M_DK), lambda i: (i, 0, 0, 0)),
                   pl.BlockSpec((rows, M_HEADS, M_DK), lambda i: (i, 0, 0)),
                   pl.BlockSpec((rows, M_HEADS), row)],
        out_shape=[jax.ShapeDtypeStruct((db, M_V), BF16),
                   jax.ShapeDtypeStruct(c0.shape, F32),
                   jax.ShapeDtypeStruct(n0.shape, F32),
                   jax.ShapeDtypeStruct(m0.shape, F32)],
        scratch_shapes=[pltpu.VMEM((rows, M_DV), F32)],
        compiler_params=_params("arbitrary"),
        name="mstep",
    )(qkv, qkv, qkv, sm, sig, hw, c0, n0, m0)


def _decode_kernel(*refs, n_pages):
    k_refs = refs[1:1 + n_pages]
    v_refs = refs[1 + n_pages:1 + 2 * n_pages]
    lf_refs = refs[1 + 2 * n_pages:1 + 3 * n_pages]
    q_ref, kn_ref, vn_ref, sm_ref, o_ref, s_ref, xn_ref = refs[1 + 3 * n_pages:]
    page = xn_ref.shape[1]
    pw = page * F_HEADS

    for p in range(n_pages):
        xn_ref[p * F_HEADS:(p + 1) * F_HEADS, :] = lf_refs[p][...]
    xn = xn_ref[...] * LOG2E
    t_id = lax.broadcasted_iota(jnp.int32, (page, pw), 0)
    l_id = lax.broadcasted_iota(jnp.int32, (page, pw), 1)
    widen = jnp.where((l_id // F_HEADS) == t_id, 1.0, 0.0).astype(BF16)
    wide = _dot_f32_lhs(xn, widen)
    r_id = lax.broadcasted_iota(jnp.int32, wide.shape, 0)
    c_id = lax.broadcasted_iota(jnp.int32, wide.shape, 1)
    own = jnp.where((r_id & (F_HEADS - 1)) == (c_id & (F_HEADS - 1)), wide, 0.0)
    g_p = lax.broadcasted_iota(jnp.int32, (n_pages, n_pages * F_HEADS), 0)
    g_r = lax.broadcasted_iota(jnp.int32, (n_pages, n_pages * F_HEADS), 1)
    gather = jnp.where((g_r // F_HEADS) == g_p, 1.0, 0.0).astype(BF16)
    x = _dot_f32_rhs(gather, own)
    lane = lax.broadcasted_iota(jnp.int32, x.shape, 1)
    y, z = x, x
    step = F_HEADS
    while step < pw:
        y = y + jnp.where(lane < pw - step, pltpu.roll(y, pw - step, axis=1), 0.0)
        z = z + pltpu.roll(z, step, axis=1)
        step *= 2
    after = y - x

    r = lax.broadcasted_iota(jnp.int32, (LANES, pw), 0)
    c = lax.broadcasted_iota(jnp.int32, (LANES, pw), 1)
    spread = jnp.where(r == LFF0 + (c & (F_HEADS - 1)), 1.0, 0.0).astype(BF16)
    run = _dot_f32_lhs(jnp.broadcast_to(sm_ref[...] * LOG2E, (SUBLANES, LANES)), spread)[0:1, :]

    q = q_ref[...]
    sub = lax.broadcasted_iota(jnp.int32, (F_HEADS, pw), 0)
    lane_h = lax.broadcasted_iota(jnp.int32, (F_HEADS, pw), 1) & (F_HEADS - 1)
    valid = lane_h == sub
    s_self = jnp.sum(q.astype(F32) * kn_ref[...], axis=1, keepdims=True)
    m_run = s_self
    for p in reversed(range(n_pages)):
        kb = k_refs[p][...].astype(BF16)
        st = _dot_nt(q, kb) + (run + after[p:p + 1, :])
        st = jnp.where(valid, st, -jnp.inf)
        s_ref[p] = st
        m_run = jnp.maximum(m_run, jnp.max(st, axis=1, keepdims=True))
        run = run + z[p:p + 1, :]
    w_self = jnp.exp2(s_self - m_run)
    l = w_self
    acc = w_self * vn_ref[...]
    for p in range(n_pages):
        pe = jnp.exp2(s_ref[p] - m_run)
        l = l + jnp.sum(pe, axis=1, keepdims=True)
        acc = acc + _dot(pe.astype(BF16), v_refs[p][...].astype(BF16))
    o_ref[...] = (acc / l).astype(BF16)


def _fox_decode(page_table, ck, cv, clf, layer, q3, k3, v3, sm3):
    db, n_pages = page_table.shape
    pw = ck.shape[2]
    page = clf.shape[3]
    seq = lambda b, pt: (b, 0, 0)

    def pg(p):
        return lambda b, pt: (layer, pt[b, p], 0, 0)

    in_specs = ([pl.BlockSpec((None, None, pw, F_DH), pg(p)) for p in range(n_pages)]
                + [pl.BlockSpec((None, None, pw, F_DH), pg(p)) for p in range(n_pages)]
                + [pl.BlockSpec((None, None, F_HEADS, page), pg(p)) for p in range(n_pages)]
                + [pl.BlockSpec((None, F_HEADS, F_DH), seq),
                   pl.BlockSpec((None, F_HEADS, F_DH), seq),
                   pl.BlockSpec((None, F_HEADS, F_DH), seq),
                   pl.BlockSpec((None, 1, LANES), seq)])
    grid_spec = pltpu.PrefetchScalarGridSpec(
        num_scalar_prefetch=1,
        grid=(db,),
        in_specs=in_specs,
        out_specs=pl.BlockSpec((None, F_HEADS, F_DH), seq),
        scratch_shapes=[pltpu.VMEM((n_pages, F_HEADS, pw), F32),
                        pltpu.VMEM((n_pages * F_HEADS, page), F32)],
    )
    return pl.pallas_call(
        functools.partial(_decode_kernel, n_pages=n_pages),
        grid_spec=grid_spec,
        out_shape=jax.ShapeDtypeStruct((db, F_HEADS, F_DH), BF16),
        compiler_params=_params("arbitrary"),
        name="decode",
    )(page_table, *([ck] * n_pages), *([cv] * n_pages), *([clf] * n_pages), q3, k3, v3, sm3)


def _merge_kernel(ym_ref, yf_ref, sga_ref, sgb_ref, x_ref, g1_ref, sc_ref, sh_ref, n2w_ref,
                  wa_ref, wb_ref, wo_ref, x1_ref, h2_ref):
    merged = sga_ref[...] * _dot(ym_ref[...], wa_ref[...]) + sgb_ref[...] * _dot(yf_ref[...], wb_ref[...])
    x1 = x_ref[...] + g1_ref[...] * _dot(merged.astype(BF16), wo_ref[...])
    x1_ref[...] = x1
    rstd = lax.rsqrt(jnp.mean(x1 * x1, axis=-1, keepdims=True) + NORM_EPS)
    h2_ref[...] = ((x1 * rstd * n2w_ref[...]) * (1.0 + sc_ref[...]) + sh_ref[...]).astype(BF16)


def _merge(ym, yf, sig, x2, mod, n2w, wts, tm):
    m, d = x2.shape
    tiles_per_group = (m // tm) // mod.shape[0]
    group_of = lambda i: i // tiles_per_group
    row = lambda i: (i, 0)
    const = lambda i: (0, 0)
    resident = dict(pipeline_mode=pl.Buffered(1))
    return pl.pallas_call(
        _merge_kernel,
        grid=(m // tm,),
        in_specs=[pl.BlockSpec((tm, M_V), row),
                  pl.BlockSpec((tm, F_W), row),
                  pl.BlockSpec((tm, d), row),
                  pl.BlockSpec((tm, d), lambda i: (i, 1)),
                  pl.BlockSpec((tm, d), row),
                  _mod_spec(mod, d, MOD_G1, group_of),
                  _mod_spec(mod, d, MOD_SC2, group_of),
                  _mod_spec(mod, d, MOD_SH2, group_of),
                  pl.BlockSpec((1, d), const),
                  pl.BlockSpec((M_V, d), const, **resident),
                  pl.BlockSpec((F_W, d), const, **resident),
                  pl.BlockSpec((d, d), const, **resident)],
        out_specs=[pl.BlockSpec((tm, d), row), pl.BlockSpec((tm, d), row)],
        out_shape=[jax.ShapeDtypeStruct((m, d), F32), jax.ShapeDtypeStruct((m, d), BF16)],
        compiler_params=_params("arbitrary"),
        name="merge",
    )(ym, yf, sig, sig, x2, mod, mod, mod, n2w, wts["wa"], wts["wb"], wts["wo"])


FFN_TF = 512


def _ffn_kernel(h2_ref, x1_ref, g2_ref, p_ref, wa_ref, wg_ref, cw_ref, cb_ref, w2_ref,
                y_ref, a_out_ref, carry_ref, *, seq):
    f = pl.program_id(2)
    tm = h2_ref.shape[0]

    @pl.when(f == 0)
    def _():
        y_ref[...] = jnp.zeros_like(y_ref)

    if seq:
        @pl.when(pl.program_id(1) == 0)
        def _():
            carry_ref[f] = p_ref[...]

    h2 = h2_ref[...]
    a = _dot(h2, wa_ref[...])
    g = _dot(h2, wg_ref[...])
    if seq:
        prev = carry_ref[f]
        p2, p1 = prev[SUBLANES - 2:SUBLANES - 1, :], prev[SUBLANES - 1:SUBLANES, :]
        r = lax.broadcasted_iota(jnp.int32, a.shape, 0)
        a1 = jnp.where(r == 0, p1, pltpu.roll(a, 1, axis=0))
        a2 = jnp.where(r == 0, p2, jnp.where(r == 1, p1, pltpu.roll(a, 2, axis=0)))
        carry_ref[f] = a[tm - SUBLANES:tm, :]
        a_out_ref[...] = a[tm - SUBLANES:tm, :]
    else:
        a2, a1 = p_ref[0], p_ref[1]
        a_out_ref[...] = a
    ac = cw_ref[0:1, :] * a2 + cw_ref[1:2, :] * a1 + cw_ref[2:3, :] * a + cb_ref[...]
    u = (jax.nn.gelu(ac, approximate=True) * g).astype(BF16)
    y_ref[...] += _dot(u, w2_ref[...])

    @pl.when(f == pl.num_programs(2) - 1)
    def _():
        y_ref[...] = x1_ref[...] + g2_ref[...] * y_ref[...]


def _ffn(h2, x1, mod, prev, wts, tm, seq):
    m, d = x1.shape
    groups = mod.shape[0]
    tiles = (m // tm) // groups
    dff = wts["w2"].shape[0]
    tf = FFN_TF
    nf = dff // tf
    row = lambda b, i, f: (b * tiles + i, 0)
    if seq:
        prev_spec = pl.BlockSpec((None, SUBLANES, tf), lambda b, i, f: (b, 0, f))
        a_spec = pl.BlockSpec((None, None, SUBLANES, tf), lambda b, i, f: (b, i, 0, f))
        a_shape = jax.ShapeDtypeStruct((groups, tiles, SUBLANES, dff), F32)
    else:
        prev_spec = pl.BlockSpec((CONV_W - 1, tm, tf), lambda b, i, f: (0, i, f))
        a_spec = pl.BlockSpec((tm, tf), lambda b, i, f: (i, f))
        a_shape = jax.ShapeDtypeStruct((m, dff), F32)
    return pl.pallas_call(
        functools.partial(_ffn_kernel, seq=seq),
        grid=(groups, tiles, nf),
        in_specs=[pl.BlockSpec((tm, d), row),
                  pl.BlockSpec((tm, d), row, pipeline_mode=pl.Buffered(1)),
                  _mod_spec(mod, d, MOD_G2, lambda b, i, f: b),
                  prev_spec,
                  pl.BlockSpec((d, tf), lambda b, i, f: (0, f)),
                  pl.BlockSpec((d, tf), lambda b, i, f: (0, f + nf)),
                  pl.BlockSpec((CONV_W, tf), lambda b, i, f: (0, f)),
                  pl.BlockSpec((1, tf), lambda b, i, f: (0, f)),
                  pl.BlockSpec((tf, d), lambda b, i, f: (f, 0))],
        out_specs=[pl.BlockSpec((tm, d), row), a_spec],
        out_shape=[jax.ShapeDtypeStruct((m, d), F32), a_shape],
        scratch_shapes=[pltpu.VMEM((nf, SUBLANES, tf), F32)],
        compiler_params=_params("arbitrary", "arbitrary", "arbitrary"),
        name="ffn_seq" if seq else "ffn_row",
    )(h2, x1, mod, prev, wts["w1"], wts["w1"], wts["cw"], wts["cb"], wts["w2"])


def _prep_weights(w_in, b_in, f_qnorm_w, f_knorm_w, w_proj_a, w_proj_b, w_out, w_ffn_in, conv_w, conv_b, w_ffn_out):
    o = _SPLIT_OFFS
    bbig = jnp.concatenate([b_in[:o[3]], b_in[o[10]:], b_in[o[3]:o[4]], b_in[o[6]:o[9]]])
    bsm = jnp.concatenate([b_in[o[4]:o[6]], b_in[o[9]:o[10]]])
    return {
        "w_in_t": w_in.T,
        "bbig": bbig.reshape(1, -1),
        "bsm": bsm.reshape(SMALL_ROWS, 1),
        "qnw": f_qnorm_w.reshape(1, F_DH),
        "knw": f_knorm_w.reshape(1, F_DH),
        "wa": w_proj_a.astype(BF16),
        "wb": w_proj_b.astype(BF16),
        "wo": w_out.astype(BF16),
        "w1": w_ffn_in.astype(BF16),
        "cw": conv_w,
        "cb": conv_b.reshape(1, -1),
        "w2": w_ffn_out.astype(BF16),
    }


def kernel(x_prompt, x_sample, c_prompt, c_sample, cache_k, cache_v, cache_logf, page_table, state_C, state_n, state_m, state_conv, w_ada, b_ada, norm1_w, w_in, b_in, m_hnorm_w, f_qnorm_w, f_knorm_w, w_proj_a, w_proj_b, w_out, norm2_w, w_ffn_in, conv_w, conv_b, w_ffn_out):
    depth = w_ada.shape[0]
    assert depth == 1, "single-layer step"
    batch, seq_len, d = x_prompt.shape
    db = x_sample.shape[0]
    assert x_sample.shape[1] == 1
    dff = w_ffn_out.shape[1]
    layer = 0

    wts = _prep_weights(w_in[layer], b_in[layer], f_qnorm_w[layer], f_knorm_w[layer], w_proj_a[layer],
                        w_proj_b[layer], w_out[layer], w_ffn_in[layer], conv_w[layer], conv_b[layer],
                        w_ffn_out[layer])
    n1w = norm1_w[layer].reshape(1, d)
    n2w = norm2_w[layer].reshape(1, d)
    hw = m_hnorm_w[layer].reshape(1, M_V)

    mod_p, mod_s = _ada(c_prompt, c_sample, w_ada[layer], b_ada[layer])
    mod_p = mod_p.reshape(batch, 1, 6 * d)
    mod_s = mod_s.reshape(1, db, 6 * d)

    def in_projection(x2, mod, tm):
        h, qkv, sm, smt = _inproj_first(x2, mod, n1w, wts, tm)
        (sig,) = _proj(h, wts, _J_GATE, _J_FQ, "sigmoid", tm)
        (qf,) = _proj(h, wts, _J_FQ, _J_FK, "headnorm", tm, nw=wts["qnw"], scale=FOX_Q_SCALE)
        kf, kfb = _proj(h, wts, _J_FK, _J_FV, "headnorm", tm, nw=wts["knw"], emit_f32=True)
        vf, vfb = _proj(h, wts, _J_FV, _J_END, "copy", tm)
        return qkv, sig, qf, kf, kfb, vf, vfb, sm, smt

    tm_p = min(1024, seq_len)
    xp = x_prompt.reshape(batch * seq_len, d)
    qkv, sig, qf, kf, kfb, vf, vfb, sm, smt = in_projection(xp, mod_p, tm_p)
    fcol = _cumsum(sm, batch)
    ym, c_p, nm_p = _mlstm_prompt(qkv, sm, smt, sig, hw, batch)
    yf = _fox_prompt(qf, kfb, vfb, fcol, batch)
    x1, h2 = _merge(ym, yf, sig, xp, mod_p, n2w, wts, min(512, seq_len))
    conv0 = jnp.zeros((batch, SUBLANES, dff), F32)
    yp, a_last = _ffn(h2, x1, mod_p, conv0, wts, tm_p, seq=True)

    y_prompt = yp.reshape(batch, seq_len, d)
    k_prompt = kf.reshape(depth, batch, seq_len, F_HEADS, F_DH)
    v_prompt = vf.reshape(depth, batch, seq_len, F_HEADS, F_DH)
    logf_prompt = sm[:, LFF0:LFF0 + F_HEADS].reshape(depth, batch, seq_len, F_HEADS)
    c_prompt_out = c_p[None]
    n_prompt = nm_p[:, :M_HEADS, :][None]
    m_prompt = nm_p[:, M_HEADS:, 0][None]
    conv_prompt = a_last[:, -1, SUBLANES - (CONV_W - 1):, :][None]

    xs = x_sample.reshape(db, d)
    qkv_s, sig_s, qf_s, kf_s, _, vf_s, _, sm_s, _ = in_projection(xs, mod_s, db)
    ym_s, c_s, n_s, m_s = _mlstm_step(qkv_s, sm_s, sig_s, hw, state_C[layer], state_n[layer], state_m[layer])
    n_pool, page = cache_k.shape[1], cache_k.shape[2]
    yf_s = _fox_decode(page_table,
                       cache_k.reshape(depth, n_pool, page * F_HEADS, F_DH),
                       cache_v.reshape(depth, n_pool, page * F_HEADS, F_DH),
                       cache_logf.transpose(0, 1, 3, 2),
                       layer,
                       qf_s.reshape(db, F_HEADS, F_DH), kf_s.reshape(db, F_HEADS, F_DH),
                       vf_s.reshape(db, F_HEADS, F_DH), sm_s.reshape(db, 1, LANES)).reshape(db, F_W)
    x1_s, h2_s = _merge(ym_s, yf_s, sig_s, xs, mod_s, n2w, wts, db)
    prev_s = state_conv[layer].transpose(1, 0, 2)
    ys, a_s = _ffn(h2_s, x1_s, mod_s, prev_s, wts, db, seq=False)

    y_sample = ys.reshape(db, 1, d)
    k_sample = kf_s.reshape(depth, db, 1, F_HEADS, F_DH)
    v_sample = vf_s.reshape(depth, db, 1, F_HEADS, F_DH)
    logf_sample = sm_s[:, LFF0:LFF0 + F_HEADS].reshape(depth, db, 1, F_HEADS)
    conv_sample = jnp.stack([state_conv[layer][:, CONV_W - 2, :], a_s], axis=1)[None]

    return (y_prompt, y_sample, k_prompt, v_prompt, logf_prompt, c_prompt_out, n_prompt, m_prompt, conv_prompt,
            k_sample, v_sample, logf_sample, c_s[None], n_s[None], m_s[None], conv_sample)
```

```python
import functools

import jax
import jax.numpy as jnp
from jax import lax
from jax.experimental import pallas as pl
from jax.experimental.pallas import tpu as pltpu

F32 = jnp.float32
BF16 = jnp.bfloat16

NORM_EPS = 1e-6
M_HEADS, M_DK, M_DV = 4, 128, 256
F_HEADS, F_DH = 8, 128
M_QK = M_HEADS * M_DK
M_V = M_HEADS * M_DV
F_W = F_HEADS * F_DH
CONV_W = 3
CHUNK = 128
LANES = 128
SUBLANES = 8
SMALL_ROWS = 16
VMEM_LIMIT = 56 * 1024 * 1024

IG0, LFM0, LFF0 = 0, M_HEADS, 2 * M_HEADS

LOG2E = 1.4426950408889634
FOX_Q_SCALE = F_DH ** -0.5 * LOG2E


def _dot(a, b):
    return jnp.dot(a, b, preferred_element_type=F32)


def _dot_nt(a, b):
    return lax.dot_general(a, b, (((1,), (1,)), ((), ())), preferred_element_type=F32)


def _dot_tn(a, b):
    return lax.dot_general(a, b, (((0,), (0,)), ((), ())), preferred_element_type=F32)


def _split3(x):
    hi = x.astype(BF16)
    r1 = x - hi.astype(F32)
    mid = r1.astype(BF16)
    lo = (r1 - mid.astype(F32)).astype(BF16)
    return hi, mid, lo


def _dot_f32_lhs(x, mat01):
    hi, mid, lo = _split3(x)
    return _dot(hi, mat01) + _dot(mid, mat01) + _dot(lo, mat01)


def _dot_f32_rhs(mat01, x):
    hi, mid, lo = _split3(x)
    return _dot(mat01, hi) + _dot(mat01, mid) + _dot(mat01, lo)


def _log_sigmoid(x):
    return jnp.minimum(x, 0.0) - jnp.log1p(jnp.exp(-jnp.abs(x)))


def _tri01(n, kind):
    r = lax.broadcasted_iota(jnp.int32, (n, n), 0)
    c = lax.broadcasted_iota(jnp.int32, (n, n), 1)
    if kind == "lower":
        m = c <= r
    elif kind == "upper":
        m = r <= c
    else:
        m = r > c
    return jnp.where(m, 1.0, 0.0).astype(BF16)


def _params(*sem):
    return pltpu.CompilerParams(dimension_semantics=sem, vmem_limit_bytes=VMEM_LIMIT)


def _ada_kernel(cp_ref, cs_ref, w_ref, b_ref, op_ref, os_ref):
    w = w_ref[...].astype(BF16)
    for c_ref, o_ref in ((cp_ref, op_ref), (cs_ref, os_ref)):
        c = c_ref[...]
        s = (c * jax.nn.sigmoid(c)).astype(BF16)
        o_ref[...] = _dot(s, w) + b_ref[...]


def _ada(c_p, c_s, w_ada, b_ada, tn=1024):
    d = c_p.shape[1]
    n = w_ada.shape[1]
    return pl.pallas_call(
        _ada_kernel,
        grid=(n // tn,),
        in_specs=[pl.BlockSpec(c_p.shape, lambda j: (0, 0)),
                  pl.BlockSpec(c_s.shape, lambda j: (0, 0)),
                  pl.BlockSpec((d, tn), lambda j: (0, j)),
                  pl.BlockSpec((1, tn), lambda j: (0, j))],
        out_specs=[pl.BlockSpec((c_p.shape[0], tn), lambda j: (0, j)),
                   pl.BlockSpec((c_s.shape[0], tn), lambda j: (0, j))],
        out_shape=[jax.ShapeDtypeStruct((c_p.shape[0], n), F32), jax.ShapeDtypeStruct((c_s.shape[0], n), F32)],
        compiler_params=_params("arbitrary"),
        name="ada",
    )(c_p, c_s, w_ada, b_ada.reshape(1, n))


MOD_SH1, MOD_SC1, MOD_G1, MOD_SH2, MOD_SC2, MOD_G2 = range(6)


PROJ_TN = 1024
_J_MIX, _J_GATE, _J_FQ, _J_FK, _J_FV, _J_END = 0, 2, 7, 8, 9, 10
_J_MO = _J_GATE + 4
_SPLIT_SIZES = (M_QK, M_QK, M_V, M_V, M_HEADS, M_HEADS, F_W, F_W, F_W, F_HEADS)
_SPLIT_OFFS = tuple(sum(_SPLIT_SIZES[:i]) for i in range(len(_SPLIT_SIZES) + 1))
_ROW_MO, _ROW_SMALL_M, _ROW_FQ, _ROW_SMALL_F, _ROW_GATES = (_SPLIT_OFFS[3], _SPLIT_OFFS[4], _SPLIT_OFFS[6],
                                                            _SPLIT_OFFS[9], _SPLIT_OFFS[10])
_TILE_RUNS = ((_J_MIX, 0), (_J_GATE, _ROW_GATES), (_J_MO, _ROW_MO), (_J_FQ, _ROW_FQ))


def _tile_row(jj):
    row = _TILE_RUNS[0][1] + PROJ_TN * (jj - _TILE_RUNS[0][0])
    for t0, r0 in _TILE_RUNS[1:]:
        row = jnp.where(jj >= t0, r0 + PROJ_TN * (jj - t0), row)
    return row


def _norm_kernel(x_ref, sc_ref, sh_ref, n1w_ref, wsm_m_ref, wsm_f_ref, bsm_ref, h_ref, sm_ref, smt_ref):
    x = x_ref[...]
    rstd = lax.rsqrt(jnp.mean(x * x, axis=-1, keepdims=True) + NORM_EPS)
    h = (x * rstd * n1w_ref[...]) * (1.0 + sc_ref[...]) + sh_ref[...]
    hb = h.astype(BF16)
    h_ref[...] = hb
    wsm = jnp.concatenate([wsm_m_ref[...], wsm_f_ref[...]], axis=0).astype(BF16)
    smt = _dot_nt(wsm, hb) + bsm_ref[...]
    row = lax.broadcasted_iota(jnp.int32, smt.shape, 0)
    smt = jnp.where(row < LFM0, smt, _log_sigmoid(smt))
    smt_ref[...] = smt
    pad = jnp.zeros((LANES - SMALL_ROWS, smt.shape[1]), F32)
    sm_ref[...] = jnp.concatenate([smt, pad], axis=0).T


def _proj_kernel(h_ref, w_ref, b_ref, *rest, kind, scale, emit_f32):
    *rest, wb_ref = rest

    @pl.when(pl.program_id(1) == 0)
    def _():
        wb_ref[...] = w_ref[...].astype(BF16)

    acc = _dot_nt(h_ref[...], wb_ref[...]) + b_ref[...]
    if kind == "mix":
        (ob_ref,) = rest
        lane = lax.broadcasted_iota(jnp.int32, (1, acc.shape[1]), 1)
        is_mk = jnp.logical_and(pl.program_id(0) == 0, lane >= M_QK)
        ob_ref[...] = (acc * jnp.where(is_mk, M_DK ** -0.5, 1.0)).astype(BF16)
    elif kind == "sigmoid":
        (ob_ref,) = rest
        ob_ref[...] = jax.nn.sigmoid(acc).astype(BF16)
    elif kind == "headnorm":
        nw_ref, *outs = rest
        for h in range(acc.shape[1] // F_DH):
            sl = acc[:, h * F_DH:(h + 1) * F_DH]
            ms = jnp.mean(sl * sl, axis=-1, keepdims=True)
            y = sl * lax.rsqrt(ms + NORM_EPS) * nw_ref[...]
            if emit_f32:
                outs[0][:, h * F_DH:(h + 1) * F_DH] = y
            outs[-1][:, h * F_DH:(h + 1) * F_DH] = (y * scale).astype(BF16)
    else:
        of_ref, ob_ref = rest
        of_ref[...] = acc
        ob_ref[...] = acc.astype(BF16)


def _feature_rows_spec(rows, d, first_row):
    def index_map(*g):
        return pl.multiple_of(jnp.asarray(first_row(*g), jnp.int32), SUBLANES), 0

    return pl.BlockSpec((pl.Element(rows), pl.Element(d)), index_map)


def _mod_spec(mod, d, chunk, group_of):
    return pl.BlockSpec((None, mod.shape[1], d), lambda *g: (group_of(*g), 0, chunk))


def _norm(x2, mod, n1w, wts, tm):
    m, d = x2.shape
    tiles_per_group = (m // tm) // mod.shape[0]
    group_of = lambda i: i // tiles_per_group
    return pl.pallas_call(
        _norm_kernel,
        grid=(m // tm,),
        in_specs=[pl.BlockSpec((tm, d), lambda i: (i, 0)),
                  _mod_spec(mod, d, MOD_SC1, group_of),
                  _mod_spec(mod, d, MOD_SH1, group_of),
                  pl.BlockSpec((1, d), lambda i: (0, 0)),
                  _feature_rows_spec(SUBLANES, d, lambda i: _ROW_SMALL_M),
                  _feature_rows_spec(SUBLANES, d, lambda i: _ROW_SMALL_F),
                  pl.BlockSpec((SMALL_ROWS, 1), lambda i: (0, 0))],
        out_specs=[pl.BlockSpec((tm, d), lambda i: (i, 0)),
                   pl.BlockSpec((tm, LANES), lambda i: (i, 0)),
                   pl.BlockSpec((SMALL_ROWS, tm), lambda i: (0, i))],
        out_shape=[jax.ShapeDtypeStruct((m, d), BF16),
                   jax.ShapeDtypeStruct((m, LANES), F32),
                   jax.ShapeDtypeStruct((SMALL_ROWS, m), F32)],
        compiler_params=_params("arbitrary"),
        name="norm",
    )(x2, mod, mod, n1w, wts["w_in_t"], wts["w_in_t"], wts["bsm"])


def _proj(h, wts, j0, j1, kind, tm, nw=None, scale=1.0, emit_f32=False):
    m, d = h.shape
    tn = PROJ_TN
    width = (j1 - j0) * tn
    tile = pl.BlockSpec((tm, tn), lambda j, i: (i, j))
    in_specs = [pl.BlockSpec((tm, d), lambda j, i: (i, 0)),
                _feature_rows_spec(tn, d, lambda j, i: _tile_row(j0 + j)),
                pl.BlockSpec((1, tn), lambda j, i: (0, j0 + j))]
    args = [h, wts["w_in_t"], wts["bbig"]]
    if kind == "headnorm":
        in_specs.append(pl.BlockSpec((1, F_DH), lambda j, i: (0, 0)))
        args.append(nw)
    two = emit_f32 or kind == "copy"
    out_shape = ([jax.ShapeDtypeStruct((m, width), F32)] if two else []) + [jax.ShapeDtypeStruct((m, width), BF16)]
    return pl.pallas_call(
        functools.partial(_proj_kernel, kind=kind, scale=scale, emit_f32=emit_f32),
        grid=(j1 - j0, m // tm),
        in_specs=in_specs,
        out_specs=[tile] * len(out_shape),
        out_shape=out_shape,
        scratch_shapes=[pltpu.VMEM((tn, d), BF16)],
        compiler_params=_params("arbitrary", "arbitrary"),
        name="proj_" + kind,
    )(*args)


def _cumsum_kernel(sm_ref, fcol_ref, carry_ref):
    c = pl.program_id(1)
    tc = sm_ref.shape[0]

    @pl.when(c == 0)
    def _():
        carry_ref[...] = jnp.zeros_like(carry_ref)

    fc = _dot_f32_rhs(_tri01(tc, "lower"), sm_ref[...]) + carry_ref[0:1, :]
    fcol_ref[...] = fc
    carry_ref[...] = jnp.broadcast_to(fc[tc - 1:tc, :], carry_ref.shape)


def _cumsum(sm, batch, tc=512):
    m = sm.shape[0]
    nt = (m // batch) // tc
    return pl.pallas_call(
        _cumsum_kernel,
        grid=(batch, nt),
        in_specs=[pl.BlockSpec((tc, LANES), lambda b, c: (b * nt + c, 0))],
        out_specs=pl.BlockSpec((tc, LANES), lambda b, c: (b * nt + c, 0)),
        out_shape=jax.ShapeDtypeStruct((m, LANES), F32),
        scratch_shapes=[pltpu.VMEM((SUBLANES, LANES), F32)],
        compiler_params=_params("arbitrary", "arbitrary"),
        name="cumsum",
    )(sm)


def _mlstm_kernel(*refs):
    _mlstm_chunk(pl.program_id(1), pl.num_programs(1) - 1, range(M_HEADS), *refs)


def _mlstm_chunk(c, last_c, heads, q_ref, k_ref, v_ref, sm_ref, smt_ref, og_ref, hw_ref,
                 ym_ref, cout_ref, nm_ref, c_ref, n_ref, m_ref):
    L = q_ref.shape[0]

    @pl.when(c == 0)
    def _():
        for h in heads:
            c_ref[h] = jnp.zeros((M_DV, M_DK), F32)
            n_ref[h:h + 1, :] = jnp.zeros((1, M_DK), F32)
            m_ref[h:h + 1, :] = jnp.zeros((1, LANES), F32)

    r = lax.broadcasted_iota(jnp.int32, (L, L), 0)
    s_idx = lax.broadcasted_iota(jnp.int32, (L, L), 1)
    causal = s_idx <= r
    sm = sm_ref[...]
    smt = smt_ref[...]
    bcol_all = _dot_f32_rhs(_tri01(L, "lower"), sm)
    brow_all = _dot_f32_lhs(smt, _tri01(L, "upper"))

    for h in heads:
        ig_col = sm[:, IG0 + h:IG0 + h + 1]
        b_col = bcol_all[:, LFM0 + h:LFM0 + h + 1]
        ig_row = smt[IG0 + h:IG0 + h + 1, :]
        b_row = brow_all[LFM0 + h:LFM0 + h + 1, :]
        m0 = m_ref[h:h + 1, 0:1]
        a_row = ig_row - b_row
        cm_col = jnp.max(jnp.where(causal, a_row, -jnp.inf), axis=1, keepdims=True)
        mx_col = jnp.maximum(m0, cm_col)
        m_col = b_col + mx_col
        dmat = jnp.exp(jnp.where(causal, a_row - mx_col, -jnp.inf))
        inter = jnp.exp(m0 - mx_col)
        q = q_ref[:, h * M_DK:(h + 1) * M_DK]
        k = k_ref[:, h * M_DK:(h + 1) * M_DK]
        v = v_ref[:, h * M_DV:(h + 1) * M_DV]
        c0 = c_ref[h]
        n0 = n_ref[h:h + 1, :]
        s = _dot_nt(q, k) * dmat
        num = _dot(s.astype(BF16), v) + inter * _dot_nt(q, c0.astype(BF16))
        den = jnp.sum(s, axis=1, keepdims=True) + inter * jnp.sum(q.astype(F32) * n0, axis=1, keepdims=True)
        hh = num / jnp.maximum(jnp.abs(den), jnp.exp(-m_col))
        hn = hh * lax.rsqrt(jnp.mean(hh * hh, axis=1, keepdims=True) + NORM_EPS) * hw_ref[:, h * M_DV:(h + 1) * M_DV]
        ym_ref[:, h * M_DV:(h + 1) * M_DV] = (og_ref[:, h * M_DV:(h + 1) * M_DV] * hn).astype(BF16)
        m_new = m_col[L - 1:L, :]
        b_last = b_col[L - 1:L, :]
        w_end = jnp.exp(ig_col + b_last - b_col - m_new)
        decay = jnp.exp(b_last + m0 - m_new)
        vw = (w_end * v.astype(F32)).astype(BF16)
        c_ref[h] = decay * c0 + _dot_tn(vw, k)
        n_ref[h:h + 1, :] = decay * n0 + jnp.sum(w_end * k.astype(F32), axis=0, keepdims=True)
        m_ref[h:h + 1, :] = jnp.broadcast_to(m_new, (1, LANES))

    @pl.when(c == last_c)
    def _():
        for h in heads:
            cout_ref[h] = c_ref[h]
            nm_ref[h:h + 1, :] = n_ref[h:h + 1, :]
            nm_ref[M_HEADS + h:M_HEADS + h + 1, :] = m_ref[h:h + 1, :]


def _mlstm_prompt(qkv, sm, smt, sig, hw, batch):
    m = qkv.shape[0]
    og_block = sig.shape[1] // M_V - 1
    nc = (m // batch) // CHUNK
    row = lambda b, c: (b * nc + c, 0)
    return pl.pallas_call(
        _mlstm_kernel,
        grid=(batch, nc),
        in_specs=[pl.BlockSpec((CHUNK, M_QK), row),
                  pl.BlockSpec((CHUNK, M_QK), lambda b, c: (b * nc + c, 1)),
                  pl.BlockSpec((CHUNK, M_V), lambda b, c: (b * nc + c, 2 * M_QK // M_V)),
                  pl.BlockSpec((CHUNK, LANES), row),
                  pl.BlockSpec((SMALL_ROWS, CHUNK), lambda b, c: (0, b * nc + c)),
                  pl.BlockSpec((CHUNK, M_V), lambda b, c: (b * nc + c, og_block)),
                  pl.BlockSpec((1, M_V), lambda b, c: (0, 0))],
        out_specs=[pl.BlockSpec((CHUNK, M_V), row),
                   pl.BlockSpec((None, M_HEADS, M_DV, M_DK), lambda b, c: (b, 0, 0, 0)),
                   pl.BlockSpec((None, 2 * M_HEADS, M_DK), lambda b, c: (b, 0, 0))],
        out_shape=[jax.ShapeDtypeStruct((m, M_V), BF16),
                   jax.ShapeDtypeStruct((batch, M_HEADS, M_DV, M_DK), F32),
                   jax.ShapeDtypeStruct((batch, 2 * M_HEADS, M_DK), F32)],
        scratch_shapes=[pltpu.VMEM((M_HEADS, M_DV, M_DK), F32),
                        pltpu.VMEM((SUBLANES, M_DK), F32),
                        pltpu.VMEM((SUBLANES, LANES), F32)],
        compiler_params=_params("arbitrary", "arbitrary"),
        name="mlstm",
    )(qkv, qkv, qkv, sm, smt, sig, hw)


FOX_HEADS_PER_STEP = 2
FOX_ROW_CHUNK = 32


def _forget_columns(f, first):
    hi, mid, lo = _split3(f)
    lane = lax.broadcasted_iota(jnp.int32, (f.shape[0], LANES), 1)
    ones = jnp.where(jnp.abs(lane - (4 - first)) <= 1, 1.0, 0.0)
    ext = jnp.where(lane == first, hi.astype(F32),
                    jnp.where(lane == first + 1, mid.astype(F32),
                              jnp.where(lane == first + 2, lo.astype(F32), ones)))
    return ext.astype(BF16)


def _fox_kernel(q_ref, k_ref, v_ref, fcol_ref, o_ref, qaug_ref, kaug_ref, s_ref, p_ref, m_ref, l_ref, acc_ref):
    hp = pl.program_id(1)
    qi = pl.program_id(2)
    tq = q_ref.shape[0]
    t = k_ref.shape[0]
    nh = FOX_HEADS_PER_STEP

    def forget_sum(fc, h2):
        lane = lax.broadcasted_iota(jnp.int32, fc.shape, 1)
        sel = jnp.where(lane == LFF0 + nh * hp + h2, fc, 0.0)
        return jnp.sum(sel, axis=1, keepdims=True) * LOG2E

    @pl.when(qi == 0)
    def _():
        fc = fcol_ref[...]
        for h2 in range(nh):
            kaug_ref[h2, :, 0:F_DH] = k_ref[:, h2 * F_DH:(h2 + 1) * F_DH]
            kaug_ref[h2, :, F_DH:2 * F_DH] = _forget_columns(-forget_sum(fc, h2), 3)

    fq = fcol_ref[pl.ds(pl.multiple_of(qi * tq, tq), tq), :]
    for h2 in range(nh):
        qaug_ref[h2, :, 0:F_DH] = q_ref[:, h2 * F_DH:(h2 + 1) * F_DH]
        qaug_ref[h2, :, F_DH:2 * F_DH] = _forget_columns(forget_sum(fq, h2), 0)
    m_ref[...] = jnp.full_like(m_ref, -jnp.inf)
    l_ref[...] = jnp.zeros_like(l_ref)
    acc_ref[...] = jnp.zeros_like(acc_ref)
    rc = FOX_ROW_CHUNK
    n_lane_tiles = tq // LANES

    def softmax_rows(h2, c, masked):
        rows = slice(c * rc, (c + 1) * rc)
        s = s_ref[h2, rows, :]
        if masked:
            r = lax.broadcasted_iota(jnp.int32, (rc, tq), 0) + c * rc
            col = lax.broadcasted_iota(jnp.int32, (rc, tq), 1)
            s = jnp.where(r >= col, s, -jnp.inf)
        tiles = [s[:, k * LANES:(k + 1) * LANES] for k in range(n_lane_tiles)]
        m_old = m_ref[h2, rows, :]
        mx = functools.reduce(jnp.maximum, tiles)
        m_new = jnp.maximum(m_old, jnp.max(mx, axis=1, keepdims=True))
        alpha = jnp.exp2(m_old - m_new)
        ps = [jnp.exp2(tile - m_new) for tile in tiles]
        row_sum = jnp.sum(functools.reduce(jnp.add, ps), axis=1, keepdims=True)
        l_ref[h2, rows, :] = alpha * l_ref[h2, rows, :] + row_sum
        for k in range(n_lane_tiles):
            p_ref[h2, rows, k * LANES:(k + 1) * LANES] = ps[k].astype(BF16)
        acc_ref[h2, rows, :] = alpha * acc_ref[h2, rows, :]
        m_ref[h2, rows, :] = m_new

    def block(kj, masked):
        off = pl.multiple_of(kj * tq, tq)
        for h2 in range(nh):
            s_ref[h2] = _dot_nt(qaug_ref[h2], kaug_ref[h2, pl.ds(off, tq), :])
        for h2 in range(nh):
            for c in range(tq // rc):
                softmax_rows(h2, c, masked)
            acc_ref[h2] += _dot(p_ref[h2], v_ref[pl.ds(off, tq), h2 * F_DH:(h2 + 1) * F_DH])

    def body(kj, carry):
        block(kj, False)
        return carry

    lax.fori_loop(0, qi, body, 0)
    block(qi, True)
    for h2 in range(nh):
        o_ref[:, h2 * F_DH:(h2 + 1) * F_DH] = (acc_ref[h2] / l_ref[h2]).astype(BF16)


def _fox_prompt(qf, kfb, vfb, fcol, batch, tq=512):
    m = qf.shape[0]
    t = m // batch
    nq = t // tq
    nh = FOX_HEADS_PER_STEP
    wide = nh * F_DH
    return pl.pallas_call(
        _fox_kernel,
        grid=(batch, F_HEADS // nh, nq),
        in_specs=[pl.BlockSpec((tq, wide), lambda b, h, i: (b * nq + i, h)),
                  pl.BlockSpec((t, wide), lambda b, h, i: (b, h)),
                  pl.BlockSpec((t, wide), lambda b, h, i: (b, h)),
                  pl.BlockSpec((t, LANES), lambda b, h, i: (b, 0))],
        out_specs=pl.BlockSpec((tq, wide), lambda b, h, i: (b * nq + i, h)),
        out_shape=jax.ShapeDtypeStruct((m, F_W), BF16),
        scratch_shapes=[pltpu.VMEM((nh, tq, 2 * F_DH), BF16),
                        pltpu.VMEM((nh, t, 2 * F_DH), BF16),
                        pltpu.VMEM((nh, tq, tq), F32),
                        pltpu.VMEM((nh, tq, tq), BF16),
                        pltpu.VMEM((nh, tq, LANES), F32),
                        pltpu.VMEM((nh, tq, LANES), F32),
                        pltpu.VMEM((nh, tq, F_DH), F32)],
        compiler_params=_params("arbitrary", "arbitrary", "arbitrary"),
        name="fox",
    )(qf, kfb, vfb, fcol)


MSTEP_ROWS = 8


def _mstep_kernel(q_ref, k_ref, v_ref, sm_ref, og_ref, hw_ref, c0_ref, n0_ref, m0_ref,
                  ym_ref, cout_ref, nout_ref, mout_ref, cq_ref):
    rows = q_ref.shape[0]
    sm = sm_ref[...]
    ig = sm[:, IG0:IG0 + M_HEADS]
    lf = sm[:, LFM0:LFM0 + M_HEADS]
    m0 = m0_ref[...]
    m_new = lf + jnp.maximum(m0, ig - lf)
    w_end = jnp.exp(ig - m_new)
    decay = jnp.exp(lf + m0 - m_new)
    mout_ref[...] = m_new
    row_id = lax.broadcasted_iota(jnp.int32, (SUBLANES, M_DV), 0)

    for h in range(M_HEADS):
        qh = q_ref[:, h * M_DK:(h + 1) * M_DK]
        kh = k_ref[:, h * M_DK:(h + 1) * M_DK]
        vh = v_ref[:, h * M_DV:(h + 1) * M_DV]
        qf = qh.astype(F32)
        kf = kh.astype(F32)
        vf = vh.astype(F32)
        wh = w_end[:, h:h + 1]
        dh = decay[:, h:h + 1]
        for r in range(rows):
            c0 = c0_ref[r, h]
            q8 = jnp.broadcast_to(qh[r:r + 1, :], (SUBLANES, M_DK))
            cq_ref[r:r + 1, :] = _dot_nt(q8, c0.astype(BF16))[0:1, :]
            vw8 = jnp.where(row_id == 0, jnp.broadcast_to(wh[r:r + 1, :] * vf[r:r + 1, :], (SUBLANES, M_DV)), 0.0)
            k8 = jnp.broadcast_to(kh[r:r + 1, :], (SUBLANES, M_DK))
            cout_ref[r, h] = dh[r:r + 1, :] * c0 + _dot_tn(vw8.astype(BF16), k8)
        n0 = n0_ref[:, h, :]
        s = jnp.sum(qf * kf, axis=1, keepdims=True) * wh
        num = s * vf + dh * cq_ref[...]
        den = s + dh * jnp.sum(n0 * qf, axis=1, keepdims=True)
        hh = num / jnp.maximum(jnp.abs(den), jnp.exp(-m_new[:, h:h + 1]))
        hn = hh * lax.rsqrt(jnp.mean(hh * hh, axis=1, keepdims=True) + NORM_EPS) * hw_ref[:, h * M_DV:(h + 1) * M_DV]
        ym_ref[:, h * M_DV:(h + 1) * M_DV] = (og_ref[:, h * M_DV:(h + 1) * M_DV] * hn).astype(BF16)
        nout_ref[:, h, :] = dh * n0 + wh * kf


def _mlstm_step(qkv, sm, sig, hw, c0, n0, m0):
    db = qkv.shape[0]
    og_block = sig.shape[1] // M_V - 1
    rows = MSTEP_ROWS
    row = lambda i: (i, 0)
    return pl.pallas_call(
        _mstep_kernel,
        grid=(db // rows,),
        in_specs=[pl.BlockSpec((rows, M_QK), row),
                  pl.BlockSpec((rows, M_QK), lambda i: (i, 1)),
                  pl.BlockSpec((rows, M_V), lambda i: (i, 2 * M_QK // M_V)),
                  pl.BlockSpec((rows, LANES), row),
                  pl.BlockSpec((rows, M_V), lambda i: (i, og_block)),
                  pl.BlockSpec((1, M_V), lambda i: (0, 0)),
                  pl.BlockSpec((rows, M_HEADS, M_DV, M_DK), lambda i: (i, 0, 0, 0)),
                  pl.BlockSpec((rows, M_HEADS, M_DK), lambda i: (i, 0, 0)),
                  pl.BlockSpec((rows, M_HEADS), row)],
        out_specs=[pl.BlockSpec((rows, M_V), row),
                   pl.BlockSpec((rows, M_HEADS, M_DV, M_DK), lambda i: (i, 0, 0, 0)),
                   pl.BlockSpec((rows, M_HEADS, M_DK), lambda i: (i, 0, 0)),
                   pl.BlockSpec((rows, M_HEADS), row)],
        out_shape=[jax.ShapeDtypeStruct((db, M_V), BF16),
                   jax.ShapeDtypeStruct(c0.shape, F32),
                   jax.ShapeDtypeStruct(n0.shape, F32),
                   jax.ShapeDtypeStruct(m0.shape, F32)],
        scratch_shapes=[pltpu.VMEM((rows, M_DV), F32)],
        compiler_params=_params("arbitrary"),
        name="mstep",
    )(qkv, qkv, qkv, sm, sig, hw, c0, n0, m0)


N_MLSTM_IN = 7
N_MLSTM_OUT = 3
N_MLSTM_SCRATCH = 3


def _decode_kernel(*refs, n_pages, mlstm_steps_per_chunk, mlstm_chunks):
    n_dec_in = 3 * n_pages + 4
    n_ml = N_MLSTM_IN if mlstm_steps_per_chunk else 0
    dec_in = refs[1:1 + n_dec_in]
    ml_in = refs[1 + n_dec_in:1 + n_dec_in + n_ml]
    outs = refs[1 + n_dec_in + n_ml:]
    o_ref, outs = outs[0], outs[1:]
    if mlstm_steps_per_chunk:
        ml_out, outs = outs[:N_MLSTM_OUT], outs[N_MLSTM_OUT:]
        ml_scratch = outs[2:]
        step = pl.program_id(0)
        chunk = (step // mlstm_steps_per_chunk) % mlstm_chunks
        heads_per_step = M_HEADS // mlstm_steps_per_chunk
        for g in range(mlstm_steps_per_chunk):
            @pl.when(step % mlstm_steps_per_chunk == g)
            def _(g=g):
                _mlstm_chunk(chunk, mlstm_chunks - 1, range(g * heads_per_step, (g + 1) * heads_per_step),
                             *ml_in, *ml_out, *ml_scratch)
    _decode_sequence(dec_in[:n_pages], dec_in[n_pages:2 * n_pages], dec_in[2 * n_pages:3 * n_pages],
                     *dec_in[3 * n_pages:], o_ref, outs[0], outs[1])


def _decode_sequence(k_refs, v_refs, lf_refs, q_ref, kn_ref, vn_ref, sm_ref, o_ref, s_ref, xn_ref):
    n_pages = len(k_refs)
    page = xn_ref.shape[1]
    pw = page * F_HEADS

    for p in range(n_pages):
        xn_ref[p * F_HEADS:(p + 1) * F_HEADS, :] = lf_refs[p][...]
    xn = xn_ref[...] * LOG2E
    t_id = lax.broadcasted_iota(jnp.int32, (page, pw), 0)
    l_id = lax.broadcasted_iota(jnp.int32, (page, pw), 1)
    widen = jnp.where((l_id // F_HEADS) == t_id, 1.0, 0.0).astype(BF16)
    wide = _dot_f32_lhs(xn, widen)
    r_id = lax.broadcasted_iota(jnp.int32, wide.shape, 0)
    c_id = lax.broadcasted_iota(jnp.int32, wide.shape, 1)
    own = jnp.where((r_id & (F_HEADS - 1)) == (c_id & (F_HEADS - 1)), wide, 0.0)
    g_p = lax.broadcasted_iota(jnp.int32, (n_pages, n_pages * F_HEADS), 0)
    g_r = lax.broadcasted_iota(jnp.int32, (n_pages, n_pages * F_HEADS), 1)
    gather = jnp.where((g_r // F_HEADS) == g_p, 1.0, 0.0).astype(BF16)
    x = _dot_f32_rhs(gather, own)
    lane = lax.broadcasted_iota(jnp.int32, x.shape, 1)
    y, z = x, x
    step = F_HEADS
    while step < pw:
        y = y + jnp.where(lane < pw - step, pltpu.roll(y, pw - step, axis=1), 0.0)
        z = z + pltpu.roll(z, step, axis=1)
        step *= 2
    after = y - x

    r = lax.broadcasted_iota(jnp.int32, (LANES, pw), 0)
    c = lax.broadcasted_iota(jnp.int32, (LANES, pw), 1)
    spread = jnp.where(r == LFF0 + (c & (F_HEADS - 1)), 1.0, 0.0).astype(BF16)
    run = _dot_f32_lhs(jnp.broadcast_to(sm_ref[...] * LOG2E, (SUBLANES, LANES)), spread)[0:1, :]

    q = q_ref[...]
    sub = lax.broadcasted_iota(jnp.int32, (F_HEADS, pw), 0)
    lane_h = lax.broadcasted_iota(jnp.int32, (F_HEADS, pw), 1) & (F_HEADS - 1)
    valid = lane_h == sub
    s_self = jnp.sum(q.astype(F32) * kn_ref[...], axis=1, keepdims=True)
    m_run = s_self
    for p in reversed(range(n_pages)):
        kb = k_refs[p][...].astype(BF16)
        st = _dot_nt(q, kb) + (run + after[p:p + 1, :])
        st = jnp.where(valid, st, -jnp.inf)
        s_ref[p] = st
        m_run = jnp.maximum(m_run, jnp.max(st, axis=1, keepdims=True))
        run = run + z[p:p + 1, :]
    w_self = jnp.exp2(s_self - m_run)
    l = w_self
    acc = w_self * vn_ref[...]
    for p in range(n_pages):
        pe = jnp.exp2(s_ref[p] - m_run)
        l = l + jnp.sum(pe, axis=1, keepdims=True)
        acc = acc + _dot(pe.astype(BF16), v_refs[p][...].astype(BF16))
    o_ref[...] = (acc / l).astype(BF16)


def _mlstm_fits_decode(db, batch, seq_len):
    chunk_steps = batch * (seq_len // CHUNK)
    return db % chunk_steps == 0 and db // chunk_steps in (1, 2, 4)


def _fox_decode(page_table, ck, cv, clf, layer, q3, k3, v3, sm3, mlstm=None):
    db, n_pages = page_table.shape
    pw = ck.shape[2]
    page = clf.shape[3]
    seq = lambda b, pt: (b, 0, 0)

    def pg(p):
        return lambda b, pt: (layer, pt[b, p], 0, 0)

    in_specs = ([pl.BlockSpec((None, None, pw, F_DH), pg(p)) for p in range(n_pages)]
                + [pl.BlockSpec((None, None, pw, F_DH), pg(p)) for p in range(n_pages)]
                + [pl.BlockSpec((None, None, F_HEADS, page), pg(p)) for p in range(n_pages)]
                + [pl.BlockSpec((None, F_HEADS, F_DH), seq),
                   pl.BlockSpec((None, F_HEADS, F_DH), seq),
                   pl.BlockSpec((None, F_HEADS, F_DH), seq),
                   pl.BlockSpec((None, 1, LANES), seq)])
    args = [page_table, *([ck] * n_pages), *([cv] * n_pages), *([clf] * n_pages), q3, k3, v3, sm3]
    out_specs = [pl.BlockSpec((None, F_HEADS, F_DH), seq)]
    out_shape = [jax.ShapeDtypeStruct((db, F_HEADS, F_DH), BF16)]
    scratch = [pltpu.VMEM((n_pages, F_HEADS, pw), F32), pltpu.VMEM((n_pages * F_HEADS, page), F32)]
    per_chunk, nc = 0, 0
    if mlstm is not None:
        qkv, sm, smt, sig, hw, batch = mlstm
        m = qkv.shape[0]
        nc = (m // batch) // CHUNK
        per_chunk = db // (batch * nc)
        og_block = sig.shape[1] // M_V - 1
        row = lambda s, pt: (s // per_chunk, 0)
        in_specs += [pl.BlockSpec((CHUNK, M_QK), row),
                     pl.BlockSpec((CHUNK, M_QK), lambda s, pt: (s // per_chunk, 1)),
                     pl.BlockSpec((CHUNK, M_V), lambda s, pt: (s // per_chunk, 2 * M_QK // M_V)),
                     pl.BlockSpec((CHUNK, LANES), row),
                     pl.BlockSpec((SMALL_ROWS, CHUNK), lambda s, pt: (0, s // per_chunk)),
                     pl.BlockSpec((CHUNK, M_V), lambda s, pt: (s // per_chunk, og_block)),
                     pl.BlockSpec((1, M_V), lambda s, pt: (0, 0))]
        args += [qkv, qkv, qkv, sm, smt, sig, hw]
        seq_of = lambda s, pt: s // (per_chunk * nc)
        out_specs += [pl.BlockSpec((CHUNK, M_V), row),
                      pl.BlockSpec((None, M_HEADS, M_DV, M_DK), lambda s, pt: (seq_of(s, pt), 0, 0, 0)),
                      pl.BlockSpec((None, 2 * M_HEADS, M_DK), lambda s, pt: (seq_of(s, pt), 0, 0))]
        out_shape += [jax.ShapeDtypeStruct((m, M_V), BF16),
                      jax.ShapeDtypeStruct((batch, M_HEADS, M_DV, M_DK), F32),
                      jax.ShapeDtypeStruct((batch, 2 * M_HEADS, M_DK), F32)]
        scratch += [pltpu.VMEM((M_HEADS, M_DV, M_DK), F32),
                    pltpu.VMEM((SUBLANES, M_DK), F32),
                    pltpu.VMEM((SUBLANES, LANES), F32)]
    grid_spec = pltpu.PrefetchScalarGridSpec(
        num_scalar_prefetch=1,
        grid=(db,),
        in_specs=in_specs,
        out_specs=out_specs,
        scratch_shapes=scratch,
    )
    outs = pl.pallas_call(
        functools.partial(_decode_kernel, n_pages=n_pages, mlstm_steps_per_chunk=per_chunk, mlstm_chunks=nc),
        grid_spec=grid_spec,
        out_shape=out_shape,
        compiler_params=_params("arbitrary"),
        name="decode",
    )(*args)
    return outs if mlstm is not None else outs[0]


def _merge_kernel(ym_ref, yf_ref, sga_ref, sgb_ref, x_ref, g1_ref, sc_ref, sh_ref, n2w_ref,
                  wa_ref, wb_ref, wo_ref, x1_ref, h2_ref):
    merged = sga_ref[...] * _dot(ym_ref[...], wa_ref[...]) + sgb_ref[...] * _dot(yf_ref[...], wb_ref[...])
    x1 = x_ref[...] + g1_ref[...] * _dot(merged.astype(BF16), wo_ref[...])
    x1_ref[...] = x1
    rstd = lax.rsqrt(jnp.mean(x1 * x1, axis=-1, keepdims=True) + NORM_EPS)
    h2_ref[...] = ((x1 * rstd * n2w_ref[...]) * (1.0 + sc_ref[...]) + sh_ref[...]).astype(BF16)


def _merge(ym, yf, sig, x2, mod, n2w, wts, tm):
    m, d = x2.shape
    tiles_per_group = (m // tm) // mod.shape[0]
    group_of = lambda i: i // tiles_per_group
    row = lambda i: (i, 0)
    const = lambda i: (0, 0)
    resident = dict(pipeline_mode=pl.Buffered(1))
    return pl.pallas_call(
        _merge_kernel,
        grid=(m // tm,),
        in_specs=[pl.BlockSpec((tm, M_V), row),
                  pl.BlockSpec((tm, F_W), row),
                  pl.BlockSpec((tm, d), row),
                  pl.BlockSpec((tm, d), lambda i: (i, 1)),
                  pl.BlockSpec((tm, d), row),
                  _mod_spec(mod, d, MOD_G1, group_of),
                  _mod_spec(mod, d, MOD_SC2, group_of),
                  _mod_spec(mod, d, MOD_SH2, group_of),
                  pl.BlockSpec((1, d), const),
                  pl.BlockSpec((M_V, d), const, **resident),
                  pl.BlockSpec((F_W, d), const, **resident),
                  pl.BlockSpec((d, d), const, **resident)],
        out_specs=[pl.BlockSpec((tm, d), row), pl.BlockSpec((tm, d), row)],
        out_shape=[jax.ShapeDtypeStruct((m, d), F32), jax.ShapeDtypeStruct((m, d), BF16)],
        compiler_params=_params("arbitrary"),
        name="merge",
    )(ym, yf, sig, sig, x2, mod, mod, mod, n2w, wts["wa"], wts["wb"], wts["wo"])


FFN_TF = 512


def _ffn_kernel(h2_ref, x1_ref, g2_ref, p_ref, wa_ref, wg_ref, cw_ref, cb_ref, w2_ref,
                y_ref, a_out_ref, carry_ref, *, seq):
    f = pl.program_id(2)
    tm = h2_ref.shape[0]

    @pl.when(f == 0)
    def _():
        y_ref[...] = jnp.zeros_like(y_ref)

    if seq:
        @pl.when(pl.program_id(1) == 0)
        def _():
            carry_ref[f] = p_ref[...]

    h2 = h2_ref[...]
    a = _dot(h2, wa_ref[...])
    g = _dot(h2, wg_ref[...])
    if seq:
        prev = carry_ref[f]
        p2, p1 = prev[SUBLANES - 2:SUBLANES - 1, :], prev[SUBLANES - 1:SUBLANES, :]
        r = lax.broadcasted_iota(jnp.int32, a.shape, 0)
        a1 = jnp.where(r == 0, p1, pltpu.roll(a, 1, axis=0))
        a2 = jnp.where(r == 0, p2, jnp.where(r == 1, p1, pltpu.roll(a, 2, axis=0)))
        carry_ref[f] = a[tm - SUBLANES:tm, :]
        a_out_ref[...] = a[tm - SUBLANES:tm, :]
    else:
        a2, a1 = p_ref[0], p_ref[1]
        a_out_ref[...] = a
    ac = cw_ref[0:1, :] * a2 + cw_ref[1:2, :] * a1 + cw_ref[2:3, :] * a + cb_ref[...]
    u = (jax.nn.gelu(ac, approximate=True) * g).astype(BF16)
    y_ref[...] += _dot(u, w2_ref[...])

    @pl.when(f == pl.num_programs(2) - 1)
    def _():
        y_ref[...] = x1_ref[...] + g2_ref[...] * y_ref[...]


def _ffn(h2, x1, mod, prev, wts, tm, seq):
    m, d = x1.shape
    groups = mod.shape[0]
    tiles = (m // tm) // groups
    dff = wts["w2"].shape[0]
    tf = FFN_TF
    nf = dff // tf
    row = lambda b, i, f: (b * tiles + i, 0)
    if seq:
        prev_spec = pl.BlockSpec((None, SUBLANES, tf), lambda b, i, f: (b, 0, f))
        a_spec = pl.BlockSpec((None, None, SUBLANES, tf), lambda b, i, f: (b, i, 0, f))
        a_shape = jax.ShapeDtypeStruct((groups, tiles, SUBLANES, dff), F32)
    else:
        prev_spec = pl.BlockSpec((CONV_W - 1, tm, tf), lambda b, i, f: (0, i, f))
        a_spec = pl.BlockSpec((tm, tf), lambda b, i, f: (i, f))
        a_shape = jax.ShapeDtypeStruct((m, dff), F32)
    return pl.pallas_call(
        functools.partial(_ffn_kernel, seq=seq),
        grid=(groups, tiles, nf),
        in_specs=[pl.BlockSpec((tm, d), row),
                  pl.BlockSpec((tm, d), row, pipeline_mode=pl.Buffered(1)),
                  _mod_spec(mod, d, MOD_G2, lambda b, i, f: b),
                  prev_spec,
                  pl.BlockSpec((d, tf), lambda b, i, f: (0, f)),
                  pl.BlockSpec((d, tf), lambda b, i, f: (0, f + nf)),
                  pl.BlockSpec((CONV_W, tf), lambda b, i, f: (0, f)),
                  pl.BlockSpec((1, tf), lambda b, i, f: (0, f)),
                  pl.BlockSpec((tf, d), lambda b, i, f: (f, 0))],
        out_specs=[pl.BlockSpec((tm, d), row), a_spec],
        out_shape=[jax.ShapeDtypeStruct((m, d), F32), a_shape],
        scratch_shapes=[pltpu.VMEM((nf, SUBLANES, tf), F32)],
        compiler_params=_params("arbitrary", "arbitrary", "arbitrary"),
        name="ffn_seq" if seq else "ffn_row",
    )(h2, x1, mod, prev, wts["w1"], wts["w1"], wts["cw"], wts["cb"], wts["w2"])


def _prep_weights(w_in, b_in, f_qnorm_w, f_knorm_w, w_proj_a, w_proj_b, w_out, w_ffn_in, conv_w, conv_b, w_ffn_out):
    o = _SPLIT_OFFS
    bbig = jnp.concatenate([b_in[:o[3]], b_in[o[10]:], b_in[o[3]:o[4]], b_in[o[6]:o[9]]])
    bsm = jnp.concatenate([b_in[o[4]:o[6]], b_in[o[9]:o[10]]])
    return {
        "w_in_t": w_in.T,
        "bbig": bbig.reshape(1, -1),
        "bsm": bsm.reshape(SMALL_ROWS, 1),
        "qnw": f_qnorm_w.reshape(1, F_DH),
        "knw": f_knorm_w.reshape(1, F_DH),
        "wa": w_proj_a.astype(BF16),
        "wb": w_proj_b.astype(BF16),
        "wo": w_out.astype(BF16),
        "w1": w_ffn_in.astype(BF16),
        "cw": conv_w,
        "cb": conv_b.reshape(1, -1),
        "w2": w_ffn_out.astype(BF16),
    }


def kernel(x_prompt, x_sample, c_prompt, c_sample, cache_k, cache_v, cache_logf, page_table, state_C, state_n, state_m, state_conv, w_ada, b_ada, norm1_w, w_in, b_in, m_hnorm_w, f_qnorm_w, f_knorm_w, w_proj_a, w_proj_b, w_out, norm2_w, w_ffn_in, conv_w, conv_b, w_ffn_out):
    depth = w_ada.shape[0]
    assert depth == 1, "single-layer step"
    batch, seq_len, d = x_prompt.shape
    db = x_sample.shape[0]
    assert x_sample.shape[1] == 1
    dff = w_ffn_out.shape[1]
    layer = 0

    wts = _prep_weights(w_in[layer], b_in[layer], f_qnorm_w[layer], f_knorm_w[layer], w_proj_a[layer],
                        w_proj_b[layer], w_out[layer], w_ffn_in[layer], conv_w[layer], conv_b[layer],
                        w_ffn_out[layer])
    n1w = norm1_w[layer].reshape(1, d)
    n2w = norm2_w[layer].reshape(1, d)
    hw = m_hnorm_w[layer].reshape(1, M_V)

    mod_p, mod_s = _ada(c_prompt, c_sample, w_ada[layer], b_ada[layer])
    mod_p = mod_p.reshape(batch, 1, 6 * d)
    mod_s = mod_s.reshape(1, db, 6 * d)

    def in_projection(x2, mod, tm):
        h, sm, smt = _norm(x2, mod, n1w, wts, tm)
        (qkv,) = _proj(h, wts, _J_MIX, _J_GATE, "mix", tm)
        (sig,) = _proj(h, wts, _J_GATE, _J_FQ, "sigmoid", tm)
        (qf,) = _proj(h, wts, _J_FQ, _J_FK, "headnorm", tm, nw=wts["qnw"], scale=FOX_Q_SCALE)
        kf, kfb = _proj(h, wts, _J_FK, _J_FV, "headnorm", tm, nw=wts["knw"], emit_f32=True)
        vf, vfb = _proj(h, wts, _J_FV, _J_END, "copy", tm)
        return qkv, sig, qf, kf, kfb, vf, vfb, sm, smt

    tm_p = min(1024, seq_len)
    xp = x_prompt.reshape(batch * seq_len, d)
    qkv, sig, qf, kf, kfb, vf, vfb, sm, smt = in_projection(xp, mod_p, tm_p)
    fcol = _cumsum(sm, batch)

    xs = x_sample.reshape(db, d)
    qkv_s, sig_s, qf_s, kf_s, _, vf_s, _, sm_s, _ = in_projection(xs, mod_s, db)
    n_pool, page = cache_k.shape[1], cache_k.shape[2]
    decode_args = (page_table,
                   cache_k.reshape(depth, n_pool, page * F_HEADS, F_DH),
                   cache_v.reshape(depth, n_pool, page * F_HEADS, F_DH),
                   cache_logf.transpose(0, 1, 3, 2),
                   layer,
                   qf_s.reshape(db, F_HEADS, F_DH), kf_s.reshape(db, F_HEADS, F_DH),
                   vf_s.reshape(db, F_HEADS, F_DH), sm_s.reshape(db, 1, LANES))
    if _mlstm_fits_decode(db, batch, seq_len):
        yf_s, ym, c_p, nm_p = _fox_decode(*decode_args, mlstm=(qkv, sm, smt, sig, hw, batch))
    else:
        yf_s = _fox_decode(*decode_args)
        ym, c_p, nm_p = _mlstm_prompt(qkv, sm, smt, sig, hw, batch)
    yf_s = yf_s.reshape(db, F_W)

    yf = _fox_prompt(qf, kfb, vfb, fcol, batch)
    x1, h2 = _merge(ym, yf, sig, xp, mod_p, n2w, wts, min(512, seq_len))
    conv0 = jnp.zeros((batch, SUBLANES, dff), F32)
    yp, a_last = _ffn(h2, x1, mod_p, conv0, wts, tm_p, seq=True)

    y_prompt = yp.reshape(batch, seq_len, d)
    k_prompt = kf.reshape(depth, batch, seq_len, F_HEADS, F_DH)
    v_prompt = vf.reshape(depth, batch, seq_len, F_HEADS, F_DH)
    logf_prompt = sm[:, LFF0:LFF0 + F_HEADS].reshape(depth, batch, seq_len, F_HEADS)
    c_prompt_out = c_p[None]
    n_prompt = nm_p[:, :M_HEADS, :][None]
    m_prompt = nm_p[:, M_HEADS:, 0][None]
    conv_prompt = a_last[:, -1, SUBLANES - (CONV_W - 1):, :][None]

    ym_s, c_s, n_s, m_s = _mlstm_step(qkv_s, sm_s, sig_s, hw, state_C[layer], state_n[layer], state_m[layer])
    x1_s, h2_s = _merge(ym_s, yf_s, sig_s, xs, mod_s, n2w, wts, db)
    prev_s = state_conv[layer].transpose(1, 0, 2)
    ys, a_s = _ffn(h2_s, x1_s, mod_s, prev_s, wts, db, seq=False)

    y_sample = ys.reshape(db, 1, d)
    k_sample = kf_s.reshape(depth, db, 1, F_HEADS, F_DH)
    v_sample = vf_s.reshape(depth, db, 1, F_HEADS, F_DH)
    logf_sample = sm_s[:, LFF0:LFF0 + F_HEADS].reshape(depth, db, 1, F_HEADS)
    conv_sample = jnp.stack([state_conv[layer][:, CONV_W - 2, :], a_s], axis=1)[None]

    return (y_prompt, y_sample, k_prompt, v_prompt, logf_prompt, c_prompt_out, n_prompt, m_prompt, conv_prompt,
            k_sample, v_sample, logf_sample, c_s[None], n_s[None], m_s[None], conv_sample)
```

```python
import functools

import jax
import jax.numpy as jnp
from jax import lax
from jax.experimental import pallas as pl
from jax.experimental.pallas import tpu as pltpu

F32 = jnp.float32
BF16 = jnp.bfloat16

NORM_EPS = 1e-6
M_HEADS, M_DK, M_DV = 4, 128, 256
F_HEADS, F_DH = 8, 128
M_QK = M_HEADS * M_DK
M_V = M_HEADS * M_DV
F_W = F_HEADS * F_DH
CONV_W = 3
CHUNK = 128
LANES = 128
SUBLANES = 8
SMALL_ROWS = 16
VMEM_LIMIT = 56 * 1024 * 1024

IG0, LFM0, LFF0 = 0, M_HEADS, 2 * M_HEADS

LOG2E = 1.4426950408889634
FOX_Q_SCALE = F_DH ** -0.5 * LOG2E


def _dot(a, b):
    return jnp.dot(a, b, preferred_element_type=F32)


def _dot_nt(a, b):
    return lax.dot_general(a, b, (((1,), (1,)), ((), ())), preferred_element_type=F32)


def _dot_tn(a, b):
    return lax.dot_general(a, b, (((0,), (0,)), ((), ())), preferred_element_type=F32)


def _split3(x):
    hi = x.astype(BF16)
    r1 = x - hi.astype(F32)
    mid = r1.astype(BF16)
    lo = (r1 - mid.astype(F32)).astype(BF16)
    return hi, mid, lo


def _dot_f32_lhs(x, mat01):
    hi, mid, lo = _split3(x)
    return _dot(hi, mat01) + _dot(mid, mat01) + _dot(lo, mat01)


def _dot_f32_rhs(mat01, x):
    hi, mid, lo = _split3(x)
    return _dot(mat01, hi) + _dot(mat01, mid) + _dot(mat01, lo)


def _log_sigmoid(x):
    return jnp.minimum(x, 0.0) - jnp.log1p(jnp.exp(-jnp.abs(x)))


def _tri01(n, kind):
    r = lax.broadcasted_iota(jnp.int32, (n, n), 0)
    c = lax.broadcasted_iota(jnp.int32, (n, n), 1)
    if kind == "lower":
        m = c <= r
    elif kind == "upper":
        m = r <= c
    else:
        m = r > c
    return jnp.where(m, 1.0, 0.0).astype(BF16)


def _params(*sem):
    return pltpu.CompilerParams(dimension_semantics=sem, vmem_limit_bytes=VMEM_LIMIT)


def _ada_kernel(cp_ref, cs_ref, w_ref, b_ref, op_ref, os_ref):
    w = w_ref[...].astype(BF16)
    for c_ref, o_ref in ((cp_ref, op_ref), (cs_ref, os_ref)):
        c = c_ref[...]
        s = (c * jax.nn.sigmoid(c)).astype(BF16)
        o_ref[...] = _dot(s, w) + b_ref[...]


def _ada(c_p, c_s, w_ada, b_ada, tn=1024):
    d = c_p.shape[1]
    n = w_ada.shape[1]
    return pl.pallas_call(
        _ada_kernel,
        grid=(n // tn,),
        in_specs=[pl.BlockSpec(c_p.shape, lambda j: (0, 0)),
                  pl.BlockSpec(c_s.shape, lambda j: (0, 0)),
                  pl.BlockSpec((d, tn), lambda j: (0, j)),
                  pl.BlockSpec((1, tn), lambda j: (0, j))],
        out_specs=[pl.BlockSpec((c_p.shape[0], tn), lambda j: (0, j)),
                   pl.BlockSpec((c_s.shape[0], tn), lambda j: (0, j))],
        out_shape=[jax.ShapeDtypeStruct((c_p.shape[0], n), F32), jax.ShapeDtypeStruct((c_s.shape[0], n), F32)],
        compiler_params=_params("arbitrary"),
        name="ada",
    )(c_p, c_s, w_ada, b_ada.reshape(1, n))


MOD_SH1, MOD_SC1, MOD_G1, MOD_SH2, MOD_SC2, MOD_G2 = range(6)


PROJ_TN = 1024
_J_MIX, _J_GATE, _J_FQ, _J_FK, _J_FV, _J_END = 0, 2, 7, 8, 9, 10
_J_MO = _J_GATE + 4
_SPLIT_SIZES = (M_QK, M_QK, M_V, M_V, M_HEADS, M_HEADS, F_W, F_W, F_W, F_HEADS)
_SPLIT_OFFS = tuple(sum(_SPLIT_SIZES[:i]) for i in range(len(_SPLIT_SIZES) + 1))
_ROW_MO, _ROW_SMALL_M, _ROW_FQ, _ROW_SMALL_F, _ROW_GATES = (_SPLIT_OFFS[3], _SPLIT_OFFS[4], _SPLIT_OFFS[6],
                                                            _SPLIT_OFFS[9], _SPLIT_OFFS[10])
_TILE_RUNS = ((_J_MIX, 0), (_J_GATE, _ROW_GATES), (_J_MO, _ROW_MO), (_J_FQ, _ROW_FQ))


def _tile_row(jj):
    row = _TILE_RUNS[0][1] + PROJ_TN * (jj - _TILE_RUNS[0][0])
    for t0, r0 in _TILE_RUNS[1:]:
        row = jnp.where(jj >= t0, r0 + PROJ_TN * (jj - t0), row)
    return row


def _norm_kernel(x_ref, sc_ref, sh_ref, n1w_ref, wsm_m_ref, wsm_f_ref, bsm_ref, h_ref, sm_ref, smt_ref):
    x = x_ref[...]
    rstd = lax.rsqrt(jnp.mean(x * x, axis=-1, keepdims=True) + NORM_EPS)
    h = (x * rstd * n1w_ref[...]) * (1.0 + sc_ref[...]) + sh_ref[...]
    hb = h.astype(BF16)
    h_ref[...] = hb
    wsm = jnp.concatenate([wsm_m_ref[...], wsm_f_ref[...]], axis=0).astype(BF16)
    smt = _dot_nt(wsm, hb) + bsm_ref[...]
    row = lax.broadcasted_iota(jnp.int32, smt.shape, 0)
    smt = jnp.where(row < LFM0, smt, _log_sigmoid(smt))
    smt_ref[...] = smt
    pad = jnp.zeros((LANES - SMALL_ROWS, smt.shape[1]), F32)
    sm_ref[...] = jnp.concatenate([smt, pad], axis=0).T


def _proj_kernel(h_ref, w_ref, b_ref, *rest, kind, scale, emit_f32):
    *rest, wb_ref = rest

    @pl.when(pl.program_id(1) == 0)
    def _():
        wb_ref[...] = w_ref[...].astype(BF16)

    acc = _dot_nt(h_ref[...], wb_ref[...]) + b_ref[...]
    if kind == "mix":
        (ob_ref,) = rest
        lane = lax.broadcasted_iota(jnp.int32, (1, acc.shape[1]), 1)
        is_mk = jnp.logical_and(pl.program_id(0) == 0, lane >= M_QK)
        ob_ref[...] = (acc * jnp.where(is_mk, M_DK ** -0.5, 1.0)).astype(BF16)
    elif kind == "sigmoid":
        (ob_ref,) = rest
        ob_ref[...] = jax.nn.sigmoid(acc).astype(BF16)
    elif kind == "headnorm":
        nw_ref, *outs = rest
        for h in range(acc.shape[1] // F_DH):
            sl = acc[:, h * F_DH:(h + 1) * F_DH]
            ms = jnp.mean(sl * sl, axis=-1, keepdims=True)
            y = sl * lax.rsqrt(ms + NORM_EPS) * nw_ref[...]
            if emit_f32:
                outs[0][:, h * F_DH:(h + 1) * F_DH] = y
            outs[-1][:, h * F_DH:(h + 1) * F_DH] = (y * scale).astype(BF16)
    else:
        of_ref, ob_ref = rest
        of_ref[...] = acc
        ob_ref[...] = acc.astype(BF16)


def _feature_rows_spec(rows, d, first_row):
    def index_map(*g):
        return pl.multiple_of(jnp.asarray(first_row(*g), jnp.int32), SUBLANES), 0

    return pl.BlockSpec((pl.Element(rows), pl.Element(d)), index_map)


def _mod_spec(mod, d, chunk, group_of):
    return pl.BlockSpec((None, mod.shape[1], d), lambda *g: (group_of(*g), 0, chunk))


def _norm(x2, mod, n1w, wts, tm):
    m, d = x2.shape
    tiles_per_group = (m // tm) // mod.shape[0]
    group_of = lambda i: i // tiles_per_group
    return pl.pallas_call(
        _norm_kernel,
        grid=(m // tm,),
        in_specs=[pl.BlockSpec((tm, d), lambda i: (i, 0)),
                  _mod_spec(mod, d, MOD_SC1, group_of),
                  _mod_spec(mod, d, MOD_SH1, group_of),
                  pl.BlockSpec((1, d), lambda i: (0, 0)),
                  _feature_rows_spec(SUBLANES, d, lambda i: _ROW_SMALL_M),
                  _feature_rows_spec(SUBLANES, d, lambda i: _ROW_SMALL_F),
                  pl.BlockSpec((SMALL_ROWS, 1), lambda i: (0, 0))],
        out_specs=[pl.BlockSpec((tm, d), lambda i: (i, 0)),
                   pl.BlockSpec((tm, LANES), lambda i: (i, 0)),
                   pl.BlockSpec((SMALL_ROWS, tm), lambda i: (0, i))],
        out_shape=[jax.ShapeDtypeStruct((m, d), BF16),
                   jax.ShapeDtypeStruct((m, LANES), F32),
                   jax.ShapeDtypeStruct((SMALL_ROWS, m), F32)],
        compiler_params=_params("arbitrary"),
        name="norm",
    )(x2, mod, mod, n1w, wts["w_in_t"], wts["w_in_t"], wts["bsm"])


def _proj(h, wts, j0, j1, kind, tm, nw=None, scale=1.0, emit_f32=False):
    m, d = h.shape
    tn = PROJ_TN
    width = (j1 - j0) * tn
    tile = pl.BlockSpec((tm, tn), lambda j, i: (i, j))
    in_specs = [pl.BlockSpec((tm, d), lambda j, i: (i, 0)),
                _feature_rows_spec(tn, d, lambda j, i: _tile_row(j0 + j)),
                pl.BlockSpec((1, tn), lambda j, i: (0, j0 + j))]
    args = [h, wts["w_in_t"], wts["bbig"]]
    if kind == "headnorm":
        in_specs.append(pl.BlockSpec((1, F_DH), lambda j, i: (0, 0)))
        args.append(nw)
    two = emit_f32 or kind == "copy"
    out_shape = ([jax.ShapeDtypeStruct((m, width), F32)] if two else []) + [jax.ShapeDtypeStruct((m, width), BF16)]
    return pl.pallas_call(
        functools.partial(_proj_kernel, kind=kind, scale=scale, emit_f32=emit_f32),
        grid=(j1 - j0, m // tm),
        in_specs=in_specs,
        out_specs=[tile] * len(out_shape),
        out_shape=out_shape,
        scratch_shapes=[pltpu.VMEM((tn, d), BF16)],
        compiler_params=_params("arbitrary", "arbitrary"),
        name="proj_" + kind,
    )(*args)


def _cumsum_kernel(sm_ref, fcol_ref, carry_ref):
    c = pl.program_id(1)
    tc = sm_ref.shape[0]

    @pl.when(c == 0)
    def _():
        carry_ref[...] = jnp.zeros_like(carry_ref)

    fc = _dot_f32_rhs(_tri01(tc, "lower"), sm_ref[...]) + carry_ref[0:1, :]
    fcol_ref[...] = fc
    carry_ref[...] = jnp.broadcast_to(fc[tc - 1:tc, :], carry_ref.shape)


def _cumsum(sm, batch, tc=512):
    m = sm.shape[0]
    nt = (m // batch) // tc
    return pl.pallas_call(
        _cumsum_kernel,
        grid=(batch, nt),
        in_specs=[pl.BlockSpec((tc, LANES), lambda b, c: (b * nt + c, 0))],
        out_specs=pl.BlockSpec((tc, LANES), lambda b, c: (b * nt + c, 0)),
        out_shape=jax.ShapeDtypeStruct((m, LANES), F32),
        scratch_shapes=[pltpu.VMEM((SUBLANES, LANES), F32)],
        compiler_params=_params("arbitrary", "arbitrary"),
        name="cumsum",
    )(sm)


def _mlstm_kernel(*refs):
    _mlstm_chunk(pl.program_id(1), pl.num_programs(1) - 1, range(M_HEADS), *refs)


def _mlstm_chunk(c, last_c, heads, q_ref, k_ref, v_ref, sm_ref, smt_ref, og_ref, hw_ref,
                 ym_ref, cout_ref, nm_ref, c_ref, n_ref, m_ref):
    L = q_ref.shape[0]

    @pl.when(c == 0)
    def _():
        for h in heads:
            c_ref[h] = jnp.zeros((M_DV, M_DK), F32)
            n_ref[h:h + 1, :] = jnp.zeros((1, M_DK), F32)
            m_ref[h:h + 1, :] = jnp.zeros((1, LANES), F32)

    r = lax.broadcasted_iota(jnp.int32, (L, L), 0)
    s_idx = lax.broadcasted_iota(jnp.int32, (L, L), 1)
    causal = s_idx <= r
    sm = sm_ref[...]
    smt = smt_ref[...]
    bcol_all = _dot_f32_rhs(_tri01(L, "lower"), sm)
    brow_all = _dot_f32_lhs(smt, _tri01(L, "upper"))

    for h in heads:
        ig_col = sm[:, IG0 + h:IG0 + h + 1]
        b_col = bcol_all[:, LFM0 + h:LFM0 + h + 1]
        ig_row = smt[IG0 + h:IG0 + h + 1, :]
        b_row = brow_all[LFM0 + h:LFM0 + h + 1, :]
        m0 = m_ref[h:h + 1, 0:1]
        a_row = ig_row - b_row
        cm_col = jnp.max(jnp.where(causal, a_row, -jnp.inf), axis=1, keepdims=True)
        mx_col = jnp.maximum(m0, cm_col)
        m_col = b_col + mx_col
        dmat = jnp.exp(jnp.where(causal, a_row - mx_col, -jnp.inf))
        inter = jnp.exp(m0 - mx_col)
        q = q_ref[:, h * M_DK:(h + 1) * M_DK]
        k = k_ref[:, h * M_DK:(h + 1) * M_DK]
        v = v_ref[:, h * M_DV:(h + 1) * M_DV]
        c0 = c_ref[h]
        n0 = n_ref[h:h + 1, :]
        s = _dot_nt(q, k) * dmat
        num = _dot(s.astype(BF16), v) + inter * _dot_nt(q, c0.astype(BF16))
        den = jnp.sum(s, axis=1, keepdims=True) + inter * jnp.sum(q.astype(F32) * n0, axis=1, keepdims=True)
        hh = num / jnp.maximum(jnp.abs(den), jnp.exp(-m_col))
        hn = hh * lax.rsqrt(jnp.mean(hh * hh, axis=1, keepdims=True) + NORM_EPS) * hw_ref[:, h * M_DV:(h + 1) * M_DV]
        ym_ref[:, h * M_DV:(h + 1) * M_DV] = (og_ref[:, h * M_DV:(h + 1) * M_DV] * hn).astype(BF16)
        m_new = m_col[L - 1:L, :]
        b_last = b_col[L - 1:L, :]
        w_end = jnp.exp(ig_col + b_last - b_col - m_new)
        decay = jnp.exp(b_last + m0 - m_new)
        vw = (w_end * v.astype(F32)).astype(BF16)
        c_ref[h] = decay * c0 + _dot_tn(vw, k)
        n_ref[h:h + 1, :] = decay * n0 + jnp.sum(w_end * k.astype(F32), axis=0, keepdims=True)
        m_ref[h:h + 1, :] = jnp.broadcast_to(m_new, (1, LANES))

    @pl.when(c == last_c)
    def _():
        for h in heads:
            cout_ref[h] = c_ref[h]
            nm_ref[h:h + 1, :] = n_ref[h:h + 1, :]
            nm_ref[M_HEADS + h:M_HEADS + h + 1, :] = m_ref[h:h + 1, :]


def _mlstm_prompt(qkv, sm, smt, sig, hw, batch):
    m = qkv.shape[0]
    og_block = sig.shape[1] // M_V - 1
    nc = (m // batch) // CHUNK
    row = lambda b, c: (b * nc + c, 0)
    return pl.pallas_call(
        _mlstm_kernel,
        grid=(batch, nc),
        in_specs=[pl.BlockSpec((CHUNK, M_QK), row),
                  pl.BlockSpec((CHUNK, M_QK), lambda b, c: (b * nc + c, 1)),
                  pl.BlockSpec((CHUNK, M_V), lambda b, c: (b * nc + c, 2 * M_QK // M_V)),
                  pl.BlockSpec((CHUNK, LANES), row),
                  pl.BlockSpec((SMALL_ROWS, CHUNK), lambda b, c: (0, b * nc + c)),
                  pl.BlockSpec((CHUNK, M_V), lambda b, c: (b * nc + c, og_block)),
                  pl.BlockSpec((1, M_V), lambda b, c: (0, 0))],
        out_specs=[pl.BlockSpec((CHUNK, M_V), row),
                   pl.BlockSpec((None, M_HEADS, M_DV, M_DK), lambda b, c: (b, 0, 0, 0)),
                   pl.BlockSpec((None, 2 * M_HEADS, M_DK), lambda b, c: (b, 0, 0))],
        out_shape=[jax.ShapeDtypeStruct((m, M_V), BF16),
                   jax.ShapeDtypeStruct((batch, M_HEADS, M_DV, M_DK), F32),
                   jax.ShapeDtypeStruct((batch, 2 * M_HEADS, M_DK), F32)],
        scratch_shapes=[pltpu.VMEM((M_HEADS, M_DV, M_DK), F32),
                        pltpu.VMEM((SUBLANES, M_DK), F32),
                        pltpu.VMEM((SUBLANES, LANES), F32)],
        compiler_params=_params("arbitrary", "arbitrary"),
        name="mlstm",
    )(qkv, qkv, qkv, sm, smt, sig, hw)


FOX_HEADS_PER_STEP = 2
FOX_ROW_CHUNK = 32


def _forget_columns(f, first):
    hi, mid, lo = _split3(f)
    lane = lax.broadcasted_iota(jnp.int32, (f.shape[0], LANES), 1)
    ones = jnp.where(jnp.abs(lane - (4 - first)) <= 1, 1.0, 0.0)
    ext = jnp.where(lane == first, hi.astype(F32),
                    jnp.where(lane == first + 1, mid.astype(F32),
                              jnp.where(lane == first + 2, lo.astype(F32), ones)))
    return ext.astype(BF16)


def _fox_kernel(q_ref, k_ref, v_ref, fcol_ref, o_ref, qaug_ref, kaug_ref, s_ref, p_ref, m_ref, l_ref, acc_ref):
    hp = pl.program_id(1)
    qi = pl.program_id(2)
    tq = q_ref.shape[0]
    t = k_ref.shape[0]
    nh = FOX_HEADS_PER_STEP

    def forget_sum(fc, h2):
        lane = lax.broadcasted_iota(jnp.int32, fc.shape, 1)
        sel = jnp.where(lane == LFF0 + nh * hp + h2, fc, 0.0)
        return jnp.sum(sel, axis=1, keepdims=True) * LOG2E

    @pl.when(qi == 0)
    def _():
        fc = fcol_ref[...]
        for h2 in range(nh):
            kaug_ref[h2, :, 0:F_DH] = k_ref[:, h2 * F_DH:(h2 + 1) * F_DH]
            kaug_ref[h2, :, F_DH:2 * F_DH] = _forget_columns(-forget_sum(fc, h2), 3)

    fq = fcol_ref[pl.ds(pl.multiple_of(qi * tq, tq), tq), :]
    for h2 in range(nh):
        qaug_ref[h2, :, 0:F_DH] = q_ref[:, h2 * F_DH:(h2 + 1) * F_DH]
        qaug_ref[h2, :, F_DH:2 * F_DH] = _forget_columns(forget_sum(fq, h2), 0)
    m_ref[...] = jnp.full_like(m_ref, -jnp.inf)
    l_ref[...] = jnp.zeros_like(l_ref)
    acc_ref[...] = jnp.zeros_like(acc_ref)
    rc = FOX_ROW_CHUNK
    n_lane_tiles = tq // LANES

    def softmax_rows(h2, c, masked):
        rows = slice(c * rc, (c + 1) * rc)
        s = s_ref[h2, rows, :]
        if masked:
            r = lax.broadcasted_iota(jnp.int32, (rc, tq), 0) + c * rc
            col = lax.broadcasted_iota(jnp.int32, (rc, tq), 1)
            s = jnp.where(r >= col, s, -jnp.inf)
        tiles = [s[:, k * LANES:(k + 1) * LANES] for k in range(n_lane_tiles)]
        m_old = m_ref[h2, rows, :]
        mx = functools.reduce(jnp.maximum, tiles)
        m_new = jnp.maximum(m_old, jnp.max(mx, axis=1, keepdims=True))
        alpha = jnp.exp2(m_old - m_new)
        ps = [jnp.exp2(tile - m_new) for tile in tiles]
        row_sum = jnp.sum(functools.reduce(jnp.add, ps), axis=1, keepdims=True)
        l_ref[h2, rows, :] = alpha * l_ref[h2, rows, :] + row_sum
        for k in range(n_lane_tiles):
            p_ref[h2, rows, k * LANES:(k + 1) * LANES] = ps[k].astype(BF16)
        acc_ref[h2, rows, :] = alpha * acc_ref[h2, rows, :]
        m_ref[h2, rows, :] = m_new

    def block(kj, masked):
        off = pl.multiple_of(kj * tq, tq)
        for h2 in range(nh):
            s_ref[h2] = _dot_nt(qaug_ref[h2], kaug_ref[h2, pl.ds(off, tq), :])
        for h2 in range(nh):
            for c in range(tq // rc):
                softmax_rows(h2, c, masked)
            acc_ref[h2] += _dot(p_ref[h2], v_ref[pl.ds(off, tq), h2 * F_DH:(h2 + 1) * F_DH])

    def body(kj, carry):
        block(kj, False)
        return carry

    lax.fori_loop(0, qi, body, 0)
    block(qi, True)
    for h2 in range(nh):
        o_ref[:, h2 * F_DH:(h2 + 1) * F_DH] = (acc_ref[h2] / l_ref[h2]).astype(BF16)


def _fox_prompt(qf, kfb, vfb, fcol, batch, tq=512):
    m = qf.shape[0]
    t = m // batch
    nq = t // tq
    nh = FOX_HEADS_PER_STEP
    wide = nh * F_DH
    return pl.pallas_call(
        _fox_kernel,
        grid=(batch, F_HEADS // nh, nq),
        in_specs=[pl.BlockSpec((tq, wide), lambda b, h, i: (b * nq + i, h)),
                  pl.BlockSpec((t, wide), lambda b, h, i: (b, h)),
                  pl.BlockSpec((t, wide), lambda b, h, i: (b, h)),
                  pl.BlockSpec((t, LANES), lambda b, h, i: (b, 0))],
        out_specs=pl.BlockSpec((tq, wide), lambda b, h, i: (b * nq + i, h)),
        out_shape=jax.ShapeDtypeStruct((m, F_W), BF16),
        scratch_shapes=[pltpu.VMEM((nh, tq, 2 * F_DH), BF16),
                        pltpu.VMEM((nh, t, 2 * F_DH), BF16),
                        pltpu.VMEM((nh, tq, tq), F32),
                        pltpu.VMEM((nh, tq, tq), BF16),
                        pltpu.VMEM((nh, tq, LANES), F32),
                        pltpu.VMEM((nh, tq, LANES), F32),
                        pltpu.VMEM((nh, tq, F_DH), F32)],
        compiler_params=_params("arbitrary", "arbitrary", "arbitrary"),
        name="fox",
    )(qf, kfb, vfb, fcol)


MSTEP_ROWS = 8


def _mstep_kernel(q_ref, k_ref, v_ref, sm_ref, og_ref, hw_ref, c0_ref, n0_ref, m0_ref,
                  ym_ref, cout_ref, nout_ref, mout_ref, cq_ref):
    rows = q_ref.shape[0]
    sm = sm_ref[...]
    ig = sm[:, IG0:IG0 + M_HEADS]
    lf = sm[:, LFM0:LFM0 + M_HEADS]
    m0 = m0_ref[...]
    m_new = lf + jnp.maximum(m0, ig - lf)
    w_end = jnp.exp(ig - m_new)
    decay = jnp.exp(lf + m0 - m_new)
    mout_ref[...] = m_new
    row_id = lax.broadcasted_iota(jnp.int32, (SUBLANES, M_DV), 0)

    for h in range(M_HEADS):
        qh = q_ref[:, h * M_DK:(h + 1) * M_DK]
        kh = k_ref[:, h * M_DK:(h + 1) * M_DK]
        vh = v_ref[:, h * M_DV:(h + 1) * M_DV]
        qf = qh.astype(F32)
        kf = kh.astype(F32)
        vf = vh.astype(F32)
        wh = w_end[:, h:h + 1]
        dh = decay[:, h:h + 1]
        for r in range(rows):
            c0 = c0_ref[r, h]
            q8 = jnp.broadcast_to(qh[r:r + 1, :], (SUBLANES, M_DK))
            cq_ref[r:r + 1, :] = _dot_nt(q8, c0.astype(BF16))[0:1, :]
            vw8 = jnp.where(row_id == 0, jnp.broadcast_to(wh[r:r + 1, :] * vf[r:r + 1, :], (SUBLANES, M_DV)), 0.0)
            k8 = jnp.broadcast_to(kh[r:r + 1, :], (SUBLANES, M_DK))
            cout_ref[r, h] = dh[r:r + 1, :] * c0 + _dot_tn(vw8.astype(BF16), k8)
        n0 = n0_ref[:, h, :]
        s = jnp.sum(qf * kf, axis=1, keepdims=True) * wh
        num = s * vf + dh * cq_ref[...]
        den = s + dh * jnp.sum(n0 * qf, axis=1, keepdims=True)
        hh = num / jnp.maximum(jnp.abs(den), jnp.exp(-m_new[:, h:h + 1]))
        hn = hh * lax.rsqrt(jnp.mean(hh * hh, axis=1, keepdims=True) + NORM_EPS) * hw_ref[:, h * M_DV:(h + 1) * M_DV]
        ym_ref[:, h * M_DV:(h + 1) * M_DV] = (og_ref[:, h * M_DV:(h + 1) * M_DV] * hn).astype(BF16)
        nout_ref[:, h, :] = dh * n0 + wh * kf


def _mlstm_step(qkv, sm, sig, hw, c0, n0, m0):
    db = qkv.shape[0]
    og_block = sig.shape[1] // M_V - 1
    rows = MSTEP_ROWS
    row = lambda i: (i, 0)
    return pl.pallas_call(
        _mstep_kernel,
        grid=(db // rows,),
        in_specs=[pl.BlockSpec((rows, M_QK), row),
                  pl.BlockSpec((rows, M_QK), lambda i: (i, 1)),
                  pl.BlockSpec((rows, M_V), lambda i: (i, 2 * M_QK // M_V)),
                  pl.BlockSpec((rows, LANES), row),
                  pl.BlockSpec((rows, M_V), lambda i: (i, og_block)),
                  pl.BlockSpec((1, M_V), lambda i: (0, 0)),
                  pl.BlockSpec((rows, M_HEADS, M_DV, M_DK), lambda i: (i, 0, 0, 0)),
                  pl.BlockSpec((rows, M_HEADS, M_DK), lambda i: (i, 0, 0)),
                  pl.BlockSpec((rows, M_HEADS), row)],
        out_specs=[pl.BlockSpec((rows, M_V), row),
                   pl.BlockSpec((rows, M_HEADS, M_DV, M_DK), lambda i: (i, 0, 0, 0)),
                   pl.BlockSpec((rows, M_HEADS, M_DK), lambda i: (i, 0, 0)),
                   pl.BlockSpec((rows, M_HEADS), row)],
        out_shape=[jax.ShapeDtypeStruct((db, M_V), BF16),
                   jax.ShapeDtypeStruct(c0.shape, F32),
                   jax.ShapeDtypeStruct(n0.shape, F32),
                   jax.ShapeDtypeStruct(m0.shape, F32)],
        scratch_shapes=[pltpu.VMEM((rows, M_DV), F32)],
        compiler_params=_params("arbitrary"),
        name="mstep",
    )(qkv, qkv, qkv, sm, sig, hw, c0, n0, m0)


N_MLSTM_IN = 7
N_MLSTM_OUT = 3
N_MLSTM_SCRATCH = 3


def _decode_kernel(*refs, n_pages, mlstm_steps_per_chunk, mlstm_chunks):
    n_dec_in = 3 * n_pages + 4
    n_ml = N_MLSTM_IN if mlstm_steps_per_chunk else 0
    dec_in = refs[1:1 + n_dec_in]
    ml_in = refs[1 + n_dec_in:1 + n_dec_in + n_ml]
    outs = refs[1 + n_dec_in + n_ml:]
    o_ref, outs = outs[0], outs[1:]
    if mlstm_steps_per_chunk:
        ml_out, outs = outs[:N_MLSTM_OUT], outs[N_MLSTM_OUT:]
        ml_scratch = outs[2:]
        step = pl.program_id(0)
        chunk = (step // mlstm_steps_per_chunk) % mlstm_chunks
        heads_per_step = M_HEADS // mlstm_steps_per_chunk
        for g in range(mlstm_steps_per_chunk):
            @pl.when(step % mlstm_steps_per_chunk == g)
            def _(g=g):
                _mlstm_chunk(chunk, mlstm_chunks - 1, range(g * heads_per_step, (g + 1) * heads_per_step),
                             *ml_in, *ml_out, *ml_scratch)
    _decode_sequence(dec_in[:n_pages], dec_in[n_pages:2 * n_pages], dec_in[2 * n_pages:3 * n_pages],
                     *dec_in[3 * n_pages:], o_ref, outs[0], outs[1])


def _decode_sequence(k_refs, v_refs, lf_refs, q_ref, kn_ref, vn_ref, sm_ref, o_ref, s_ref, xn_ref):
    n_pages = len(k_refs)
    page = xn_ref.shape[1]
    pw = page * F_HEADS

    for p in range(n_pages):
        xn_ref[p * F_HEADS:(p + 1) * F_HEADS, :] = lf_refs[p][...]
    xn = xn_ref[...] * LOG2E
    t_id = lax.broadcasted_iota(jnp.int32, (page, pw), 0)
    l_id = lax.broadcasted_iota(jnp.int32, (page, pw), 1)
    widen = jnp.where((l_id // F_HEADS) == t_id, 1.0, 0.0).astype(BF16)
    wide = _dot_f32_lhs(xn, widen)
    r_id = lax.broadcasted_iota(jnp.int32, wide.shape, 0)
    c_id = lax.broadcasted_iota(jnp.int32, wide.shape, 1)
    own = jnp.where((r_id & (F_HEADS - 1)) == (c_id & (F_HEADS - 1)), wide, 0.0)
    g_p = lax.broadcasted_iota(jnp.int32, (n_pages, n_pages * F_HEADS), 0)
    g_r = lax.broadcasted_iota(jnp.int32, (n_pages, n_pages * F_HEADS), 1)
    gather = jnp.where((g_r // F_HEADS) == g_p, 1.0, 0.0).astype(BF16)
    x = _dot_f32_rhs(gather, own)
    lane = lax.broadcasted_iota(jnp.int32, x.shape, 1)
    y, z = x, x
    step = F_HEADS
    while step < pw:
        y = y + jnp.where(lane < pw - step, pltpu.roll(y, pw - step, axis=1), 0.0)
        z = z + pltpu.roll(z, step, axis=1)
        step *= 2
    after = y - x

    r = lax.broadcasted_iota(jnp.int32, (LANES, pw), 0)
    c = lax.broadcasted_iota(jnp.int32, (LANES, pw), 1)
    spread = jnp.where(r == LFF0 + (c & (F_HEADS - 1)), 1.0, 0.0).astype(BF16)
    run = _dot_f32_lhs(jnp.broadcast_to(sm_ref[...] * LOG2E, (SUBLANES, LANES)), spread)[0:1, :]

    q = q_ref[...]
    sub = lax.broadcasted_iota(jnp.int32, (F_HEADS, pw), 0)
    lane_h = lax.broadcasted_iota(jnp.int32, (F_HEADS, pw), 1) & (F_HEADS - 1)
    valid = lane_h == sub
    s_self = jnp.sum(q.astype(F32) * kn_ref[...], axis=1, keepdims=True)
    m_run = s_self
    for p in reversed(range(n_pages)):
        kb = k_refs[p][...].astype(BF16)
        st = _dot_nt(q, kb) + (run + after[p:p + 1, :])
        st = jnp.where(valid, st, -jnp.inf)
        s_ref[p] = st
        m_run = jnp.maximum(m_run, jnp.max(st, axis=1, keepdims=True))
        run = run + z[p:p + 1, :]
    w_self = jnp.exp2(s_self - m_run)
    l = w_self
    acc = w_self * vn_ref[...]
    for p in range(n_pages):
        pe = jnp.exp2(s_ref[p] - m_run)
        l = l + jnp.sum(pe, axis=1, keepdims=True)
        acc = acc + _dot(pe.astype(BF16), v_refs[p][...].astype(BF16))
    o_ref[...] = (acc / l).astype(BF16)


def _mlstm_fits_decode(db, batch, seq_len):
    chunk_steps = batch * (seq_len // CHUNK)
    return db % chunk_steps == 0 and db // chunk_steps in (1, 2, 4)


def _fox_decode(page_table, ck, cv, clf, layer, q3, k3, v3, sm3, mlstm=None):
    db, n_pages = page_table.shape
    pw = ck.shape[2]
    page = clf.shape[3]
    seq = lambda b, pt: (b, 0, 0)

    def pg(p):
        return lambda b, pt: (layer, pt[b, p], 0, 0)

    in_specs = ([pl.BlockSpec((None, None, pw, F_DH), pg(p)) for p in range(n_pages)]
                + [pl.BlockSpec((None, None, pw, F_DH), pg(p)) for p in range(n_pages)]
                + [pl.BlockSpec((None, None, F_HEADS, page), pg(p)) for p in range(n_pages)]
                + [pl.BlockSpec((None, F_HEADS, F_DH), seq),
                   pl.BlockSpec((None, F_HEADS, F_DH), seq),
                   pl.BlockSpec((None, F_HEADS, F_DH), seq),
                   pl.BlockSpec((None, 1, LANES), seq)])
    args = [page_table, *([ck] * n_pages), *([cv] * n_pages), *([clf] * n_pages), q3, k3, v3, sm3]
    out_specs = [pl.BlockSpec((None, F_HEADS, F_DH), seq)]
    out_shape = [jax.ShapeDtypeStruct((db, F_HEADS, F_DH), BF16)]
    scratch = [pltpu.VMEM((n_pages, F_HEADS, pw), F32), pltpu.VMEM((n_pages * F_HEADS, page), F32)]
    per_chunk, nc = 0, 0
    if mlstm is not None:
        qkv, sm, smt, sig, hw, batch = mlstm
        m = qkv.shape[0]
        nc = (m // batch) // CHUNK
        per_chunk = db // (batch * nc)
        og_block = sig.shape[1] // M_V - 1
        row = lambda s, pt: (s // per_chunk, 0)
        in_specs += [pl.BlockSpec((CHUNK, M_QK), row),
                     pl.BlockSpec((CHUNK, M_QK), lambda s, pt: (s // per_chunk, 1)),
                     pl.BlockSpec((CHUNK, M_V), lambda s, pt: (s // per_chunk, 2 * M_QK // M_V)),
                     pl.BlockSpec((CHUNK, LANES), row),
                     pl.BlockSpec((SMALL_ROWS, CHUNK), lambda s, pt: (0, s // per_chunk)),
                     pl.BlockSpec((CHUNK, M_V), lambda s, pt: (s // per_chunk, og_block)),
                     pl.BlockSpec((1, M_V), lambda s, pt: (0, 0))]
        args += [qkv, qkv, qkv, sm, smt, sig, hw]
        seq_of = lambda s, pt: s // (per_chunk * nc)
        out_specs += [pl.BlockSpec((CHUNK, M_V), row),
                      pl.BlockSpec((None, M_HEADS, M_DV, M_DK), lambda s, pt: (seq_of(s, pt), 0, 0, 0)),
                      pl.BlockSpec((None, 2 * M_HEADS, M_DK), lambda s, pt: (seq_of(s, pt), 0, 0))]
        out_shape += [jax.ShapeDtypeStruct((m, M_V), BF16),
                      jax.ShapeDtypeStruct((batch, M_HEADS, M_DV, M_DK), F32),
                      jax.ShapeDtypeStruct((batch, 2 * M_HEADS, M_DK), F32)]
        scratch += [pltpu.VMEM((M_HEADS, M_DV, M_DK), F32),
                    pltpu.VMEM((SUBLANES, M_DK), F32),
                    pltpu.VMEM((SUBLANES, LANES), F32)]
    grid_spec = pltpu.PrefetchScalarGridSpec(
        num_scalar_prefetch=1,
        grid=(db,),
        in_specs=in_specs,
        out_specs=out_specs,
        scratch_shapes=scratch,
    )
    outs = pl.pallas_call(
        functools.partial(_decode_kernel, n_pages=n_pages, mlstm_steps_per_chunk=per_chunk, mlstm_chunks=nc),
        grid_spec=grid_spec,
        out_shape=out_shape,
        compiler_params=_params("arbitrary"),
        name="decode",
    )(*args)
    return outs if mlstm is not None else outs[0]


def _merge_kernel(ym_ref, yf_ref, sga_ref, sgb_ref, x_ref, g1_ref, sc_ref, sh_ref, n2w_ref,
                  wa_ref, wb_ref, wo_ref, x1_ref, h2_ref):
    merged = sga_ref[...] * _dot(ym_ref[...], wa_ref[...]) + sgb_ref[...] * _dot(yf_ref[...], wb_ref[...])
    x1 = x_ref[...] + g1_ref[...] * _dot(merged.astype(BF16), wo_ref[...])
    x1_ref[...] = x1
    rstd = lax.rsqrt(jnp.mean(x1 * x1, axis=-1, keepdims=True) + NORM_EPS)
    h2_ref[...] = ((x1 * rstd * n2w_ref[...]) * (1.0 + sc_ref[...]) + sh_ref[...]).astype(BF16)


def _merge(ym, yf, sig, x2, mod, n2w, wts, tm):
    m, d = x2.shape
    tiles_per_group = (m // tm) // mod.shape[0]
    group_of = lambda i: i // tiles_per_group
    row = lambda i: (i, 0)
    const = lambda i: (0, 0)
    resident = dict(pipeline_mode=pl.Buffered(1))
    return pl.pallas_call(
        _merge_kernel,
        grid=(m // tm,),
        in_specs=[pl.BlockSpec((tm, M_V), row),
                  pl.BlockSpec((tm, F_W), row),
                  pl.BlockSpec((tm, d), row),
                  pl.BlockSpec((tm, d), lambda i: (i, 1)),
                  pl.BlockSpec((tm, d), row),
                  _mod_spec(mod, d, MOD_G1, group_of),
                  _mod_spec(mod, d, MOD_SC2, group_of),
                  _mod_spec(mod, d, MOD_SH2, group_of),
                  pl.BlockSpec((1, d), const),
                  pl.BlockSpec((M_V, d), const, **resident),
                  pl.BlockSpec((F_W, d), const, **resident),
                  pl.BlockSpec((d, d), const, **resident)],
        out_specs=[pl.BlockSpec((tm, d), row), pl.BlockSpec((tm, d), row)],
        out_shape=[jax.ShapeDtypeStruct((m, d), F32), jax.ShapeDtypeStruct((m, d), BF16)],
        compiler_params=_params("arbitrary"),
        name="merge",
    )(ym, yf, sig, sig, x2, mod, mod, mod, n2w, wts["wa"], wts["wb"], wts["wo"])


FFN_TF = 512


def _ffn_kernel(h2_ref, x1_ref, g2_ref, p_ref, wa_ref, wg_ref, cw_ref, cb_ref, w2_ref,
                y_ref, a_out_ref, *rest, seq):
    f = pl.program_id(2)
    tm = h2_ref.shape[0]
    carry_ref = rest[-1]

    @pl.when(f == 0)
    def _():
        y_ref[...] = jnp.zeros_like(y_ref)

    if seq:
        @pl.when(pl.program_id(1) == 0)
        def _():
            carry_ref[f] = p_ref[...]
        wa, wg, w2 = wa_ref[...], wg_ref[...], w2_ref[...]
    else:
        wa, wg, w2 = wa_ref[...].astype(BF16), wg_ref[...].astype(BF16), w2_ref[...].astype(BF16)
        wa_out_ref, wg_out_ref, w2_out_ref = rest[:3]
        wa_out_ref[...] = wa
        wg_out_ref[...] = wg
        w2_out_ref[...] = w2

    h2 = h2_ref[...]
    a = _dot(h2, wa)
    g = _dot(h2, wg)
    if seq:
        prev = carry_ref[f]
        p2, p1 = prev[SUBLANES - 2:SUBLANES - 1, :], prev[SUBLANES - 1:SUBLANES, :]
        r = lax.broadcasted_iota(jnp.int32, a.shape, 0)
        a1 = jnp.where(r == 0, p1, pltpu.roll(a, 1, axis=0))
        a2 = jnp.where(r == 0, p2, jnp.where(r == 1, p1, pltpu.roll(a, 2, axis=0)))
        carry_ref[f] = a[tm - SUBLANES:tm, :]
        a_out_ref[...] = a[tm - SUBLANES:tm, :]
    else:
        a2, a1 = p_ref[0], p_ref[1]
        a_out_ref[...] = a
    ac = cw_ref[0:1, :] * a2 + cw_ref[1:2, :] * a1 + cw_ref[2:3, :] * a + cb_ref[...]
    u = (jax.nn.gelu(ac, approximate=True) * g).astype(BF16)
    y_ref[...] += _dot(u, w2)

    @pl.when(f == pl.num_programs(2) - 1)
    def _():
        y_ref[...] = x1_ref[...] + g2_ref[...] * y_ref[...]


def _ffn(h2, x1, mod, prev, wts, tm, seq):
    m, d = x1.shape
    groups = mod.shape[0]
    tiles = (m // tm) // groups
    dff = wts["w2"].shape[0]
    tf = FFN_TF
    nf = dff // tf
    row = lambda b, i, f: (b * tiles + i, 0)
    up_tile = pl.BlockSpec((None, d, tf), lambda b, i, f: (f, 0, 0))
    down_tile = pl.BlockSpec((tf, d), lambda b, i, f: (f, 0))
    out_specs = [pl.BlockSpec((tm, d), row)]
    out_shape = [jax.ShapeDtypeStruct((m, d), F32)]
    if seq:
        prev_spec = pl.BlockSpec((None, SUBLANES, tf), lambda b, i, f: (b, 0, f))
        out_specs.append(pl.BlockSpec((None, None, SUBLANES, tf), lambda b, i, f: (b, i, 0, f)))
        out_shape.append(jax.ShapeDtypeStruct((groups, tiles, SUBLANES, dff), F32))
        w_specs = [up_tile, up_tile]
        w_args = [wts["wa_t"], wts["wg_t"]]
    else:
        assert groups == 1 and tiles == 1, "the bf16 weight tiles are written once"
        prev_spec = pl.BlockSpec((CONV_W - 1, tm, tf), lambda b, i, f: (0, i, f))
        out_specs += [pl.BlockSpec((tm, tf), lambda b, i, f: (i, f)), up_tile, up_tile, down_tile]
        out_shape += [jax.ShapeDtypeStruct((m, dff), F32),
                      jax.ShapeDtypeStruct((nf, d, tf), BF16),
                      jax.ShapeDtypeStruct((nf, d, tf), BF16),
                      jax.ShapeDtypeStruct((dff, d), BF16)]
        w_specs = [pl.BlockSpec((d, tf), lambda b, i, f: (0, f)),
                   pl.BlockSpec((d, tf), lambda b, i, f: (0, f + nf))]
        w_args = [wts["w1"], wts["w1"]]
    return pl.pallas_call(
        functools.partial(_ffn_kernel, seq=seq),
        grid=(groups, tiles, nf),
        in_specs=[pl.BlockSpec((tm, d), row),
                  pl.BlockSpec((tm, d), row, pipeline_mode=pl.Buffered(1)),
                  _mod_spec(mod, d, MOD_G2, lambda b, i, f: b),
                  prev_spec,
                  *w_specs,
                  pl.BlockSpec((CONV_W, tf), lambda b, i, f: (0, f)),
                  pl.BlockSpec((1, tf), lambda b, i, f: (0, f)),
                  down_tile],
        out_specs=out_specs,
        out_shape=out_shape,
        scratch_shapes=[pltpu.VMEM((nf, SUBLANES, tf), F32)],
        compiler_params=_params("arbitrary", "arbitrary", "arbitrary"),
        name="ffn_seq" if seq else "ffn_row",
    )(h2, x1, mod, prev, *w_args, wts["cw"], wts["cb"], wts["w2"])


def _prep_weights(w_in, b_in, f_qnorm_w, f_knorm_w, w_proj_a, w_proj_b, w_out, w_ffn_in, conv_w, conv_b, w_ffn_out):
    o = _SPLIT_OFFS
    bbig = jnp.concatenate([b_in[:o[3]], b_in[o[10]:], b_in[o[3]:o[4]], b_in[o[6]:o[9]]])
    bsm = jnp.concatenate([b_in[o[4]:o[6]], b_in[o[9]:o[10]]])
    return {
        "w_in_t": w_in.T,
        "bbig": bbig.reshape(1, -1),
        "bsm": bsm.reshape(SMALL_ROWS, 1),
        "qnw": f_qnorm_w.reshape(1, F_DH),
        "knw": f_knorm_w.reshape(1, F_DH),
        "wa": w_proj_a.astype(BF16),
        "wb": w_proj_b.astype(BF16),
        "wo": w_out.astype(BF16),
        "w1": w_ffn_in,
        "cw": conv_w,
        "cb": conv_b.reshape(1, -1),
        "w2": w_ffn_out,
    }


def kernel(x_prompt, x_sample, c_prompt, c_sample, cache_k, cache_v, cache_logf, page_table, state_C, state_n, state_m, state_conv, w_ada, b_ada, norm1_w, w_in, b_in, m_hnorm_w, f_qnorm_w, f_knorm_w, w_proj_a, w_proj_b, w_out, norm2_w, w_ffn_in, conv_w, conv_b, w_ffn_out):
    depth = w_ada.shape[0]
    assert depth == 1, "single-layer step"
    batch, seq_len, d = x_prompt.shape
    db = x_sample.shape[0]
    assert x_sample.shape[1] == 1
    dff = w_ffn_out.shape[1]
    layer = 0

    wts = _prep_weights(w_in[layer], b_in[layer], f_qnorm_w[layer], f_knorm_w[layer], w_proj_a[layer],
                        w_proj_b[layer], w_out[layer], w_ffn_in[layer], conv_w[layer], conv_b[layer],
                        w_ffn_out[layer])
    n1w = norm1_w[layer].reshape(1, d)
    n2w = norm2_w[layer].reshape(1, d)
    hw = m_hnorm_w[layer].reshape(1, M_V)

    mod_p, mod_s = _ada(c_prompt, c_sample, w_ada[layer], b_ada[layer])
    mod_p = mod_p.reshape(batch, 1, 6 * d)
    mod_s = mod_s.reshape(1, db, 6 * d)

    def in_projection(x2, mod, tm):
        h, sm, smt = _norm(x2, mod, n1w, wts, tm)
        (qkv,) = _proj(h, wts, _J_MIX, _J_GATE, "mix", tm)
        (sig,) = _proj(h, wts, _J_GATE, _J_FQ, "sigmoid", tm)
        (qf,) = _proj(h, wts, _J_FQ, _J_FK, "headnorm", tm, nw=wts["qnw"], scale=FOX_Q_SCALE)
        kf, kfb = _proj(h, wts, _J_FK, _J_FV, "headnorm", tm, nw=wts["knw"], emit_f32=True)
        vf, vfb = _proj(h, wts, _J_FV, _J_END, "copy", tm)
        return qkv, sig, qf, kf, kfb, vf, vfb, sm, smt

    tm_p = min(1024, seq_len)
    xp = x_prompt.reshape(batch * seq_len, d)
    qkv, sig, qf, kf, kfb, vf, vfb, sm, smt = in_projection(xp, mod_p, tm_p)
    fcol = _cumsum(sm, batch)

    xs = x_sample.reshape(db, d)
    qkv_s, sig_s, qf_s, kf_s, _, vf_s, _, sm_s, _ = in_projection(xs, mod_s, db)
    n_pool, page = cache_k.shape[1], cache_k.shape[2]
    decode_args = (page_table,
                   cache_k.reshape(depth, n_pool, page * F_HEADS, F_DH),
                   cache_v.reshape(depth, n_pool, page * F_HEADS, F_DH),
                   cache_logf.transpose(0, 1, 3, 2),
                   layer,
                   qf_s.reshape(db, F_HEADS, F_DH), kf_s.reshape(db, F_HEADS, F_DH),
                   vf_s.reshape(db, F_HEADS, F_DH), sm_s.reshape(db, 1, LANES))
    if _mlstm_fits_decode(db, batch, seq_len):
        yf_s, ym, c_p, nm_p = _fox_decode(*decode_args, mlstm=(qkv, sm, smt, sig, hw, batch))
    else:
        yf_s = _fox_decode(*decode_args)
        ym, c_p, nm_p = _mlstm_prompt(qkv, sm, smt, sig, hw, batch)
    yf_s = yf_s.reshape(db, F_W)

    ym_s, c_s, n_s, m_s = _mlstm_step(qkv_s, sm_s, sig_s, hw, state_C[layer], state_n[layer], state_m[layer])
    x1_s, h2_s = _merge(ym_s, yf_s, sig_s, xs, mod_s, n2w, wts, db)
    prev_s = state_conv[layer].transpose(1, 0, 2)
    ys, a_s, wa_t, wg_t, w2_b = _ffn(h2_s, x1_s, mod_s, prev_s, wts, db, seq=False)
    wts_ffn = {"wa_t": wa_t, "wg_t": wg_t, "w2": w2_b, "cw": wts["cw"], "cb": wts["cb"]}

    yf = _fox_prompt(qf, kfb, vfb, fcol, batch)
    x1, h2 = _merge(ym, yf, sig, xp, mod_p, n2w, wts, min(512, seq_len))
    conv0 = jnp.zeros((batch, SUBLANES, dff), F32)
    yp, a_last = _ffn(h2, x1, mod_p, conv0, wts_ffn, tm_p, seq=True)

    y_prompt = yp.reshape(batch, seq_len, d)
    k_prompt = kf.reshape(depth, batch, seq_len, F_HEADS, F_DH)
    v_prompt = vf.reshape(depth, batch, seq_len, F_HEADS, F_DH)
    logf_prompt = sm[:, LFF0:LFF0 + F_HEADS].reshape(depth, batch, seq_len, F_HEADS)
    c_prompt_out = c_p[None]
    n_prompt = nm_p[:, :M_HEADS, :][None]
    m_prompt = nm_p[:, M_HEADS:, 0][None]
    conv_prompt = a_last[:, -1, SUBLANES - (CONV_W - 1):, :][None]

    y_sample = ys.reshape(db, 1, d)
    k_sample = kf_s.reshape(depth, db, 1, F_HEADS, F_DH)
    v_sample = vf_s.reshape(depth, db, 1, F_HEADS, F_DH)
    logf_sample = sm_s[:, LFF0:LFF0 + F_HEADS].reshape(depth, db, 1, F_HEADS)
    conv_sample = jnp.stack([state_conv[layer][:, CONV_W - 2, :], a_s], axis=1)[None]

    return (y_prompt, y_sample, k_prompt, v_prompt, logf_prompt, c_prompt_out, n_prompt, m_prompt, conv_prompt,
            k_sample, v_sample, logf_sample, c_s[None], n_s[None], m_s[None], conv_sample)
```

```python
import functools

import jax
import jax.numpy as jnp
from jax import lax
from jax.experimental import pallas as pl
from jax.experimental.pallas import tpu as pltpu

F32 = jnp.float32
BF16 = jnp.bfloat16

NORM_EPS = 1e-6
M_HEADS, M_DK, M_DV = 4, 128, 256
F_HEADS, F_DH = 8, 128
M_QK = M_HEADS * M_DK
M_V = M_HEADS * M_DV
F_W = F_HEADS * F_DH
CONV_W = 3
CHUNK = 128
LANES = 128
SUBLANES = 8
SMALL_ROWS = 16
VMEM_LIMIT = 56 * 1024 * 1024

IG0, LFM0, LFF0 = 0, M_HEADS, 2 * M_HEADS

LOG2E = 1.4426950408889634
FOX_Q_SCALE = F_DH ** -0.5 * LOG2E


def _dot(a, b):
    return jnp.dot(a, b, preferred_element_type=F32)


def _dot_nt(a, b):
    return lax.dot_general(a, b, (((1,), (1,)), ((), ())), preferred_element_type=F32)


def _dot_tn(a, b):
    return lax.dot_general(a, b, (((0,), (0,)), ((), ())), preferred_element_type=F32)


def _split3(x):
    hi = x.astype(BF16)
    r1 = x - hi.astype(F32)
    mid = r1.astype(BF16)
    lo = (r1 - mid.astype(F32)).astype(BF16)
    return hi, mid, lo


def _dot_f32_lhs(x, mat01):
    hi, mid, lo = _split3(x)
    return _dot(hi, mat01) + _dot(mid, mat01) + _dot(lo, mat01)


def _dot_f32_rhs(mat01, x):
    hi, mid, lo = _split3(x)
    return _dot(mat01, hi) + _dot(mat01, mid) + _dot(mat01, lo)


def _log_sigmoid(x):
    return jnp.minimum(x, 0.0) - jnp.log1p(jnp.exp(-jnp.abs(x)))


def _tri01(n, kind):
    r = lax.broadcasted_iota(jnp.int32, (n, n), 0)
    c = lax.broadcasted_iota(jnp.int32, (n, n), 1)
    if kind == "lower":
        m = c <= r
    elif kind == "upper":
        m = r <= c
    else:
        m = r > c
    return jnp.where(m, 1.0, 0.0).astype(BF16)


def _params(*sem):
    return pltpu.CompilerParams(dimension_semantics=sem, vmem_limit_bytes=VMEM_LIMIT)


def _ada_kernel(cp_ref, cs_ref, w_ref, b_ref, op_ref, os_ref):
    w = w_ref[...].astype(BF16)
    for c_ref, o_ref in ((cp_ref, op_ref), (cs_ref, os_ref)):
        c = c_ref[...]
        s = (c * jax.nn.sigmoid(c)).astype(BF16)
        o_ref[...] = _dot(s, w) + b_ref[...]


def _ada(c_p, c_s, w_ada, b_ada, tn=1024):
    d = c_p.shape[1]
    n = w_ada.shape[1]
    return pl.pallas_call(
        _ada_kernel,
        grid=(n // tn,),
        in_specs=[pl.BlockSpec(c_p.shape, lambda j: (0, 0)),
                  pl.BlockSpec(c_s.shape, lambda j: (0, 0)),
                  pl.BlockSpec((d, tn), lambda j: (0, j)),
                  pl.BlockSpec((1, tn), lambda j: (0, j))],
        out_specs=[pl.BlockSpec((c_p.shape[0], tn), lambda j: (0, j)),
                   pl.BlockSpec((c_s.shape[0], tn), lambda j: (0, j))],
        out_shape=[jax.ShapeDtypeStruct((c_p.shape[0], n), F32), jax.ShapeDtypeStruct((c_s.shape[0], n), F32)],
        compiler_params=_params("arbitrary"),
        name="ada",
    )(c_p, c_s, w_ada, b_ada.reshape(1, n))


MOD_SH1, MOD_SC1, MOD_G1, MOD_SH2, MOD_SC2, MOD_G2 = range(6)


PROJ_TN = 1024
_J_MIX, _J_GATE, _J_FQ, _J_FK, _J_FV, _J_END = 0, 2, 7, 8, 9, 10
_J_MO = _J_GATE + 4
_SPLIT_SIZES = (M_QK, M_QK, M_V, M_V, M_HEADS, M_HEADS, F_W, F_W, F_W, F_HEADS)
_SPLIT_OFFS = tuple(sum(_SPLIT_SIZES[:i]) for i in range(len(_SPLIT_SIZES) + 1))
_ROW_MO, _ROW_SMALL_M, _ROW_FQ, _ROW_SMALL_F, _ROW_GATES = (_SPLIT_OFFS[3], _SPLIT_OFFS[4], _SPLIT_OFFS[6],
                                                            _SPLIT_OFFS[9], _SPLIT_OFFS[10])
_TILE_RUNS = ((_J_MIX, 0), (_J_GATE, _ROW_GATES), (_J_MO, _ROW_MO), (_J_FQ, _ROW_FQ))


def _tile_row(jj):
    row = _TILE_RUNS[0][1] + PROJ_TN * (jj - _TILE_RUNS[0][0])
    for t0, r0 in _TILE_RUNS[1:]:
        row = jnp.where(jj >= t0, r0 + PROJ_TN * (jj - t0), row)
    return row


def _norm_kernel(x_ref, sc_ref, sh_ref, n1w_ref, wsm_m_ref, wsm_f_ref, bsm_ref, h_ref, sm_ref, smt_ref):
    x = x_ref[...]
    rstd = lax.rsqrt(jnp.mean(x * x, axis=-1, keepdims=True) + NORM_EPS)
    h = (x * rstd * n1w_ref[...]) * (1.0 + sc_ref[...]) + sh_ref[...]
    hb = h.astype(BF16)
    h_ref[...] = hb
    wsm = jnp.concatenate([wsm_m_ref[...], wsm_f_ref[...]], axis=0).astype(BF16)
    smt = _dot_nt(wsm, hb) + bsm_ref[...]
    row = lax.broadcasted_iota(jnp.int32, smt.shape, 0)
    smt = jnp.where(row < LFM0, smt, _log_sigmoid(smt))
    smt_ref[...] = smt
    pad = jnp.zeros((LANES - SMALL_ROWS, smt.shape[1]), F32)
    sm_ref[...] = jnp.concatenate([smt, pad], axis=0).T


def _proj_kernel(h_ref, w_ref, b_ref, *rest, kind, scale, emit_f32):
    *rest, wb_ref = rest

    @pl.when(pl.program_id(1) == 0)
    def _():
        wb_ref[...] = w_ref[...].astype(BF16)

    acc = _dot_nt(h_ref[...], wb_ref[...]) + b_ref[...]
    if kind == "mix":
        (ob_ref,) = rest
        lane = lax.broadcasted_iota(jnp.int32, (1, acc.shape[1]), 1)
        is_mk = jnp.logical_and(pl.program_id(0) == 0, lane >= M_QK)
        ob_ref[...] = (acc * jnp.where(is_mk, M_DK ** -0.5, 1.0)).astype(BF16)
    elif kind == "sigmoid":
        (ob_ref,) = rest
        ob_ref[...] = jax.nn.sigmoid(acc).astype(BF16)
    elif kind == "headnorm":
        nw_ref, *outs = rest
        for h in range(acc.shape[1] // F_DH):
            sl = acc[:, h * F_DH:(h + 1) * F_DH]
            ms = jnp.mean(sl * sl, axis=-1, keepdims=True)
            y = sl * lax.rsqrt(ms + NORM_EPS) * nw_ref[...]
            if emit_f32:
                outs[0][:, h * F_DH:(h + 1) * F_DH] = y
            outs[-1][:, h * F_DH:(h + 1) * F_DH] = (y * scale).astype(BF16)
    else:
        of_ref, ob_ref = rest
        of_ref[...] = acc
        ob_ref[...] = acc.astype(BF16)


def _feature_rows_spec(rows, d, first_row):
    def index_map(*g):
        return pl.multiple_of(jnp.asarray(first_row(*g), jnp.int32), SUBLANES), 0

    return pl.BlockSpec((pl.Element(rows), pl.Element(d)), index_map)


def _mod_spec(mod, d, chunk, group_of):
    return pl.BlockSpec((None, mod.shape[1], d), lambda *g: (group_of(*g), 0, chunk))


def _norm(x2, mod, n1w, wts, tm):
    m, d = x2.shape
    tiles_per_group = (m // tm) // mod.shape[0]
    group_of = lambda i: i // tiles_per_group
    return pl.pallas_call(
        _norm_kernel,
        grid=(m // tm,),
        in_specs=[pl.BlockSpec((tm, d), lambda i: (i, 0)),
                  _mod_spec(mod, d, MOD_SC1, group_of),
                  _mod_spec(mod, d, MOD_SH1, group_of),
                  pl.BlockSpec((1, d), lambda i: (0, 0)),
                  _feature_rows_spec(SUBLANES, d, lambda i: _ROW_SMALL_M),
                  _feature_rows_spec(SUBLANES, d, lambda i: _ROW_SMALL_F),
                  pl.BlockSpec((SMALL_ROWS, 1), lambda i: (0, 0))],
        out_specs=[pl.BlockSpec((tm, d), lambda i: (i, 0)),
                   pl.BlockSpec((tm, LANES), lambda i: (i, 0)),
                   pl.BlockSpec((SMALL_ROWS, tm), lambda i: (0, i))],
        out_shape=[jax.ShapeDtypeStruct((m, d), BF16),
                   jax.ShapeDtypeStruct((m, LANES), F32),
                   jax.ShapeDtypeStruct((SMALL_ROWS, m), F32)],
        compiler_params=_params("arbitrary"),
        name="norm",
    )(x2, mod, mod, n1w, wts["w_in_t"], wts["w_in_t"], wts["bsm"])


def _proj(h, wts, j0, j1, kind, tm, nw=None, scale=1.0, emit_f32=False):
    m, d = h.shape
    tn = PROJ_TN
    width = (j1 - j0) * tn
    tile = pl.BlockSpec((tm, tn), lambda j, i: (i, j))
    in_specs = [pl.BlockSpec((tm, d), lambda j, i: (i, 0)),
                _feature_rows_spec(tn, d, lambda j, i: _tile_row(j0 + j)),
                pl.BlockSpec((1, tn), lambda j, i: (0, j0 + j))]
    args = [h, wts["w_in_t"], wts["bbig"]]
    if kind == "headnorm":
        in_specs.append(pl.BlockSpec((1, F_DH), lambda j, i: (0, 0)))
        args.append(nw)
    two = emit_f32 or kind == "copy"
    out_shape = ([jax.ShapeDtypeStruct((m, width), F32)] if two else []) + [jax.ShapeDtypeStruct((m, width), BF16)]
    return pl.pallas_call(
        functools.partial(_proj_kernel, kind=kind, scale=scale, emit_f32=emit_f32),
        grid=(j1 - j0, m // tm),
        in_specs=in_specs,
        out_specs=[tile] * len(out_shape),
        out_shape=out_shape,
        scratch_shapes=[pltpu.VMEM((tn, d), BF16)],
        compiler_params=_params("arbitrary", "arbitrary"),
        name="proj_" + kind,
    )(*args)


def _cumsum_kernel(sm_ref, fcol_ref, carry_ref):
    c = pl.program_id(1)
    tc = sm_ref.shape[0]

    @pl.when(c == 0)
    def _():
        carry_ref[...] = jnp.zeros_like(carry_ref)

    fc = _dot_f32_rhs(_tri01(tc, "lower"), sm_ref[...]) + carry_ref[0:1, :]
    fcol_ref[...] = fc
    carry_ref[...] = jnp.broadcast_to(fc[tc - 1:tc, :], carry_ref.shape)


def _cumsum(sm, batch, tc=512):
    m = sm.shape[0]
    nt = (m // batch) // tc
    return pl.pallas_call(
        _cumsum_kernel,
        grid=(batch, nt),
        in_specs=[pl.BlockSpec((tc, LANES), lambda b, c: (b * nt + c, 0))],
        out_specs=pl.BlockSpec((tc, LANES), lambda b, c: (b * nt + c, 0)),
        out_shape=jax.ShapeDtypeStruct((m, LANES), F32),
        scratch_shapes=[pltpu.VMEM((SUBLANES, LANES), F32)],
        compiler_params=_params("arbitrary", "arbitrary"),
        name="cumsum",
    )(sm)


def _mlstm_kernel(*refs):
    _mlstm_chunk(pl.program_id(1), pl.num_programs(1) - 1, range(M_HEADS), *refs)


def _mlstm_chunk(c, last_c, heads, q_ref, k_ref, v_ref, sm_ref, smt_ref, og_ref, hw_ref,
                 ym_ref, cout_ref, nm_ref, c_ref, n_ref, m_ref):
    L = q_ref.shape[0]

    @pl.when(c == 0)
    def _():
        for h in heads:
            c_ref[h] = jnp.zeros((M_DV, M_DK), F32)
            n_ref[h:h + 1, :] = jnp.zeros((1, M_DK), F32)
            m_ref[h:h + 1, :] = jnp.zeros((1, LANES), F32)

    r = lax.broadcasted_iota(jnp.int32, (L, L), 0)
    s_idx = lax.broadcasted_iota(jnp.int32, (L, L), 1)
    causal = s_idx <= r
    sm = sm_ref[...]
    smt = smt_ref[...]
    bcol_all = _dot_f32_rhs(_tri01(L, "lower"), sm)
    brow_all = _dot_f32_lhs(smt, _tri01(L, "upper"))

    for h in heads:
        ig_col = sm[:, IG0 + h:IG0 + h + 1]
        b_col = bcol_all[:, LFM0 + h:LFM0 + h + 1]
        ig_row = smt[IG0 + h:IG0 + h + 1, :]
        b_row = brow_all[LFM0 + h:LFM0 + h + 1, :]
        m0 = m_ref[h:h + 1, 0:1]
        a_row = ig_row - b_row
        cm_col = jnp.max(jnp.where(causal, a_row, -jnp.inf), axis=1, keepdims=True)
        mx_col = jnp.maximum(m0, cm_col)
        m_col = b_col + mx_col
        dmat = jnp.exp(jnp.where(causal, a_row - mx_col, -jnp.inf))
        inter = jnp.exp(m0 - mx_col)
        q = q_ref[:, h * M_DK:(h + 1) * M_DK]
        k = k_ref[:, h * M_DK:(h + 1) * M_DK]
        v = v_ref[:, h * M_DV:(h + 1) * M_DV]
        c0 = c_ref[h]
        n0 = n_ref[h:h + 1, :]
        s = _dot_nt(q, k) * dmat
        num = _dot(s.astype(BF16), v) + inter * _dot_nt(q, c0.astype(BF16))
        den = jnp.sum(s, axis=1, keepdims=True) + inter * jnp.sum(q.astype(F32) * n0, axis=1, keepdims=True)
        hh = num / jnp.maximum(jnp.abs(den), jnp.exp(-m_col))
        hn = hh * lax.rsqrt(jnp.mean(hh * hh, axis=1, keepdims=True) + NORM_EPS) * hw_ref[:, h * M_DV:(h + 1) * M_DV]
        ym_ref[:, h * M_DV:(h + 1) * M_DV] = (og_ref[:, h * M_DV:(h + 1) * M_DV] * hn).astype(BF16)
        m_new = m_col[L - 1:L, :]
        b_last = b_col[L - 1:L, :]
        w_end = jnp.exp(ig_col + b_last - b_col - m_new)
        decay = jnp.exp(b_last + m0 - m_new)
        vw = (w_end * v.astype(F32)).astype(BF16)
        c_ref[h] = decay * c0 + _dot_tn(vw, k)
        n_ref[h:h + 1, :] = decay * n0 + jnp.sum(w_end * k.astype(F32), axis=0, keepdims=True)
        m_ref[h:h + 1, :] = jnp.broadcast_to(m_new, (1, LANES))

    @pl.when(c == last_c)
    def _():
        for h in heads:
            cout_ref[h] = c_ref[h]
            nm_ref[h:h + 1, :] = n_ref[h:h + 1, :]
            nm_ref[M_HEADS + h:M_HEADS + h + 1, :] = m_ref[h:h + 1, :]


def _mlstm_prompt(qkv, sm, smt, sig, hw, batch):
    m = qkv.shape[0]
    og_block = sig.shape[1] // M_V - 1
    nc = (m // batch) // CHUNK
    row = lambda b, c: (b * nc + c, 0)
    return pl.pallas_call(
        _mlstm_kernel,
        grid=(batch, nc),
        in_specs=[pl.BlockSpec((CHUNK, M_QK), row),
                  pl.BlockSpec((CHUNK, M_QK), lambda b, c: (b * nc + c, 1)),
                  pl.BlockSpec((CHUNK, M_V), lambda b, c: (b * nc + c, 2 * M_QK // M_V)),
                  pl.BlockSpec((CHUNK, LANES), row),
                  pl.BlockSpec((SMALL_ROWS, CHUNK), lambda b, c: (0, b * nc + c)),
                  pl.BlockSpec((CHUNK, M_V), lambda b, c: (b * nc + c, og_block)),
                  pl.BlockSpec((1, M_V), lambda b, c: (0, 0))],
        out_specs=[pl.BlockSpec((CHUNK, M_V), row),
                   pl.BlockSpec((None, M_HEADS, M_DV, M_DK), lambda b, c: (b, 0, 0, 0)),
                   pl.BlockSpec((None, 2 * M_HEADS, M_DK), lambda b, c: (b, 0, 0))],
        out_shape=[jax.ShapeDtypeStruct((m, M_V), BF16),
                   jax.ShapeDtypeStruct((batch, M_HEADS, M_DV, M_DK), F32),
                   jax.ShapeDtypeStruct((batch, 2 * M_HEADS, M_DK), F32)],
        scratch_shapes=[pltpu.VMEM((M_HEADS, M_DV, M_DK), F32),
                        pltpu.VMEM((SUBLANES, M_DK), F32),
                        pltpu.VMEM((SUBLANES, LANES), F32)],
        compiler_params=_params("arbitrary", "arbitrary"),
        name="mlstm",
    )(qkv, qkv, qkv, sm, smt, sig, hw)


FOX_HEADS_PER_STEP = 4
FOX_ROW_CHUNK = 32


def _forget_columns(f, first):
    hi, mid, lo = _split3(f)
    lane = lax.broadcasted_iota(jnp.int32, (f.shape[0], LANES), 1)
    ones = jnp.where(jnp.abs(lane - (4 - first)) <= 1, 1.0, 0.0)
    ext = jnp.where(lane == first, hi.astype(F32),
                    jnp.where(lane == first + 1, mid.astype(F32),
                              jnp.where(lane == first + 2, lo.astype(F32), ones)))
    return ext.astype(BF16)


def _fox_kernel(q_ref, k_ref, v_ref, fcol_ref, o_ref, qaug_ref, kaug_ref, s_ref, p_ref, m_ref, l_ref, acc_ref):
    hp = pl.program_id(1)
    qi = pl.program_id(2)
    tq = q_ref.shape[0]
    t = k_ref.shape[0]
    nh = FOX_HEADS_PER_STEP

    def forget_sum(fc, h2):
        lane = lax.broadcasted_iota(jnp.int32, fc.shape, 1)
        sel = jnp.where(lane == LFF0 + nh * hp + h2, fc, 0.0)
        return jnp.sum(sel, axis=1, keepdims=True) * LOG2E

    @pl.when(qi == 0)
    def _():
        fc = fcol_ref[...]
        for h2 in range(nh):
            kaug_ref[h2, :, 0:F_DH] = k_ref[:, h2 * F_DH:(h2 + 1) * F_DH]
            kaug_ref[h2, :, F_DH:2 * F_DH] = _forget_columns(-forget_sum(fc, h2), 3)

    fq = fcol_ref[pl.ds(pl.multiple_of(qi * tq, tq), tq), :]
    for h2 in range(nh):
        qaug_ref[h2, :, 0:F_DH] = q_ref[:, h2 * F_DH:(h2 + 1) * F_DH]
        qaug_ref[h2, :, F_DH:2 * F_DH] = _forget_columns(forget_sum(fq, h2), 0)
    m_ref[...] = jnp.full_like(m_ref, -jnp.inf)
    l_ref[...] = jnp.zeros_like(l_ref)
    acc_ref[...] = jnp.zeros_like(acc_ref)
    rc = FOX_ROW_CHUNK
    n_lane_tiles = tq // LANES

    def softmax_rows(h2, c, masked):
        rows = slice(c * rc, (c + 1) * rc)
        s = s_ref[h2, rows, :]
        if masked:
            r = lax.broadcasted_iota(jnp.int32, (rc, tq), 0) + c * rc
            col = lax.broadcasted_iota(jnp.int32, (rc, tq), 1)
            s = jnp.where(r >= col, s, -jnp.inf)
        tiles = [s[:, k * LANES:(k + 1) * LANES] for k in range(n_lane_tiles)]
        m_old = m_ref[h2, rows, :]
        mx = functools.reduce(jnp.maximum, tiles)
        m_new = jnp.maximum(m_old, jnp.max(mx, axis=1, keepdims=True))
        alpha = jnp.exp2(m_old - m_new)
        ps = [jnp.exp2(tile - m_new) for tile in tiles]
        row_sum = jnp.sum(functools.reduce(jnp.add, ps), axis=1, keepdims=True)
        l_ref[h2, rows, :] = alpha * l_ref[h2, rows, :] + row_sum
        for k in range(n_lane_tiles):
            p_ref[h2, rows, k * LANES:(k + 1) * LANES] = ps[k].astype(BF16)
        acc_ref[h2, rows, :] = alpha * acc_ref[h2, rows, :]
        m_ref[h2, rows, :] = m_new

    def block(kj, masked):
        off = pl.multiple_of(kj * tq, tq)
        for h2 in range(nh):
            s_ref[h2] = _dot_nt(qaug_ref[h2], kaug_ref[h2, pl.ds(off, tq), :])
        for h2 in range(nh):
            for c in range(tq // rc):
                softmax_rows(h2, c, masked)
            acc_ref[h2] += _dot(p_ref[h2], v_ref[pl.ds(off, tq), h2 * F_DH:(h2 + 1) * F_DH])

    def body(kj, carry):
        block(kj, False)
        return carry

    lax.fori_loop(0, qi, body, 0)
    block(qi, True)
    for h2 in range(nh):
        o_ref[:, h2 * F_DH:(h2 + 1) * F_DH] = (acc_ref[h2] / l_ref[h2]).astype(BF16)


def _fox_prompt(qf, kfb, vfb, fcol, batch, tq=512):
    m = qf.shape[0]
    t = m // batch
    nq = t // tq
    nh = FOX_HEADS_PER_STEP
    wide = nh * F_DH
    return pl.pallas_call(
        _fox_kernel,
        grid=(batch, F_HEADS // nh, nq),
        in_specs=[pl.BlockSpec((tq, wide), lambda b, h, i: (b * nq + i, h)),
                  pl.BlockSpec((t, wide), lambda b, h, i: (b, h)),
                  pl.BlockSpec((t, wide), lambda b, h, i: (b, h)),
                  pl.BlockSpec((t, LANES), lambda b, h, i: (b, 0))],
        out_specs=pl.BlockSpec((tq, wide), lambda b, h, i: (b * nq + i, h)),
        out_shape=jax.ShapeDtypeStruct((m, F_W), BF16),
        scratch_shapes=[pltpu.VMEM((nh, tq, 2 * F_DH), BF16),
                        pltpu.VMEM((nh, t, 2 * F_DH), BF16),
                        pltpu.VMEM((nh, tq, tq), F32),
                        pltpu.VMEM((nh, tq, tq), BF16),
                        pltpu.VMEM((nh, tq, LANES), F32),
                        pltpu.VMEM((nh, tq, LANES), F32),
                        pltpu.VMEM((nh, tq, F_DH), F32)],
        compiler_params=_params("arbitrary", "arbitrary", "arbitrary"),
        name="fox",
    )(qf, kfb, vfb, fcol)


MSTEP_ROWS = 8


def _mstep_kernel(q_ref, k_ref, v_ref, sm_ref, og_ref, hw_ref, c0_ref, n0_ref, m0_ref,
                  ym_ref, cout_ref, nout_ref, mout_ref, cq_ref):
    rows = q_ref.shape[0]
    sm = sm_ref[...]
    ig = sm[:, IG0:IG0 + M_HEADS]
    lf = sm[:, LFM0:LFM0 + M_HEADS]
    m0 = m0_ref[...]
    m_new = lf + jnp.maximum(m0, ig - lf)
    w_end = jnp.exp(ig - m_new)
    decay = jnp.exp(lf + m0 - m_new)
    mout_ref[...] = m_new
    row_id = lax.broadcasted_iota(jnp.int32, (SUBLANES, M_DV), 0)

    for h in range(M_HEADS):
        qh = q_ref[:, h * M_DK:(h + 1) * M_DK]
        kh = k_ref[:, h * M_DK:(h + 1) * M_DK]
        vh = v_ref[:, h * M_DV:(h + 1) * M_DV]
        qf = qh.astype(F32)
        kf = kh.astype(F32)
        vf = vh.astype(F32)
        wh = w_end[:, h:h + 1]
        dh = decay[:, h:h + 1]
        for r in range(rows):
            c0 = c0_ref[r, h]
            q8 = jnp.broadcast_to(qh[r:r + 1, :], (SUBLANES, M_DK))
            cq_ref[r:r + 1, :] = _dot_nt(q8, c0.astype(BF16))[0:1, :]
            vw8 = jnp.where(row_id == 0, jnp.broadcast_to(wh[r:r + 1, :] * vf[r:r + 1, :], (SUBLANES, M_DV)), 0.0)
            k8 = jnp.broadcast_to(kh[r:r + 1, :], (SUBLANES, M_DK))
            cout_ref[r, h] = dh[r:r + 1, :] * c0 + _dot_tn(vw8.astype(BF16), k8)
        n0 = n0_ref[:, h, :]
        s = jnp.sum(qf * kf, axis=1, keepdims=True) * wh
        num = s * vf + dh * cq_ref[...]
        den = s + dh * jnp.sum(n0 * qf, axis=1, keepdims=True)
        hh = num / jnp.maximum(jnp.abs(den), jnp.exp(-m_new[:, h:h + 1]))
        hn = hh * lax.rsqrt(jnp.mean(hh * hh, axis=1, keepdims=True) + NORM_EPS) * hw_ref[:, h * M_DV:(h + 1) * M_DV]
        ym_ref[:, h * M_DV:(h + 1) * M_DV] = (og_ref[:, h * M_DV:(h + 1) * M_DV] * hn).astype(BF16)
        nout_ref[:, h, :] = dh * n0 + wh * kf


def _mlstm_step(qkv, sm, sig, hw, c0, n0, m0):
    db = qkv.shape[0]
    og_block = sig.shape[1] // M_V - 1
    rows = MSTEP_ROWS
    row = lambda i: (i, 0)
    return pl.pallas_call(
        _mstep_kernel,
        grid=(db // rows,),
        in_specs=[pl.BlockSpec((rows, M_QK), row),
                  pl.BlockSpec((rows, M_QK), lambda i: (i, 1)),
                  pl.BlockSpec((rows, M_V), lambda i: (i, 2 * M_QK // M_V)),
                  pl.BlockSpec((rows, LANES), row),
                  pl.BlockSpec((rows, M_V), lambda i: (i, og_block)),
                  pl.BlockSpec((1, M_V), lambda i: (0, 0)),
                  pl.BlockSpec((rows, M_HEADS, M_DV, M_DK), lambda i: (i, 0, 0, 0)),
                  pl.BlockSpec((rows, M_HEADS, M_DK), lambda i: (i, 0, 0)),
                  pl.BlockSpec((rows, M_HEADS), row)],
        out_specs=[pl.BlockSpec((rows, M_V), row),
                   pl.BlockSpec((rows, M_HEADS, M_DV, M_DK), lambda i: (i, 0, 0, 0)),
                   pl.BlockSpec((rows, M_HEADS, M_DK), lambda i: (i, 0, 0)),
                   pl.BlockSpec((rows, M_HEADS), row)],
        out_shape=[jax.ShapeDtypeStruct((db, M_V), BF16),
                   jax.ShapeDtypeStruct(c0.shape, F32),
                   jax.ShapeDtypeStruct(n0.shape, F32),
                   jax.ShapeDtypeStruct(m0.shape, F32)],
        scratch_shapes=[pltpu.VMEM((rows, M_DV), F32)],
        compiler_params=_params("arbitrary"),
        name="mstep",
    )(qkv, qkv, qkv, sm, sig, hw, c0, n0, m0)


N_MLSTM_IN = 7
N_MLSTM_OUT = 3
N_MLSTM_SCRATCH = 3
N_DECODE_SCRATCH = 4


def _decode_kernel(*refs, n_pages, mlstm_steps_per_chunk, mlstm_chunks):
    n_dec_in = 3 * n_pages + 4
    n_ml = N_MLSTM_IN if mlstm_steps_per_chunk else 0
    dec_in = refs[1:1 + n_dec_in]
    ml_in = refs[1 + n_dec_in:1 + n_dec_in + n_ml]
    outs = refs[1 + n_dec_in + n_ml:]
    o_ref, outs = outs[0], outs[1:]
    if mlstm_steps_per_chunk:
        ml_out, outs = outs[:N_MLSTM_OUT], outs[N_MLSTM_OUT:]
        ml_scratch = outs[N_DECODE_SCRATCH:]
        step = pl.program_id(0)
        chunk = (step // mlstm_steps_per_chunk) % mlstm_chunks
        heads_per_step = M_HEADS // mlstm_steps_per_chunk
        for g in range(mlstm_steps_per_chunk):
            @pl.when(step % mlstm_steps_per_chunk == g)
            def _(g=g):
                _mlstm_chunk(chunk, mlstm_chunks - 1, range(g * heads_per_step, (g + 1) * heads_per_step),
                             *ml_in, *ml_out, *ml_scratch)
    _decode_sequence(dec_in[:n_pages], dec_in[n_pages:2 * n_pages], dec_in[2 * n_pages:3 * n_pages],
                     *dec_in[3 * n_pages:], o_ref, *outs[:N_DECODE_SCRATCH])


def _decode_sequence(k_refs, v_refs, lf_refs, q_ref, kn_ref, vn_ref, sm_ref, o_ref,
                     s_ref, xn_ref, widen_ref, spread_ref):
    n_pages = len(k_refs)
    page = xn_ref.shape[1]
    pw = page * F_HEADS

    for p in range(n_pages):
        xn_ref[p * F_HEADS:(p + 1) * F_HEADS, :] = lf_refs[p][...]
    xn = xn_ref[...] * LOG2E

    @pl.when(pl.program_id(0) == 0)
    def _():
        t_id = lax.broadcasted_iota(jnp.int32, (page, pw), 0)
        l_id = lax.broadcasted_iota(jnp.int32, (page, pw), 1)
        widen_ref[...] = jnp.where((l_id // F_HEADS) == t_id, 1.0, 0.0).astype(BF16)
        r = lax.broadcasted_iota(jnp.int32, (LANES, pw), 0)
        c = lax.broadcasted_iota(jnp.int32, (LANES, pw), 1)
        spread_ref[...] = jnp.where(r == LFF0 + (c & (F_HEADS - 1)), 1.0, 0.0).astype(BF16)

    wide = _dot_f32_lhs(xn, widen_ref[...])
    r_id = lax.broadcasted_iota(jnp.int32, wide.shape, 0)
    c_id = lax.broadcasted_iota(jnp.int32, wide.shape, 1)
    own = jnp.where((r_id & (F_HEADS - 1)) == (c_id & (F_HEADS - 1)), wide, 0.0)
    g_p = lax.broadcasted_iota(jnp.int32, (n_pages, n_pages * F_HEADS), 0)
    g_r = lax.broadcasted_iota(jnp.int32, (n_pages, n_pages * F_HEADS), 1)
    gather = jnp.where((g_r // F_HEADS) == g_p, 1.0, 0.0).astype(BF16)
    x = _dot_f32_rhs(gather, own)
    lane = lax.broadcasted_iota(jnp.int32, x.shape, 1)
    y, z = x, x
    step = F_HEADS
    while step < pw:
        y = y + jnp.where(lane < pw - step, pltpu.roll(y, pw - step, axis=1), 0.0)
        z = z + pltpu.roll(z, step, axis=1)
        step *= 2
    after = y - x

    run = _dot_f32_lhs(jnp.broadcast_to(sm_ref[...] * LOG2E, (SUBLANES, LANES)), spread_ref[...])[0:1, :]

    q = q_ref[...]
    sub = lax.broadcasted_iota(jnp.int32, (F_HEADS, pw), 0)
    lane_h = lax.broadcasted_iota(jnp.int32, (F_HEADS, pw), 1) & (F_HEADS - 1)
    valid = lane_h == sub
    s_self = jnp.sum(q.astype(F32) * kn_ref[...], axis=1, keepdims=True)
    m_run = s_self
    for p in reversed(range(n_pages)):
        kb = k_refs[p][...].astype(BF16)
        st = _dot_nt(q, kb) + (run + after[p:p + 1, :])
        st = jnp.where(valid, st, -jnp.inf)
        s_ref[p] = st
        m_run = jnp.maximum(m_run, jnp.max(st, axis=1, keepdims=True))
        run = run + z[p:p + 1, :]
    w_self = jnp.exp2(s_self - m_run)
    l = w_self
    acc = w_self * vn_ref[...]
    for p in range(n_pages):
        pe = jnp.exp2(s_ref[p] - m_run)
        l = l + jnp.sum(pe, axis=1, keepdims=True)
        acc = acc + _dot(pe.astype(BF16), v_refs[p][...].astype(BF16))
    o_ref[...] = (acc / l).astype(BF16)


def _mlstm_fits_decode(db, batch, seq_len):
    chunk_steps = batch * (seq_len // CHUNK)
    return db % chunk_steps == 0 and db // chunk_steps in (1, 2, 4)


def _fox_decode(page_table, ck, cv, clf, layer, q3, k3, v3, sm3, mlstm=None):
    db, n_pages = page_table.shape
    pw = ck.shape[2]
    page = clf.shape[3]
    seq = lambda b, pt: (b, 0, 0)

    def pg(p):
        return lambda b, pt: (layer, pt[b, p], 0, 0)

    in_specs = ([pl.BlockSpec((None, None, pw, F_DH), pg(p)) for p in range(n_pages)]
                + [pl.BlockSpec((None, None, pw, F_DH), pg(p)) for p in range(n_pages)]
                + [pl.BlockSpec((None, None, F_HEADS, page), pg(p)) for p in range(n_pages)]
                + [pl.BlockSpec((None, F_HEADS, F_DH), seq),
                   pl.BlockSpec((None, F_HEADS, F_DH), seq),
                   pl.BlockSpec((None, F_HEADS, F_DH), seq),
                   pl.BlockSpec((None, 1, LANES), seq)])
    args = [page_table, *([ck] * n_pages), *([cv] * n_pages), *([clf] * n_pages), q3, k3, v3, sm3]
    out_specs = [pl.BlockSpec((None, F_HEADS, F_DH), seq)]
    out_shape = [jax.ShapeDtypeStruct((db, F_HEADS, F_DH), BF16)]
    scratch = [pltpu.VMEM((n_pages, F_HEADS, pw), F32), pltpu.VMEM((n_pages * F_HEADS, page), F32),
               pltpu.VMEM((page, pw), BF16), pltpu.VMEM((LANES, pw), BF16)]
    per_chunk, nc = 0, 0
    if mlstm is not None:
        qkv, sm, smt, sig, hw, batch = mlstm
        m = qkv.shape[0]
        nc = (m // batch) // CHUNK
        per_chunk = db // (batch * nc)
        og_block = sig.shape[1] // M_V - 1
        row = lambda s, pt: (s // per_chunk, 0)
        in_specs += [pl.BlockSpec((CHUNK, M_QK), row),
                     pl.BlockSpec((CHUNK, M_QK), lambda s, pt: (s // per_chunk, 1)),
                     pl.BlockSpec((CHUNK, M_V), lambda s, pt: (s // per_chunk, 2 * M_QK // M_V)),
                     pl.BlockSpec((CHUNK, LANES), row),
                     pl.BlockSpec((SMALL_ROWS, CHUNK), lambda s, pt: (0, s // per_chunk)),
                     pl.BlockSpec((CHUNK, M_V), lambda s, pt: (s // per_chunk, og_block)),
                     pl.BlockSpec((1, M_V), lambda s, pt: (0, 0))]
        args += [qkv, qkv, qkv, sm, smt, sig, hw]
        seq_of = lambda s, pt: s // (per_chunk * nc)
        out_specs += [pl.BlockSpec((CHUNK, M_V), row),
                      pl.BlockSpec((None, M_HEADS, M_DV, M_DK), lambda s, pt: (seq_of(s, pt), 0, 0, 0)),
                      pl.BlockSpec((None, 2 * M_HEADS, M_DK), lambda s, pt: (seq_of(s, pt), 0, 0))]
        out_shape += [jax.ShapeDtypeStruct((m, M_V), BF16),
                      jax.ShapeDtypeStruct((batch, M_HEADS, M_DV, M_DK), F32),
                      jax.ShapeDtypeStruct((batch, 2 * M_HEADS, M_DK), F32)]
        scratch += [pltpu.VMEM((M_HEADS, M_DV, M_DK), F32),
                    pltpu.VMEM((SUBLANES, M_DK), F32),
                    pltpu.VMEM((SUBLANES, LANES), F32)]
    grid_spec = pltpu.PrefetchScalarGridSpec(
        num_scalar_prefetch=1,
        grid=(db,),
        in_specs=in_specs,
        out_specs=out_specs,
        scratch_shapes=scratch,
    )
    outs = pl.pallas_call(
        functools.partial(_decode_kernel, n_pages=n_pages, mlstm_steps_per_chunk=per_chunk, mlstm_chunks=nc),
        grid_spec=grid_spec,
        out_shape=out_shape,
        compiler_params=_params("arbitrary"),
        name="decode",
    )(*args)
    return outs if mlstm is not None else outs[0]


def _merge_kernel(ym_ref, yf_ref, sga_ref, sgb_ref, x_ref, g1_ref, sc_ref, sh_ref, n2w_ref,
                  wa_ref, wb_ref, wo_ref, x1_ref, h2_ref):
    merged = sga_ref[...] * _dot(ym_ref[...], wa_ref[...]) + sgb_ref[...] * _dot(yf_ref[...], wb_ref[...])
    x1 = x_ref[...] + g1_ref[...] * _dot(merged.astype(BF16), wo_ref[...])
    x1_ref[...] = x1
    rstd = lax.rsqrt(jnp.mean(x1 * x1, axis=-1, keepdims=True) + NORM_EPS)
    h2_ref[...] = ((x1 * rstd * n2w_ref[...]) * (1.0 + sc_ref[...]) + sh_ref[...]).astype(BF16)


def _merge(ym, yf, sig, x2, mod, n2w, wts, tm):
    m, d = x2.shape
    tiles_per_group = (m // tm) // mod.shape[0]
    group_of = lambda i: i // tiles_per_group
    row = lambda i: (i, 0)
    const = lambda i: (0, 0)
    resident = dict(pipeline_mode=pl.Buffered(1))
    return pl.pallas_call(
        _merge_kernel,
        grid=(m // tm,),
        in_specs=[pl.BlockSpec((tm, M_V), row),
                  pl.BlockSpec((tm, F_W), row),
                  pl.BlockSpec((tm, d), row),
                  pl.BlockSpec((tm, d), lambda i: (i, 1)),
                  pl.BlockSpec((tm, d), row),
                  _mod_spec(mod, d, MOD_G1, group_of),
                  _mod_spec(mod, d, MOD_SC2, group_of),
                  _mod_spec(mod, d, MOD_SH2, group_of),
                  pl.BlockSpec((1, d), const),
                  pl.BlockSpec((M_V, d), const, **resident),
                  pl.BlockSpec((F_W, d), const, **resident),
                  pl.BlockSpec((d, d), const, **resident)],
        out_specs=[pl.BlockSpec((tm, d), row), pl.BlockSpec((tm, d), row)],
        out_shape=[jax.ShapeDtypeStruct((m, d), F32), jax.ShapeDtypeStruct((m, d), BF16)],
        compiler_params=_params("arbitrary"),
        name="merge",
    )(ym, yf, sig, sig, x2, mod, mod, mod, n2w, wts["wa"], wts["wb"], wts["wo"])


FFN_TF = 512


def _ffn_kernel(h2_ref, x1_ref, g2_ref, p_ref, wa_ref, wg_ref, cw_ref, cb_ref, w2_ref,
                y_ref, a_out_ref, *rest, seq):
    f = pl.program_id(2)
    tm = h2_ref.shape[0]
    carry_ref = rest[-1]

    @pl.when(f == 0)
    def _():
        y_ref[...] = jnp.zeros_like(y_ref)

    if seq:
        @pl.when(pl.program_id(1) == 0)
        def _():
            carry_ref[f] = p_ref[...]
        wa, wg, w2 = wa_ref[...], wg_ref[...], w2_ref[...]
    else:
        wa, wg, w2 = wa_ref[...].astype(BF16), wg_ref[...].astype(BF16), w2_ref[...].astype(BF16)
        wa_out_ref, wg_out_ref, w2_out_ref = rest[:3]
        wa_out_ref[...] = wa
        wg_out_ref[...] = wg
        w2_out_ref[...] = w2

    h2 = h2_ref[...]
    a = _dot(h2, wa)
    g = _dot(h2, wg)
    if seq:
        prev = carry_ref[f]
        p2, p1 = prev[SUBLANES - 2:SUBLANES - 1, :], prev[SUBLANES - 1:SUBLANES, :]
        r = lax.broadcasted_iota(jnp.int32, a.shape, 0)
        a1 = jnp.where(r == 0, p1, pltpu.roll(a, 1, axis=0))
        a2 = jnp.where(r == 0, p2, jnp.where(r == 1, p1, pltpu.roll(a, 2, axis=0)))
        carry_ref[f] = a[tm - SUBLANES:tm, :]
        a_out_ref[...] = a[tm - SUBLANES:tm, :]
    else:
        a2, a1 = p_ref[0], p_ref[1]
        a_out_ref[...] = a
    ac = cw_ref[0:1, :] * a2 + cw_ref[1:2, :] * a1 + cw_ref[2:3, :] * a + cb_ref[...]
    u = (jax.nn.gelu(ac, approximate=True) * g).astype(BF16)
    y_ref[...] += _dot(u, w2)

    @pl.when(f == pl.num_programs(2) - 1)
    def _():
        y_ref[...] = x1_ref[...] + g2_ref[...] * y_ref[...]


def _ffn(h2, x1, mod, prev, wts, tm, seq):
    m, d = x1.shape
    groups = mod.shape[0]
    tiles = (m // tm) // groups
    dff = wts["w2"].shape[0]
    tf = FFN_TF
    nf = dff // tf
    row = lambda b, i, f: (b * tiles + i, 0)
    up_tile = pl.BlockSpec((None, d, tf), lambda b, i, f: (f, 0, 0))
    down_tile = pl.BlockSpec((tf, d), lambda b, i, f: (f, 0))
    out_specs = [pl.BlockSpec((tm, d), row)]
    out_shape = [jax.ShapeDtypeStruct((m, d), F32)]
    if seq:
        prev_spec = pl.BlockSpec((None, SUBLANES, tf), lambda b, i, f: (b, 0, f))
        out_specs.append(pl.BlockSpec((None, None, SUBLANES, tf), lambda b, i, f: (b, i, 0, f)))
        out_shape.append(jax.ShapeDtypeStruct((groups, tiles, SUBLANES, dff), F32))
        w_specs = [up_tile, up_tile]
        w_args = [wts["wa_t"], wts["wg_t"]]
    else:
        assert groups == 1 and tiles == 1, "the bf16 weight tiles are written once"
        prev_spec = pl.BlockSpec((CONV_W - 1, tm, tf), lambda b, i, f: (0, i, f))
        out_specs += [pl.BlockSpec((tm, tf), lambda b, i, f: (i, f)), up_tile, up_tile, down_tile]
        out_shape += [jax.ShapeDtypeStruct((m, dff), F32),
                      jax.ShapeDtypeStruct((nf, d, tf), BF16),
                      jax.ShapeDtypeStruct((nf, d, tf), BF16),
                      jax.ShapeDtypeStruct((dff, d), BF16)]
        w_specs = [pl.BlockSpec((d, tf), lambda b, i, f: (0, f)),
                   pl.BlockSpec((d, tf), lambda b, i, f: (0, f + nf))]
        w_args = [wts["w1"], wts["w1"]]
    return pl.pallas_call(
        functools.partial(_ffn_kernel, seq=seq),
        grid=(groups, tiles, nf),
        in_specs=[pl.BlockSpec((tm, d), row),
                  pl.BlockSpec((tm, d), row, pipeline_mode=pl.Buffered(1)),
                  _mod_spec(mod, d, MOD_G2, lambda b, i, f: b),
                  prev_spec,
                  *w_specs,
                  pl.BlockSpec((CONV_W, tf), lambda b, i, f: (0, f)),
                  pl.BlockSpec((1, tf), lambda b, i, f: (0, f)),
                  down_tile],
        out_specs=out_specs,
        out_shape=out_shape,
        scratch_shapes=[pltpu.VMEM((nf, SUBLANES, tf), F32)],
        compiler_params=_params("arbitrary", "arbitrary", "arbitrary"),
        name="ffn_seq" if seq else "ffn_row",
    )(h2, x1, mod, prev, *w_args, wts["cw"], wts["cb"], wts["w2"])


def _prep_weights(w_in, b_in, f_qnorm_w, f_knorm_w, w_proj_a, w_proj_b, w_out, w_ffn_in, conv_w, conv_b, w_ffn_out):
    o = _SPLIT_OFFS
    bbig = jnp.concatenate([b_in[:o[3]], b_in[o[10]:], b_in[o[3]:o[4]], b_in[o[6]:o[9]]])
    bsm = jnp.concatenate([b_in[o[4]:o[6]], b_in[o[9]:o[10]]])
    return {
        "w_in_t": w_in.T,
        "bbig": bbig.reshape(1, -1),
        "bsm": bsm.reshape(SMALL_ROWS, 1),
        "qnw": f_qnorm_w.reshape(1, F_DH),
        "knw": f_knorm_w.reshape(1, F_DH),
        "wa": w_proj_a.astype(BF16),
        "wb": w_proj_b.astype(BF16),
        "wo": w_out.astype(BF16),
        "w1": w_ffn_in,
        "cw": conv_w,
        "cb": conv_b.reshape(1, -1),
        "w2": w_ffn_out,
    }


def kernel(x_prompt, x_sample, c_prompt, c_sample, cache_k, cache_v, cache_logf, page_table, state_C, state_n, state_m, state_conv, w_ada, b_ada, norm1_w, w_in, b_in, m_hnorm_w, f_qnorm_w, f_knorm_w, w_proj_a, w_proj_b, w_out, norm2_w, w_ffn_in, conv_w, conv_b, w_ffn_out):
    depth = w_ada.shape[0]
    assert depth == 1, "single-layer step"
    batch, seq_len, d = x_prompt.shape
    db = x_sample.shape[0]
    assert x_sample.shape[1] == 1
    dff = w_ffn_out.shape[1]
    layer = 0

    wts = _prep_weights(w_in[layer], b_in[layer], f_qnorm_w[layer], f_knorm_w[layer], w_proj_a[layer],
                        w_proj_b[layer], w_out[layer], w_ffn_in[layer], conv_w[layer], conv_b[layer],
                        w_ffn_out[layer])
    n1w = norm1_w[layer].reshape(1, d)
    n2w = norm2_w[layer].reshape(1, d)
    hw = m_hnorm_w[layer].reshape(1, M_V)

    mod_p, mod_s = _ada(c_prompt, c_sample, w_ada[layer], b_ada[layer])
    mod_p = mod_p.reshape(batch, 1, 6 * d)
    mod_s = mod_s.reshape(1, db, 6 * d)

    def in_projection(x2, mod, tm):
        h, sm, smt = _norm(x2, mod, n1w, wts, tm)
        (qkv,) = _proj(h, wts, _J_MIX, _J_GATE, "mix", tm)
        (sig,) = _proj(h, wts, _J_GATE, _J_FQ, "sigmoid", tm)
        (qf,) = _proj(h, wts, _J_FQ, _J_FK, "headnorm", tm, nw=wts["qnw"], scale=FOX_Q_SCALE)
        kf, kfb = _proj(h, wts, _J_FK, _J_FV, "headnorm", tm, nw=wts["knw"], emit_f32=True)
        vf, vfb = _proj(h, wts, _J_FV, _J_END, "copy", tm)
        return qkv, sig, qf, kf, kfb, vf, vfb, sm, smt

    tm_p = min(1024, seq_len)
    xp = x_prompt.reshape(batch * seq_len, d)
    qkv, sig, qf, kf, kfb, vf, vfb, sm, smt = in_projection(xp, mod_p, tm_p)
    fcol = _cumsum(sm, batch)

    xs = x_sample.reshape(db, d)
    qkv_s, sig_s, qf_s, kf_s, _, vf_s, _, sm_s, _ = in_projection(xs, mod_s, db)
    n_pool, page = cache_k.shape[1], cache_k.shape[2]
    decode_args = (page_table,
                   cache_k.reshape(depth, n_pool, page * F_HEADS, F_DH),
                   cache_v.reshape(depth, n_pool, page * F_HEADS, F_DH),
                   cache_logf.transpose(0, 1, 3, 2),
                   layer,
                   qf_s.reshape(db, F_HEADS, F_DH), kf_s.reshape(db, F_HEADS, F_DH),
                   vf_s.reshape(db, F_HEADS, F_DH), sm_s.reshape(db, 1, LANES))
    if _mlstm_fits_decode(db, batch, seq_len):
        yf_s, ym, c_p, nm_p = _fox_decode(*decode_args, mlstm=(qkv, sm, smt, sig, hw, batch))
    else:
        yf_s = _fox_decode(*decode_args)
        ym, c_p, nm_p = _mlstm_prompt(qkv, sm, smt, sig, hw, batch)
    yf_s = yf_s.reshape(db, F_W)

    ym_s, c_s, n_s, m_s = _mlstm_step(qkv_s, sm_s, sig_s, hw, state_C[layer], state_n[layer], state_m[layer])
    x1_s, h2_s = _merge(ym_s, yf_s, sig_s, xs, mod_s, n2w, wts, db)
    prev_s = state_conv[layer].transpose(1, 0, 2)
    ys, a_s, wa_t, wg_t, w2_b = _ffn(h2_s, x1_s, mod_s, prev_s, wts, db, seq=False)
    wts_ffn = {"wa_t": wa_t, "wg_t": wg_t, "w2": w2_b, "cw": wts["cw"], "cb": wts["cb"]}

    yf = _fox_prompt(qf, kfb, vfb, fcol, batch)
    x1, h2 = _merge(ym, yf, sig, xp, mod_p, n2w, wts, min(512, seq_len))
    conv0 = jnp.zeros((batch, SUBLANES, dff), F32)
    yp, a_last = _ffn(h2, x1, mod_p, conv0, wts_ffn, tm_p, seq=True)

    y_prompt = yp.reshape(batch, seq_len, d)
    k_prompt = kf.reshape(depth, batch, seq_len, F_HEADS, F_DH)
    v_prompt = vf.reshape(depth, batch, seq_len, F_HEADS, F_DH)
    logf_prompt = sm[:, LFF0:LFF0 + F_HEADS].reshape(depth, batch, seq_len, F_HEADS)
    c_prompt_out = c_p[None]
    n_prompt = nm_p[:, :M_HEADS, :][None]
    m_prompt = nm_p[:, M_HEADS:, 0][None]
    conv_prompt = a_last[:, -1, SUBLANES - (CONV_W - 1):, :][None]

    y_sample = ys.reshape(db, 1, d)
    k_sample = kf_s.reshape(depth, db, 1, F_HEADS, F_DH)
    v_sample = vf_s.reshape(depth, db, 1, F_HEADS, F_DH)
    logf_sample = sm_s[:, LFF0:LFF0 + F_HEADS].reshape(depth, db, 1, F_HEADS)
    conv_sample = jnp.stack([state_conv[layer][:, CONV_W - 2, :], a_s], axis=1)[None]

    return (y_prompt, y_sample, k_prompt, v_prompt, logf_prompt, c_prompt_out, n_prompt, m_prompt, conv_prompt,
            k_sample, v_sample, logf_sample, c_s[None], n_s[None], m_s[None], conv_sample)
```

```python
import functools

import jax
import jax.numpy as jnp
from jax import lax
from jax.experimental import pallas as pl
from jax.experimental.pallas import tpu as pltpu

F32 = jnp.float32
BF16 = jnp.bfloat16

NORM_EPS = 1e-6
M_HEADS, M_DK, M_DV = 4, 128, 256
F_HEADS, F_DH = 8, 128
M_QK = M_HEADS * M_DK
M_V = M_HEADS * M_DV
F_W = F_HEADS * F_DH
CONV_W = 3
CHUNK = 128
LANES = 128
SUBLANES = 8
SMALL_ROWS = 16
VMEM_LIMIT = 56 * 1024 * 1024

IG0, LFM0, LFF0 = 0, M_HEADS, 2 * M_HEADS

LOG2E = 1.4426950408889634
FOX_Q_SCALE = F_DH ** -0.5 * LOG2E


def _dot(a, b):
    return jnp.dot(a, b, preferred_element_type=F32)


def _dot_nt(a, b):
    return lax.dot_general(a, b, (((1,), (1,)), ((), ())), preferred_element_type=F32)


def _dot_tn(a, b):
    return lax.dot_general(a, b, (((0,), (0,)), ((), ())), preferred_element_type=F32)


def _split3(x):
    hi = x.astype(BF16)
    r1 = x - hi.astype(F32)
    mid = r1.astype(BF16)
    lo = (r1 - mid.astype(F32)).astype(BF16)
    return hi, mid, lo


def _dot_f32_lhs(x, mat01):
    hi, mid, lo = _split3(x)
    return _dot(hi, mat01) + _dot(mid, mat01) + _dot(lo, mat01)


def _dot_f32_rhs(mat01, x):
    hi, mid, lo = _split3(x)
    return _dot(mat01, hi) + _dot(mat01, mid) + _dot(mat01, lo)


def _log_sigmoid(x):
    return jnp.minimum(x, 0.0) - jnp.log1p(jnp.exp(-jnp.abs(x)))


def _tri01(n, kind):
    r = lax.broadcasted_iota(jnp.int32, (n, n), 0)
    c = lax.broadcasted_iota(jnp.int32, (n, n), 1)
    if kind == "lower":
        m = c <= r
    elif kind == "upper":
        m = r <= c
    else:
        m = r > c
    return jnp.where(m, 1.0, 0.0).astype(BF16)


def _params(*sem):
    return pltpu.CompilerParams(dimension_semantics=sem, vmem_limit_bytes=VMEM_LIMIT)


def _ada_kernel(cp_ref, cs_ref, w_ref, b_ref, op_ref, os_ref):
    w = w_ref[...].astype(BF16)
    for c_ref, o_ref in ((cp_ref, op_ref), (cs_ref, os_ref)):
        c = c_ref[...]
        s = (c * jax.nn.sigmoid(c)).astype(BF16)
        o_ref[...] = _dot(s, w) + b_ref[...]


def _ada(c_p, c_s, w_ada, b_ada, tn=1024):
    d = c_p.shape[1]
    n = w_ada.shape[1]
    return pl.pallas_call(
        _ada_kernel,
        grid=(n // tn,),
        in_specs=[pl.BlockSpec(c_p.shape, lambda j: (0, 0)),
                  pl.BlockSpec(c_s.shape, lambda j: (0, 0)),
                  pl.BlockSpec((d, tn), lambda j: (0, j)),
                  pl.BlockSpec((1, tn), lambda j: (0, j))],
        out_specs=[pl.BlockSpec((c_p.shape[0], tn), lambda j: (0, j)),
                   pl.BlockSpec((c_s.shape[0], tn), lambda j: (0, j))],
        out_shape=[jax.ShapeDtypeStruct((c_p.shape[0], n), F32), jax.ShapeDtypeStruct((c_s.shape[0], n), F32)],
        compiler_params=_params("arbitrary"),
        name="ada",
    )(c_p, c_s, w_ada, b_ada.reshape(1, n))


MOD_SH1, MOD_SC1, MOD_G1, MOD_SH2, MOD_SC2, MOD_G2 = range(6)


PROJ_TN = 1024
_J_MIX, _J_GATE, _J_FQ, _J_FK, _J_FV, _J_END = 0, 2, 7, 8, 9, 10
_J_MO = _J_GATE + 4
_SPLIT_SIZES = (M_QK, M_QK, M_V, M_V, M_HEADS, M_HEADS, F_W, F_W, F_W, F_HEADS)
_SPLIT_OFFS = tuple(sum(_SPLIT_SIZES[:i]) for i in range(len(_SPLIT_SIZES) + 1))
_ROW_MO, _ROW_SMALL_M, _ROW_FQ, _ROW_SMALL_F, _ROW_GATES = (_SPLIT_OFFS[3], _SPLIT_OFFS[4], _SPLIT_OFFS[6],
                                                            _SPLIT_OFFS[9], _SPLIT_OFFS[10])
_TILE_RUNS = ((_J_MIX, 0), (_J_GATE, _ROW_GATES), (_J_MO, _ROW_MO), (_J_FQ, _ROW_FQ))


def _tile_row(jj):
    row = _TILE_RUNS[0][1] + PROJ_TN * (jj - _TILE_RUNS[0][0])
    for t0, r0 in _TILE_RUNS[1:]:
        row = jnp.where(jj >= t0, r0 + PROJ_TN * (jj - t0), row)
    return row


def _norm_kernel(x_ref, sc_ref, sh_ref, n1w_ref, wsm_m_ref, wsm_f_ref, bsm_ref, h_ref, sm_ref, smt_ref):
    x = x_ref[...]
    rstd = lax.rsqrt(jnp.mean(x * x, axis=-1, keepdims=True) + NORM_EPS)
    h = (x * rstd * n1w_ref[...]) * (1.0 + sc_ref[...]) + sh_ref[...]
    hb = h.astype(BF16)
    h_ref[...] = hb
    wsm = jnp.concatenate([wsm_m_ref[...], wsm_f_ref[...]], axis=0).astype(BF16)
    smt = _dot_nt(wsm, hb) + bsm_ref[...]
    row = lax.broadcasted_iota(jnp.int32, smt.shape, 0)
    smt = jnp.where(row < LFM0, smt, _log_sigmoid(smt))
    smt_ref[...] = smt
    pad = jnp.zeros((LANES - SMALL_ROWS, smt.shape[1]), F32)
    sm_ref[...] = jnp.concatenate([smt, pad], axis=0).T


def _proj_kernel(h_ref, hs_ref, w_ref, b_ref, *rest, kind, scale, emit_f32):
    *rest, wb_ref = rest
    nw_ref = None
    if kind == "headnorm":
        nw_ref, *rest = rest
    n_out = len(rest) // 2
    i = pl.program_id(1)
    last = pl.num_programs(1) - 1

    @pl.when(i == 0)
    def _():
        wb_ref[...] = w_ref[...].astype(BF16)

    def project(rows_ref, outs):
        acc = _dot_nt(rows_ref[...], wb_ref[...]) + b_ref[...]
        if kind == "mix":
            lane = lax.broadcasted_iota(jnp.int32, (1, acc.shape[1]), 1)
            is_mk = jnp.logical_and(pl.program_id(0) == 0, lane >= M_QK)
            outs[0][...] = (acc * jnp.where(is_mk, M_DK ** -0.5, 1.0)).astype(BF16)
        elif kind == "sigmoid":
            outs[0][...] = jax.nn.sigmoid(acc).astype(BF16)
        elif kind == "headnorm":
            for h in range(acc.shape[1] // F_DH):
                sl = acc[:, h * F_DH:(h + 1) * F_DH]
                ms = jnp.mean(sl * sl, axis=-1, keepdims=True)
                y = sl * lax.rsqrt(ms + NORM_EPS) * nw_ref[...]
                if emit_f32:
                    outs[0][:, h * F_DH:(h + 1) * F_DH] = y
                outs[-1][:, h * F_DH:(h + 1) * F_DH] = (y * scale).astype(BF16)
        else:
            outs[0][...] = acc
            outs[1][...] = acc.astype(BF16)

    @pl.when(i < last)
    def _():
        project(h_ref, rest[:n_out])

    @pl.when(i == last)
    def _():
        project(hs_ref, rest[n_out:])


def _feature_rows_spec(rows, d, first_row):
    def index_map(*g):
        return pl.multiple_of(jnp.asarray(first_row(*g), jnp.int32), SUBLANES), 0

    return pl.BlockSpec((pl.Element(rows), pl.Element(d)), index_map)


def _mod_spec(mod, d, chunk, group_of):
    return pl.BlockSpec((None, mod.shape[1], d), lambda *g: (group_of(*g), 0, chunk))


def _norm(x2, mod, n1w, wts, tm):
    m, d = x2.shape
    tiles_per_group = (m // tm) // mod.shape[0]
    group_of = lambda i: i // tiles_per_group
    return pl.pallas_call(
        _norm_kernel,
        grid=(m // tm,),
        in_specs=[pl.BlockSpec((tm, d), lambda i: (i, 0)),
                  _mod_spec(mod, d, MOD_SC1, group_of),
                  _mod_spec(mod, d, MOD_SH1, group_of),
                  pl.BlockSpec((1, d), lambda i: (0, 0)),
                  _feature_rows_spec(SUBLANES, d, lambda i: _ROW_SMALL_M),
                  _feature_rows_spec(SUBLANES, d, lambda i: _ROW_SMALL_F),
                  pl.BlockSpec((SMALL_ROWS, 1), lambda i: (0, 0))],
        out_specs=[pl.BlockSpec((tm, d), lambda i: (i, 0)),
                   pl.BlockSpec((tm, LANES), lambda i: (i, 0)),
                   pl.BlockSpec((SMALL_ROWS, tm), lambda i: (0, i))],
        out_shape=[jax.ShapeDtypeStruct((m, d), BF16),
                   jax.ShapeDtypeStruct((m, LANES), F32),
                   jax.ShapeDtypeStruct((SMALL_ROWS, m), F32)],
        compiler_params=_params("arbitrary"),
        name="norm",
    )(x2, mod, mod, n1w, wts["w_in_t"], wts["w_in_t"], wts["bsm"])


def _proj(h, hs, wts, j0, j1, kind, tm, nw=None, scale=1.0, emit_f32=False):
    m, d = h.shape
    ms = hs.shape[0]
    n_row_tiles = m // tm
    tn = PROJ_TN
    width = (j1 - j0) * tn
    prompt_row = lambda i: jnp.minimum(i, n_row_tiles - 1)
    in_specs = [pl.BlockSpec((tm, d), lambda j, i: (prompt_row(i), 0)),
                pl.BlockSpec((ms, d), lambda j, i: (0, 0)),
                _feature_rows_spec(tn, d, lambda j, i: _tile_row(j0 + j)),
                pl.BlockSpec((1, tn), lambda j, i: (0, j0 + j))]
    args = [h, hs, wts["w_in_t"], wts["bbig"]]
    if kind == "headnorm":
        in_specs.append(pl.BlockSpec((1, F_DH), lambda j, i: (0, 0)))
        args.append(nw)
    dtypes = ([F32] if emit_f32 or kind == "copy" else []) + [BF16]
    n_out = len(dtypes)
    out_shape = ([jax.ShapeDtypeStruct((m, width), t) for t in dtypes]
                 + [jax.ShapeDtypeStruct((ms, width), t) for t in dtypes])
    out_specs = ([pl.BlockSpec((tm, tn), lambda j, i: (prompt_row(i), j))] * n_out
                 + [pl.BlockSpec((ms, tn), lambda j, i: (0, j))] * n_out)
    outs = pl.pallas_call(
        functools.partial(_proj_kernel, kind=kind, scale=scale, emit_f32=emit_f32),
        grid=(j1 - j0, n_row_tiles + 1),
        in_specs=in_specs,
        out_specs=out_specs,
        out_shape=out_shape,
        scratch_shapes=[pltpu.VMEM((tn, d), BF16)],
        compiler_params=_params("arbitrary", "arbitrary"),
        name="proj_" + kind,
    )(*args)
    return outs[:n_out], outs[n_out:]


def _cumsum_kernel(sm_ref, fcol_ref, carry_ref):
    c = pl.program_id(1)
    tc = sm_ref.shape[0]

    @pl.when(c == 0)
    def _():
        carry_ref[...] = jnp.zeros_like(carry_ref)

    fc = _dot_f32_rhs(_tri01(tc, "lower"), sm_ref[...]) + carry_ref[0:1, :]
    fcol_ref[...] = fc
    carry_ref[...] = jnp.broadcast_to(fc[tc - 1:tc, :], carry_ref.shape)


def _cumsum(sm, batch, tc=512):
    m = sm.shape[0]
    nt = (m // batch) // tc
    return pl.pallas_call(
        _cumsum_kernel,
        grid=(batch, nt),
        in_specs=[pl.BlockSpec((tc, LANES), lambda b, c: (b * nt + c, 0))],
        out_specs=pl.BlockSpec((tc, LANES), lambda b, c: (b * nt + c, 0)),
        out_shape=jax.ShapeDtypeStruct((m, LANES), F32),
        scratch_shapes=[pltpu.VMEM((SUBLANES, LANES), F32)],
        compiler_params=_params("arbitrary", "arbitrary"),
        name="cumsum",
    )(sm)


def _mlstm_kernel(*refs):
    _mlstm_chunk(pl.program_id(1), pl.num_programs(1) - 1, range(M_HEADS), *refs)


def _mlstm_chunk(c, last_c, heads, q_ref, k_ref, v_ref, sm_ref, smt_ref, og_ref, hw_ref,
                 ym_ref, cout_ref, nm_ref, c_ref, n_ref, m_ref):
    L = q_ref.shape[0]

    @pl.when(c == 0)
    def _():
        for h in heads:
            c_ref[h] = jnp.zeros((M_DV, M_DK), F32)
            n_ref[h:h + 1, :] = jnp.zeros((1, M_DK), F32)
            m_ref[h:h + 1, :] = jnp.zeros((1, LANES), F32)

    r = lax.broadcasted_iota(jnp.int32, (L, L), 0)
    s_idx = lax.broadcasted_iota(jnp.int32, (L, L), 1)
    causal = s_idx <= r
    sm = sm_ref[...]
    smt = smt_ref[...]
    bcol_all = _dot_f32_rhs(_tri01(L, "lower"), sm)
    brow_all = _dot_f32_lhs(smt, _tri01(L, "upper"))

    for h in heads:
        ig_col = sm[:, IG0 + h:IG0 + h + 1]
        b_col = bcol_all[:, LFM0 + h:LFM0 + h + 1]
        ig_row = smt[IG0 + h:IG0 + h + 1, :]
        b_row = brow_all[LFM0 + h:LFM0 + h + 1, :]
        m0 = m_ref[h:h + 1, 0:1]
        a_row = ig_row - b_row
        cm_col = jnp.max(jnp.where(causal, a_row, -jnp.inf), axis=1, keepdims=True)
        mx_col = jnp.maximum(m0, cm_col)
        m_col = b_col + mx_col
        dmat = jnp.exp(jnp.where(causal, a_row - mx_col, -jnp.inf))
        inter = jnp.exp(m0 - mx_col)
        q = q_ref[:, h * M_DK:(h + 1) * M_DK]
        k = k_ref[:, h * M_DK:(h + 1) * M_DK]
        v = v_ref[:, h * M_DV:(h + 1) * M_DV]
        c0 = c_ref[h]
        n0 = n_ref[h:h + 1, :]
        s = _dot_nt(q, k) * dmat
        num = _dot(s.astype(BF16), v) + inter * _dot_nt(q, c0.astype(BF16))
        den = jnp.sum(s, axis=1, keepdims=True) + inter * jnp.sum(q.astype(F32) * n0, axis=1, keepdims=True)
        hh = num / jnp.maximum(jnp.abs(den), jnp.exp(-m_col))
        hn = hh * lax.rsqrt(jnp.mean(hh * hh, axis=1, keepdims=True) + NORM_EPS) * hw_ref[:, h * M_DV:(h + 1) * M_DV]
        ym_ref[:, h * M_DV:(h + 1) * M_DV] = (og_ref[:, h * M_DV:(h + 1) * M_DV] * hn).astype(BF16)
        m_new = m_col[L - 1:L, :]
        b_last = b_col[L - 1:L, :]
        w_end = jnp.exp(ig_col + b_last - b_col - m_new)
        decay = jnp.exp(b_last + m0 - m_new)
        vw = (w_end * v.astype(F32)).astype(BF16)
        c_ref[h] = decay * c0 + _dot_tn(vw, k)
        n_ref[h:h + 1, :] = decay * n0 + jnp.sum(w_end * k.astype(F32), axis=0, keepdims=True)
        m_ref[h:h + 1, :] = jnp.broadcast_to(m_new, (1, LANES))

    @pl.when(c == last_c)
    def _():
        for h in heads:
            cout_ref[h] = c_ref[h]
            nm_ref[h:h + 1, :] = n_ref[h:h + 1, :]
            nm_ref[M_HEADS + h:M_HEADS + h + 1, :] = m_ref[h:h + 1, :]


def _mlstm_prompt(qkv, sm, smt, sig, hw, batch):
    m = qkv.shape[0]
    og_block = sig.shape[1] // M_V - 1
    nc = (m // batch) // CHUNK
    row = lambda b, c: (b * nc + c, 0)
    return pl.pallas_call(
        _mlstm_kernel,
        grid=(batch, nc),
        in_specs=[pl.BlockSpec((CHUNK, M_QK), row),
                  pl.BlockSpec((CHUNK, M_QK), lambda b, c: (b * nc + c, 1)),
                  pl.BlockSpec((CHUNK, M_V), lambda b, c: (b * nc + c, 2 * M_QK // M_V)),
                  pl.BlockSpec((CHUNK, LANES), row),
                  pl.BlockSpec((SMALL_ROWS, CHUNK), lambda b, c: (0, b * nc + c)),
                  pl.BlockSpec((CHUNK, M_V), lambda b, c: (b * nc + c, og_block)),
                  pl.BlockSpec((1, M_V), lambda b, c: (0, 0))],
        out_specs=[pl.BlockSpec((CHUNK, M_V), row),
                   pl.BlockSpec((None, M_HEADS, M_DV, M_DK), lambda b, c: (b, 0, 0, 0)),
                   pl.BlockSpec((None, 2 * M_HEADS, M_DK), lambda b, c: (b, 0, 0))],
        out_shape=[jax.ShapeDtypeStruct((m, M_V), BF16),
                   jax.ShapeDtypeStruct((batch, M_HEADS, M_DV, M_DK), F32),
                   jax.ShapeDtypeStruct((batch, 2 * M_HEADS, M_DK), F32)],
        scratch_shapes=[pltpu.VMEM((M_HEADS, M_DV, M_DK), F32),
                        pltpu.VMEM((SUBLANES, M_DK), F32),
                        pltpu.VMEM((SUBLANES, LANES), F32)],
        compiler_params=_params("arbitrary", "arbitrary"),
        name="mlstm",
    )(qkv, qkv, qkv, sm, smt, sig, hw)


FOX_HEADS_PER_STEP = 4
FOX_ROW_CHUNK = 32


def _forget_columns(f, first):
    hi, mid, lo = _split3(f)
    lane = lax.broadcasted_iota(jnp.int32, (f.shape[0], LANES), 1)
    ones = jnp.where(jnp.abs(lane - (4 - first)) <= 1, 1.0, 0.0)
    ext = jnp.where(lane == first, hi.astype(F32),
                    jnp.where(lane == first + 1, mid.astype(F32),
                              jnp.where(lane == first + 2, lo.astype(F32), ones)))
    return ext.astype(BF16)


def _fox_kernel(q_ref, k_ref, v_ref, fcol_ref, o_ref, qaug_ref, kaug_ref, s_ref, p_ref, m_ref, l_ref, acc_ref):
    hp = pl.program_id(1)
    qi = pl.program_id(2)
    tq = q_ref.shape[0]
    t = k_ref.shape[0]
    nh = FOX_HEADS_PER_STEP

    def forget_sum(fc, h2):
        lane = lax.broadcasted_iota(jnp.int32, fc.shape, 1)
        sel = jnp.where(lane == LFF0 + nh * hp + h2, fc, 0.0)
        return jnp.sum(sel, axis=1, keepdims=True) * LOG2E

    @pl.when(qi == 0)
    def _():
        fc = fcol_ref[...]
        for h2 in range(nh):
            kaug_ref[h2, :, 0:F_DH] = k_ref[:, h2 * F_DH:(h2 + 1) * F_DH]
            kaug_ref[h2, :, F_DH:2 * F_DH] = _forget_columns(-forget_sum(fc, h2), 3)

    fq = fcol_ref[pl.ds(pl.multiple_of(qi * tq, tq), tq), :]
    for h2 in range(nh):
        qaug_ref[h2, :, 0:F_DH] = q_ref[:, h2 * F_DH:(h2 + 1) * F_DH]
        qaug_ref[h2, :, F_DH:2 * F_DH] = _forget_columns(forget_sum(fq, h2), 0)
    m_ref[...] = jnp.full_like(m_ref, -jnp.inf)
    l_ref[...] = jnp.zeros_like(l_ref)
    acc_ref[...] = jnp.zeros_like(acc_ref)
    rc = FOX_ROW_CHUNK
    n_lane_tiles = tq // LANES

    def softmax_rows(h2, c, masked):
        rows = slice(c * rc, (c + 1) * rc)
        s = s_ref[h2, rows, :]
        if masked:
            r = lax.broadcasted_iota(jnp.int32, (rc, tq), 0) + c * rc
            col = lax.broadcasted_iota(jnp.int32, (rc, tq), 1)
            s = jnp.where(r >= col, s, -jnp.inf)
        tiles = [s[:, k * LANES:(k + 1) * LANES] for k in range(n_lane_tiles)]
        m_old = m_ref[h2, rows, :]
        mx = functools.reduce(jnp.maximum, tiles)
        m_new = jnp.maximum(m_old, jnp.max(mx, axis=1, keepdims=True))
        alpha = jnp.exp2(m_old - m_new)
        ps = [jnp.exp2(tile - m_new) for tile in tiles]
        row_sum = jnp.sum(functools.reduce(jnp.add, ps), axis=1, keepdims=True)
        l_ref[h2, rows, :] = alpha * l_ref[h2, rows, :] + row_sum
        for k in range(n_lane_tiles):
            p_ref[h2, rows, k * LANES:(k + 1) * LANES] = ps[k].astype(BF16)
        acc_ref[h2, rows, :] = alpha * acc_ref[h2, rows, :]
        m_ref[h2, rows, :] = m_new

    def block(kj, masked):
        off = pl.multiple_of(kj * tq, tq)
        for h2 in range(nh):
            s_ref[h2] = _dot_nt(qaug_ref[h2], kaug_ref[h2, pl.ds(off, tq), :])
        for h2 in range(nh):
            for c in range(tq // rc):
                softmax_rows(h2, c, masked)
            acc_ref[h2] += _dot(p_ref[h2], v_ref[pl.ds(off, tq), h2 * F_DH:(h2 + 1) * F_DH])

    def body(kj, carry):
        block(kj, False)
        return carry

    lax.fori_loop(0, qi, body, 0)
    block(qi, True)
    for h2 in range(nh):
        o_ref[:, h2 * F_DH:(h2 + 1) * F_DH] = (acc_ref[h2] / l_ref[h2]).astype(BF16)


def _fox_prompt(qf, kfb, vfb, fcol, batch, tq=512):
    m = qf.shape[0]
    t = m // batch
    nq = t // tq
    nh = FOX_HEADS_PER_STEP
    wide = nh * F_DH
    return pl.pallas_call(
        _fox_kernel,
        grid=(batch, F_HEADS // nh, nq),
        in_specs=[pl.BlockSpec((tq, wide), lambda b, h, i: (b * nq + i, h)),
                  pl.BlockSpec((t, wide), lambda b, h, i: (b, h)),
                  pl.BlockSpec((t, wide), lambda b, h, i: (b, h)),
                  pl.BlockSpec((t, LANES), lambda b, h, i: (b, 0))],
        out_specs=pl.BlockSpec((tq, wide), lambda b, h, i: (b * nq + i, h)),
        out_shape=jax.ShapeDtypeStruct((m, F_W), BF16),
        scratch_shapes=[pltpu.VMEM((nh, tq, 2 * F_DH), BF16),
                        pltpu.VMEM((nh, t, 2 * F_DH), BF16),
                        pltpu.VMEM((nh, tq, tq), F32),
                        pltpu.VMEM((nh, tq, tq), BF16),
                        pltpu.VMEM((nh, tq, LANES), F32),
                        pltpu.VMEM((nh, tq, LANES), F32),
                        pltpu.VMEM((nh, tq, F_DH), F32)],
        compiler_params=_params("arbitrary", "arbitrary", "arbitrary"),
        name="fox",
    )(qf, kfb, vfb, fcol)


MSTEP_ROWS = 8


def _mstep_kernel(q_ref, k_ref, v_ref, sm_ref, og_ref, hw_ref, c0_ref, n0_ref, m0_ref,
                  ym_ref, cout_ref, nout_ref, mout_ref, cq_ref):
    rows = q_ref.shape[0]
    sm = sm_ref[...]
    ig = sm[:, IG0:IG0 + M_HEADS]
    lf = sm[:, LFM0:LFM0 + M_HEADS]
    m0 = m0_ref[...]
    m_new = lf + jnp.maximum(m0, ig - lf)
    w_end = jnp.exp(ig - m_new)
    decay = jnp.exp(lf + m0 - m_new)
    mout_ref[...] = m_new
    row_id = lax.broadcasted_iota(jnp.int32, (SUBLANES, M_DV), 0)

    for h in range(M_HEADS):
        qh = q_ref[:, h * M_DK:(h + 1) * M_DK]
        kh = k_ref[:, h * M_DK:(h + 1) * M_DK]
        vh = v_ref[:, h * M_DV:(h + 1) * M_DV]
        qf = qh.astype(F32)
        kf = kh.astype(F32)
        vf = vh.astype(F32)
        wh = w_end[:, h:h + 1]
        dh = decay[:, h:h + 1]
        for r in range(rows):
            c0 = c0_ref[r, h]
            q8 = jnp.broadcast_to(qh[r:r + 1, :], (SUBLANES, M_DK))
            cq_ref[r:r + 1, :] = _dot_nt(q8, c0.astype(BF16))[0:1, :]
            vw8 = jnp.where(row_id == 0, jnp.broadcast_to(wh[r:r + 1, :] * vf[r:r + 1, :], (SUBLANES, M_DV)), 0.0)
            k8 = jnp.broadcast_to(kh[r:r + 1, :], (SUBLANES, M_DK))
            cout_ref[r, h] = dh[r:r + 1, :] * c0 + _dot_tn(vw8.astype(BF16), k8)
        n0 = n0_ref[:, h, :]
        s = jnp.sum(qf * kf, axis=1, keepdims=True) * wh
        num = s * vf + dh * cq_ref[...]
        den = s + dh * jnp.sum(n0 * qf, axis=1, keepdims=True)
        hh = num / jnp.maximum(jnp.abs(den), jnp.exp(-m_new[:, h:h + 1]))
        hn = hh * lax.rsqrt(jnp.mean(hh * hh, axis=1, keepdims=True) + NORM_EPS) * hw_ref[:, h * M_DV:(h + 1) * M_DV]
        ym_ref[:, h * M_DV:(h + 1) * M_DV] = (og_ref[:, h * M_DV:(h + 1) * M_DV] * hn).astype(BF16)
        nout_ref[:, h, :] = dh * n0 + wh * kf


def _mlstm_step(qkv, sm, sig, hw, c0, n0, m0):
    db = qkv.shape[0]
    og_block = sig.shape[1] // M_V - 1
    rows = MSTEP_ROWS
    row = lambda i: (i, 0)
    return pl.pallas_call(
        _mstep_kernel,
        grid=(db // rows,),
        in_specs=[pl.BlockSpec((rows, M_QK), row),
                  pl.BlockSpec((rows, M_QK), lambda i: (i, 1)),
                  pl.BlockSpec((rows, M_V), lambda i: (i, 2 * M_QK // M_V)),
                  pl.BlockSpec((rows, LANES), row),
                  pl.BlockSpec((rows, M_V), lambda i: (i, og_block)),
                  pl.BlockSpec((1, M_V), lambda i: (0, 0)),
                  pl.BlockSpec((rows, M_HEADS, M_DV, M_DK), lambda i: (i, 0, 0, 0)),
                  pl.BlockSpec((rows, M_HEADS, M_DK), lambda i: (i, 0, 0)),
                  pl.BlockSpec((rows, M_HEADS), row)],
        out_specs=[pl.BlockSpec((rows, M_V), row),
                   pl.BlockSpec((rows, M_HEADS, M_DV, M_DK), lambda i: (i, 0, 0, 0)),
                   pl.BlockSpec((rows, M_HEADS, M_DK), lambda i: (i, 0, 0)),
                   pl.BlockSpec((rows, M_HEADS), row)],
        out_shape=[jax.ShapeDtypeStruct((db, M_V), BF16),
                   jax.ShapeDtypeStruct(c0.shape, F32),
                   jax.ShapeDtypeStruct(n0.shape, F32),
                   jax.ShapeDtypeStruct(m0.shape, F32)],
        scratch_shapes=[pltpu.VMEM((rows, M_DV), F32)],
        compiler_params=_params("arbitrary"),
        name="mstep",
    )(qkv, qkv, qkv, sm, sig, hw, c0, n0, m0)


N_MLSTM_IN = 7
N_MLSTM_OUT = 3
N_MLSTM_SCRATCH = 3
N_DECODE_SCRATCH = 4


def _decode_kernel(*refs, n_pages, mlstm_steps_per_chunk, mlstm_chunks):
    n_dec_in = 3 * n_pages + 4
    n_ml = N_MLSTM_IN if mlstm_steps_per_chunk else 0
    dec_in = refs[1:1 + n_dec_in]
    ml_in = refs[1 + n_dec_in:1 + n_dec_in + n_ml]
    outs = refs[1 + n_dec_in + n_ml:]
    o_ref, outs = outs[0], outs[1:]
    if mlstm_steps_per_chunk:
        ml_out, outs = outs[:N_MLSTM_OUT], outs[N_MLSTM_OUT:]
        ml_scratch = outs[N_DECODE_SCRATCH:]
        step = pl.program_id(0)
        chunk = (step // mlstm_steps_per_chunk) % mlstm_chunks
        heads_per_step = M_HEADS // mlstm_steps_per_chunk
        for g in range(mlstm_steps_per_chunk):
            @pl.when(step % mlstm_steps_per_chunk == g)
            def _(g=g):
                _mlstm_chunk(chunk, mlstm_chunks - 1, range(g * heads_per_step, (g + 1) * heads_per_step),
                             *ml_in, *ml_out, *ml_scratch)
    _decode_sequence(dec_in[:n_pages], dec_in[n_pages:2 * n_pages], dec_in[2 * n_pages:3 * n_pages],
                     *dec_in[3 * n_pages:], o_ref, *outs[:N_DECODE_SCRATCH])


def _decode_sequence(k_refs, v_refs, lf_refs, q_ref, kn_ref, vn_ref, sm_ref, o_ref,
                     s_ref, xn_ref, widen_ref, spread_ref):
    n_pages = len(k_refs)
    page = xn_ref.shape[1]
    pw = page * F_HEADS

    for p in range(n_pages):
        xn_ref[p * F_HEADS:(p + 1) * F_HEADS, :] = lf_refs[p][...]
    xn = xn_ref[...] * LOG2E

    @pl.when(pl.program_id(0) == 0)
    def _():
        t_id = lax.broadcasted_iota(jnp.int32, (page, pw), 0)
        l_id = lax.broadcasted_iota(jnp.int32, (page, pw), 1)
        widen_ref[...] = jnp.where((l_id // F_HEADS) == t_id, 1.0, 0.0).astype(BF16)
        r = lax.broadcasted_iota(jnp.int32, (LANES, pw), 0)
        c = lax.broadcasted_iota(jnp.int32, (LANES, pw), 1)
        spread_ref[...] = jnp.where(r == LFF0 + (c & (F_HEADS - 1)), 1.0, 0.0).astype(BF16)

    wide = _dot_f32_lhs(xn, widen_ref[...])
    r_id = lax.broadcasted_iota(jnp.int32, wide.shape, 0)
    c_id = lax.broadcasted_iota(jnp.int32, wide.shape, 1)
    own = jnp.where((r_id & (F_HEADS - 1)) == (c_id & (F_HEADS - 1)), wide, 0.0)
    g_p = lax.broadcasted_iota(jnp.int32, (n_pages, n_pages * F_HEADS), 0)
    g_r = lax.broadcasted_iota(jnp.int32, (n_pages, n_pages * F_HEADS), 1)
    gather = jnp.where((g_r // F_HEADS) == g_p, 1.0, 0.0).astype(BF16)
    x = _dot_f32_rhs(gather, own)
    lane = lax.broadcasted_iota(jnp.int32, x.shape, 1)
    y, z = x, x
    step = F_HEADS
    while step < pw:
        y = y + jnp.where(lane < pw - step, pltpu.roll(y, pw - step, axis=1), 0.0)
        z = z + pltpu.roll(z, step, axis=1)
        step *= 2
    after = y - x

    run = _dot_f32_lhs(jnp.broadcast_to(sm_ref[...] * LOG2E, (SUBLANES, LANES)), spread_ref[...])[0:1, :]

    q = q_ref[...]
    sub = lax.broadcasted_iota(jnp.int32, (F_HEADS, pw), 0)
    lane_h = lax.broadcasted_iota(jnp.int32, (F_HEADS, pw), 1) & (F_HEADS - 1)
    valid = lane_h == sub
    s_self = jnp.sum(q.astype(F32) * kn_ref[...], axis=1, keepdims=True)
    m_run = s_self
    for p in reversed(range(n_pages)):
        kb = k_refs[p][...].astype(BF16)
        st = _dot_nt(q, kb) + (run + after[p:p + 1, :])
        st = jnp.where(valid, st, -jnp.inf)
        s_ref[p] = st
        m_run = jnp.maximum(m_run, jnp.max(st, axis=1, keepdims=True))
        run = run + z[p:p + 1, :]
    w_self = jnp.exp2(s_self - m_run)
    l = w_self
    acc = w_self * vn_ref[...]
    for p in range(n_pages):
        pe = jnp.exp2(s_ref[p] - m_run)
        l = l + jnp.sum(pe, axis=1, keepdims=True)
        acc = acc + _dot(pe.astype(BF16), v_refs[p][...].astype(BF16))
    o_ref[...] = (acc / l).astype(BF16)


def _mlstm_fits_decode(db, batch, seq_len):
    chunk_steps = batch * (seq_len // CHUNK)
    return db % chunk_steps == 0 and db // chunk_steps in (1, 2, 4)


def _fox_decode(page_table, ck, cv, clf, layer, q3, k3, v3, sm3, mlstm=None):
    db, n_pages = page_table.shape
    pw = ck.shape[2]
    page = clf.shape[3]
    seq = lambda b, pt: (b, 0, 0)

    def pg(p):
        return lambda b, pt: (layer, pt[b, p], 0, 0)

    in_specs = ([pl.BlockSpec((None, None, pw, F_DH), pg(p)) for p in range(n_pages)]
                + [pl.BlockSpec((None, None, pw, F_DH), pg(p)) for p in range(n_pages)]
                + [pl.BlockSpec((None, None, F_HEADS, page), pg(p)) for p in range(n_pages)]
                + [pl.BlockSpec((None, F_HEADS, F_DH), seq),
                   pl.BlockSpec((None, F_HEADS, F_DH), seq),
                   pl.BlockSpec((None, F_HEADS, F_DH), seq),
                   pl.BlockSpec((None, 1, LANES), seq)])
    args = [page_table, *([ck] * n_pages), *([cv] * n_pages), *([clf] * n_pages), q3, k3, v3, sm3]
    out_specs = [pl.BlockSpec((None, F_HEADS, F_DH), seq)]
    out_shape = [jax.ShapeDtypeStruct((db, F_HEADS, F_DH), BF16)]
    scratch = [pltpu.VMEM((n_pages, F_HEADS, pw), F32), pltpu.VMEM((n_pages * F_HEADS, page), F32),
               pltpu.VMEM((page, pw), BF16), pltpu.VMEM((LANES, pw), BF16)]
    per_chunk, nc = 0, 0
    if mlstm is not None:
        qkv, sm, smt, sig, hw, batch = mlstm
        m = qkv.shape[0]
        nc = (m // batch) // CHUNK
        per_chunk = db // (batch * nc)
        og_block = sig.shape[1] // M_V - 1
        row = lambda s, pt: (s // per_chunk, 0)
        in_specs += [pl.BlockSpec((CHUNK, M_QK), row),
                     pl.BlockSpec((CHUNK, M_QK), lambda s, pt: (s // per_chunk, 1)),
                     pl.BlockSpec((CHUNK, M_V), lambda s, pt: (s // per_chunk, 2 * M_QK // M_V)),
                     pl.BlockSpec((CHUNK, LANES), row),
                     pl.BlockSpec((SMALL_ROWS, CHUNK), lambda s, pt: (0, s // per_chunk)),
                     pl.BlockSpec((CHUNK, M_V), lambda s, pt: (s // per_chunk, og_block)),
                     pl.BlockSpec((1, M_V), lambda s, pt: (0, 0))]
        args += [qkv, qkv, qkv, sm, smt, sig, hw]
        seq_of = lambda s, pt: s // (per_chunk * nc)
        out_specs += [pl.BlockSpec((CHUNK, M_V), row),
                      pl.BlockSpec((None, M_HEADS, M_DV, M_DK), lambda s, pt: (seq_of(s, pt), 0, 0, 0)),
                      pl.BlockSpec((None, 2 * M_HEADS, M_DK), lambda s, pt: (seq_of(s, pt), 0, 0))]
        out_shape += [jax.ShapeDtypeStruct((m, M_V), BF16),
                      jax.ShapeDtypeStruct((batch, M_HEADS, M_DV, M_DK), F32),
                      jax.ShapeDtypeStruct((batch, 2 * M_HEADS, M_DK), F32)]
        scratch += [pltpu.VMEM((M_HEADS, M_DV, M_DK), F32),
                    pltpu.VMEM((SUBLANES, M_DK), F32),
                    pltpu.VMEM((SUBLANES, LANES), F32)]
    grid_spec = pltpu.PrefetchScalarGridSpec(
        num_scalar_prefetch=1,
        grid=(db,),
        in_specs=in_specs,
        out_specs=out_specs,
        scratch_shapes=scratch,
    )
    outs = pl.pallas_call(
        functools.partial(_decode_kernel, n_pages=n_pages, mlstm_steps_per_chunk=per_chunk, mlstm_chunks=nc),
        grid_spec=grid_spec,
        out_shape=out_shape,
        compiler_params=_params("arbitrary"),
        name="decode",
    )(*args)
    return outs if mlstm is not None else outs[0]


def _merge_kernel(ym_ref, yf_ref, sga_ref, sgb_ref, x_ref, g1_ref, sc_ref, sh_ref, n2w_ref,
                  wa_ref, wb_ref, wo_ref, x1_ref, h2_ref):
    merged = sga_ref[...] * _dot(ym_ref[...], wa_ref[...]) + sgb_ref[...] * _dot(yf_ref[...], wb_ref[...])
    x1 = x_ref[...] + g1_ref[...] * _dot(merged.astype(BF16), wo_ref[...])
    x1_ref[...] = x1
    rstd = lax.rsqrt(jnp.mean(x1 * x1, axis=-1, keepdims=True) + NORM_EPS)
    h2_ref[...] = ((x1 * rstd * n2w_ref[...]) * (1.0 + sc_ref[...]) + sh_ref[...]).astype(BF16)


def _merge(ym, yf, sig, x2, mod, n2w, wts, tm):
    m, d = x2.shape
    tiles_per_group = (m // tm) // mod.shape[0]
    group_of = lambda i: i // tiles_per_group
    row = lambda i: (i, 0)
    const = lambda i: (0, 0)
    resident = dict(pipeline_mode=pl.Buffered(1))
    return pl.pallas_call(
        _merge_kernel,
        grid=(m // tm,),
        in_specs=[pl.BlockSpec((tm, M_V), row),
                  pl.BlockSpec((tm, F_W), row),
                  pl.BlockSpec((tm, d), row),
                  pl.BlockSpec((tm, d), lambda i: (i, 1)),
                  pl.BlockSpec((tm, d), row),
                  _mod_spec(mod, d, MOD_G1, group_of),
                  _mod_spec(mod, d, MOD_SC2, group_of),
                  _mod_spec(mod, d, MOD_SH2, group_of),
                  pl.BlockSpec((1, d), const),
                  pl.BlockSpec((M_V, d), const, **resident),
                  pl.BlockSpec((F_W, d), const, **resident),
                  pl.BlockSpec((d, d), const, **resident)],
        out_specs=[pl.BlockSpec((tm, d), row), pl.BlockSpec((tm, d), row)],
        out_shape=[jax.ShapeDtypeStruct((m, d), F32), jax.ShapeDtypeStruct((m, d), BF16)],
        compiler_params=_params("arbitrary"),
        name="merge",
    )(ym, yf, sig, sig, x2, mod, mod, mod, n2w, wts["wa"], wts["wb"], wts["wo"])


FFN_TF = 512


def _ffn_kernel(h2_ref, x1_ref, g2_ref, p_ref, wa_ref, wg_ref, cw_ref, cb_ref, w2_ref,
                y_ref, a_out_ref, *rest, seq):
    f = pl.program_id(2)
    tm = h2_ref.shape[0]
    carry_ref = rest[-1]

    @pl.when(f == 0)
    def _():
        y_ref[...] = jnp.zeros_like(y_ref)

    if seq:
        @pl.when(pl.program_id(1) == 0)
        def _():
            carry_ref[f] = p_ref[...]
        wa, wg, w2 = wa_ref[...], wg_ref[...], w2_ref[...]
    else:
        wa, wg, w2 = wa_ref[...].astype(BF16), wg_ref[...].astype(BF16), w2_ref[...].astype(BF16)
        wa_out_ref, wg_out_ref, w2_out_ref = rest[:3]
        wa_out_ref[...] = wa
        wg_out_ref[...] = wg
        w2_out_ref[...] = w2

    h2 = h2_ref[...]
    a = _dot(h2, wa)
    g = _dot(h2, wg)
    if seq:
        prev = carry_ref[f]
        p2, p1 = prev[SUBLANES - 2:SUBLANES - 1, :], prev[SUBLANES - 1:SUBLANES, :]
        r = lax.broadcasted_iota(jnp.int32, a.shape, 0)
        a1 = jnp.where(r == 0, p1, pltpu.roll(a, 1, axis=0))
        a2 = jnp.where(r == 0, p2, jnp.where(r == 1, p1, pltpu.roll(a, 2, axis=0)))
        carry_ref[f] = a[tm - SUBLANES:tm, :]
        a_out_ref[...] = a[tm - SUBLANES:tm, :]
    else:
        a2, a1 = p_ref[0], p_ref[1]
        a_out_ref[...] = a
    ac = cw_ref[0:1, :] * a2 + cw_ref[1:2, :] * a1 + cw_ref[2:3, :] * a + cb_ref[...]
    u = (jax.nn.gelu(ac, approximate=True) * g).astype(BF16)
    y_ref[...] += _dot(u, w2)

    @pl.when(f == pl.num_programs(2) - 1)
    def _():
        y_ref[...] = x1_ref[...] + g2_ref[...] * y_ref[...]


def _ffn(h2, x1, mod, prev, wts, tm, seq):
    m, d = x1.shape
    groups = mod.shape[0]
    tiles = (m // tm) // groups
    dff = wts["w2"].shape[0]
    tf = FFN_TF
    nf = dff // tf
    row = lambda b, i, f: (b * tiles + i, 0)
    up_tile = pl.BlockSpec((None, d, tf), lambda b, i, f: (f, 0, 0))
    down_tile = pl.BlockSpec((tf, d), lambda b, i, f: (f, 0))
    out_specs = [pl.BlockSpec((tm, d), row)]
    out_shape = [jax.ShapeDtypeStruct((m, d), F32)]
    if seq:
        prev_spec = pl.BlockSpec((None, SUBLANES, tf), lambda b, i, f: (b, 0, f))
        out_specs.append(pl.BlockSpec((None, None, SUBLANES, tf), lambda b, i, f: (b, i, 0, f)))
        out_shape.append(jax.ShapeDtypeStruct((groups, tiles, SUBLANES, dff), F32))
        w_specs = [up_tile, up_tile]
        w_args = [wts["wa_t"], wts["wg_t"]]
    else:
        assert groups == 1 and tiles == 1, "the bf16 weight tiles are written once"
        prev_spec = pl.BlockSpec((CONV_W - 1, tm, tf), lambda b, i, f: (0, i, f))
        out_specs += [pl.BlockSpec((tm, tf), lambda b, i, f: (i, f)), up_tile, up_tile, down_tile]
        out_shape += [jax.ShapeDtypeStruct((m, dff), F32),
                      jax.ShapeDtypeStruct((nf, d, tf), BF16),
                      jax.ShapeDtypeStruct((nf, d, tf), BF16),
                      jax.ShapeDtypeStruct((dff, d), BF16)]
        w_specs = [pl.BlockSpec((d, tf), lambda b, i, f: (0, f)),
                   pl.BlockSpec((d, tf), lambda b, i, f: (0, f + nf))]
        w_args = [wts["w1"], wts["w1"]]
    return pl.pallas_call(
        functools.partial(_ffn_kernel, seq=seq),
        grid=(groups, tiles, nf),
        in_specs=[pl.BlockSpec((tm, d), row),
                  pl.BlockSpec((tm, d), row, pipeline_mode=pl.Buffered(1)),
                  _mod_spec(mod, d, MOD_G2, lambda b, i, f: b),
                  prev_spec,
                  *w_specs,
                  pl.BlockSpec((CONV_W, tf), lambda b, i, f: (0, f)),
                  pl.BlockSpec((1, tf), lambda b, i, f: (0, f)),
                  down_tile],
        out_specs=out_specs,
        out_shape=out_shape,
        scratch_shapes=[pltpu.VMEM((nf, SUBLANES, tf), F32)],
        compiler_params=_params("arbitrary", "arbitrary", "arbitrary"),
        name="ffn_seq" if seq else "ffn_row",
    )(h2, x1, mod, prev, *w_args, wts["cw"], wts["cb"], wts["w2"])


def _prep_weights(w_in, b_in, f_qnorm_w, f_knorm_w, w_proj_a, w_proj_b, w_out, w_ffn_in, conv_w, conv_b, w_ffn_out):
    o = _SPLIT_OFFS
    bbig = jnp.concatenate([b_in[:o[3]], b_in[o[10]:], b_in[o[3]:o[4]], b_in[o[6]:o[9]]])
    bsm = jnp.concatenate([b_in[o[4]:o[6]], b_in[o[9]:o[10]]])
    return {
        "w_in_t": w_in.T,
        "bbig": bbig.reshape(1, -1),
        "bsm": bsm.reshape(SMALL_ROWS, 1),
        "qnw": f_qnorm_w.reshape(1, F_DH),
        "knw": f_knorm_w.reshape(1, F_DH),
        "wa": w_proj_a.astype(BF16),
        "wb": w_proj_b.astype(BF16),
        "wo": w_out.astype(BF16),
        "w1": w_ffn_in,
        "cw": conv_w,
        "cb": conv_b.reshape(1, -1),
        "w2": w_ffn_out,
    }


def kernel(x_prompt, x_sample, c_prompt, c_sample, cache_k, cache_v, cache_logf, page_table, state_C, state_n, state_m, state_conv, w_ada, b_ada, norm1_w, w_in, b_in, m_hnorm_w, f_qnorm_w, f_knorm_w, w_proj_a, w_proj_b, w_out, norm2_w, w_ffn_in, conv_w, conv_b, w_ffn_out):
    depth = w_ada.shape[0]
    assert depth == 1, "single-layer step"
    batch, seq_len, d = x_prompt.shape
    db = x_sample.shape[0]
    assert x_sample.shape[1] == 1
    dff = w_ffn_out.shape[1]
    layer = 0

    wts = _prep_weights(w_in[layer], b_in[layer], f_qnorm_w[layer], f_knorm_w[layer], w_proj_a[layer],
                        w_proj_b[layer], w_out[layer], w_ffn_in[layer], conv_w[layer], conv_b[layer],
                        w_ffn_out[layer])
    n1w = norm1_w[layer].reshape(1, d)
    n2w = norm2_w[layer].reshape(1, d)
    hw = m_hnorm_w[layer].reshape(1, M_V)

    mod_p, mod_s = _ada(c_prompt, c_sample, w_ada[layer], b_ada[layer])
    mod_p = mod_p.reshape(batch, 1, 6 * d)
    mod_s = mod_s.reshape(1, db, 6 * d)

    tm_p = min(1024, seq_len)
    xp = x_prompt.reshape(batch * seq_len, d)
    xs = x_sample.reshape(db, d)
    h, sm, smt = _norm(xp, mod_p, n1w, wts, tm_p)
    h_s, sm_s, _ = _norm(xs, mod_s, n1w, wts, db)
    (qkv,), (qkv_s,) = _proj(h, h_s, wts, _J_MIX, _J_GATE, "mix", tm_p)
    (sig,), (sig_s,) = _proj(h, h_s, wts, _J_GATE, _J_FQ, "sigmoid", tm_p)
    (qf,), (qf_s,) = _proj(h, h_s, wts, _J_FQ, _J_FK, "headnorm", tm_p, nw=wts["qnw"], scale=FOX_Q_SCALE)
    (kf, kfb), (kf_s, _) = _proj(h, h_s, wts, _J_FK, _J_FV, "headnorm", tm_p, nw=wts["knw"], emit_f32=True)
    (vf, vfb), (vf_s, _) = _proj(h, h_s, wts, _J_FV, _J_END, "copy", tm_p)
    fcol = _cumsum(sm, batch)

    n_pool, page = cache_k.shape[1], cache_k.shape[2]
    decode_args = (page_table,
                   cache_k.reshape(depth, n_pool, page * F_HEADS, F_DH),
                   cache_v.reshape(depth, n_pool, page * F_HEADS, F_DH),
                   cache_logf.transpose(0, 1, 3, 2),
                   layer,
                   qf_s.reshape(db, F_HEADS, F_DH), kf_s.reshape(db, F_HEADS, F_DH),
                   vf_s.reshape(db, F_HEADS, F_DH), sm_s.reshape(db, 1, LANES))
    if _mlstm_fits_decode(db, batch, seq_len):
        yf_s, ym, c_p, nm_p = _fox_decode(*decode_args, mlstm=(qkv, sm, smt, sig, hw, batch))
    else:
        yf_s = _fox_decode(*decode_args)
        ym, c_p, nm_p = _mlstm_prompt(qkv, sm, smt, sig, hw, batch)
    yf_s = yf_s.reshape(db, F_W)

    ym_s, c_s, n_s, m_s = _mlstm_step(qkv_s, sm_s, sig_s, hw, state_C[layer], state_n[layer], state_m[layer])
    x1_s, h2_s = _merge(ym_s, yf_s, sig_s, xs, mod_s, n2w, wts, db)
    prev_s = state_conv[layer].transpose(1, 0, 2)
    ys, a_s, wa_t, wg_t, w2_b = _ffn(h2_s, x1_s, mod_s, prev_s, wts, db, seq=False)
    wts_ffn = {"wa_t": wa_t, "wg_t": wg_t, "w2": w2_b, "cw": wts["cw"], "cb": wts["cb"]}

    yf = _fox_prompt(qf, kfb, vfb, fcol, batch)
    x1, h2 = _merge(ym, yf, sig, xp, mod_p, n2w, wts, min(512, seq_len))
    conv0 = jnp.zeros((batch, SUBLANES, dff), F32)
    yp, a_last = _ffn(h2, x1, mod_p, conv0, wts_ffn, tm_p, seq=True)

    y_prompt = yp.reshape(batch, seq_len, d)
    k_prompt = kf.reshape(depth, batch, seq_len, F_HEADS, F_DH)
    v_prompt = vf.reshape(depth, batch, seq_len, F_HEADS, F_DH)
    logf_prompt = sm[:, LFF0:LFF0 + F_HEADS].reshape(depth, batch, seq_len, F_HEADS)
    c_prompt_out = c_p[None]
    n_prompt = nm_p[:, :M_HEADS, :][None]
    m_prompt = nm_p[:, M_HEADS:, 0][None]
    conv_prompt = a_last[:, -1, SUBLANES - (CONV_W - 1):, :][None]

    y_sample = ys.reshape(db, 1, d)
    k_sample = kf_s.reshape(depth, db, 1, F_HEADS, F_DH)
    v_sample = vf_s.reshape(depth, db, 1, F_HEADS, F_DH)
    logf_sample = sm_s[:, LFF0:LFF0 + F_HEADS].reshape(depth, db, 1, F_HEADS)
    conv_sample = jnp.stack([state_conv[layer][:, CONV_W - 2, :], a_s], axis=1)[None]

    return (y_prompt, y_sample, k_prompt, v_prompt, logf_prompt, c_prompt_out, n_prompt, m_prompt, conv_prompt,
            k_sample, v_sample, logf_sample, c_s[None], n_s[None], m_s[None], conv_sample)
```

```python
import functools

import jax
import jax.numpy as jnp
from jax import lax
from jax.experimental import pallas as pl
from jax.experimental.pallas import tpu as pltpu

F32 = jnp.float32
BF16 = jnp.bfloat16

NORM_EPS = 1e-6
M_HEADS, M_DK, M_DV = 4, 128, 256
F_HEADS, F_DH = 8, 128
M_QK = M_HEADS * M_DK
M_V = M_HEADS * M_DV
F_W = F_HEADS * F_DH
CONV_W = 3
CHUNK = 128
LANES = 128
SUBLANES = 8
SMALL_ROWS = 16
VMEM_LIMIT = 56 * 1024 * 1024

IG0, LFM0, LFF0 = 0, M_HEADS, 2 * M_HEADS

LOG2E = 1.4426950408889634
FOX_Q_SCALE = F_DH ** -0.5 * LOG2E


def _dot(a, b):
    return jnp.dot(a, b, preferred_element_type=F32)


def _dot_nt(a, b):
    return lax.dot_general(a, b, (((1,), (1,)), ((), ())), preferred_element_type=F32)


def _dot_tn(a, b):
    return lax.dot_general(a, b, (((0,), (0,)), ((), ())), preferred_element_type=F32)


def _split3(x):
    hi = x.astype(BF16)
    r1 = x - hi.astype(F32)
    mid = r1.astype(BF16)
    lo = (r1 - mid.astype(F32)).astype(BF16)
    return hi, mid, lo


def _dot_f32_lhs(x, mat01):
    hi, mid, lo = _split3(x)
    return _dot(hi, mat01) + _dot(mid, mat01) + _dot(lo, mat01)


def _dot_f32_rhs(mat01, x):
    hi, mid, lo = _split3(x)
    return _dot(mat01, hi) + _dot(mat01, mid) + _dot(mat01, lo)


def _log_sigmoid(x):
    return jnp.minimum(x, 0.0) - jnp.log1p(jnp.exp(-jnp.abs(x)))


def _tri01(n, kind):
    r = lax.broadcasted_iota(jnp.int32, (n, n), 0)
    c = lax.broadcasted_iota(jnp.int32, (n, n), 1)
    if kind == "lower":
        m = c <= r
    elif kind == "upper":
        m = r <= c
    else:
        m = r > c
    return jnp.where(m, 1.0, 0.0).astype(BF16)


def _params(*sem):
    return pltpu.CompilerParams(dimension_semantics=sem, vmem_limit_bytes=VMEM_LIMIT)


def _ada_kernel(cp_ref, cs_ref, w_ref, b_ref, op_ref, os_ref):
    w = w_ref[...].astype(BF16)
    for c_ref, o_ref in ((cp_ref, op_ref), (cs_ref, os_ref)):
        c = c_ref[...]
        s = (c * jax.nn.sigmoid(c)).astype(BF16)
        o_ref[...] = _dot(s, w) + b_ref[...]


def _ada(c_p, c_s, w_ada, b_ada, tn=1024):
    d = c_p.shape[1]
    n = w_ada.shape[1]
    return pl.pallas_call(
        _ada_kernel,
        grid=(n // tn,),
        in_specs=[pl.BlockSpec(c_p.shape, lambda j: (0, 0)),
                  pl.BlockSpec(c_s.shape, lambda j: (0, 0)),
                  pl.BlockSpec((d, tn), lambda j: (0, j)),
                  pl.BlockSpec((1, tn), lambda j: (0, j))],
        out_specs=[pl.BlockSpec((c_p.shape[0], tn), lambda j: (0, j)),
                   pl.BlockSpec((c_s.shape[0], tn), lambda j: (0, j))],
        out_shape=[jax.ShapeDtypeStruct((c_p.shape[0], n), F32), jax.ShapeDtypeStruct((c_s.shape[0], n), F32)],
        compiler_params=_params("arbitrary"),
        name="ada",
    )(c_p, c_s, w_ada, b_ada.reshape(1, n))


MOD_SH1, MOD_SC1, MOD_G1, MOD_SH2, MOD_SC2, MOD_G2 = range(6)


PROJ_TN = 1024
_J_MIX, _J_GATE, _J_FQ, _J_FK, _J_FV, _J_END = 0, 2, 7, 8, 9, 10
_J_MO = _J_GATE + 4
_SPLIT_SIZES = (M_QK, M_QK, M_V, M_V, M_HEADS, M_HEADS, F_W, F_W, F_W, F_HEADS)
_SPLIT_OFFS = tuple(sum(_SPLIT_SIZES[:i]) for i in range(len(_SPLIT_SIZES) + 1))
_ROW_MO, _ROW_SMALL_M, _ROW_FQ, _ROW_SMALL_F, _ROW_GATES = (_SPLIT_OFFS[3], _SPLIT_OFFS[4], _SPLIT_OFFS[6],
                                                            _SPLIT_OFFS[9], _SPLIT_OFFS[10])
_TILE_RUNS = ((_J_MIX, 0), (_J_GATE, _ROW_GATES), (_J_MO, _ROW_MO), (_J_FQ, _ROW_FQ))


def _tile_row(jj):
    row = _TILE_RUNS[0][1] + PROJ_TN * (jj - _TILE_RUNS[0][0])
    for t0, r0 in _TILE_RUNS[1:]:
        row = jnp.where(jj >= t0, r0 + PROJ_TN * (jj - t0), row)
    return row


def _norm_kernel(x_ref, sc_ref, sh_ref, n1w_ref, wsm_m_ref, wsm_f_ref, bsm_ref, h_ref, sm_ref, smt_ref):
    x = x_ref[...]
    rstd = lax.rsqrt(jnp.mean(x * x, axis=-1, keepdims=True) + NORM_EPS)
    h = (x * rstd * n1w_ref[...]) * (1.0 + sc_ref[...]) + sh_ref[...]
    hb = h.astype(BF16)
    h_ref[...] = hb
    wsm = jnp.concatenate([wsm_m_ref[...], wsm_f_ref[...]], axis=0).astype(BF16)
    smt = _dot_nt(wsm, hb) + bsm_ref[...]
    row = lax.broadcasted_iota(jnp.int32, smt.shape, 0)
    smt = jnp.where(row < LFM0, smt, _log_sigmoid(smt))
    smt_ref[...] = smt
    pad = jnp.zeros((LANES - SMALL_ROWS, smt.shape[1]), F32)
    sm_ref[...] = jnp.concatenate([smt, pad], axis=0).T


def _proj_kernel(h_ref, hs_ref, w_ref, b_ref, *rest, kind, scale, emit_f32):
    *rest, wb_ref = rest
    nw_ref = None
    if kind == "headnorm":
        nw_ref, *rest = rest
    n_out = len(rest) // 2
    i = pl.program_id(1)

    @pl.when(i == 0)
    def _():
        wb_ref[...] = w_ref[...].astype(BF16)

    def project(rows_ref, outs):
        acc = _dot_nt(rows_ref[...], wb_ref[...]) + b_ref[...]
        if kind == "mix":
            lane = lax.broadcasted_iota(jnp.int32, (1, acc.shape[1]), 1)
            is_mk = jnp.logical_and(pl.program_id(0) == 0, lane >= M_QK)
            outs[0][...] = (acc * jnp.where(is_mk, M_DK ** -0.5, 1.0)).astype(BF16)
        elif kind == "sigmoid":
            outs[0][...] = jax.nn.sigmoid(acc).astype(BF16)
        elif kind == "headnorm":
            for h in range(acc.shape[1] // F_DH):
                sl = acc[:, h * F_DH:(h + 1) * F_DH]
                ms = jnp.mean(sl * sl, axis=-1, keepdims=True)
                y = sl * lax.rsqrt(ms + NORM_EPS) * nw_ref[...]
                if emit_f32:
                    outs[0][:, h * F_DH:(h + 1) * F_DH] = y
                outs[-1][:, h * F_DH:(h + 1) * F_DH] = (y * scale).astype(BF16)
        else:
            outs[0][...] = acc
            outs[1][...] = acc.astype(BF16)

    @pl.when(i == 0)
    def _():
        project(hs_ref, rest[n_out:])

    @pl.when(i > 0)
    def _():
        project(h_ref, rest[:n_out])


def _feature_rows_spec(rows, d, first_row):
    def index_map(*g):
        return pl.multiple_of(jnp.asarray(first_row(*g), jnp.int32), SUBLANES), 0

    return pl.BlockSpec((pl.Element(rows), pl.Element(d)), index_map)


def _mod_spec(mod, d, chunk, group_of):
    return pl.BlockSpec((None, mod.shape[1], d), lambda *g: (group_of(*g), 0, chunk))


def _norm(x2, mod, n1w, wts, tm):
    m, d = x2.shape
    tiles_per_group = (m // tm) // mod.shape[0]
    group_of = lambda i: i // tiles_per_group
    return pl.pallas_call(
        _norm_kernel,
        grid=(m // tm,),
        in_specs=[pl.BlockSpec((tm, d), lambda i: (i, 0)),
                  _mod_spec(mod, d, MOD_SC1, group_of),
                  _mod_spec(mod, d, MOD_SH1, group_of),
                  pl.BlockSpec((1, d), lambda i: (0, 0)),
                  _feature_rows_spec(SUBLANES, d, lambda i: _ROW_SMALL_M),
                  _feature_rows_spec(SUBLANES, d, lambda i: _ROW_SMALL_F),
                  pl.BlockSpec((SMALL_ROWS, 1), lambda i: (0, 0))],
        out_specs=[pl.BlockSpec((tm, d), lambda i: (i, 0)),
                   pl.BlockSpec((tm, LANES), lambda i: (i, 0)),
                   pl.BlockSpec((SMALL_ROWS, tm), lambda i: (0, i))],
        out_shape=[jax.ShapeDtypeStruct((m, d), BF16),
                   jax.ShapeDtypeStruct((m, LANES), F32),
                   jax.ShapeDtypeStruct((SMALL_ROWS, m), F32)],
        compiler_params=_params("arbitrary"),
        name="norm",
    )(x2, mod, mod, n1w, wts["w_in_t"], wts["w_in_t"], wts["bsm"])


def _proj(h, hs, wts, j0, j1, kind, tm, nw=None, scale=1.0, emit_f32=False):
    m, d = h.shape
    ms = hs.shape[0]
    n_row_tiles = m // tm
    tn = PROJ_TN
    width = (j1 - j0) * tn
    prompt_row = lambda i: jnp.maximum(i - 1, 0)
    in_specs = [pl.BlockSpec((tm, d), lambda j, i: (prompt_row(i), 0)),
                pl.BlockSpec((ms, d), lambda j, i: (0, 0)),
                _feature_rows_spec(tn, d, lambda j, i: _tile_row(j0 + j)),
                pl.BlockSpec((1, tn), lambda j, i: (0, j0 + j))]
    args = [h, hs, wts["w_in_t"], wts["bbig"]]
    if kind == "headnorm":
        in_specs.append(pl.BlockSpec((1, F_DH), lambda j, i: (0, 0)))
        args.append(nw)
    dtypes = ([F32] if emit_f32 or kind == "copy" else []) + [BF16]
    n_out = len(dtypes)
    out_shape = ([jax.ShapeDtypeStruct((m, width), t) for t in dtypes]
                 + [jax.ShapeDtypeStruct((ms, width), t) for t in dtypes])
    out_specs = ([pl.BlockSpec((tm, tn), lambda j, i: (prompt_row(i), j))] * n_out
                 + [pl.BlockSpec((ms, tn), lambda j, i: (0, j))] * n_out)
    outs = pl.pallas_call(
        functools.partial(_proj_kernel, kind=kind, scale=scale, emit_f32=emit_f32),
        grid=(j1 - j0, n_row_tiles + 1),
        in_specs=in_specs,
        out_specs=out_specs,
        out_shape=out_shape,
        scratch_shapes=[pltpu.VMEM((tn, d), BF16)],
        compiler_params=_params("arbitrary", "arbitrary"),
        name="proj_" + kind,
    )(*args)
    return outs[:n_out], outs[n_out:]


def _cumsum_kernel(sm_ref, fcol_ref, carry_ref):
    c = pl.program_id(1)
    tc = sm_ref.shape[0]

    @pl.when(c == 0)
    def _():
        carry_ref[...] = jnp.zeros_like(carry_ref)

    fc = _dot_f32_rhs(_tri01(tc, "lower"), sm_ref[...]) + carry_ref[0:1, :]
    fcol_ref[...] = fc
    carry_ref[...] = jnp.broadcast_to(fc[tc - 1:tc, :], carry_ref.shape)


def _cumsum(sm, batch, tc=512):
    m = sm.shape[0]
    nt = (m // batch) // tc
    return pl.pallas_call(
        _cumsum_kernel,
        grid=(batch, nt),
        in_specs=[pl.BlockSpec((tc, LANES), lambda b, c: (b * nt + c, 0))],
        out_specs=pl.BlockSpec((tc, LANES), lambda b, c: (b * nt + c, 0)),
        out_shape=jax.ShapeDtypeStruct((m, LANES), F32),
        scratch_shapes=[pltpu.VMEM((SUBLANES, LANES), F32)],
        compiler_params=_params("arbitrary", "arbitrary"),
        name="cumsum",
    )(sm)


def _mlstm_kernel(*refs):
    _mlstm_chunk(pl.program_id(1), pl.num_programs(1) - 1, range(M_HEADS), *refs)


def _mlstm_chunk(c, last_c, heads, q_ref, k_ref, v_ref, sm_ref, smt_ref, og_ref, hw_ref,
                 ym_ref, cout_ref, nm_ref, c_ref, n_ref, m_ref):
    L = q_ref.shape[0]

    @pl.when(c == 0)
    def _():
        for h in heads:
            c_ref[h] = jnp.zeros((M_DV, M_DK), F32)
            n_ref[h:h + 1, :] = jnp.zeros((1, M_DK), F32)
            m_ref[h:h + 1, :] = jnp.zeros((1, LANES), F32)

    r = lax.broadcasted_iota(jnp.int32, (L, L), 0)
    s_idx = lax.broadcasted_iota(jnp.int32, (L, L), 1)
    causal = s_idx <= r
    sm = sm_ref[...]
    smt = smt_ref[...]
    bcol_all = _dot_f32_rhs(_tri01(L, "lower"), sm)
    brow_all = _dot_f32_lhs(smt, _tri01(L, "upper"))

    for h in heads:
        ig_col = sm[:, IG0 + h:IG0 + h + 1]
        b_col = bcol_all[:, LFM0 + h:LFM0 + h + 1]
        ig_row = smt[IG0 + h:IG0 + h + 1, :]
        b_row = brow_all[LFM0 + h:LFM0 + h + 1, :]
        m0 = m_ref[h:h + 1, 0:1]
        a_row = ig_row - b_row
        cm_col = jnp.max(jnp.where(causal, a_row, -jnp.inf), axis=1, keepdims=True)
        mx_col = jnp.maximum(m0, cm_col)
        m_col = b_col + mx_col
        dmat = jnp.exp(jnp.where(causal, a_row - mx_col, -jnp.inf))
        inter = jnp.exp(m0 - mx_col)
        q = q_ref[:, h * M_DK:(h + 1) * M_DK]
        k = k_ref[:, h * M_DK:(h + 1) * M_DK]
        v = v_ref[:, h * M_DV:(h + 1) * M_DV]
        c0 = c_ref[h]
        n0 = n_ref[h:h + 1, :]
        s = _dot_nt(q, k) * dmat
        num = _dot(s.astype(BF16), v) + inter * _dot_nt(q, c0.astype(BF16))
        den = jnp.sum(s, axis=1, keepdims=True) + inter * jnp.sum(q.astype(F32) * n0, axis=1, keepdims=True)
        hh = num / jnp.maximum(jnp.abs(den), jnp.exp(-m_col))
        hn = hh * lax.rsqrt(jnp.mean(hh * hh, axis=1, keepdims=True) + NORM_EPS) * hw_ref[:, h * M_DV:(h + 1) * M_DV]
        ym_ref[:, h * M_DV:(h + 1) * M_DV] = (og_ref[:, h * M_DV:(h + 1) * M_DV] * hn).astype(BF16)
        m_new = m_col[L - 1:L, :]
        b_last = b_col[L - 1:L, :]
        w_end = jnp.exp(ig_col + b_last - b_col - m_new)
        decay = jnp.exp(b_last + m0 - m_new)
        vw = (w_end * v.astype(F32)).astype(BF16)
        c_ref[h] = decay * c0 + _dot_tn(vw, k)
        n_ref[h:h + 1, :] = decay * n0 + jnp.sum(w_end * k.astype(F32), axis=0, keepdims=True)
        m_ref[h:h + 1, :] = jnp.broadcast_to(m_new, (1, LANES))

    @pl.when(c == last_c)
    def _():
        for h in heads:
            cout_ref[h] = c_ref[h]
            nm_ref[h:h + 1, :] = n_ref[h:h + 1, :]
            nm_ref[M_HEADS + h:M_HEADS + h + 1, :] = m_ref[h:h + 1, :]


def _mlstm_prompt(qkv, sm, smt, sig, hw, batch):
    m = qkv.shape[0]
    og_block = sig.shape[1] // M_V - 1
    nc = (m // batch) // CHUNK
    row = lambda b, c: (b * nc + c, 0)
    return pl.pallas_call(
        _mlstm_kernel,
        grid=(batch, nc),
        in_specs=[pl.BlockSpec((CHUNK, M_QK), row),
                  pl.BlockSpec((CHUNK, M_QK), lambda b, c: (b * nc + c, 1)),
                  pl.BlockSpec((CHUNK, M_V), lambda b, c: (b * nc + c, 2 * M_QK // M_V)),
                  pl.BlockSpec((CHUNK, LANES), row),
                  pl.BlockSpec((SMALL_ROWS, CHUNK), lambda b, c: (0, b * nc + c)),
                  pl.BlockSpec((CHUNK, M_V), lambda b, c: (b * nc + c, og_block)),
                  pl.BlockSpec((1, M_V), lambda b, c: (0, 0))],
        out_specs=[pl.BlockSpec((CHUNK, M_V), row),
                   pl.BlockSpec((None, M_HEADS, M_DV, M_DK), lambda b, c: (b, 0, 0, 0)),
                   pl.BlockSpec((None, 2 * M_HEADS, M_DK), lambda b, c: (b, 0, 0))],
        out_shape=[jax.ShapeDtypeStruct((m, M_V), BF16),
                   jax.ShapeDtypeStruct((batch, M_HEADS, M_DV, M_DK), F32),
                   jax.ShapeDtypeStruct((batch, 2 * M_HEADS, M_DK), F32)],
        scratch_shapes=[pltpu.VMEM((M_HEADS, M_DV, M_DK), F32),
                        pltpu.VMEM((SUBLANES, M_DK), F32),
                        pltpu.VMEM((SUBLANES, LANES), F32)],
        compiler_params=_params("arbitrary", "arbitrary"),
        name="mlstm",
    )(qkv, qkv, qkv, sm, smt, sig, hw)


FOX_HEADS_PER_STEP = 4
FOX_ROW_CHUNK = 32


def _forget_columns(f, first):
    hi, mid, lo = _split3(f)
    lane = lax.broadcasted_iota(jnp.int32, (f.shape[0], LANES), 1)
    ones = jnp.where(jnp.abs(lane - (4 - first)) <= 1, 1.0, 0.0)
    ext = jnp.where(lane == first, hi.astype(F32),
                    jnp.where(lane == first + 1, mid.astype(F32),
                              jnp.where(lane == first + 2, lo.astype(F32), ones)))
    return ext.astype(BF16)


def _fox_kernel(q_ref, k_ref, v_ref, fcol_ref, o_ref, qaug_ref, kaug_ref, s_ref, p_ref, m_ref, l_ref, acc_ref):
    hp = pl.program_id(1)
    qi = pl.program_id(2)
    tq = q_ref.shape[0]
    t = k_ref.shape[0]
    nh = FOX_HEADS_PER_STEP

    def forget_sum(fc, h2):
        lane = lax.broadcasted_iota(jnp.int32, fc.shape, 1)
        sel = jnp.where(lane == LFF0 + nh * hp + h2, fc, 0.0)
        return jnp.sum(sel, axis=1, keepdims=True) * LOG2E

    @pl.when(qi == 0)
    def _():
        fc = fcol_ref[...]
        for h2 in range(nh):
            kaug_ref[h2, :, 0:F_DH] = k_ref[:, h2 * F_DH:(h2 + 1) * F_DH]
            kaug_ref[h2, :, F_DH:2 * F_DH] = _forget_columns(-forget_sum(fc, h2), 3)

    fq = fcol_ref[pl.ds(pl.multiple_of(qi * tq, tq), tq), :]
    for h2 in range(nh):
        qaug_ref[h2, :, 0:F_DH] = q_ref[:, h2 * F_DH:(h2 + 1) * F_DH]
        qaug_ref[h2, :, F_DH:2 * F_DH] = _forget_columns(forget_sum(fq, h2), 0)
    m_ref[...] = jnp.full_like(m_ref, -jnp.inf)
    l_ref[...] = jnp.zeros_like(l_ref)
    acc_ref[...] = jnp.zeros_like(acc_ref)
    rc = FOX_ROW_CHUNK
    n_lane_tiles = tq // LANES

    def softmax_rows(h2, c, masked):
        rows = slice(c * rc, (c + 1) * rc)
        s = s_ref[h2, rows, :]
        if masked:
            r = lax.broadcasted_iota(jnp.int32, (rc, tq), 0) + c * rc
            col = lax.broadcasted_iota(jnp.int32, (rc, tq), 1)
            s = jnp.where(r >= col, s, -jnp.inf)
        tiles = [s[:, k * LANES:(k + 1) * LANES] for k in range(n_lane_tiles)]
        m_old = m_ref[h2, rows, :]
        mx = functools.reduce(jnp.maximum, tiles)
        m_new = jnp.maximum(m_old, jnp.max(mx, axis=1, keepdims=True))
        alpha = jnp.exp2(m_old - m_new)
        ps = [jnp.exp2(tile - m_new) for tile in tiles]
        row_sum = jnp.sum(functools.reduce(jnp.add, ps), axis=1, keepdims=True)
        l_ref[h2, rows, :] = alpha * l_ref[h2, rows, :] + row_sum
        for k in range(n_lane_tiles):
            p_ref[h2, rows, k * LANES:(k + 1) * LANES] = ps[k].astype(BF16)
        acc_ref[h2, rows, :] = alpha * acc_ref[h2, rows, :]
        m_ref[h2, rows, :] = m_new

    def block(kj, masked):
        off = pl.multiple_of(kj * tq, tq)
        for h2 in range(nh):
            s_ref[h2] = _dot_nt(qaug_ref[h2], kaug_ref[h2, pl.ds(off, tq), :])
        for h2 in range(nh):
            for c in range(tq // rc):
                softmax_rows(h2, c, masked)
            acc_ref[h2] += _dot(p_ref[h2], v_ref[pl.ds(off, tq), h2 * F_DH:(h2 + 1) * F_DH])

    def body(kj, carry):
        block(kj, False)
        return carry

    lax.fori_loop(0, qi, body, 0)
    block(qi, True)
    for h2 in range(nh):
        o_ref[:, h2 * F_DH:(h2 + 1) * F_DH] = (acc_ref[h2] / l_ref[h2]).astype(BF16)


def _fox_prompt(qf, kfb, vfb, fcol, batch, tq=512):
    m = qf.shape[0]
    t = m // batch
    nq = t // tq
    nh = FOX_HEADS_PER_STEP
    wide = nh * F_DH
    return pl.pallas_call(
        _fox_kernel,
        grid=(batch, F_HEADS // nh, nq),
        in_specs=[pl.BlockSpec((tq, wide), lambda b, h, i: (b * nq + i, h)),
                  pl.BlockSpec((t, wide), lambda b, h, i: (b, h)),
                  pl.BlockSpec((t, wide), lambda b, h, i: (b, h)),
                  pl.BlockSpec((t, LANES), lambda b, h, i: (b, 0))],
        out_specs=pl.BlockSpec((tq, wide), lambda b, h, i: (b * nq + i, h)),
        out_shape=jax.ShapeDtypeStruct((m, F_W), BF16),
        scratch_shapes=[pltpu.VMEM((nh, tq, 2 * F_DH), BF16),
                        pltpu.VMEM((nh, t, 2 * F_DH), BF16),
                        pltpu.VMEM((nh, tq, tq), F32),
                        pltpu.VMEM((nh, tq, tq), BF16),
                        pltpu.VMEM((nh, tq, LANES), F32),
                        pltpu.VMEM((nh, tq, LANES), F32),
                        pltpu.VMEM((nh, tq, F_DH), F32)],
        compiler_params=_params("arbitrary", "arbitrary", "arbitrary"),
        name="fox",
    )(qf, kfb, vfb, fcol)


MSTEP_ROWS = 8


def _mstep_kernel(q_ref, k_ref, v_ref, sm_ref, og_ref, hw_ref, c0_ref, n0_ref, m0_ref,
                  ym_ref, cout_ref, nout_ref, mout_ref, cq_ref):
    rows = q_ref.shape[0]
    sm = sm_ref[...]
    ig = sm[:, IG0:IG0 + M_HEADS]
    lf = sm[:, LFM0:LFM0 + M_HEADS]
    m0 = m0_ref[...]
    m_new = lf + jnp.maximum(m0, ig - lf)
    w_end = jnp.exp(ig - m_new)
    decay = jnp.exp(lf + m0 - m_new)
    mout_ref[...] = m_new
    row_id = lax.broadcasted_iota(jnp.int32, (SUBLANES, M_DV), 0)

    for h in range(M_HEADS):
        qh = q_ref[:, h * M_DK:(h + 1) * M_DK]
        kh = k_ref[:, h * M_DK:(h + 1) * M_DK]
        vh = v_ref[:, h * M_DV:(h + 1) * M_DV]
        qf = qh.astype(F32)
        kf = kh.astype(F32)
        vf = vh.astype(F32)
        wh = w_end[:, h:h + 1]
        dh = decay[:, h:h + 1]
        for r in range(rows):
            c0 = c0_ref[r, h]
            q8 = jnp.broadcast_to(qh[r:r + 1, :], (SUBLANES, M_DK))
            cq_ref[r:r + 1, :] = _dot_nt(q8, c0.astype(BF16))[0:1, :]
            vw8 = jnp.where(row_id == 0, jnp.broadcast_to(wh[r:r + 1, :] * vf[r:r + 1, :], (SUBLANES, M_DV)), 0.0)
            k8 = jnp.broadcast_to(kh[r:r + 1, :], (SUBLANES, M_DK))
            cout_ref[r, h] = dh[r:r + 1, :] * c0 + _dot_tn(vw8.astype(BF16), k8)
        n0 = n0_ref[:, h, :]
        s = jnp.sum(qf * kf, axis=1, keepdims=True) * wh
        num = s * vf + dh * cq_ref[...]
        den = s + dh * jnp.sum(n0 * qf, axis=1, keepdims=True)
        hh = num / jnp.maximum(jnp.abs(den), jnp.exp(-m_new[:, h:h + 1]))
        hn = hh * lax.rsqrt(jnp.mean(hh * hh, axis=1, keepdims=True) + NORM_EPS) * hw_ref[:, h * M_DV:(h + 1) * M_DV]
        ym_ref[:, h * M_DV:(h + 1) * M_DV] = (og_ref[:, h * M_DV:(h + 1) * M_DV] * hn).astype(BF16)
        nout_ref[:, h, :] = dh * n0 + wh * kf


def _mlstm_step(qkv, sm, sig, hw, c0, n0, m0):
    db = qkv.shape[0]
    og_block = sig.shape[1] // M_V - 1
    rows = MSTEP_ROWS
    row = lambda i: (i, 0)
    return pl.pallas_call(
        _mstep_kernel,
        grid=(db // rows,),
        in_specs=[pl.BlockSpec((rows, M_QK), row),
                  pl.BlockSpec((rows, M_QK), lambda i: (i, 1)),
                  pl.BlockSpec((rows, M_V), lambda i: (i, 2 * M_QK // M_V)),
                  pl.BlockSpec((rows, LANES), row),
                  pl.BlockSpec((rows, M_V), lambda i: (i, og_block)),
                  pl.BlockSpec((1, M_V), lambda i: (0, 0)),
                  pl.BlockSpec((rows, M_HEADS, M_DV, M_DK), lambda i: (i, 0, 0, 0)),
                  pl.BlockSpec((rows, M_HEADS, M_DK), lambda i: (i, 0, 0)),
                  pl.BlockSpec((rows, M_HEADS), row)],
        out_specs=[pl.BlockSpec((rows, M_V), row),
                   pl.BlockSpec((rows, M_HEADS, M_DV, M_DK), lambda i: (i, 0, 0, 0)),
                   pl.BlockSpec((rows, M_HEADS, M_DK), lambda i: (i, 0, 0)),
                   pl.BlockSpec((rows, M_HEADS), row)],
        out_shape=[jax.ShapeDtypeStruct((db, M_V), BF16),
                   jax.ShapeDtypeStruct(c0.shape, F32),
                   jax.ShapeDtypeStruct(n0.shape, F32),
                   jax.ShapeDtypeStruct(m0.shape, F32)],
        scratch_shapes=[pltpu.VMEM((rows, M_DV), F32)],
        compiler_params=_params("arbitrary"),
        name="mstep",
    )(qkv, qkv, qkv, sm, sig, hw, c0, n0, m0)


N_MLSTM_IN = 7
N_MLSTM_OUT = 3
N_MLSTM_SCRATCH = 3
N_DECODE_SCRATCH = 4


def _decode_kernel(*refs, n_pages, mlstm_steps_per_chunk, mlstm_chunks):
    n_dec_in = 3 * n_pages + 4
    n_ml = N_MLSTM_IN if mlstm_steps_per_chunk else 0
    dec_in = refs[1:1 + n_dec_in]
    ml_in = refs[1 + n_dec_in:1 + n_dec_in + n_ml]
    outs = refs[1 + n_dec_in + n_ml:]
    o_ref, outs = outs[0], outs[1:]
    if mlstm_steps_per_chunk:
        ml_out, outs = outs[:N_MLSTM_OUT], outs[N_MLSTM_OUT:]
        ml_scratch = outs[N_DECODE_SCRATCH:]
        step = pl.program_id(0)
        chunk = (step // mlstm_steps_per_chunk) % mlstm_chunks
        heads_per_step = M_HEADS // mlstm_steps_per_chunk
        for g in range(mlstm_steps_per_chunk):
            @pl.when(step % mlstm_steps_per_chunk == g)
            def _(g=g):
                _mlstm_chunk(chunk, mlstm_chunks - 1, range(g * heads_per_step, (g + 1) * heads_per_step),
                             *ml_in, *ml_out, *ml_scratch)
    _decode_sequence(dec_in[:n_pages], dec_in[n_pages:2 * n_pages], dec_in[2 * n_pages:3 * n_pages],
                     *dec_in[3 * n_pages:], o_ref, *outs[:N_DECODE_SCRATCH])


def _decode_sequence(k_refs, v_refs, lf_refs, q_ref, kn_ref, vn_ref, sm_ref, o_ref,
                     s_ref, xn_ref, widen_ref, spread_ref):
    n_pages = len(k_refs)
    page = xn_ref.shape[1]
    pw = page * F_HEADS

    for p in range(n_pages):
        xn_ref[p * F_HEADS:(p + 1) * F_HEADS, :] = lf_refs[p][...]
    xn = xn_ref[...] * LOG2E

    @pl.when(pl.program_id(0) == 0)
    def _():
        t_id = lax.broadcasted_iota(jnp.int32, (page, pw), 0)
        l_id = lax.broadcasted_iota(jnp.int32, (page, pw), 1)
        widen_ref[...] = jnp.where((l_id // F_HEADS) == t_id, 1.0, 0.0).astype(BF16)
        r = lax.broadcasted_iota(jnp.int32, (LANES, pw), 0)
        c = lax.broadcasted_iota(jnp.int32, (LANES, pw), 1)
        spread_ref[...] = jnp.where(r == LFF0 + (c & (F_HEADS - 1)), 1.0, 0.0).astype(BF16)

    wide = _dot_f32_lhs(xn, widen_ref[...])
    r_id = lax.broadcasted_iota(jnp.int32, wide.shape, 0)
    c_id = lax.broadcasted_iota(jnp.int32, wide.shape, 1)
    own = jnp.where((r_id & (F_HEADS - 1)) == (c_id & (F_HEADS - 1)), wide, 0.0)
    g_p = lax.broadcasted_iota(jnp.int32, (n_pages, n_pages * F_HEADS), 0)
    g_r = lax.broadcasted_iota(jnp.int32, (n_pages, n_pages * F_HEADS), 1)
    gather = jnp.where((g_r // F_HEADS) == g_p, 1.0, 0.0).astype(BF16)
    x = _dot_f32_rhs(gather, own)
    lane = lax.broadcasted_iota(jnp.int32, x.shape, 1)
    y, z = x, x
    step = F_HEADS
    while step < pw:
        y = y + jnp.where(lane < pw - step, pltpu.roll(y, pw - step, axis=1), 0.0)
        z = z + pltpu.roll(z, step, axis=1)
        step *= 2
    after = y - x

    run = _dot_f32_lhs(jnp.broadcast_to(sm_ref[...] * LOG2E, (SUBLANES, LANES)), spread_ref[...])[0:1, :]

    q = q_ref[...]
    sub = lax.broadcasted_iota(jnp.int32, (F_HEADS, pw), 0)
    lane_h = lax.broadcasted_iota(jnp.int32, (F_HEADS, pw), 1) & (F_HEADS - 1)
    valid = lane_h == sub
    s_self = jnp.sum(q.astype(F32) * kn_ref[...], axis=1, keepdims=True)
    m_run = s_self
    for p in reversed(range(n_pages)):
        kb = k_refs[p][...].astype(BF16)
        st = _dot_nt(q, kb) + (run + after[p:p + 1, :])
        st = jnp.where(valid, st, -jnp.inf)
        s_ref[p] = st
        m_run = jnp.maximum(m_run, jnp.max(st, axis=1, keepdims=True))
        run = run + z[p:p + 1, :]
    w_self = jnp.exp2(s_self - m_run)
    l = w_self
    acc = w_self * vn_ref[...]
    for p in range(n_pages):
        pe = jnp.exp2(s_ref[p] - m_run)
        l = l + jnp.sum(pe, axis=1, keepdims=True)
        acc = acc + _dot(pe.astype(BF16), v_refs[p][...].astype(BF16))
    o_ref[...] = (acc / l).astype(BF16)


def _mlstm_fits_decode(db, batch, seq_len):
    chunk_steps = batch * (seq_len // CHUNK)
    return db % chunk_steps == 0 and db // chunk_steps in (1, 2, 4)


def _fox_decode(page_table, ck, cv, clf, layer, q3, k3, v3, sm3, mlstm=None):
    db, n_pages = page_table.shape
    pw = ck.shape[2]
    page = clf.shape[3]
    seq = lambda b, pt: (b, 0, 0)

    def pg(p):
        return lambda b, pt: (layer, pt[b, p], 0, 0)

    in_specs = ([pl.BlockSpec((None, None, pw, F_DH), pg(p)) for p in range(n_pages)]
                + [pl.BlockSpec((None, None, pw, F_DH), pg(p)) for p in range(n_pages)]
                + [pl.BlockSpec((None, None, F_HEADS, page), pg(p)) for p in range(n_pages)]
                + [pl.BlockSpec((None, F_HEADS, F_DH), seq),
                   pl.BlockSpec((None, F_HEADS, F_DH), seq),
                   pl.BlockSpec((None, F_HEADS, F_DH), seq),
                   pl.BlockSpec((None, 1, LANES), seq)])
    args = [page_table, *([ck] * n_pages), *([cv] * n_pages), *([clf] * n_pages), q3, k3, v3, sm3]
    out_specs = [pl.BlockSpec((None, F_HEADS, F_DH), seq)]
    out_shape = [jax.ShapeDtypeStruct((db, F_HEADS, F_DH), BF16)]
    scratch = [pltpu.VMEM((n_pages, F_HEADS, pw), F32), pltpu.VMEM((n_pages * F_HEADS, page), F32),
               pltpu.VMEM((page, pw), BF16), pltpu.VMEM((LANES, pw), BF16)]
    per_chunk, nc = 0, 0
    if mlstm is not None:
        qkv, sm, smt, sig, hw, batch = mlstm
        m = qkv.shape[0]
        nc = (m // batch) // CHUNK
        per_chunk = db // (batch * nc)
        og_block = sig.shape[1] // M_V - 1
        row = lambda s, pt: (s // per_chunk, 0)
        in_specs += [pl.BlockSpec((CHUNK, M_QK), row),
                     pl.BlockSpec((CHUNK, M_QK), lambda s, pt: (s // per_chunk, 1)),
                     pl.BlockSpec((CHUNK, M_V), lambda s, pt: (s // per_chunk, 2 * M_QK // M_V)),
                     pl.BlockSpec((CHUNK, LANES), row),
                     pl.BlockSpec((SMALL_ROWS, CHUNK), lambda s, pt: (0, s // per_chunk)),
                     pl.BlockSpec((CHUNK, M_V), lambda s, pt: (s // per_chunk, og_block)),
                     pl.BlockSpec((1, M_V), lambda s, pt: (0, 0))]
        args += [qkv, qkv, qkv, sm, smt, sig, hw]
        seq_of = lambda s, pt: s // (per_chunk * nc)
        out_specs += [pl.BlockSpec((CHUNK, M_V), row),
                      pl.BlockSpec((None, M_HEADS, M_DV, M_DK), lambda s, pt: (seq_of(s, pt), 0, 0, 0)),
                      pl.BlockSpec((None, 2 * M_HEADS, M_DK), lambda s, pt: (seq_of(s, pt), 0, 0))]
        out_shape += [jax.ShapeDtypeStruct((m, M_V), BF16),
                      jax.ShapeDtypeStruct((batch, M_HEADS, M_DV, M_DK), F32),
                      jax.ShapeDtypeStruct((batch, 2 * M_HEADS, M_DK), F32)]
        scratch += [pltpu.VMEM((M_HEADS, M_DV, M_DK), F32),
                    pltpu.VMEM((SUBLANES, M_DK), F32),
                    pltpu.VMEM((SUBLANES, LANES), F32)]
    grid_spec = pltpu.PrefetchScalarGridSpec(
        num_scalar_prefetch=1,
        grid=(db,),
        in_specs=in_specs,
        out_specs=out_specs,
        scratch_shapes=scratch,
    )
    outs = pl.pallas_call(
        functools.partial(_decode_kernel, n_pages=n_pages, mlstm_steps_per_chunk=per_chunk, mlstm_chunks=nc),
        grid_spec=grid_spec,
        out_shape=out_shape,
        compiler_params=_params("arbitrary"),
        name="decode",
    )(*args)
    return outs if mlstm is not None else outs[0]


def _merge_kernel(ym_ref, yf_ref, sga_ref, sgb_ref, x_ref, g1_ref, sc_ref, sh_ref, n2w_ref,
                  wa_ref, wb_ref, wo_ref, x1_ref, h2_ref):
    merged = sga_ref[...] * _dot(ym_ref[...], wa_ref[...]) + sgb_ref[...] * _dot(yf_ref[...], wb_ref[...])
    x1 = x_ref[...] + g1_ref[...] * _dot(merged.astype(BF16), wo_ref[...])
    x1_ref[...] = x1
    rstd = lax.rsqrt(jnp.mean(x1 * x1, axis=-1, keepdims=True) + NORM_EPS)
    h2_ref[...] = ((x1 * rstd * n2w_ref[...]) * (1.0 + sc_ref[...]) + sh_ref[...]).astype(BF16)


def _merge(ym, yf, sig, x2, mod, n2w, wts, tm):
    m, d = x2.shape
    tiles_per_group = (m // tm) // mod.shape[0]
    group_of = lambda i: i // tiles_per_group
    row = lambda i: (i, 0)
    const = lambda i: (0, 0)
    resident = dict(pipeline_mode=pl.Buffered(1))
    return pl.pallas_call(
        _merge_kernel,
        grid=(m // tm,),
        in_specs=[pl.BlockSpec((tm, M_V), row),
                  pl.BlockSpec((tm, F_W), row),
                  pl.BlockSpec((tm, d), row),
                  pl.BlockSpec((tm, d), lambda i: (i, 1)),
                  pl.BlockSpec((tm, d), row),
                  _mod_spec(mod, d, MOD_G1, group_of),
                  _mod_spec(mod, d, MOD_SC2, group_of),
                  _mod_spec(mod, d, MOD_SH2, group_of),
                  pl.BlockSpec((1, d), const),
                  pl.BlockSpec((M_V, d), const, **resident),
                  pl.BlockSpec((F_W, d), const, **resident),
                  pl.BlockSpec((d, d), const, **resident)],
        out_specs=[pl.BlockSpec((tm, d), row), pl.BlockSpec((tm, d), row)],
        out_shape=[jax.ShapeDtypeStruct((m, d), F32), jax.ShapeDtypeStruct((m, d), BF16)],
        compiler_params=_params("arbitrary"),
        name="merge",
    )(ym, yf, sig, sig, x2, mod, mod, mod, n2w, wts["wa"], wts["wb"], wts["wo"])


FFN_TF = 512


def _ffn_kernel(h2_ref, x1_ref, g2_ref, p_ref, wa_ref, wg_ref, cw_ref, cb_ref, w2_ref,
                y_ref, a_out_ref, *rest, seq):
    f = pl.program_id(2)
    tm = h2_ref.shape[0]
    carry_ref = rest[-1]

    @pl.when(f == 0)
    def _():
        y_ref[...] = jnp.zeros_like(y_ref)

    if seq:
        @pl.when(pl.program_id(1) == 0)
        def _():
            carry_ref[f] = p_ref[...]
        wa, wg, w2 = wa_ref[...], wg_ref[...], w2_ref[...]
    else:
        wa, wg, w2 = wa_ref[...].astype(BF16), wg_ref[...].astype(BF16), w2_ref[...].astype(BF16)
        wa_out_ref, wg_out_ref, w2_out_ref = rest[:3]
        wa_out_ref[...] = wa
        wg_out_ref[...] = wg
        w2_out_ref[...] = w2

    h2 = h2_ref[...]
    a = _dot(h2, wa)
    g = _dot(h2, wg)
    if seq:
        prev = carry_ref[f]
        p2, p1 = prev[SUBLANES - 2:SUBLANES - 1, :], prev[SUBLANES - 1:SUBLANES, :]
        r = lax.broadcasted_iota(jnp.int32, a.shape, 0)
        a1 = jnp.where(r == 0, p1, pltpu.roll(a, 1, axis=0))
        a2 = jnp.where(r == 0, p2, jnp.where(r == 1, p1, pltpu.roll(a, 2, axis=0)))
        carry_ref[f] = a[tm - SUBLANES:tm, :]
        a_out_ref[...] = a[tm - SUBLANES:tm, :]
    else:
        a2, a1 = p_ref[0], p_ref[1]
        a_out_ref[...] = a
    ac = cw_ref[0:1, :] * a2 + cw_ref[1:2, :] * a1 + cw_ref[2:3, :] * a + cb_ref[...]
    u = (jax.nn.gelu(ac, approximate=True) * g).astype(BF16)
    y_ref[...] += _dot(u, w2)

    @pl.when(f == pl.num_programs(2) - 1)
    def _():
        y_ref[...] = x1_ref[...] + g2_ref[...] * y_ref[...]


def _ffn(h2, x1, mod, prev, wts, tm, seq):
    m, d = x1.shape
    groups = mod.shape[0]
    tiles = (m // tm) // groups
    dff = wts["w2"].shape[0]
    tf = FFN_TF
    nf = dff // tf
    row = lambda b, i, f: (b * tiles + i, 0)
    up_tile = pl.BlockSpec((None, d, tf), lambda b, i, f: (f, 0, 0))
    down_tile = pl.BlockSpec((tf, d), lambda b, i, f: (f, 0))
    out_specs = [pl.BlockSpec((tm, d), row)]
    out_shape = [jax.ShapeDtypeStruct((m, d), F32)]
    if seq:
        prev_spec = pl.BlockSpec((None, SUBLANES, tf), lambda b, i, f: (b, 0, f))
        out_specs.append(pl.BlockSpec((None, None, SUBLANES, tf), lambda b, i, f: (b, i, 0, f)))
        out_shape.append(jax.ShapeDtypeStruct((groups, tiles, SUBLANES, dff), F32))
        w_specs = [up_tile, up_tile]
        w_args = [wts["wa_t"], wts["wg_t"]]
    else:
        assert groups == 1 and tiles == 1, "the bf16 weight tiles are written once"
        prev_spec = pl.BlockSpec((CONV_W - 1, tm, tf), lambda b, i, f: (0, i, f))
        out_specs += [pl.BlockSpec((tm, tf), lambda b, i, f: (i, f)), up_tile, up_tile, down_tile]
        out_shape += [jax.ShapeDtypeStruct((m, dff), F32),
                      jax.ShapeDtypeStruct((nf, d, tf), BF16),
                      jax.ShapeDtypeStruct((nf, d, tf), BF16),
                      jax.ShapeDtypeStruct((dff, d), BF16)]
        w_specs = [pl.BlockSpec((d, tf), lambda b, i, f: (0, f)),
                   pl.BlockSpec((d, tf), lambda b, i, f: (0, f + nf))]
        w_args = [wts["w1"], wts["w1"]]
    return pl.pallas_call(
        functools.partial(_ffn_kernel, seq=seq),
        grid=(groups, tiles, nf),
        in_specs=[pl.BlockSpec((tm, d), row),
                  pl.BlockSpec((tm, d), row, pipeline_mode=pl.Buffered(1)),
                  _mod_spec(mod, d, MOD_G2, lambda b, i, f: b),
                  prev_spec,
                  *w_specs,
                  pl.BlockSpec((CONV_W, tf), lambda b, i, f: (0, f)),
                  pl.BlockSpec((1, tf), lambda b, i, f: (0, f)),
                  down_tile],
        out_specs=out_specs,
        out_shape=out_shape,
        scratch_shapes=[pltpu.VMEM((nf, SUBLANES, tf), F32)],
        compiler_params=_params("arbitrary", "arbitrary", "arbitrary"),
        name="ffn_seq" if seq else "ffn_row",
    )(h2, x1, mod, prev, *w_args, wts["cw"], wts["cb"], wts["w2"])


def _prep_weights(w_in, b_in, f_qnorm_w, f_knorm_w, w_proj_a, w_proj_b, w_out, w_ffn_in, conv_w, conv_b, w_ffn_out):
    o = _SPLIT_OFFS
    bbig = jnp.concatenate([b_in[:o[3]], b_in[o[10]:], b_in[o[3]:o[4]], b_in[o[6]:o[9]]])
    bsm = jnp.concatenate([b_in[o[4]:o[6]], b_in[o[9]:o[10]]])
    return {
        "w_in_t": w_in.T,
        "bbig": bbig.reshape(1, -1),
        "bsm": bsm.reshape(SMALL_ROWS, 1),
        "qnw": f_qnorm_w.reshape(1, F_DH),
        "knw": f_knorm_w.reshape(1, F_DH),
        "wa": w_proj_a.astype(BF16),
        "wb": w_proj_b.astype(BF16),
        "wo": w_out.astype(BF16),
        "w1": w_ffn_in,
        "cw": conv_w,
        "cb": conv_b.reshape(1, -1),
        "w2": w_ffn_out,
    }


def kernel(x_prompt, x_sample, c_prompt, c_sample, cache_k, cache_v, cache_logf, page_table, state_C, state_n, state_m, state_conv, w_ada, b_ada, norm1_w, w_in, b_in, m_hnorm_w, f_qnorm_w, f_knorm_w, w_proj_a, w_proj_b, w_out, norm2_w, w_ffn_in, conv_w, conv_b, w_ffn_out):
    depth = w_ada.shape[0]
    assert depth == 1, "single-layer step"
    batch, seq_len, d = x_prompt.shape
    db = x_sample.shape[0]
    assert x_sample.shape[1] == 1
    dff = w_ffn_out.shape[1]
    layer = 0

    wts = _prep_weights(w_in[layer], b_in[layer], f_qnorm_w[layer], f_knorm_w[layer], w_proj_a[layer],
                        w_proj_b[layer], w_out[layer], w_ffn_in[layer], conv_w[layer], conv_b[layer],
                        w_ffn_out[layer])
    n1w = norm1_w[layer].reshape(1, d)
    n2w = norm2_w[layer].reshape(1, d)
    hw = m_hnorm_w[layer].reshape(1, M_V)

    mod_p, mod_s = _ada(c_prompt, c_sample, w_ada[layer], b_ada[layer])
    mod_p = mod_p.reshape(batch, 1, 6 * d)
    mod_s = mod_s.reshape(1, db, 6 * d)

    tm_p = min(1024, seq_len)
    xp = x_prompt.reshape(batch * seq_len, d)
    xs = x_sample.reshape(db, d)
    h, sm, smt = _norm(xp, mod_p, n1w, wts, tm_p)
    h_s, sm_s, _ = _norm(xs, mod_s, n1w, wts, db)
    (qkv,), (qkv_s,) = _proj(h, h_s, wts, _J_MIX, _J_GATE, "mix", tm_p)
    (sig,), (sig_s,) = _proj(h, h_s, wts, _J_GATE, _J_FQ, "sigmoid", tm_p)
    (qf,), (qf_s,) = _proj(h, h_s, wts, _J_FQ, _J_FK, "headnorm", tm_p, nw=wts["qnw"], scale=FOX_Q_SCALE)
    (kf, kfb), (kf_s, _) = _proj(h, h_s, wts, _J_FK, _J_FV, "headnorm", tm_p, nw=wts["knw"], emit_f32=True)
    (vf, vfb), (vf_s, _) = _proj(h, h_s, wts, _J_FV, _J_END, "copy", tm_p)
    fcol = _cumsum(sm, batch)

    n_pool, page = cache_k.shape[1], cache_k.shape[2]
    decode_args = (page_table,
                   cache_k.reshape(depth, n_pool, page * F_HEADS, F_DH),
                   cache_v.reshape(depth, n_pool, page * F_HEADS, F_DH),
                   cache_logf.transpose(0, 1, 3, 2),
                   layer,
                   qf_s.reshape(db, F_HEADS, F_DH), kf_s.reshape(db, F_HEADS, F_DH),
                   vf_s.reshape(db, F_HEADS, F_DH), sm_s.reshape(db, 1, LANES))
    if _mlstm_fits_decode(db, batch, seq_len):
        yf_s, ym, c_p, nm_p = _fox_decode(*decode_args, mlstm=(qkv, sm, smt, sig, hw, batch))
    else:
        yf_s = _fox_decode(*decode_args)
        ym, c_p, nm_p = _mlstm_prompt(qkv, sm, smt, sig, hw, batch)
    yf_s = yf_s.reshape(db, F_W)

    ym_s, c_s, n_s, m_s = _mlstm_step(qkv_s, sm_s, sig_s, hw, state_C[layer], state_n[layer], state_m[layer])
    x1_s, h2_s = _merge(ym_s, yf_s, sig_s, xs, mod_s, n2w, wts, db)
    prev_s = state_conv[layer].transpose(1, 0, 2)
    ys, a_s, wa_t, wg_t, w2_b = _ffn(h2_s, x1_s, mod_s, prev_s, wts, db, seq=False)
    wts_ffn = {"wa_t": wa_t, "wg_t": wg_t, "w2": w2_b, "cw": wts["cw"], "cb": wts["cb"]}

    yf = _fox_prompt(qf, kfb, vfb, fcol, batch)
    x1, h2 = _merge(ym, yf, sig, xp, mod_p, n2w, wts, min(512, seq_len))
    conv0 = jnp.zeros((batch, SUBLANES, dff), F32)
    yp, a_last = _ffn(h2, x1, mod_p, conv0, wts_ffn, tm_p, seq=True)

    y_prompt = yp.reshape(batch, seq_len, d)
    k_prompt = kf.reshape(depth, batch, seq_len, F_HEADS, F_DH)
    v_prompt = vf.reshape(depth, batch, seq_len, F_HEADS, F_DH)
    logf_prompt = sm[:, LFF0:LFF0 + F_HEADS].reshape(depth, batch, seq_len, F_HEADS)
    c_prompt_out = c_p[None]
    n_prompt = nm_p[:, :M_HEADS, :][None]
    m_prompt = nm_p[:, M_HEADS:, 0][None]
    conv_prompt = a_last[:, -1, SUBLANES - (CONV_W - 1):, :][None]

    y_sample = ys.reshape(db, 1, d)
    k_sample = kf_s.reshape(depth, db, 1, F_HEADS, F_DH)
    v_sample = vf_s.reshape(depth, db, 1, F_HEADS, F_DH)
    logf_sample = sm_s[:, LFF0:LFF0 + F_HEADS].reshape(depth, db, 1, F_HEADS)
    conv_sample = jnp.stack([state_conv[layer][:, CONV_W - 2, :], a_s], axis=1)[None]

    return (y_prompt, y_sample, k_prompt, v_prompt, logf_prompt, c_prompt_out, n_prompt, m_prompt, conv_prompt,
            k_sample, v_sample, logf_sample, c_s[None], n_s[None], m_s[None], conv_sample)
```

```python
import functools

import jax
import jax.numpy as jnp
from jax import lax
from jax.experimental import pallas as pl
from jax.experimental.pallas import tpu as pltpu

F32 = jnp.float32
BF16 = jnp.bfloat16

NORM_EPS = 1e-6
M_HEADS, M_DK, M_DV = 4, 128, 256
F_HEADS, F_DH = 8, 128
M_QK = M_HEADS * M_DK
M_V = M_HEADS * M_DV
F_W = F_HEADS * F_DH
CONV_W = 3
CHUNK = 128
LANES = 128
SUBLANES = 8
SMALL_ROWS = 16
VMEM_LIMIT = 56 * 1024 * 1024

IG0, LFM0, LFF0 = 0, M_HEADS, 2 * M_HEADS

LOG2E = 1.4426950408889634
FOX_Q_SCALE = F_DH ** -0.5 * LOG2E


def _dot(a, b):
    return jnp.dot(a, b, preferred_element_type=F32)


def _dot_nt(a, b):
    return lax.dot_general(a, b, (((1,), (1,)), ((), ())), preferred_element_type=F32)


def _dot_tn(a, b):
    return lax.dot_general(a, b, (((0,), (0,)), ((), ())), preferred_element_type=F32)


def _split3(x):
    hi = x.astype(BF16)
    r1 = x - hi.astype(F32)
    mid = r1.astype(BF16)
    lo = (r1 - mid.astype(F32)).astype(BF16)
    return hi, mid, lo


def _dot_f32_lhs(x, mat01):
    hi, mid, lo = _split3(x)
    return _dot(hi, mat01) + _dot(mid, mat01) + _dot(lo, mat01)


def _dot_f32_rhs(mat01, x):
    hi, mid, lo = _split3(x)
    return _dot(mat01, hi) + _dot(mat01, mid) + _dot(mat01, lo)


def _log_sigmoid(x):
    return jnp.minimum(x, 0.0) - jnp.log1p(jnp.exp(-jnp.abs(x)))


def _tri01(n, kind):
    r = lax.broadcasted_iota(jnp.int32, (n, n), 0)
    c = lax.broadcasted_iota(jnp.int32, (n, n), 1)
    m = c <= r if kind == "lower" else r <= c
    return jnp.where(m, 1.0, 0.0).astype(BF16)


def _params(*sem):
    return pltpu.CompilerParams(dimension_semantics=sem, vmem_limit_bytes=VMEM_LIMIT)


def _ada_kernel(cp_ref, cs_ref, w_ref, b_ref, op_ref, os_ref):
    w = w_ref[...].astype(BF16)
    for c_ref, o_ref in ((cp_ref, op_ref), (cs_ref, os_ref)):
        c = c_ref[...]
        s = (c * jax.nn.sigmoid(c)).astype(BF16)
        o_ref[...] = _dot(s, w) + b_ref[...]


def _ada(c_p, c_s, w_ada, b_ada, tn=1024):
    d = c_p.shape[1]
    n = w_ada.shape[1]
    return pl.pallas_call(
        _ada_kernel,
        grid=(n // tn,),
        in_specs=[pl.BlockSpec(c_p.shape, lambda j: (0, 0)),
                  pl.BlockSpec(c_s.shape, lambda j: (0, 0)),
                  pl.BlockSpec((d, tn), lambda j: (0, j)),
                  pl.BlockSpec((1, tn), lambda j: (0, j))],
        out_specs=[pl.BlockSpec((c_p.shape[0], tn), lambda j: (0, j)),
                   pl.BlockSpec((c_s.shape[0], tn), lambda j: (0, j))],
        out_shape=[jax.ShapeDtypeStruct((c_p.shape[0], n), F32), jax.ShapeDtypeStruct((c_s.shape[0], n), F32)],
        compiler_params=_params("arbitrary"),
        name="ada",
    )(c_p, c_s, w_ada, b_ada.reshape(1, n))


MOD_SH1, MOD_SC1, MOD_G1, MOD_SH2, MOD_SC2, MOD_G2 = range(6)


PROJ_TN = 1024
_J_MIX, _J_GATE, _J_FQ, _J_FK, _J_FV, _J_END = 0, 2, 7, 8, 9, 10
_J_MO = _J_GATE + 4
_SPLIT_SIZES = (M_QK, M_QK, M_V, M_V, M_HEADS, M_HEADS, F_W, F_W, F_W, F_HEADS)
_SPLIT_OFFS = tuple(sum(_SPLIT_SIZES[:i]) for i in range(len(_SPLIT_SIZES) + 1))
_ROW_MO, _ROW_SMALL_M, _ROW_FQ, _ROW_SMALL_F, _ROW_GATES = (_SPLIT_OFFS[3], _SPLIT_OFFS[4], _SPLIT_OFFS[6],
                                                            _SPLIT_OFFS[9], _SPLIT_OFFS[10])
_TILE_RUNS = ((_J_MIX, 0), (_J_GATE, _ROW_GATES), (_J_MO, _ROW_MO), (_J_FQ, _ROW_FQ))


def _tile_row(jj):
    row = _TILE_RUNS[0][1] + PROJ_TN * (jj - _TILE_RUNS[0][0])
    for t0, r0 in _TILE_RUNS[1:]:
        row = jnp.where(jj >= t0, r0 + PROJ_TN * (jj - t0), row)
    return row


def _norm_kernel(x_ref, sc_ref, sh_ref, n1w_ref, wsm_m_ref, wsm_f_ref, bsm_ref, h_ref, sm_ref, smt_ref):
    x = x_ref[...]
    rstd = lax.rsqrt(jnp.mean(x * x, axis=-1, keepdims=True) + NORM_EPS)
    h = (x * rstd * n1w_ref[...]) * (1.0 + sc_ref[...]) + sh_ref[...]
    hb = h.astype(BF16)
    h_ref[...] = hb
    wsm = jnp.concatenate([wsm_m_ref[...], wsm_f_ref[...]], axis=0).astype(BF16)
    smt = _dot_nt(wsm, hb) + bsm_ref[...]
    row = lax.broadcasted_iota(jnp.int32, smt.shape, 0)
    smt = jnp.where(row < LFM0, smt, _log_sigmoid(smt))
    smt_ref[...] = smt
    pad = jnp.zeros((LANES - SMALL_ROWS, smt.shape[1]), F32)
    sm_ref[...] = jnp.concatenate([smt, pad], axis=0).T


def _proj_kernel(h_ref, hs_ref, w_ref, b_ref, *rest, kind, scale, emit_f32):
    *rest, wb_ref = rest
    nw_ref = None
    if kind == "headnorm":
        nw_ref, *rest = rest
    n_out = len(rest) // 2
    i = pl.program_id(1)

    @pl.when(i == 0)
    def _():
        wb_ref[...] = w_ref[...].astype(BF16)

    def project(rows_ref, outs):
        acc = _dot_nt(rows_ref[...], wb_ref[...]) + b_ref[...]
        if kind == "mix":
            lane = lax.broadcasted_iota(jnp.int32, (1, acc.shape[1]), 1)
            is_mk = jnp.logical_and(pl.program_id(0) == 0, lane >= M_QK)
            outs[0][...] = (acc * jnp.where(is_mk, M_DK ** -0.5, 1.0)).astype(BF16)
        elif kind == "sigmoid":
            outs[0][...] = jax.nn.sigmoid(acc).astype(BF16)
        elif kind == "headnorm":
            for h in range(acc.shape[1] // F_DH):
                sl = acc[:, h * F_DH:(h + 1) * F_DH]
                ms = jnp.mean(sl * sl, axis=-1, keepdims=True)
                y = sl * lax.rsqrt(ms + NORM_EPS) * nw_ref[...]
                if emit_f32:
                    outs[0][:, h * F_DH:(h + 1) * F_DH] = y
                outs[-1][:, h * F_DH:(h + 1) * F_DH] = (y * scale).astype(BF16)
        else:
            outs[0][...] = acc
            outs[1][...] = acc.astype(BF16)

    @pl.when(i == 0)
    def _():
        project(hs_ref, rest[n_out:])

    @pl.when(i > 0)
    def _():
        project(h_ref, rest[:n_out])


def _feature_rows_spec(rows, d, first_row):
    def index_map(*g):
        return pl.multiple_of(jnp.asarray(first_row(*g), jnp.int32), SUBLANES), 0

    return pl.BlockSpec((pl.Element(rows), pl.Element(d)), index_map)


def _mod_spec(mod, d, chunk, group_of):
    return pl.BlockSpec((None, mod.shape[1], d), lambda *g: (group_of(*g), 0, chunk))


def _norm(x2, mod, n1w, wts, tm):
    m, d = x2.shape
    tiles_per_group = (m // tm) // mod.shape[0]
    group_of = lambda i: i // tiles_per_group
    return pl.pallas_call(
        _norm_kernel,
        grid=(m // tm,),
        in_specs=[pl.BlockSpec((tm, d), lambda i: (i, 0)),
                  _mod_spec(mod, d, MOD_SC1, group_of),
                  _mod_spec(mod, d, MOD_SH1, group_of),
                  pl.BlockSpec((1, d), lambda i: (0, 0)),
                  _feature_rows_spec(SUBLANES, d, lambda i: _ROW_SMALL_M),
                  _feature_rows_spec(SUBLANES, d, lambda i: _ROW_SMALL_F),
                  pl.BlockSpec((SMALL_ROWS, 1), lambda i: (0, 0))],
        out_specs=[pl.BlockSpec((tm, d), lambda i: (i, 0)),
                   pl.BlockSpec((tm, LANES), lambda i: (i, 0)),
                   pl.BlockSpec((SMALL_ROWS, tm), lambda i: (0, i))],
        out_shape=[jax.ShapeDtypeStruct((m, d), BF16),
                   jax.ShapeDtypeStruct((m, LANES), F32),
                   jax.ShapeDtypeStruct((SMALL_ROWS, m), F32)],
        compiler_params=_params("arbitrary"),
        name="norm",
    )(x2, mod, mod, n1w, wts["w_in_t"], wts["w_in_t"], wts["bsm"])


def _proj(h, hs, wts, j0, j1, kind, tm, nw=None, scale=1.0, emit_f32=False):
    m, d = h.shape
    ms = hs.shape[0]
    n_row_tiles = m // tm
    tn = PROJ_TN
    width = (j1 - j0) * tn
    prompt_row = lambda i: jnp.maximum(i - 1, 0)
    in_specs = [pl.BlockSpec((tm, d), lambda j, i: (prompt_row(i), 0)),
                pl.BlockSpec((ms, d), lambda j, i: (0, 0)),
                _feature_rows_spec(tn, d, lambda j, i: _tile_row(j0 + j)),
                pl.BlockSpec((1, tn), lambda j, i: (0, j0 + j))]
    args = [h, hs, wts["w_in_t"], wts["bbig"]]
    if kind == "headnorm":
        in_specs.append(pl.BlockSpec((1, F_DH), lambda j, i: (0, 0)))
        args.append(nw)
    dtypes = ([F32] if emit_f32 or kind == "copy" else []) + [BF16]
    n_out = len(dtypes)
    out_shape = ([jax.ShapeDtypeStruct((m, width), t) for t in dtypes]
                 + [jax.ShapeDtypeStruct((ms, width), t) for t in dtypes])
    out_specs = ([pl.BlockSpec((tm, tn), lambda j, i: (prompt_row(i), j))] * n_out
                 + [pl.BlockSpec((ms, tn), lambda j, i: (0, j))] * n_out)
    outs = pl.pallas_call(
        functools.partial(_proj_kernel, kind=kind, scale=scale, emit_f32=emit_f32),
        grid=(j1 - j0, n_row_tiles + 1),
        in_specs=in_specs,
        out_specs=out_specs,
        out_shape=out_shape,
        scratch_shapes=[pltpu.VMEM((tn, d), BF16)],
        compiler_params=_params("arbitrary", "arbitrary"),
        name="proj_" + kind,
    )(*args)
    return outs[:n_out], outs[n_out:]


def _cumsum_kernel(sm_ref, fcol_ref, carry_ref):
    c = pl.program_id(1)
    tc = sm_ref.shape[0]

    @pl.when(c == 0)
    def _():
        carry_ref[...] = jnp.zeros_like(carry_ref)

    fc = _dot_f32_rhs(_tri01(tc, "lower"), sm_ref[...]) + carry_ref[0:1, :]
    fcol_ref[...] = fc
    carry_ref[...] = jnp.broadcast_to(fc[tc - 1:tc, :], carry_ref.shape)


def _cumsum(sm, batch, tc=512):
    m = sm.shape[0]
    nt = (m // batch) // tc
    return pl.pallas_call(
        _cumsum_kernel,
        grid=(batch, nt),
        in_specs=[pl.BlockSpec((tc, LANES), lambda b, c: (b * nt + c, 0))],
        out_specs=pl.BlockSpec((tc, LANES), lambda b, c: (b * nt + c, 0)),
        out_shape=jax.ShapeDtypeStruct((m, LANES), F32),
        scratch_shapes=[pltpu.VMEM((SUBLANES, LANES), F32)],
        compiler_params=_params("arbitrary", "arbitrary"),
        name="cumsum",
    )(sm)


def _mlstm_kernel(*refs):
    _mlstm_chunk(pl.program_id(1), pl.num_programs(1) - 1, range(M_HEADS), *refs)


def _mlstm_chunk(c, last_c, heads, q_ref, k_ref, v_ref, sm_ref, smt_ref, og_ref, hw_ref,
                 ym_ref, cout_ref, nm_ref, c_ref, n_ref, m_ref):
    L = q_ref.shape[0]

    @pl.when(c == 0)
    def _():
        for h in heads:
            c_ref[h] = jnp.zeros((M_DV, M_DK), F32)
            n_ref[h:h + 1, :] = jnp.zeros((1, M_DK), F32)
            m_ref[h:h + 1, :] = jnp.zeros((1, LANES), F32)

    r = lax.broadcasted_iota(jnp.int32, (L, L), 0)
    s_idx = lax.broadcasted_iota(jnp.int32, (L, L), 1)
    causal = s_idx <= r
    sm = sm_ref[...]
    smt = smt_ref[...]
    bcol_all = _dot_f32_rhs(_tri01(L, "lower"), sm)
    brow_all = _dot_f32_lhs(smt, _tri01(L, "upper"))

    for h in heads:
        ig_col = sm[:, IG0 + h:IG0 + h + 1]
        b_col = bcol_all[:, LFM0 + h:LFM0 + h + 1]
        ig_row = smt[IG0 + h:IG0 + h + 1, :]
        b_row = brow_all[LFM0 + h:LFM0 + h + 1, :]
        m0 = m_ref[h:h + 1, 0:1]
        a_row = ig_row - b_row
        cm_col = jnp.max(jnp.where(causal, a_row, -jnp.inf), axis=1, keepdims=True)
        mx_col = jnp.maximum(m0, cm_col)
        m_col = b_col + mx_col
        dmat = jnp.exp(jnp.where(causal, a_row - mx_col, -jnp.inf))
        inter = jnp.exp(m0 - mx_col)
        q = q_ref[:, h * M_DK:(h + 1) * M_DK]
        k = k_ref[:, h * M_DK:(h + 1) * M_DK]
        v = v_ref[:, h * M_DV:(h + 1) * M_DV]
        c0 = c_ref[h]
        n0 = n_ref[h:h + 1, :]
        s = _dot_nt(q, k) * dmat
        num = _dot(s.astype(BF16), v) + inter * _dot_nt(q, c0.astype(BF16))
        den = jnp.sum(s, axis=1, keepdims=True) + inter * jnp.sum(q.astype(F32) * n0, axis=1, keepdims=True)
        hh = num / jnp.maximum(jnp.abs(den), jnp.exp(-m_col))
        hn = hh * lax.rsqrt(jnp.mean(hh * hh, axis=1, keepdims=True) + NORM_EPS) * hw_ref[:, h * M_DV:(h + 1) * M_DV]
        ym_ref[:, h * M_DV:(h + 1) * M_DV] = (og_ref[:, h * M_DV:(h + 1) * M_DV] * hn).astype(BF16)
        m_new = m_col[L - 1:L, :]
        b_last = b_col[L - 1:L, :]
        w_end = jnp.exp(ig_col + b_last - b_col - m_new)
        decay = jnp.exp(b_last + m0 - m_new)
        vw = (w_end * v.astype(F32)).astype(BF16)
        c_ref[h] = decay * c0 + _dot_tn(vw, k)
        n_ref[h:h + 1, :] = decay * n0 + jnp.sum(w_end * k.astype(F32), axis=0, keepdims=True)
        m_ref[h:h + 1, :] = jnp.broadcast_to(m_new, (1, LANES))

    @pl.when(c == last_c)
    def _():
        for h in heads:
            cout_ref[h] = c_ref[h]
            nm_ref[h:h + 1, :] = n_ref[h:h + 1, :]
            nm_ref[M_HEADS + h:M_HEADS + h + 1, :] = m_ref[h:h + 1, :]


def _mlstm_prompt(qkv, sm, smt, sig, hw, batch):
    m = qkv.shape[0]
    og_block = sig.shape[1] // M_V - 1
    nc = (m // batch) // CHUNK
    row = lambda b, c: (b * nc + c, 0)
    return pl.pallas_call(
        _mlstm_kernel,
        grid=(batch, nc),
        in_specs=[pl.BlockSpec((CHUNK, M_QK), row),
                  pl.BlockSpec((CHUNK, M_QK), lambda b, c: (b * nc + c, 1)),
                  pl.BlockSpec((CHUNK, M_V), lambda b, c: (b * nc + c, 2 * M_QK // M_V)),
                  pl.BlockSpec((CHUNK, LANES), row),
                  pl.BlockSpec((SMALL_ROWS, CHUNK), lambda b, c: (0, b * nc + c)),
                  pl.BlockSpec((CHUNK, M_V), lambda b, c: (b * nc + c, og_block)),
                  pl.BlockSpec((1, M_V), lambda b, c: (0, 0))],
        out_specs=[pl.BlockSpec((CHUNK, M_V), row),
                   pl.BlockSpec((None, M_HEADS, M_DV, M_DK), lambda b, c: (b, 0, 0, 0)),
                   pl.BlockSpec((None, 2 * M_HEADS, M_DK), lambda b, c: (b, 0, 0))],
        out_shape=[jax.ShapeDtypeStruct((m, M_V), BF16),
                   jax.ShapeDtypeStruct((batch, M_HEADS, M_DV, M_DK), F32),
                   jax.ShapeDtypeStruct((batch, 2 * M_HEADS, M_DK), F32)],
        scratch_shapes=[pltpu.VMEM((M_HEADS, M_DV, M_DK), F32),
                        pltpu.VMEM((SUBLANES, M_DK), F32),
                        pltpu.VMEM((SUBLANES, LANES), F32)],
        compiler_params=_params("arbitrary", "arbitrary"),
        name="mlstm",
    )(qkv, qkv, qkv, sm, smt, sig, hw)


FOX_HEADS_PER_STEP = 4
FOX_ROW_CHUNK = 32


def _forget_columns(f, first):
    hi, mid, lo = _split3(f)
    lane = lax.broadcasted_iota(jnp.int32, (f.shape[0], LANES), 1)
    ones = jnp.where(jnp.abs(lane - (4 - first)) <= 1, 1.0, 0.0)
    ext = jnp.where(lane == first, hi.astype(F32),
                    jnp.where(lane == first + 1, mid.astype(F32),
                              jnp.where(lane == first + 2, lo.astype(F32), ones)))
    return ext.astype(BF16)


def _fox_kernel(q_ref, k_ref, v_ref, fcol_ref, o_ref, qaug_ref, kaug_ref, s_ref, p_ref, m_ref, l_ref, acc_ref):
    hp = pl.program_id(1)
    qi = pl.program_id(2)
    tq = q_ref.shape[0]
    t = k_ref.shape[0]
    nh = FOX_HEADS_PER_STEP

    def forget_sum(fc, h2):
        lane = lax.broadcasted_iota(jnp.int32, fc.shape, 1)
        sel = jnp.where(lane == LFF0 + nh * hp + h2, fc, 0.0)
        return jnp.sum(sel, axis=1, keepdims=True) * LOG2E

    @pl.when(qi == 0)
    def _():
        fc = fcol_ref[...]
        for h2 in range(nh):
            kaug_ref[h2, :, 0:F_DH] = k_ref[:, h2 * F_DH:(h2 + 1) * F_DH]
            kaug_ref[h2, :, F_DH:2 * F_DH] = _forget_columns(-forget_sum(fc, h2), 3)

    fq = fcol_ref[pl.ds(pl.multiple_of(qi * tq, tq), tq), :]
    for h2 in range(nh):
        qaug_ref[h2, :, 0:F_DH] = q_ref[:, h2 * F_DH:(h2 + 1) * F_DH]
        qaug_ref[h2, :, F_DH:2 * F_DH] = _forget_columns(forget_sum(fq, h2), 0)
    m_ref[...] = jnp.full_like(m_ref, -jnp.inf)
    l_ref[...] = jnp.zeros_like(l_ref)
    acc_ref[...] = jnp.zeros_like(acc_ref)
    rc = FOX_ROW_CHUNK
    n_lane_tiles = tq // LANES

    def softmax_rows(h2, c, masked):
        rows = slice(c * rc, (c + 1) * rc)
        s = s_ref[h2, rows, :]
        if masked:
            r = lax.broadcasted_iota(jnp.int32, (rc, tq), 0) + c * rc
            col = lax.broadcasted_iota(jnp.int32, (rc, tq), 1)
            s = jnp.where(r >= col, s, -jnp.inf)
        tiles = [s[:, k * LANES:(k + 1) * LANES] for k in range(n_lane_tiles)]
        m_old = m_ref[h2, rows, :]
        mx = functools.reduce(jnp.maximum, tiles)
        m_new = jnp.maximum(m_old, jnp.max(mx, axis=1, keepdims=True))
        alpha = jnp.exp2(m_old - m_new)
        ps = [jnp.exp2(tile - m_new) for tile in tiles]
        row_sum = jnp.sum(functools.reduce(jnp.add, ps), axis=1, keepdims=True)
        l_ref[h2, rows, :] = alpha * l_ref[h2, rows, :] + row_sum
        for k in range(n_lane_tiles):
            p_ref[h2, rows, k * LANES:(k + 1) * LANES] = ps[k].astype(BF16)
        acc_ref[h2, rows, :] = alpha * acc_ref[h2, rows, :]
        m_ref[h2, rows, :] = m_new

    def block(kj, masked):
        off = pl.multiple_of(kj * tq, tq)
        for h2 in range(nh):
            s_ref[h2] = _dot_nt(qaug_ref[h2], kaug_ref[h2, pl.ds(off, tq), :])
        for h2 in range(nh):
            for c in range(tq // rc):
                softmax_rows(h2, c, masked)
            acc_ref[h2] += _dot(p_ref[h2], v_ref[pl.ds(off, tq), h2 * F_DH:(h2 + 1) * F_DH])

    def body(kj, carry):
        block(kj, False)
        return carry

    lax.fori_loop(0, qi, body, 0)
    block(qi, True)
    for h2 in range(nh):
        o_ref[:, h2 * F_DH:(h2 + 1) * F_DH] = (acc_ref[h2] / l_ref[h2]).astype(BF16)


def _fox_prompt(qf, kfb, vfb, fcol, batch, tq=512):
    m = qf.shape[0]
    t = m // batch
    nq = t // tq
    nh = FOX_HEADS_PER_STEP
    wide = nh * F_DH
    return pl.pallas_call(
        _fox_kernel,
        grid=(batch, F_HEADS // nh, nq),
        in_specs=[pl.BlockSpec((tq, wide), lambda b, h, i: (b * nq + i, h)),
                  pl.BlockSpec((t, wide), lambda b, h, i: (b, h)),
                  pl.BlockSpec((t, wide), lambda b, h, i: (b, h)),
                  pl.BlockSpec((t, LANES), lambda b, h, i: (b, 0))],
        out_specs=pl.BlockSpec((tq, wide), lambda b, h, i: (b * nq + i, h)),
        out_shape=jax.ShapeDtypeStruct((m, F_W), BF16),
        scratch_shapes=[pltpu.VMEM((nh, tq, 2 * F_DH), BF16),
                        pltpu.VMEM((nh, t, 2 * F_DH), BF16),
                        pltpu.VMEM((nh, tq, tq), F32),
                        pltpu.VMEM((nh, tq, tq), BF16),
                        pltpu.VMEM((nh, tq, LANES), F32),
                        pltpu.VMEM((nh, tq, LANES), F32),
                        pltpu.VMEM((nh, tq, F_DH), F32)],
        compiler_params=_params("arbitrary", "arbitrary", "arbitrary"),
        name="fox",
    )(qf, kfb, vfb, fcol)


MSTEP_ROWS = 8


def _mstep_kernel(q_ref, k_ref, v_ref, sm_ref, og_ref, hw_ref, c0_ref, n0_ref, m0_ref,
                  ym_ref, cout_ref, nout_ref, mout_ref, cq_ref):
    rows = q_ref.shape[0]
    sm = sm_ref[...]
    ig = sm[:, IG0:IG0 + M_HEADS]
    lf = sm[:, LFM0:LFM0 + M_HEADS]
    m0 = m0_ref[...]
    m_new = lf + jnp.maximum(m0, ig - lf)
    w_end = jnp.exp(ig - m_new)
    decay = jnp.exp(lf + m0 - m_new)
    mout_ref[...] = m_new
    row_id = lax.broadcasted_iota(jnp.int32, (SUBLANES, M_DV), 0)

    for h in range(M_HEADS):
        qh = q_ref[:, h * M_DK:(h + 1) * M_DK]
        kh = k_ref[:, h * M_DK:(h + 1) * M_DK]
        vh = v_ref[:, h * M_DV:(h + 1) * M_DV]
        qf = qh.astype(F32)
        kf = kh.astype(F32)
        vf = vh.astype(F32)
        wh = w_end[:, h:h + 1]
        dh = decay[:, h:h + 1]
        for r in range(rows):
            c0 = c0_ref[r, h]
            q8 = jnp.broadcast_to(qh[r:r + 1, :], (SUBLANES, M_DK))
            cq_ref[r:r + 1, :] = _dot_nt(q8, c0.astype(BF16))[0:1, :]
            vw8 = jnp.where(row_id == 0, jnp.broadcast_to(wh[r:r + 1, :] * vf[r:r + 1, :], (SUBLANES, M_DV)), 0.0)
            k8 = jnp.broadcast_to(kh[r:r + 1, :], (SUBLANES, M_DK))
            cout_ref[r, h] = dh[r:r + 1, :] * c0 + _dot_tn(vw8.astype(BF16), k8)
        n0 = n0_ref[:, h, :]
        s = jnp.sum(qf * kf, axis=1, keepdims=True) * wh
        num = s * vf + dh * cq_ref[...]
        den = s + dh * jnp.sum(n0 * qf, axis=1, keepdims=True)
        hh = num / jnp.maximum(jnp.abs(den), jnp.exp(-m_new[:, h:h + 1]))
        hn = hh * lax.rsqrt(jnp.mean(hh * hh, axis=1, keepdims=True) + NORM_EPS) * hw_ref[:, h * M_DV:(h + 1) * M_DV]
        ym_ref[:, h * M_DV:(h + 1) * M_DV] = (og_ref[:, h * M_DV:(h + 1) * M_DV] * hn).astype(BF16)
        nout_ref[:, h, :] = dh * n0 + wh * kf


def _mlstm_step(qkv, sm, sig, hw, c0, n0, m0):
    db = qkv.shape[0]
    og_block = sig.shape[1] // M_V - 1
    rows = MSTEP_ROWS
    row = lambda i: (i, 0)
    return pl.pallas_call(
        _mstep_kernel,
        grid=(db // rows,),
        in_specs=[pl.BlockSpec((rows, M_QK), row),
                  pl.BlockSpec((rows, M_QK), lambda i: (i, 1)),
                  pl.BlockSpec((rows, M_V), lambda i: (i, 2 * M_QK // M_V)),
                  pl.BlockSpec((rows, LANES), row),
                  pl.BlockSpec((rows, M_V), lambda i: (i, og_block)),
                  pl.BlockSpec((1, M_V), lambda i: (0, 0)),
                  pl.BlockSpec((rows, M_HEADS, M_DV, M_DK), lambda i: (i, 0, 0, 0)),
                  pl.BlockSpec((rows, M_HEADS, M_DK), lambda i: (i, 0, 0)),
                  pl.BlockSpec((rows, M_HEADS), row)],
        out_specs=[pl.BlockSpec((rows, M_V), row),
                   pl.BlockSpec((rows, M_HEADS, M_DV, M_DK), lambda i: (i, 0, 0, 0)),
                   pl.BlockSpec((rows, M_HEADS, M_DK), lambda i: (i, 0, 0)),
                   pl.BlockSpec((rows, M_HEADS), row)],
        out_shape=[jax.ShapeDtypeStruct((db, M_V), BF16),
                   jax.ShapeDtypeStruct(c0.shape, F32),
                   jax.ShapeDtypeStruct(n0.shape, F32),
                   jax.ShapeDtypeStruct(m0.shape, F32)],
        scratch_shapes=[pltpu.VMEM((rows, M_DV), F32)],
        compiler_params=_params("arbitrary"),
        name="mstep",
    )(qkv, qkv, qkv, sm, sig, hw, c0, n0, m0)


N_MLSTM_IN = 7
N_MLSTM_OUT = 3
N_DECODE_SCRATCH = 4


def _decode_kernel(*refs, n_pages, mlstm_steps_per_chunk, mlstm_chunks):
    n_dec_in = 3 * n_pages + 4
    n_ml = N_MLSTM_IN if mlstm_steps_per_chunk else 0
    dec_in = refs[1:1 + n_dec_in]
    ml_in = refs[1 + n_dec_in:1 + n_dec_in + n_ml]
    outs = refs[1 + n_dec_in + n_ml:]
    o_ref, outs = outs[0], outs[1:]
    if mlstm_steps_per_chunk:
        ml_out, outs = outs[:N_MLSTM_OUT], outs[N_MLSTM_OUT:]
        ml_scratch = outs[N_DECODE_SCRATCH:]
        step = pl.program_id(0)
        chunk = (step // mlstm_steps_per_chunk) % mlstm_chunks
        heads_per_step = M_HEADS // mlstm_steps_per_chunk
        for g in range(mlstm_steps_per_chunk):
            @pl.when(step % mlstm_steps_per_chunk == g)
            def _(g=g):
                _mlstm_chunk(chunk, mlstm_chunks - 1, range(g * heads_per_step, (g + 1) * heads_per_step),
                             *ml_in, *ml_out, *ml_scratch)
    _decode_sequence(dec_in[:n_pages], dec_in[n_pages:2 * n_pages], dec_in[2 * n_pages:3 * n_pages],
                     *dec_in[3 * n_pages:], o_ref, *outs[:N_DECODE_SCRATCH])


def _decode_sequence(k_refs, v_refs, lf_refs, q_ref, kn_ref, vn_ref, sm_ref, o_ref,
                     s_ref, xn_ref, widen_ref, spread_ref):
    n_pages = len(k_refs)
    page = xn_ref.shape[1]
    pw = page * F_HEADS

    for p in range(n_pages):
        xn_ref[p * F_HEADS:(p + 1) * F_HEADS, :] = lf_refs[p][...]
    xn = xn_ref[...] * LOG2E

    @pl.when(pl.program_id(0) == 0)
    def _():
        t_id = lax.broadcasted_iota(jnp.int32, (page, pw), 0)
        l_id = lax.broadcasted_iota(jnp.int32, (page, pw), 1)
        widen_ref[...] = jnp.where((l_id // F_HEADS) == t_id, 1.0, 0.0).astype(BF16)
        r = lax.broadcasted_iota(jnp.int32, (LANES, pw), 0)
        c = lax.broadcasted_iota(jnp.int32, (LANES, pw), 1)
        spread_ref[...] = jnp.where(r == LFF0 + (c & (F_HEADS - 1)), 1.0, 0.0).astype(BF16)

    wide = _dot_f32_lhs(xn, widen_ref[...])
    r_id = lax.broadcasted_iota(jnp.int32, wide.shape, 0)
    c_id = lax.broadcasted_iota(jnp.int32, wide.shape, 1)
    own = jnp.where((r_id & (F_HEADS - 1)) == (c_id & (F_HEADS - 1)), wide, 0.0)
    g_p = lax.broadcasted_iota(jnp.int32, (n_pages, n_pages * F_HEADS), 0)
    g_r = lax.broadcasted_iota(jnp.int32, (n_pages, n_pages * F_HEADS), 1)
    gather = jnp.where((g_r // F_HEADS) == g_p, 1.0, 0.0).astype(BF16)
    x = _dot_f32_rhs(gather, own)
    lane = lax.broadcasted_iota(jnp.int32, x.shape, 1)
    y, z = x, x
    step = F_HEADS
    while step < pw:
        y = y + jnp.where(lane < pw - step, pltpu.roll(y, pw - step, axis=1), 0.0)
        z = z + pltpu.roll(z, step, axis=1)
        step *= 2
    after = y - x

    run = _dot_f32_lhs(jnp.broadcast_to(sm_ref[...] * LOG2E, (SUBLANES, LANES)), spread_ref[...])[0:1, :]

    q = q_ref[...]
    sub = lax.broadcasted_iota(jnp.int32, (F_HEADS, pw), 0)
    lane_h = lax.broadcasted_iota(jnp.int32, (F_HEADS, pw), 1) & (F_HEADS - 1)
    valid = lane_h == sub
    s_self = jnp.sum(q.astype(F32) * kn_ref[...], axis=1, keepdims=True)
    m_run = s_self
    for p in reversed(range(n_pages)):
        kb = k_refs[p][...].astype(BF16)
        st = _dot_nt(q, kb) + (run + after[p:p + 1, :])
        st = jnp.where(valid, st, -jnp.inf)
        s_ref[p] = st
        m_run = jnp.maximum(m_run, jnp.max(st, axis=1, keepdims=True))
        run = run + z[p:p + 1, :]
    w_self = jnp.exp2(s_self - m_run)
    l = w_self
    acc = w_self * vn_ref[...]
    for p in range(n_pages):
        pe = jnp.exp2(s_ref[p] - m_run)
        l = l + jnp.sum(pe, axis=1, keepdims=True)
        acc = acc + _dot(pe.astype(BF16), v_refs[p][...].astype(BF16))
    o_ref[...] = (acc / l).astype(BF16)


def _mlstm_fits_decode(db, batch, seq_len):
    chunk_steps = batch * (seq_len // CHUNK)
    return db % chunk_steps == 0 and db // chunk_steps in (1, 2, 4)


def _fox_decode(page_table, ck, cv, clf, layer, q3, k3, v3, sm3, mlstm=None):
    db, n_pages = page_table.shape
    pw = ck.shape[2]
    page = clf.shape[3]
    seq = lambda b, pt: (b, 0, 0)

    def pg(p):
        return lambda b, pt: (layer, pt[b, p], 0, 0)

    in_specs = ([pl.BlockSpec((None, None, pw, F_DH), pg(p)) for p in range(n_pages)]
                + [pl.BlockSpec((None, None, pw, F_DH), pg(p)) for p in range(n_pages)]
                + [pl.BlockSpec((None, None, F_HEADS, page), pg(p)) for p in range(n_pages)]
                + [pl.BlockSpec((None, F_HEADS, F_DH), seq),
                   pl.BlockSpec((None, F_HEADS, F_DH), seq),
                   pl.BlockSpec((None, F_HEADS, F_DH), seq),
                   pl.BlockSpec((None, 1, LANES), seq)])
    args = [page_table, *([ck] * n_pages), *([cv] * n_pages), *([clf] * n_pages), q3, k3, v3, sm3]
    out_specs = [pl.BlockSpec((None, F_HEADS, F_DH), seq)]
    out_shape = [jax.ShapeDtypeStruct((db, F_HEADS, F_DH), BF16)]
    scratch = [pltpu.VMEM((n_pages, F_HEADS, pw), F32), pltpu.VMEM((n_pages * F_HEADS, page), F32),
               pltpu.VMEM((page, pw), BF16), pltpu.VMEM((LANES, pw), BF16)]
    per_chunk, nc = 0, 0
    if mlstm is not None:
        qkv, sm, smt, sig, hw, batch = mlstm
        m = qkv.shape[0]
        nc = (m // batch) // CHUNK
        per_chunk = db // (batch * nc)
        og_block = sig.shape[1] // M_V - 1
        row = lambda s, pt: (s // per_chunk, 0)
        in_specs += [pl.BlockSpec((CHUNK, M_QK), row),
                     pl.BlockSpec((CHUNK, M_QK), lambda s, pt: (s // per_chunk, 1)),
                     pl.BlockSpec((CHUNK, M_V), lambda s, pt: (s // per_chunk, 2 * M_QK // M_V)),
                     pl.BlockSpec((CHUNK, LANES), row),
                     pl.BlockSpec((SMALL_ROWS, CHUNK), lambda s, pt: (0, s // per_chunk)),
                     pl.BlockSpec((CHUNK, M_V), lambda s, pt: (s // per_chunk, og_block)),
                     pl.BlockSpec((1, M_V), lambda s, pt: (0, 0))]
        args += [qkv, qkv, qkv, sm, smt, sig, hw]
        seq_of = lambda s, pt: s // (per_chunk * nc)
        out_specs += [pl.BlockSpec((CHUNK, M_V), row),
                      pl.BlockSpec((None, M_HEADS, M_DV, M_DK), lambda s, pt: (seq_of(s, pt), 0, 0, 0)),
                      pl.BlockSpec((None, 2 * M_HEADS, M_DK), lambda s, pt: (seq_of(s, pt), 0, 0))]
        out_shape += [jax.ShapeDtypeStruct((m, M_V), BF16),
                      jax.ShapeDtypeStruct((batch, M_HEADS, M_DV, M_DK), F32),
                      jax.ShapeDtypeStruct((batch, 2 * M_HEADS, M_DK), F32)]
        scratch += [pltpu.VMEM((M_HEADS, M_DV, M_DK), F32),
                    pltpu.VMEM((SUBLANES, M_DK), F32),
                    pltpu.VMEM((SUBLANES, LANES), F32)]
    grid_spec = pltpu.PrefetchScalarGridSpec(
        num_scalar_prefetch=1,
        grid=(db,),
        in_specs=in_specs,
        out_specs=out_specs,
        scratch_shapes=scratch,
    )
    outs = pl.pallas_call(
        functools.partial(_decode_kernel, n_pages=n_pages, mlstm_steps_per_chunk=per_chunk, mlstm_chunks=nc),
        grid_spec=grid_spec,
        out_shape=out_shape,
        compiler_params=_params("arbitrary"),
        name="decode",
    )(*args)
    return outs if mlstm is not None else outs[0]


def _merge_kernel(ym_ref, yf_ref, sga_ref, sgb_ref, x_ref, g1_ref, sc_ref, sh_ref, n2w_ref,
                  wa_ref, wb_ref, wo_ref, x1_ref, h2_ref):
    merged = sga_ref[...] * _dot(ym_ref[...], wa_ref[...]) + sgb_ref[...] * _dot(yf_ref[...], wb_ref[...])
    x1 = x_ref[...] + g1_ref[...] * _dot(merged.astype(BF16), wo_ref[...])
    x1_ref[...] = x1
    rstd = lax.rsqrt(jnp.mean(x1 * x1, axis=-1, keepdims=True) + NORM_EPS)
    h2_ref[...] = ((x1 * rstd * n2w_ref[...]) * (1.0 + sc_ref[...]) + sh_ref[...]).astype(BF16)


def _merge(ym, yf, sig, x2, mod, n2w, wts, tm):
    m, d = x2.shape
    tiles_per_group = (m // tm) // mod.shape[0]
    group_of = lambda i: i // tiles_per_group
    row = lambda i: (i, 0)
    const = lambda i: (0, 0)
    resident = dict(pipeline_mode=pl.Buffered(1))
    return pl.pallas_call(
        _merge_kernel,
        grid=(m // tm,),
        in_specs=[pl.BlockSpec((tm, M_V), row),
                  pl.BlockSpec((tm, F_W), row),
                  pl.BlockSpec((tm, d), row),
                  pl.BlockSpec((tm, d), lambda i: (i, 1)),
                  pl.BlockSpec((tm, d), row),
                  _mod_spec(mod, d, MOD_G1, group_of),
                  _mod_spec(mod, d, MOD_SC2, group_of),
                  _mod_spec(mod, d, MOD_SH2, group_of),
                  pl.BlockSpec((1, d), const),
                  pl.BlockSpec((M_V, d), const, **resident),
                  pl.BlockSpec((F_W, d), const, **resident),
                  pl.BlockSpec((d, d), const, **resident)],
        out_specs=[pl.BlockSpec((tm, d), row), pl.BlockSpec((tm, d), row)],
        out_shape=[jax.ShapeDtypeStruct((m, d), F32), jax.ShapeDtypeStruct((m, d), BF16)],
        compiler_params=_params("arbitrary"),
        name="merge",
    )(ym, yf, sig, sig, x2, mod, mod, mod, n2w, wts["wa"], wts["wb"], wts["wo"])


FFN_TF = 512


def _ffn_kernel(h2_ref, x1_ref, g2_ref, p_ref, wa_ref, wg_ref, cw_ref, cb_ref, w2_ref,
                y_ref, a_out_ref, *rest, seq):
    f = pl.program_id(2)
    tm = h2_ref.shape[0]
    carry_ref = rest[-1]

    @pl.when(f == 0)
    def _():
        y_ref[...] = jnp.zeros_like(y_ref)

    if seq:
        @pl.when(pl.program_id(1) == 0)
        def _():
            carry_ref[f] = p_ref[...]
        wa, wg, w2 = wa_ref[...], wg_ref[...], w2_ref[...]
    else:
        wa, wg, w2 = wa_ref[...].astype(BF16), wg_ref[...].astype(BF16), w2_ref[...].astype(BF16)
        wa_out_ref, wg_out_ref, w2_out_ref = rest[:3]
        wa_out_ref[...] = wa
        wg_out_ref[...] = wg
        w2_out_ref[...] = w2

    h2 = h2_ref[...]
    a = _dot(h2, wa)
    g = _dot(h2, wg)
    if seq:
        prev = carry_ref[f]
        p2, p1 = prev[SUBLANES - 2:SUBLANES - 1, :], prev[SUBLANES - 1:SUBLANES, :]
        r = lax.broadcasted_iota(jnp.int32, a.shape, 0)
        a1 = jnp.where(r == 0, p1, pltpu.roll(a, 1, axis=0))
        a2 = jnp.where(r == 0, p2, jnp.where(r == 1, p1, pltpu.roll(a, 2, axis=0)))
        carry_ref[f] = a[tm - SUBLANES:tm, :]
        a_out_ref[...] = a[tm - SUBLANES:tm, :]
    else:
        a2, a1 = p_ref[0], p_ref[1]
        a_out_ref[...] = a
    ac = cw_ref[0:1, :] * a2 + cw_ref[1:2, :] * a1 + cw_ref[2:3, :] * a + cb_ref[...]
    u = (jax.nn.gelu(ac, approximate=True) * g).astype(BF16)
    y_ref[...] += _dot(u, w2)

    @pl.when(f == pl.num_programs(2) - 1)
    def _():
        y_ref[...] = x1_ref[...] + g2_ref[...] * y_ref[...]


def _ffn(h2, x1, mod, prev, wts, tm, seq):
    m, d = x1.shape
    groups = mod.shape[0]
    tiles = (m // tm) // groups
    dff = wts["w2"].shape[0]
    tf = FFN_TF
    nf = dff // tf
    row = lambda b, i, f: (b * tiles + i, 0)
    up_tile = pl.BlockSpec((None, d, tf), lambda b, i, f: (f, 0, 0))
    down_tile = pl.BlockSpec((tf, d), lambda b, i, f: (f, 0))
    out_specs = [pl.BlockSpec((tm, d), row)]
    out_shape = [jax.ShapeDtypeStruct((m, d), F32)]
    if seq:
        prev_spec = pl.BlockSpec((None, SUBLANES, tf), lambda b, i, f: (b, 0, f))
        out_specs.append(pl.BlockSpec((None, None, SUBLANES, tf), lambda b, i, f: (b, i, 0, f)))
        out_shape.append(jax.ShapeDtypeStruct((groups, tiles, SUBLANES, dff), F32))
        w_specs = [up_tile, up_tile]
        w_args = [wts["wa_t"], wts["wg_t"]]
    else:
        assert groups == 1 and tiles == 1, "the bf16 weight tiles are written once"
        prev_spec = pl.BlockSpec((CONV_W - 1, tm, tf), lambda b, i, f: (0, i, f))
        out_specs += [pl.BlockSpec((tm, tf), lambda b, i, f: (i, f)), up_tile, up_tile, down_tile]
        out_shape += [jax.ShapeDtypeStruct((m, dff), F32),
                      jax.ShapeDtypeStruct((nf, d, tf), BF16),
                      jax.ShapeDtypeStruct((nf, d, tf), BF16),
                      jax.ShapeDtypeStruct((dff, d), BF16)]
        w_specs = [pl.BlockSpec((d, tf), lambda b, i, f: (0, f)),
                   pl.BlockSpec((d, tf), lambda b, i, f: (0, f + nf))]
        w_args = [wts["w1"], wts["w1"]]
    return pl.pallas_call(
        functools.partial(_ffn_kernel, seq=seq),
        grid=(groups, tiles, nf),
        in_specs=[pl.BlockSpec((tm, d), row),
                  pl.BlockSpec((tm, d), row, pipeline_mode=pl.Buffered(1)),
                  _mod_spec(mod, d, MOD_G2, lambda b, i, f: b),
                  prev_spec,
                  *w_specs,
                  pl.BlockSpec((CONV_W, tf), lambda b, i, f: (0, f)),
                  pl.BlockSpec((1, tf), lambda b, i, f: (0, f)),
                  down_tile],
        out_specs=out_specs,
        out_shape=out_shape,
        scratch_shapes=[pltpu.VMEM((nf, SUBLANES, tf), F32)],
        compiler_params=_params("arbitrary", "arbitrary", "arbitrary"),
        name="ffn_seq" if seq else "ffn_row",
    )(h2, x1, mod, prev, *w_args, wts["cw"], wts["cb"], wts["w2"])


def _prep_weights(w_in, b_in, f_qnorm_w, f_knorm_w, w_proj_a, w_proj_b, w_out, w_ffn_in, conv_w, conv_b, w_ffn_out):
    o = _SPLIT_OFFS
    bbig = jnp.concatenate([b_in[:o[3]], b_in[o[10]:], b_in[o[3]:o[4]], b_in[o[6]:o[9]]])
    bsm = jnp.concatenate([b_in[o[4]:o[6]], b_in[o[9]:o[10]]])
    return {
        "w_in_t": w_in.T,
        "bbig": bbig.reshape(1, -1),
        "bsm": bsm.reshape(SMALL_ROWS, 1),
        "qnw": f_qnorm_w.reshape(1, F_DH),
        "knw": f_knorm_w.reshape(1, F_DH),
        "wa": w_proj_a.astype(BF16),
        "wb": w_proj_b.astype(BF16),
        "wo": w_out.astype(BF16),
        "w1": w_ffn_in,
        "cw": conv_w,
        "cb": conv_b.reshape(1, -1),
        "w2": w_ffn_out,
    }


def kernel(x_prompt, x_sample, c_prompt, c_sample, cache_k, cache_v, cache_logf, page_table, state_C, state_n, state_m, state_conv, w_ada, b_ada, norm1_w, w_in, b_in, m_hnorm_w, f_qnorm_w, f_knorm_w, w_proj_a, w_proj_b, w_out, norm2_w, w_ffn_in, conv_w, conv_b, w_ffn_out):
    depth = w_ada.shape[0]
    assert depth == 1, "single-layer step"
    batch, seq_len, d = x_prompt.shape
    db = x_sample.shape[0]
    assert x_sample.shape[1] == 1
    dff = w_ffn_out.shape[1]
    layer = 0

    wts = _prep_weights(w_in[layer], b_in[layer], f_qnorm_w[layer], f_knorm_w[layer], w_proj_a[layer],
                        w_proj_b[layer], w_out[layer], w_ffn_in[layer], conv_w[layer], conv_b[layer],
                        w_ffn_out[layer])
    n1w = norm1_w[layer].reshape(1, d)
    n2w = norm2_w[layer].reshape(1, d)
    hw = m_hnorm_w[layer].reshape(1, M_V)

    mod_p, mod_s = _ada(c_prompt, c_sample, w_ada[layer], b_ada[layer])
    mod_p = mod_p.reshape(batch, 1, 6 * d)
    mod_s = mod_s.reshape(1, db, 6 * d)

    tm_p = min(1024, seq_len)
    xp = x_prompt.reshape(batch * seq_len, d)
    xs = x_sample.reshape(db, d)
    h, sm, smt = _norm(xp, mod_p, n1w, wts, tm_p)
    h_s, sm_s, _ = _norm(xs, mod_s, n1w, wts, db)
    (qkv,), (qkv_s,) = _proj(h, h_s, wts, _J_MIX, _J_GATE, "mix", tm_p)
    (sig,), (sig_s,) = _proj(h, h_s, wts, _J_GATE, _J_FQ, "sigmoid", tm_p)
    (qf,), (qf_s,) = _proj(h, h_s, wts, _J_FQ, _J_FK, "headnorm", tm_p, nw=wts["qnw"], scale=FOX_Q_SCALE)
    (kf, kfb), (kf_s, _) = _proj(h, h_s, wts, _J_FK, _J_FV, "headnorm", tm_p, nw=wts["knw"], emit_f32=True)
    (vf, vfb), (vf_s, _) = _proj(h, h_s, wts, _J_FV, _J_END, "copy", tm_p)
    fcol = _cumsum(sm, batch)

    n_pool, page = cache_k.shape[1], cache_k.shape[2]
    decode_args = (page_table,
                   cache_k.reshape(depth, n_pool, page * F_HEADS, F_DH),
                   cache_v.reshape(depth, n_pool, page * F_HEADS, F_DH),
                   cache_logf.transpose(0, 1, 3, 2),
                   layer,
                   qf_s.reshape(db, F_HEADS, F_DH), kf_s.reshape(db, F_HEADS, F_DH),
                   vf_s.reshape(db, F_HEADS, F_DH), sm_s.reshape(db, 1, LANES))
    if _mlstm_fits_decode(db, batch, seq_len):
        yf_s, ym, c_p, nm_p = _fox_decode(*decode_args, mlstm=(qkv, sm, smt, sig, hw, batch))
    else:
        yf_s = _fox_decode(*decode_args)
        ym, c_p, nm_p = _mlstm_prompt(qkv, sm, smt, sig, hw, batch)
    yf_s = yf_s.reshape(db, F_W)

    ym_s, c_s, n_s, m_s = _mlstm_step(qkv_s, sm_s, sig_s, hw, state_C[layer], state_n[layer], state_m[layer])
    x1_s, h2_s = _merge(ym_s, yf_s, sig_s, xs, mod_s, n2w, wts, db)
    prev_s = state_conv[layer].transpose(1, 0, 2)
    ys, a_s, wa_t, wg_t, w2_b = _ffn(h2_s, x1_s, mod_s, prev_s, wts, db, seq=False)
    wts_ffn = {"wa_t": wa_t, "wg_t": wg_t, "w2": w2_b, "cw": wts["cw"], "cb": wts["cb"]}

    yf = _fox_prompt(qf, kfb, vfb, fcol, batch)
    x1, h2 = _merge(ym, yf, sig, xp, mod_p, n2w, wts, min(512, seq_len))
    conv0 = jnp.zeros((batch, SUBLANES, dff), F32)
    yp, a_last = _ffn(h2, x1, mod_p, conv0, wts_ffn, tm_p, seq=True)

    y_prompt = yp.reshape(batch, seq_len, d)
    k_prompt = kf.reshape(depth, batch, seq_len, F_HEADS, F_DH)
    v_prompt = vf.reshape(depth, batch, seq_len, F_HEADS, F_DH)
    logf_prompt = sm[:, LFF0:LFF0 + F_HEADS].reshape(depth, batch, seq_len, F_HEADS)
    c_prompt_out = c_p[None]
    n_prompt = nm_p[:, :M_HEADS, :][None]
    m_prompt = nm_p[:, M_HEADS:, 0][None]
    conv_prompt = a_last[:, -1, SUBLANES - (CONV_W - 1):, :][None]

    y_sample = ys.reshape(db, 1, d)
    k_sample = kf_s.reshape(depth, db, 1, F_HEADS, F_DH)
    v_sample = vf_s.reshape(depth, db, 1, F_HEADS, F_DH)
    logf_sample = sm_s[:, LFF0:LFF0 + F_HEADS].reshape(depth, db, 1, F_HEADS)
    conv_sample = jnp.stack([state_conv[layer][:, CONV_W - 2, :], a_s], axis=1)[None]

    return (y_prompt, y_sample, k_prompt, v_prompt, logf_prompt, c_prompt_out, n_prompt, m_prompt, conv_prompt,
            k_sample, v_sample, logf_sample, c_s[None], n_s[None], m_s[None], conv_sample)
```

```python
import functools

import jax
import jax.numpy as jnp
from jax import lax
from jax.experimental import pallas as pl
from jax.experimental.pallas import tpu as pltpu

F32 = jnp.float32
BF16 = jnp.bfloat16

NORM_EPS = 1e-6
M_HEADS, M_DK, M_DV = 4, 128, 256
F_HEADS, F_DH = 8, 128
M_QK = M_HEADS * M_DK
M_V = M_HEADS * M_DV
F_W = F_HEADS * F_DH
CONV_W = 3
CHUNK = 128
LANES = 128
SUBLANES = 8
SMALL_ROWS = 16
VMEM_LIMIT = 56 * 1024 * 1024

IG0, LFM0, LFF0 = 0, M_HEADS, 2 * M_HEADS

LOG2E = 1.4426950408889634
FOX_Q_SCALE = F_DH ** -0.5 * LOG2E


def _dot(a, b):
    return jnp.dot(a, b, preferred_element_type=F32)


def _dot_nt(a, b):
    return lax.dot_general(a, b, (((1,), (1,)), ((), ())), preferred_element_type=F32)


def _dot_tn(a, b):
    return lax.dot_general(a, b, (((0,), (0,)), ((), ())), preferred_element_type=F32)


def _split3(x):
    hi = x.astype(BF16)
    r1 = x - hi.astype(F32)
    mid = r1.astype(BF16)
    lo = (r1 - mid.astype(F32)).astype(BF16)
    return hi, mid, lo


def _dot_f32_lhs(x, mat01):
    hi, mid, lo = _split3(x)
    return _dot(hi, mat01) + _dot(mid, mat01) + _dot(lo, mat01)


def _dot_f32_rhs(mat01, x):
    hi, mid, lo = _split3(x)
    return _dot(mat01, hi) + _dot(mat01, mid) + _dot(mat01, lo)


def _log_sigmoid(x):
    return jnp.minimum(x, 0.0) - jnp.log1p(jnp.exp(-jnp.abs(x)))


def _tri01(n, kind):
    r = lax.broadcasted_iota(jnp.int32, (n, n), 0)
    c = lax.broadcasted_iota(jnp.int32, (n, n), 1)
    m = c <= r if kind == "lower" else r <= c
    return jnp.where(m, 1.0, 0.0).astype(BF16)


def _params(*sem):
    return pltpu.CompilerParams(dimension_semantics=sem, vmem_limit_bytes=VMEM_LIMIT)


def _ada_kernel(cp_ref, cs_ref, w_ref, b_ref, op_ref, os_ref):
    w = w_ref[...].astype(BF16)
    for c_ref, o_ref in ((cp_ref, op_ref), (cs_ref, os_ref)):
        c = c_ref[...]
        s = (c * jax.nn.sigmoid(c)).astype(BF16)
        o_ref[...] = _dot(s, w) + b_ref[...]


def _ada(c_p, c_s, w_ada, b_ada, tn=1024):
    d = c_p.shape[1]
    n = w_ada.shape[1]
    return pl.pallas_call(
        _ada_kernel,
        grid=(n // tn,),
        in_specs=[pl.BlockSpec(c_p.shape, lambda j: (0, 0)),
                  pl.BlockSpec(c_s.shape, lambda j: (0, 0)),
                  pl.BlockSpec((d, tn), lambda j: (0, j)),
                  pl.BlockSpec((1, tn), lambda j: (0, j))],
        out_specs=[pl.BlockSpec((c_p.shape[0], tn), lambda j: (0, j)),
                   pl.BlockSpec((c_s.shape[0], tn), lambda j: (0, j))],
        out_shape=[jax.ShapeDtypeStruct((c_p.shape[0], n), F32), jax.ShapeDtypeStruct((c_s.shape[0], n), F32)],
        compiler_params=_params("arbitrary"),
        name="ada",
    )(c_p, c_s, w_ada, b_ada.reshape(1, n))


MOD_SH1, MOD_SC1, MOD_G1, MOD_SH2, MOD_SC2, MOD_G2 = range(6)


PROJ_TN = 1024
_J_MIX, _J_GATE, _J_FQ, _J_FK, _J_FV, _J_END = 0, 2, 7, 8, 9, 10
_J_MO = _J_GATE + 4
_SPLIT_SIZES = (M_QK, M_QK, M_V, M_V, M_HEADS, M_HEADS, F_W, F_W, F_W, F_HEADS)
_SPLIT_OFFS = tuple(sum(_SPLIT_SIZES[:i]) for i in range(len(_SPLIT_SIZES) + 1))
_ROW_MO, _ROW_SMALL_M, _ROW_FQ, _ROW_SMALL_F, _ROW_GATES = (_SPLIT_OFFS[3], _SPLIT_OFFS[4], _SPLIT_OFFS[6],
                                                            _SPLIT_OFFS[9], _SPLIT_OFFS[10])
_TILE_RUNS = ((_J_MIX, 0), (_J_GATE, _ROW_GATES), (_J_MO, _ROW_MO), (_J_FQ, _ROW_FQ))


def _tile_row(jj):
    row = _TILE_RUNS[0][1] + PROJ_TN * (jj - _TILE_RUNS[0][0])
    for t0, r0 in _TILE_RUNS[1:]:
        row = jnp.where(jj >= t0, r0 + PROJ_TN * (jj - t0), row)
    return row


def _norm_kernel(x_ref, sc_ref, sh_ref, n1w_ref, wsm_m_ref, wsm_f_ref, bsm_ref, h_ref, sm_ref, smt_ref):
    x = x_ref[...]
    rstd = lax.rsqrt(jnp.mean(x * x, axis=-1, keepdims=True) + NORM_EPS)
    h = (x * rstd * n1w_ref[...]) * (1.0 + sc_ref[...]) + sh_ref[...]
    hb = h.astype(BF16)
    h_ref[...] = hb
    wsm = jnp.concatenate([wsm_m_ref[...], wsm_f_ref[...]], axis=0).astype(BF16)
    smt = _dot_nt(wsm, hb) + bsm_ref[...]
    row = lax.broadcasted_iota(jnp.int32, smt.shape, 0)
    smt = jnp.where(row < LFM0, smt, _log_sigmoid(smt))
    smt_ref[...] = smt
    pad = jnp.zeros((LANES - SMALL_ROWS, smt.shape[1]), F32)
    sm_ref[...] = jnp.concatenate([smt, pad], axis=0).T


def _proj_kernel(h_ref, hs_ref, w_ref, b_ref, *rest, kind, scale, emit_f32):
    *rest, wb_ref = rest
    nw_ref = None
    if kind == "headnorm":
        nw_ref, *rest = rest
    n_out = len(rest) // 2
    i = pl.program_id(1)

    @pl.when(i == 0)
    def _():
        wb_ref[...] = w_ref[...].astype(BF16)

    def project(rows_ref, outs):
        acc = _dot_nt(rows_ref[...], wb_ref[...]) + b_ref[...]
        if kind == "mix":
            lane = lax.broadcasted_iota(jnp.int32, (1, acc.shape[1]), 1)
            is_mk = jnp.logical_and(pl.program_id(0) == 0, lane >= M_QK)
            outs[0][...] = (acc * jnp.where(is_mk, M_DK ** -0.5, 1.0)).astype(BF16)
        elif kind == "sigmoid":
            outs[0][...] = jax.nn.sigmoid(acc).astype(BF16)
        elif kind == "headnorm":
            for h in range(acc.shape[1] // F_DH):
                sl = acc[:, h * F_DH:(h + 1) * F_DH]
                ms = jnp.mean(sl * sl, axis=-1, keepdims=True)
                y = sl * lax.rsqrt(ms + NORM_EPS) * nw_ref[...]
                if emit_f32:
                    outs[0][:, h * F_DH:(h + 1) * F_DH] = y
                outs[-1][:, h * F_DH:(h + 1) * F_DH] = (y * scale).astype(BF16)
        else:
            outs[0][...] = acc
            outs[1][...] = acc.astype(BF16)

    @pl.when(i == 0)
    def _():
        project(hs_ref, rest[n_out:])

    @pl.when(i > 0)
    def _():
        project(h_ref, rest[:n_out])


def _feature_rows_spec(rows, d, first_row):
    def index_map(*g):
        return pl.multiple_of(jnp.asarray(first_row(*g), jnp.int32), SUBLANES), 0

    return pl.BlockSpec((pl.Element(rows), pl.Element(d)), index_map)


def _mod_spec(mod, d, chunk, group_of):
    return pl.BlockSpec((None, mod.shape[1], d), lambda *g: (group_of(*g), 0, chunk))


def _norm(x2, mod, n1w, wts, tm):
    m, d = x2.shape
    tiles_per_group = (m // tm) // mod.shape[0]
    group_of = lambda i: i // tiles_per_group
    return pl.pallas_call(
        _norm_kernel,
        grid=(m // tm,),
        in_specs=[pl.BlockSpec((tm, d), lambda i: (i, 0)),
                  _mod_spec(mod, d, MOD_SC1, group_of),
                  _mod_spec(mod, d, MOD_SH1, group_of),
                  pl.BlockSpec((1, d), lambda i: (0, 0)),
                  _feature_rows_spec(SUBLANES, d, lambda i: _ROW_SMALL_M),
                  _feature_rows_spec(SUBLANES, d, lambda i: _ROW_SMALL_F),
                  pl.BlockSpec((SMALL_ROWS, 1), lambda i: (0, 0))],
        out_specs=[pl.BlockSpec((tm, d), lambda i: (i, 0)),
                   pl.BlockSpec((tm, LANES), lambda i: (i, 0)),
                   pl.BlockSpec((SMALL_ROWS, tm), lambda i: (0, i))],
        out_shape=[jax.ShapeDtypeStruct((m, d), BF16),
                   jax.ShapeDtypeStruct((m, LANES), F32),
                   jax.ShapeDtypeStruct((SMALL_ROWS, m), F32)],
        compiler_params=_params("arbitrary"),
        name="norm",
    )(x2, mod, mod, n1w, wts["w_in_t"], wts["w_in_t"], wts["bsm"])


def _proj(h, hs, wts, j0, j1, kind, tm, nw=None, scale=1.0, emit_f32=False):
    m, d = h.shape
    ms = hs.shape[0]
    n_row_tiles = m // tm
    tn = PROJ_TN
    width = (j1 - j0) * tn
    prompt_row = lambda i: jnp.maximum(i - 1, 0)
    in_specs = [pl.BlockSpec((tm, d), lambda j, i: (prompt_row(i), 0)),
                pl.BlockSpec((ms, d), lambda j, i: (0, 0)),
                _feature_rows_spec(tn, d, lambda j, i: _tile_row(j0 + j)),
                pl.BlockSpec((1, tn), lambda j, i: (0, j0 + j))]
    args = [h, hs, wts["w_in_t"], wts["bbig"]]
    if kind == "headnorm":
        in_specs.append(pl.BlockSpec((1, F_DH), lambda j, i: (0, 0)))
        args.append(nw)
    dtypes = ([F32] if emit_f32 or kind == "copy" else []) + [BF16]
    n_out = len(dtypes)
    out_shape = ([jax.ShapeDtypeStruct((m, width), t) for t in dtypes]
                 + [jax.ShapeDtypeStruct((ms, width), t) for t in dtypes])
    out_specs = ([pl.BlockSpec((tm, tn), lambda j, i: (prompt_row(i), j))] * n_out
                 + [pl.BlockSpec((ms, tn), lambda j, i: (0, j))] * n_out)
    outs = pl.pallas_call(
        functools.partial(_proj_kernel, kind=kind, scale=scale, emit_f32=emit_f32),
        grid=(j1 - j0, n_row_tiles + 1),
        in_specs=in_specs,
        out_specs=out_specs,
        out_shape=out_shape,
        scratch_shapes=[pltpu.VMEM((tn, d), BF16)],
        compiler_params=_params("arbitrary", "arbitrary"),
        name="proj_" + kind,
    )(*args)
    return outs[:n_out], outs[n_out:]


def _cumsum_kernel(sm_ref, fcol_ref, carry_ref):
    c = pl.program_id(1)
    tc = sm_ref.shape[0]

    @pl.when(c == 0)
    def _():
        carry_ref[...] = jnp.zeros_like(carry_ref)

    fc = _dot_f32_rhs(_tri01(tc, "lower"), sm_ref[...]) + carry_ref[0:1, :]
    fcol_ref[...] = fc
    carry_ref[...] = jnp.broadcast_to(fc[tc - 1:tc, :], carry_ref.shape)


def _cumsum(sm, batch, tc=1024):
    m = sm.shape[0]
    nt = (m // batch) // tc
    return pl.pallas_call(
        _cumsum_kernel,
        grid=(batch, nt),
        in_specs=[pl.BlockSpec((tc, LANES), lambda b, c: (b * nt + c, 0))],
        out_specs=pl.BlockSpec((tc, LANES), lambda b, c: (b * nt + c, 0)),
        out_shape=jax.ShapeDtypeStruct((m, LANES), F32),
        scratch_shapes=[pltpu.VMEM((SUBLANES, LANES), F32)],
        compiler_params=_params("arbitrary", "arbitrary"),
        name="cumsum",
    )(sm)


def _mlstm_kernel(*refs):
    _mlstm_chunk(pl.program_id(1), pl.num_programs(1) - 1, range(M_HEADS), *refs)


def _mlstm_chunk(c, last_c, heads, q_ref, k_ref, v_ref, sm_ref, smt_ref, og_ref, hw_ref,
                 ym_ref, cout_ref, nm_ref, c_ref, n_ref, m_ref):
    L = q_ref.shape[0]

    @pl.when(c == 0)
    def _():
        for h in heads:
            c_ref[h] = jnp.zeros((M_DV, M_DK), F32)
            n_ref[h:h + 1, :] = jnp.zeros((1, M_DK), F32)
            m_ref[h:h + 1, :] = jnp.zeros((1, LANES), F32)

    r = lax.broadcasted_iota(jnp.int32, (L, L), 0)
    s_idx = lax.broadcasted_iota(jnp.int32, (L, L), 1)
    causal = s_idx <= r
    sm = sm_ref[...]
    smt = smt_ref[...]
    bcol_all = _dot_f32_rhs(_tri01(L, "lower"), sm)
    brow_all = _dot_f32_lhs(smt, _tri01(L, "upper"))

    for h in heads:
        ig_col = sm[:, IG0 + h:IG0 + h + 1]
        b_col = bcol_all[:, LFM0 + h:LFM0 + h + 1]
        ig_row = smt[IG0 + h:IG0 + h + 1, :]
        b_row = brow_all[LFM0 + h:LFM0 + h + 1, :]
        m0 = m_ref[h:h + 1, 0:1]
        a_row = ig_row - b_row
        cm_col = jnp.max(jnp.where(causal, a_row, -jnp.inf), axis=1, keepdims=True)
        mx_col = jnp.maximum(m0, cm_col)
        m_col = b_col + mx_col
        dmat = jnp.exp(jnp.where(causal, a_row - mx_col, -jnp.inf))
        inter = jnp.exp(m0 - mx_col)
        q = q_ref[:, h * M_DK:(h + 1) * M_DK]
        k = k_ref[:, h * M_DK:(h + 1) * M_DK]
        v = v_ref[:, h * M_DV:(h + 1) * M_DV]
        c0 = c_ref[h]
        n0 = n_ref[h:h + 1, :]
        s = _dot_nt(q, k) * dmat
        num = _dot(s.astype(BF16), v) + inter * _dot_nt(q, c0.astype(BF16))
        den = jnp.sum(s, axis=1, keepdims=True) + inter * jnp.sum(q.astype(F32) * n0, axis=1, keepdims=True)
        hh = num / jnp.maximum(jnp.abs(den), jnp.exp(-m_col))
        hn = hh * lax.rsqrt(jnp.mean(hh * hh, axis=1, keepdims=True) + NORM_EPS) * hw_ref[:, h * M_DV:(h + 1) * M_DV]
        ym_ref[:, h * M_DV:(h + 1) * M_DV] = (og_ref[:, h * M_DV:(h + 1) * M_DV] * hn).astype(BF16)
        m_new = m_col[L - 1:L, :]
        b_last = b_col[L - 1:L, :]
        w_end = jnp.exp(ig_col + b_last - b_col - m_new)
        decay = jnp.exp(b_last + m0 - m_new)
        vw = (w_end * v.astype(F32)).astype(BF16)
        c_ref[h] = decay * c0 + _dot_tn(vw, k)
        n_ref[h:h + 1, :] = decay * n0 + jnp.sum(w_end * k.astype(F32), axis=0, keepdims=True)
        m_ref[h:h + 1, :] = jnp.broadcast_to(m_new, (1, LANES))

    @pl.when(c == last_c)
    def _():
        for h in heads:
            cout_ref[h] = c_ref[h]
            nm_ref[h:h + 1, :] = n_ref[h:h + 1, :]
            nm_ref[M_HEADS + h:M_HEADS + h + 1, :] = m_ref[h:h + 1, :]


def _mlstm_prompt(qkv, sm, smt, sig, hw, batch):
    m = qkv.shape[0]
    og_block = sig.shape[1] // M_V - 1
    nc = (m // batch) // CHUNK
    row = lambda b, c: (b * nc + c, 0)
    return pl.pallas_call(
        _mlstm_kernel,
        grid=(batch, nc),
        in_specs=[pl.BlockSpec((CHUNK, M_QK), row),
                  pl.BlockSpec((CHUNK, M_QK), lambda b, c: (b * nc + c, 1)),
                  pl.BlockSpec((CHUNK, M_V), lambda b, c: (b * nc + c, 2 * M_QK // M_V)),
                  pl.BlockSpec((CHUNK, LANES), row),
                  pl.BlockSpec((SMALL_ROWS, CHUNK), lambda b, c: (0, b * nc + c)),
                  pl.BlockSpec((CHUNK, M_V), lambda b, c: (b * nc + c, og_block)),
                  pl.BlockSpec((1, M_V), lambda b, c: (0, 0))],
        out_specs=[pl.BlockSpec((CHUNK, M_V), row),
                   pl.BlockSpec((None, M_HEADS, M_DV, M_DK), lambda b, c: (b, 0, 0, 0)),
                   pl.BlockSpec((None, 2 * M_HEADS, M_DK), lambda b, c: (b, 0, 0))],
        out_shape=[jax.ShapeDtypeStruct((m, M_V), BF16),
                   jax.ShapeDtypeStruct((batch, M_HEADS, M_DV, M_DK), F32),
                   jax.ShapeDtypeStruct((batch, 2 * M_HEADS, M_DK), F32)],
        scratch_shapes=[pltpu.VMEM((M_HEADS, M_DV, M_DK), F32),
                        pltpu.VMEM((SUBLANES, M_DK), F32),
                        pltpu.VMEM((SUBLANES, LANES), F32)],
        compiler_params=_params("arbitrary", "arbitrary"),
        name="mlstm",
    )(qkv, qkv, qkv, sm, smt, sig, hw)


FOX_HEADS_PER_STEP = 4
FOX_ROW_CHUNK = 32


def _forget_columns(f, first):
    hi, mid, lo = _split3(f)
    lane = lax.broadcasted_iota(jnp.int32, (f.shape[0], LANES), 1)
    ones = jnp.where(jnp.abs(lane - (4 - first)) <= 1, 1.0, 0.0)
    ext = jnp.where(lane == first, hi.astype(F32),
                    jnp.where(lane == first + 1, mid.astype(F32),
                              jnp.where(lane == first + 2, lo.astype(F32), ones)))
    return ext.astype(BF16)


def _fox_kernel(q_ref, k_ref, v_ref, fcol_ref, o_ref, qaug_ref, kaug_ref, s_ref, p_ref, m_ref, l_ref, acc_ref):
    hp = pl.program_id(1)
    qi = pl.program_id(2)
    tq = q_ref.shape[0]
    t = k_ref.shape[0]
    nh = FOX_HEADS_PER_STEP

    def forget_sum(fc, h2):
        lane = lax.broadcasted_iota(jnp.int32, fc.shape, 1)
        sel = jnp.where(lane == LFF0 + nh * hp + h2, fc, 0.0)
        return jnp.sum(sel, axis=1, keepdims=True) * LOG2E

    @pl.when(qi == 0)
    def _():
        fc = fcol_ref[...]
        for h2 in range(nh):
            kaug_ref[h2, :, 0:F_DH] = k_ref[:, h2 * F_DH:(h2 + 1) * F_DH]
            kaug_ref[h2, :, F_DH:2 * F_DH] = _forget_columns(-forget_sum(fc, h2), 3)

    fq = fcol_ref[pl.ds(pl.multiple_of(qi * tq, tq), tq), :]
    for h2 in range(nh):
        qaug_ref[h2, :, 0:F_DH] = q_ref[:, h2 * F_DH:(h2 + 1) * F_DH]
        qaug_ref[h2, :, F_DH:2 * F_DH] = _forget_columns(forget_sum(fq, h2), 0)
    m_ref[...] = jnp.full_like(m_ref, -jnp.inf)
    l_ref[...] = jnp.zeros_like(l_ref)
    acc_ref[...] = jnp.zeros_like(acc_ref)
    rc = FOX_ROW_CHUNK
    n_lane_tiles = tq // LANES

    def softmax_rows(h2, c, masked):
        rows = slice(c * rc, (c + 1) * rc)
        s = s_ref[h2, rows, :]
        if masked:
            r = lax.broadcasted_iota(jnp.int32, (rc, tq), 0) + c * rc
            col = lax.broadcasted_iota(jnp.int32, (rc, tq), 1)
            s = jnp.where(r >= col, s, -jnp.inf)
        tiles = [s[:, k * LANES:(k + 1) * LANES] for k in range(n_lane_tiles)]
        m_old = m_ref[h2, rows, :]
        mx = functools.reduce(jnp.maximum, tiles)
        m_new = jnp.maximum(m_old, jnp.max(mx, axis=1, keepdims=True))
        alpha = jnp.exp2(m_old - m_new)
        ps = [jnp.exp2(tile - m_new) for tile in tiles]
        row_sum = jnp.sum(functools.reduce(jnp.add, ps), axis=1, keepdims=True)
        l_ref[h2, rows, :] = alpha * l_ref[h2, rows, :] + row_sum
        for k in range(n_lane_tiles):
            p_ref[h2, rows, k * LANES:(k + 1) * LANES] = ps[k].astype(BF16)
        acc_ref[h2, rows, :] = alpha * acc_ref[h2, rows, :]
        m_ref[h2, rows, :] = m_new

    def block(kj, masked):
        off = pl.multiple_of(kj * tq, tq)
        for h2 in range(nh):
            s_ref[h2] = _dot_nt(qaug_ref[h2], kaug_ref[h2, pl.ds(off, tq), :])
        for h2 in range(nh):
            for c in range(tq // rc):
                softmax_rows(h2, c, masked)
            acc_ref[h2] += _dot(p_ref[h2], v_ref[pl.ds(off, tq), h2 * F_DH:(h2 + 1) * F_DH])

    def body(kj, carry):
        block(kj, False)
        return carry

    lax.fori_loop(0, qi, body, 0)
    block(qi, True)
    for h2 in range(nh):
        o_ref[:, h2 * F_DH:(h2 + 1) * F_DH] = (acc_ref[h2] / l_ref[h2]).astype(BF16)


def _fox_prompt(qf, kfb, vfb, fcol, batch, tq=512):
    m = qf.shape[0]
    t = m // batch
    nq = t // tq
    nh = FOX_HEADS_PER_STEP
    wide = nh * F_DH
    return pl.pallas_call(
        _fox_kernel,
        grid=(batch, F_HEADS // nh, nq),
        in_specs=[pl.BlockSpec((tq, wide), lambda b, h, i: (b * nq + i, h)),
                  pl.BlockSpec((t, wide), lambda b, h, i: (b, h)),
                  pl.BlockSpec((t, wide), lambda b, h, i: (b, h)),
                  pl.BlockSpec((t, LANES), lambda b, h, i: (b, 0))],
        out_specs=pl.BlockSpec((tq, wide), lambda b, h, i: (b * nq + i, h)),
        out_shape=jax.ShapeDtypeStruct((m, F_W), BF16),
        scratch_shapes=[pltpu.VMEM((nh, tq, 2 * F_DH), BF16),
                        pltpu.VMEM((nh, t, 2 * F_DH), BF16),
                        pltpu.VMEM((nh, tq, tq), F32),
                        pltpu.VMEM((nh, tq, tq), BF16),
                        pltpu.VMEM((nh, tq, LANES), F32),
                        pltpu.VMEM((nh, tq, LANES), F32),
                        pltpu.VMEM((nh, tq, F_DH), F32)],
        compiler_params=_params("arbitrary", "arbitrary", "arbitrary"),
        name="fox",
    )(qf, kfb, vfb, fcol)


MSTEP_ROWS = 8


def _mstep_kernel(q_ref, k_ref, v_ref, sm_ref, og_ref, hw_ref, c0_ref, n0_ref, m0_ref,
                  ym_ref, cout_ref, nout_ref, mout_ref, cq_ref):
    rows = q_ref.shape[0]
    sm = sm_ref[...]
    ig = sm[:, IG0:IG0 + M_HEADS]
    lf = sm[:, LFM0:LFM0 + M_HEADS]
    m0 = m0_ref[...]
    m_new = lf + jnp.maximum(m0, ig - lf)
    w_end = jnp.exp(ig - m_new)
    decay = jnp.exp(lf + m0 - m_new)
    mout_ref[...] = m_new
    row_id = lax.broadcasted_iota(jnp.int32, (SUBLANES, M_DV), 0)

    for h in range(M_HEADS):
        qh = q_ref[:, h * M_DK:(h + 1) * M_DK]
        kh = k_ref[:, h * M_DK:(h + 1) * M_DK]
        vh = v_ref[:, h * M_DV:(h + 1) * M_DV]
        qf = qh.astype(F32)
        kf = kh.astype(F32)
        vf = vh.astype(F32)
        wh = w_end[:, h:h + 1]
        dh = decay[:, h:h + 1]
        for r in range(rows):
            c0 = c0_ref[r, h]
            q8 = jnp.broadcast_to(qh[r:r + 1, :], (SUBLANES, M_DK))
            cq_ref[r:r + 1, :] = _dot_nt(q8, c0.astype(BF16))[0:1, :]
            vw8 = jnp.where(row_id == 0, jnp.broadcast_to(wh[r:r + 1, :] * vf[r:r + 1, :], (SUBLANES, M_DV)), 0.0)
            k8 = jnp.broadcast_to(kh[r:r + 1, :], (SUBLANES, M_DK))
            cout_ref[r, h] = dh[r:r + 1, :] * c0 + _dot_tn(vw8.astype(BF16), k8)
        n0 = n0_ref[:, h, :]
        s = jnp.sum(qf * kf, axis=1, keepdims=True) * wh
        num = s * vf + dh * cq_ref[...]
        den = s + dh * jnp.sum(n0 * qf, axis=1, keepdims=True)
        hh = num / jnp.maximum(jnp.abs(den), jnp.exp(-m_new[:, h:h + 1]))
        hn = hh * lax.rsqrt(jnp.mean(hh * hh, axis=1, keepdims=True) + NORM_EPS) * hw_ref[:, h * M_DV:(h + 1) * M_DV]
        ym_ref[:, h * M_DV:(h + 1) * M_DV] = (og_ref[:, h * M_DV:(h + 1) * M_DV] * hn).astype(BF16)
        nout_ref[:, h, :] = dh * n0 + wh * kf


def _mlstm_step(qkv, sm, sig, hw, c0, n0, m0):
    db = qkv.shape[0]
    og_block = sig.shape[1] // M_V - 1
    rows = MSTEP_ROWS
    row = lambda i: (i, 0)
    return pl.pallas_call(
        _mstep_kernel,
        grid=(db // rows,),
        in_specs=[pl.BlockSpec((rows, M_QK), row),
                  pl.BlockSpec((rows, M_QK), lambda i: (i, 1)),
                  pl.BlockSpec((rows, M_V), lambda i: (i, 2 * M_QK // M_V)),
                  pl.BlockSpec((rows, LANES), row),
                  pl.BlockSpec((rows, M_V), lambda i: (i, og_block)),
                  pl.BlockSpec((1, M_V), lambda i: (0, 0)),
                  pl.BlockSpec((rows, M_HEADS, M_DV, M_DK), lambda i: (i, 0, 0, 0)),
                  pl.BlockSpec((rows, M_HEADS, M_DK), lambda i: (i, 0, 0)),
                  pl.BlockSpec((rows, M_HEADS), row)],
        out_specs=[pl.BlockSpec((rows, M_V), row),
                   pl.BlockSpec((rows, M_HEADS, M_DV, M_DK), lambda i: (i, 0, 0, 0)),
                   pl.BlockSpec((rows, M_HEADS, M_DK), lambda i: (i, 0, 0)),
                   pl.BlockSpec((rows, M_HEADS), row)],
        out_shape=[jax.ShapeDtypeStruct((db, M_V), BF16),
                   jax.ShapeDtypeStruct(c0.shape, F32),
                   jax.ShapeDtypeStruct(n0.shape, F32),
                   jax.ShapeDtypeStruct(m0.shape, F32)],
        scratch_shapes=[pltpu.VMEM((rows, M_DV), F32)],
        compiler_params=_params("arbitrary"),
        name="mstep",
    )(qkv, qkv, qkv, sm, sig, hw, c0, n0, m0)


N_MLSTM_IN = 7
N_MLSTM_OUT = 3
N_DECODE_SCRATCH = 4


def _decode_kernel(*refs, n_pages, mlstm_steps_per_chunk, mlstm_chunks):
    n_dec_in = 3 * n_pages + 4
    n_ml = N_MLSTM_IN if mlstm_steps_per_chunk else 0
    dec_in = refs[1:1 + n_dec_in]
    ml_in = refs[1 + n_dec_in:1 + n_dec_in + n_ml]
    outs = refs[1 + n_dec_in + n_ml:]
    o_ref, outs = outs[0], outs[1:]
    if mlstm_steps_per_chunk:
        ml_out, outs = outs[:N_MLSTM_OUT], outs[N_MLSTM_OUT:]
        ml_scratch = outs[N_DECODE_SCRATCH:]
        step = pl.program_id(0)
        chunk = (step // mlstm_steps_per_chunk) % mlstm_chunks
        heads_per_step = M_HEADS // mlstm_steps_per_chunk
        for g in range(mlstm_steps_per_chunk):
            @pl.when(step % mlstm_steps_per_chunk == g)
            def _(g=g):
                _mlstm_chunk(chunk, mlstm_chunks - 1, range(g * heads_per_step, (g + 1) * heads_per_step),
                             *ml_in, *ml_out, *ml_scratch)
    _decode_sequence(dec_in[:n_pages], dec_in[n_pages:2 * n_pages], dec_in[2 * n_pages:3 * n_pages],
                     *dec_in[3 * n_pages:], o_ref, *outs[:N_DECODE_SCRATCH])


def _decode_sequence(k_refs, v_refs, lf_refs, q_ref, kn_ref, vn_ref, sm_ref, o_ref,
                     s_ref, xn_ref, widen_ref, spread_ref):
    n_pages = len(k_refs)
    page = xn_ref.shape[1]
    pw = page * F_HEADS

    for p in range(n_pages):
        xn_ref[p * F_HEADS:(p + 1) * F_HEADS, :] = lf_refs[p][...]
    xn = xn_ref[...] * LOG2E

    @pl.when(pl.program_id(0) == 0)
    def _():
        t_id = lax.broadcasted_iota(jnp.int32, (page, pw), 0)
        l_id = lax.broadcasted_iota(jnp.int32, (page, pw), 1)
        widen_ref[...] = jnp.where((l_id // F_HEADS) == t_id, 1.0, 0.0).astype(BF16)
        r = lax.broadcasted_iota(jnp.int32, (LANES, pw), 0)
        c = lax.broadcasted_iota(jnp.int32, (LANES, pw), 1)
        spread_ref[...] = jnp.where(r == LFF0 + (c & (F_HEADS - 1)), 1.0, 0.0).astype(BF16)

    wide = _dot_f32_lhs(xn, widen_ref[...])
    r_id = lax.broadcasted_iota(jnp.int32, wide.shape, 0)
    c_id = lax.broadcasted_iota(jnp.int32, wide.shape, 1)
    own = jnp.where((r_id & (F_HEADS - 1)) == (c_id & (F_HEADS - 1)), wide, 0.0)
    g_p = lax.broadcasted_iota(jnp.int32, (n_pages, n_pages * F_HEADS), 0)
    g_r = lax.broadcasted_iota(jnp.int32, (n_pages, n_pages * F_HEADS), 1)
    gather = jnp.where((g_r // F_HEADS) == g_p, 1.0, 0.0).astype(BF16)
    x = _dot_f32_rhs(gather, own)
    lane = lax.broadcasted_iota(jnp.int32, x.shape, 1)
    y, z = x, x
    step = F_HEADS
    while step < pw:
        y = y + jnp.where(lane < pw - step, pltpu.roll(y, pw - step, axis=1), 0.0)
        z = z + pltpu.roll(z, step, axis=1)
        step *= 2
    after = y - x

    run = _dot_f32_lhs(jnp.broadcast_to(sm_ref[...] * LOG2E, (SUBLANES, LANES)), spread_ref[...])[0:1, :]

    q = q_ref[...]
    sub = lax.broadcasted_iota(jnp.int32, (F_HEADS, pw), 0)
    lane_h = lax.broadcasted_iota(jnp.int32, (F_HEADS, pw), 1) & (F_HEADS - 1)
    valid = lane_h == sub
    s_self = jnp.sum(q.astype(F32) * kn_ref[...], axis=1, keepdims=True)
    m_run = s_self
    for p in reversed(range(n_pages)):
        kb = k_refs[p][...].astype(BF16)
        st = _dot_nt(q, kb) + (run + after[p:p + 1, :])
        st = jnp.where(valid, st, -jnp.inf)
        s_ref[p] = st
        m_run = jnp.maximum(m_run, jnp.max(st, axis=1, keepdims=True))
        run = run + z[p:p + 1, :]
    w_self = jnp.exp2(s_self - m_run)
    l = w_self
    acc = w_self * vn_ref[...]
    for p in range(n_pages):
        pe = jnp.exp2(s_ref[p] - m_run)
        l = l + jnp.sum(pe, axis=1, keepdims=True)
        acc = acc + _dot(pe.astype(BF16), v_refs[p][...].astype(BF16))
    o_ref[...] = (acc / l).astype(BF16)


def _mlstm_fits_decode(db, batch, seq_len):
    chunk_steps = batch * (seq_len // CHUNK)
    return db % chunk_steps == 0 and db // chunk_steps in (1, 2, 4)


def _fox_decode(page_table, ck, cv, clf, layer, q3, k3, v3, sm3, mlstm=None):
    db, n_pages = page_table.shape
    pw = ck.shape[2]
    page = clf.shape[3]
    seq = lambda b, pt: (b, 0, 0)

    def pg(p):
        return lambda b, pt: (layer, pt[b, p], 0, 0)

    in_specs = ([pl.BlockSpec((None, None, pw, F_DH), pg(p)) for p in range(n_pages)]
                + [pl.BlockSpec((None, None, pw, F_DH), pg(p)) for p in range(n_pages)]
                + [pl.BlockSpec((None, None, F_HEADS, page), pg(p)) for p in range(n_pages)]
                + [pl.BlockSpec((None, F_HEADS, F_DH), seq),
                   pl.BlockSpec((None, F_HEADS, F_DH), seq),
                   pl.BlockSpec((None, F_HEADS, F_DH), seq),
                   pl.BlockSpec((None, 1, LANES), seq)])
    args = [page_table, *([ck] * n_pages), *([cv] * n_pages), *([clf] * n_pages), q3, k3, v3, sm3]
    out_specs = [pl.BlockSpec((None, F_HEADS, F_DH), seq)]
    out_shape = [jax.ShapeDtypeStruct((db, F_HEADS, F_DH), BF16)]
    scratch = [pltpu.VMEM((n_pages, F_HEADS, pw), F32), pltpu.VMEM((n_pages * F_HEADS, page), F32),
               pltpu.VMEM((page, pw), BF16), pltpu.VMEM((LANES, pw), BF16)]
    per_chunk, nc = 0, 0
    if mlstm is not None:
        qkv, sm, smt, sig, hw, batch = mlstm
        m = qkv.shape[0]
        nc = (m // batch) // CHUNK
        per_chunk = db // (batch * nc)
        og_block = sig.shape[1] // M_V - 1
        row = lambda s, pt: (s // per_chunk, 0)
        in_specs += [pl.BlockSpec((CHUNK, M_QK), row),
                     pl.BlockSpec((CHUNK, M_QK), lambda s, pt: (s // per_chunk, 1)),
                     pl.BlockSpec((CHUNK, M_V), lambda s, pt: (s // per_chunk, 2 * M_QK // M_V)),
                     pl.BlockSpec((CHUNK, LANES), row),
                     pl.BlockSpec((SMALL_ROWS, CHUNK), lambda s, pt: (0, s // per_chunk)),
                     pl.BlockSpec((CHUNK, M_V), lambda s, pt: (s // per_chunk, og_block)),
                     pl.BlockSpec((1, M_V), lambda s, pt: (0, 0))]
        args += [qkv, qkv, qkv, sm, smt, sig, hw]
        seq_of = lambda s, pt: s // (per_chunk * nc)
        out_specs += [pl.BlockSpec((CHUNK, M_V), row),
                      pl.BlockSpec((None, M_HEADS, M_DV, M_DK), lambda s, pt: (seq_of(s, pt), 0, 0, 0)),
                      pl.BlockSpec((None, 2 * M_HEADS, M_DK), lambda s, pt: (seq_of(s, pt), 0, 0))]
        out_shape += [jax.ShapeDtypeStruct((m, M_V), BF16),
                      jax.ShapeDtypeStruct((batch, M_HEADS, M_DV, M_DK), F32),
                      jax.ShapeDtypeStruct((batch, 2 * M_HEADS, M_DK), F32)]
        scratch += [pltpu.VMEM((M_HEADS, M_DV, M_DK), F32),
                    pltpu.VMEM((SUBLANES, M_DK), F32),
                    pltpu.VMEM((SUBLANES, LANES), F32)]
    grid_spec = pltpu.PrefetchScalarGridSpec(
        num_scalar_prefetch=1,
        grid=(db,),
        in_specs=in_specs,
        out_specs=out_specs,
        scratch_shapes=scratch,
    )
    outs = pl.pallas_call(
        functools.partial(_decode_kernel, n_pages=n_pages, mlstm_steps_per_chunk=per_chunk, mlstm_chunks=nc),
        grid_spec=grid_spec,
        out_shape=out_shape,
        compiler_params=_params("arbitrary"),
        name="decode",
    )(*args)
    return outs if mlstm is not None else outs[0]


def _merge_kernel(ym_ref, yf_ref, sga_ref, sgb_ref, x_ref, g1_ref, sc_ref, sh_ref, n2w_ref,
                  wa_ref, wb_ref, wo_ref, x1_ref, h2_ref):
    merged = sga_ref[...] * _dot(ym_ref[...], wa_ref[...]) + sgb_ref[...] * _dot(yf_ref[...], wb_ref[...])
    x1 = x_ref[...] + g1_ref[...] * _dot(merged.astype(BF16), wo_ref[...])
    x1_ref[...] = x1
    rstd = lax.rsqrt(jnp.mean(x1 * x1, axis=-1, keepdims=True) + NORM_EPS)
    h2_ref[...] = ((x1 * rstd * n2w_ref[...]) * (1.0 + sc_ref[...]) + sh_ref[...]).astype(BF16)


def _merge(ym, yf, sig, x2, mod, n2w, wts, tm):
    m, d = x2.shape
    tiles_per_group = (m // tm) // mod.shape[0]
    group_of = lambda i: i // tiles_per_group
    row = lambda i: (i, 0)
    const = lambda i: (0, 0)
    resident = dict(pipeline_mode=pl.Buffered(1))
    return pl.pallas_call(
        _merge_kernel,
        grid=(m // tm,),
        in_specs=[pl.BlockSpec((tm, M_V), row),
                  pl.BlockSpec((tm, F_W), row),
                  pl.BlockSpec((tm, d), row),
                  pl.BlockSpec((tm, d), lambda i: (i, 1)),
                  pl.BlockSpec((tm, d), row),
                  _mod_spec(mod, d, MOD_G1, group_of),
                  _mod_spec(mod, d, MOD_SC2, group_of),
                  _mod_spec(mod, d, MOD_SH2, group_of),
                  pl.BlockSpec((1, d), const),
                  pl.BlockSpec((M_V, d), const, **resident),
                  pl.BlockSpec((F_W, d), const, **resident),
                  pl.BlockSpec((d, d), const, **resident)],
        out_specs=[pl.BlockSpec((tm, d), row), pl.BlockSpec((tm, d), row)],
        out_shape=[jax.ShapeDtypeStruct((m, d), F32), jax.ShapeDtypeStruct((m, d), BF16)],
        compiler_params=_params("arbitrary"),
        name="merge",
    )(ym, yf, sig, sig, x2, mod, mod, mod, n2w, wts["wa"], wts["wb"], wts["wo"])


FFN_TF = 512


def _ffn_kernel(h2_ref, x1_ref, g2_ref, p_ref, wa_ref, wg_ref, cw_ref, cb_ref, w2_ref,
                y_ref, a_out_ref, *rest, seq):
    f = pl.program_id(2)
    tm = h2_ref.shape[0]
    carry_ref = rest[-1]

    @pl.when(f == 0)
    def _():
        y_ref[...] = jnp.zeros_like(y_ref)

    if seq:
        @pl.when(pl.program_id(1) == 0)
        def _():
            carry_ref[f] = p_ref[...]
        wa, wg, w2 = wa_ref[...], wg_ref[...], w2_ref[...]
    else:
        wa, wg, w2 = wa_ref[...].astype(BF16), wg_ref[...].astype(BF16), w2_ref[...].astype(BF16)
        wa_out_ref, wg_out_ref, w2_out_ref = rest[:3]
        wa_out_ref[...] = wa
        wg_out_ref[...] = wg
        w2_out_ref[...] = w2

    h2 = h2_ref[...]
    a = _dot(h2, wa)
    g = _dot(h2, wg)
    if seq:
        prev = carry_ref[f]
        p2, p1 = prev[SUBLANES - 2:SUBLANES - 1, :], prev[SUBLANES - 1:SUBLANES, :]
        r = lax.broadcasted_iota(jnp.int32, a.shape, 0)
        a1 = jnp.where(r == 0, p1, pltpu.roll(a, 1, axis=0))
        a2 = jnp.where(r == 0, p2, jnp.where(r == 1, p1, pltpu.roll(a, 2, axis=0)))
        carry_ref[f] = a[tm - SUBLANES:tm, :]
        a_out_ref[...] = a[tm - SUBLANES:tm, :]
    else:
        a2, a1 = p_ref[0], p_ref[1]
        a_out_ref[...] = a
    ac = cw_ref[0:1, :] * a2 + cw_ref[1:2, :] * a1 + cw_ref[2:3, :] * a + cb_ref[...]
    u = (jax.nn.gelu(ac, approximate=True) * g).astype(BF16)
    y_ref[...] += _dot(u, w2)

    @pl.when(f == pl.num_programs(2) - 1)
    def _():
        y_ref[...] = x1_ref[...] + g2_ref[...] * y_ref[...]


def _ffn(h2, x1, mod, prev, wts, tm, seq):
    m, d = x1.shape
    groups = mod.shape[0]
    tiles = (m // tm) // groups
    dff = wts["w2"].shape[0]
    tf = FFN_TF
    nf = dff // tf
    row = lambda b, i, f: (b * tiles + i, 0)
    up_tile = pl.BlockSpec((None, d, tf), lambda b, i, f: (f, 0, 0))
    down_tile = pl.BlockSpec((tf, d), lambda b, i, f: (f, 0))
    out_specs = [pl.BlockSpec((tm, d), row)]
    out_shape = [jax.ShapeDtypeStruct((m, d), F32)]
    if seq:
        prev_spec = pl.BlockSpec((None, SUBLANES, tf), lambda b, i, f: (b, 0, f))
        out_specs.append(pl.BlockSpec((None, None, SUBLANES, tf), lambda b, i, f: (b, i, 0, f)))
        out_shape.append(jax.ShapeDtypeStruct((groups, tiles, SUBLANES, dff), F32))
        w_specs = [up_tile, up_tile]
        w_args = [wts["wa_t"], wts["wg_t"]]
    else:
        assert groups == 1 and tiles == 1, "the bf16 weight tiles are written once"
        prev_spec = pl.BlockSpec((CONV_W - 1, tm, tf), lambda b, i, f: (0, i, f))
        out_specs += [pl.BlockSpec((tm, tf), lambda b, i, f: (i, f)), up_tile, up_tile, down_tile]
        out_shape += [jax.ShapeDtypeStruct((m, dff), F32),
                      jax.ShapeDtypeStruct((nf, d, tf), BF16),
                      jax.ShapeDtypeStruct((nf, d, tf), BF16),
                      jax.ShapeDtypeStruct((dff, d), BF16)]
        w_specs = [pl.BlockSpec((d, tf), lambda b, i, f: (0, f)),
                   pl.BlockSpec((d, tf), lambda b, i, f: (0, f + nf))]
        w_args = [wts["w1"], wts["w1"]]
    return pl.pallas_call(
        functools.partial(_ffn_kernel, seq=seq),
        grid=(groups, tiles, nf),
        in_specs=[pl.BlockSpec((tm, d), row, pipeline_mode=pl.Buffered(1)),
                  pl.BlockSpec((tm, d), row),
                  _mod_spec(mod, d, MOD_G2, lambda b, i, f: b),
                  prev_spec,
                  *w_specs,
                  pl.BlockSpec((CONV_W, tf), lambda b, i, f: (0, f)),
                  pl.BlockSpec((1, tf), lambda b, i, f: (0, f)),
                  down_tile],
        out_specs=out_specs,
        out_shape=out_shape,
        scratch_shapes=[pltpu.VMEM((nf, SUBLANES, tf), F32)],
        compiler_params=_params("arbitrary", "arbitrary", "arbitrary"),
        name="ffn_seq" if seq else "ffn_row",
    )(h2, x1, mod, prev, *w_args, wts["cw"], wts["cb"], wts["w2"])


def _prep_weights(w_in, b_in, f_qnorm_w, f_knorm_w, w_proj_a, w_proj_b, w_out, w_ffn_in, conv_w, conv_b, w_ffn_out):
    o = _SPLIT_OFFS
    bbig = jnp.concatenate([b_in[:o[3]], b_in[o[10]:], b_in[o[3]:o[4]], b_in[o[6]:o[9]]])
    bsm = jnp.concatenate([b_in[o[4]:o[6]], b_in[o[9]:o[10]]])
    return {
        "w_in_t": w_in.T,
        "bbig": bbig.reshape(1, -1),
        "bsm": bsm.reshape(SMALL_ROWS, 1),
        "qnw": f_qnorm_w.reshape(1, F_DH),
        "knw": f_knorm_w.reshape(1, F_DH),
        "wa": w_proj_a.astype(BF16),
        "wb": w_proj_b.astype(BF16),
        "wo": w_out.astype(BF16),
        "w1": w_ffn_in,
        "cw": conv_w,
        "cb": conv_b.reshape(1, -1),
        "w2": w_ffn_out,
    }


def kernel(x_prompt, x_sample, c_prompt, c_sample, cache_k, cache_v, cache_logf, page_table, state_C, state_n, state_m, state_conv, w_ada, b_ada, norm1_w, w_in, b_in, m_hnorm_w, f_qnorm_w, f_knorm_w, w_proj_a, w_proj_b, w_out, norm2_w, w_ffn_in, conv_w, conv_b, w_ffn_out):
    depth = w_ada.shape[0]
    assert depth == 1, "single-layer step"
    batch, seq_len, d = x_prompt.shape
    db = x_sample.shape[0]
    assert x_sample.shape[1] == 1
    dff = w_ffn_out.shape[1]
    layer = 0

    wts = _prep_weights(w_in[layer], b_in[layer], f_qnorm_w[layer], f_knorm_w[layer], w_proj_a[layer],
                        w_proj_b[layer], w_out[layer], w_ffn_in[layer], conv_w[layer], conv_b[layer],
                        w_ffn_out[layer])
    n1w = norm1_w[layer].reshape(1, d)
    n2w = norm2_w[layer].reshape(1, d)
    hw = m_hnorm_w[layer].reshape(1, M_V)

    mod_p, mod_s = _ada(c_prompt, c_sample, w_ada[layer], b_ada[layer])
    mod_p = mod_p.reshape(batch, 1, 6 * d)
    mod_s = mod_s.reshape(1, db, 6 * d)

    tm_p = min(1024, seq_len)
    xp = x_prompt.reshape(batch * seq_len, d)
    xs = x_sample.reshape(db, d)
    h, sm, smt = _norm(xp, mod_p, n1w, wts, tm_p)
    h_s, sm_s, _ = _norm(xs, mod_s, n1w, wts, db)
    (qkv,), (qkv_s,) = _proj(h, h_s, wts, _J_MIX, _J_GATE, "mix", tm_p)
    (sig,), (sig_s,) = _proj(h, h_s, wts, _J_GATE, _J_FQ, "sigmoid", tm_p)
    (qf,), (qf_s,) = _proj(h, h_s, wts, _J_FQ, _J_FK, "headnorm", tm_p, nw=wts["qnw"], scale=FOX_Q_SCALE)
    (kf, kfb), (kf_s, _) = _proj(h, h_s, wts, _J_FK, _J_FV, "headnorm", tm_p, nw=wts["knw"], emit_f32=True)
    (vf, vfb), (vf_s, _) = _proj(h, h_s, wts, _J_FV, _J_END, "copy", tm_p)
    fcol = _cumsum(sm, batch)

    n_pool, page = cache_k.shape[1], cache_k.shape[2]
    decode_args = (page_table,
                   cache_k.reshape(depth, n_pool, page * F_HEADS, F_DH),
                   cache_v.reshape(depth, n_pool, page * F_HEADS, F_DH),
                   cache_logf.transpose(0, 1, 3, 2),
                   layer,
                   qf_s.reshape(db, F_HEADS, F_DH), kf_s.reshape(db, F_HEADS, F_DH),
                   vf_s.reshape(db, F_HEADS, F_DH), sm_s.reshape(db, 1, LANES))
    if _mlstm_fits_decode(db, batch, seq_len):
        yf_s, ym, c_p, nm_p = _fox_decode(*decode_args, mlstm=(qkv, sm, smt, sig, hw, batch))
    else:
        yf_s = _fox_decode(*decode_args)
        ym, c_p, nm_p = _mlstm_prompt(qkv, sm, smt, sig, hw, batch)
    yf_s = yf_s.reshape(db, F_W)

    ym_s, c_s, n_s, m_s = _mlstm_step(qkv_s, sm_s, sig_s, hw, state_C[layer], state_n[layer], state_m[layer])
    x1_s, h2_s = _merge(ym_s, yf_s, sig_s, xs, mod_s, n2w, wts, db)
    prev_s = state_conv[layer].transpose(1, 0, 2)
    ys, a_s, wa_t, wg_t, w2_b = _ffn(h2_s, x1_s, mod_s, prev_s, wts, db, seq=False)
    wts_ffn = {"wa_t": wa_t, "wg_t": wg_t, "w2": w2_b, "cw": wts["cw"], "cb": wts["cb"]}

    yf = _fox_prompt(qf, kfb, vfb, fcol, batch)
    x1, h2 = _merge(ym, yf, sig, xp, mod_p, n2w, wts, min(512, seq_len))
    conv0 = jnp.zeros((batch, SUBLANES, dff), F32)
    yp, a_last = _ffn(h2, x1, mod_p, conv0, wts_ffn, tm_p, seq=True)

    y_prompt = yp.reshape(batch, seq_len, d)
    k_prompt = kf.reshape(depth, batch, seq_len, F_HEADS, F_DH)
    v_prompt = vf.reshape(depth, batch, seq_len, F_HEADS, F_DH)
    logf_prompt = sm[:, LFF0:LFF0 + F_HEADS].reshape(depth, batch, seq_len, F_HEADS)
    c_prompt_out = c_p[None]
    n_prompt = nm_p[:, :M_HEADS, :][None]
    m_prompt = nm_p[:, M_HEADS:, 0][None]
    conv_prompt = a_last[:, -1, SUBLANES - (CONV_W - 1):, :][None]

    y_sample = ys.reshape(db, 1, d)
    k_sample = kf_s.reshape(depth, db, 1, F_HEADS, F_DH)
    v_sample = vf_s.reshape(depth, db, 1, F_HEADS, F_DH)
    logf_sample = sm_s[:, LFF0:LFF0 + F_HEADS].reshape(depth, db, 1, F_HEADS)
    conv_sample = jnp.stack([state_conv[layer][:, CONV_W - 2, :], a_s], axis=1)[None]

    return (y_prompt, y_sample, k_prompt, v_prompt, logf_prompt, c_prompt_out, n_prompt, m_prompt, conv_prompt,
            k_sample, v_sample, logf_sample, c_s[None], n_s[None], m_s[None], conv_sample)
```

```python
import functools

import jax
import jax.numpy as jnp
from jax import lax
from jax.experimental import pallas as pl
from jax.experimental.pallas import tpu as pltpu

F32 = jnp.float32
BF16 = jnp.bfloat16

NORM_EPS = 1e-6
M_HEADS, M_DK, M_DV = 4, 128, 256
F_HEADS, F_DH = 8, 128
M_QK = M_HEADS * M_DK
M_V = M_HEADS * M_DV
F_W = F_HEADS * F_DH
CONV_W = 3
CHUNK = 128
LANES = 128
SUBLANES = 8
SMALL_ROWS = 16
VMEM_LIMIT = 60 * 1024 * 1024

IG0, LFM0, LFF0 = 0, M_HEADS, 2 * M_HEADS

LOG2E = 1.4426950408889634
FOX_Q_SCALE = F_DH ** -0.5 * LOG2E


def _dot(a, b):
    return jnp.dot(a, b, preferred_element_type=F32)


def _dot_nt(a, b):
    return lax.dot_general(a, b, (((1,), (1,)), ((), ())), preferred_element_type=F32)


def _dot_tn(a, b):
    return lax.dot_general(a, b, (((0,), (0,)), ((), ())), preferred_element_type=F32)


def _split3(x):
    hi = x.astype(BF16)
    r1 = x - hi.astype(F32)
    mid = r1.astype(BF16)
    lo = (r1 - mid.astype(F32)).astype(BF16)
    return hi, mid, lo


def _dot_f32_lhs(x, mat01):
    hi, mid, lo = _split3(x)
    return _dot(hi, mat01) + _dot(mid, mat01) + _dot(lo, mat01)


def _dot_f32_rhs(mat01, x):
    hi, mid, lo = _split3(x)
    return _dot(mat01, hi) + _dot(mat01, mid) + _dot(mat01, lo)


def _log_sigmoid(x):
    return jnp.minimum(x, 0.0) - jnp.log1p(jnp.exp(-jnp.abs(x)))


def _tri01(n, kind):
    r = lax.broadcasted_iota(jnp.int32, (n, n), 0)
    c = lax.broadcasted_iota(jnp.int32, (n, n), 1)
    m = c <= r if kind == "lower" else r <= c
    return jnp.where(m, 1.0, 0.0).astype(BF16)


def _params(*sem):
    return pltpu.CompilerParams(dimension_semantics=sem, vmem_limit_bytes=VMEM_LIMIT)


def _ada_kernel(cp_ref, cs_ref, w_ref, b_ref, op_ref, os_ref):
    w = w_ref[...].astype(BF16)
    for c_ref, o_ref in ((cp_ref, op_ref), (cs_ref, os_ref)):
        c = c_ref[...]
        s = (c * jax.nn.sigmoid(c)).astype(BF16)
        o_ref[...] = _dot(s, w) + b_ref[...]


def _ada(c_p, c_s, w_ada, b_ada, tn=1024):
    d = c_p.shape[1]
    n = w_ada.shape[1]
    return pl.pallas_call(
        _ada_kernel,
        grid=(n // tn,),
        in_specs=[pl.BlockSpec(c_p.shape, lambda j: (0, 0)),
                  pl.BlockSpec(c_s.shape, lambda j: (0, 0)),
                  pl.BlockSpec((d, tn), lambda j: (0, j)),
                  pl.BlockSpec((1, tn), lambda j: (0, j))],
        out_specs=[pl.BlockSpec((c_p.shape[0], tn), lambda j: (0, j)),
                   pl.BlockSpec((c_s.shape[0], tn), lambda j: (0, j))],
        out_shape=[jax.ShapeDtypeStruct((c_p.shape[0], n), F32), jax.ShapeDtypeStruct((c_s.shape[0], n), F32)],
        compiler_params=_params("arbitrary"),
        name="ada",
    )(c_p, c_s, w_ada, b_ada.reshape(1, n))


MOD_SH1, MOD_SC1, MOD_G1, MOD_SH2, MOD_SC2, MOD_G2 = range(6)


PROJ_TN = 1024
_J_MIX, _J_GATE, _J_FQ, _J_FK, _J_FV, _J_END = 0, 2, 7, 8, 9, 10
_J_MO = _J_GATE + 4
_SPLIT_SIZES = (M_QK, M_QK, M_V, M_V, M_HEADS, M_HEADS, F_W, F_W, F_W, F_HEADS)
_SPLIT_OFFS = tuple(sum(_SPLIT_SIZES[:i]) for i in range(len(_SPLIT_SIZES) + 1))
_ROW_MO, _ROW_SMALL_M, _ROW_FQ, _ROW_SMALL_F, _ROW_GATES = (_SPLIT_OFFS[3], _SPLIT_OFFS[4], _SPLIT_OFFS[6],
                                                            _SPLIT_OFFS[9], _SPLIT_OFFS[10])
_TILE_RUNS = ((_J_MIX, 0), (_J_GATE, _ROW_GATES), (_J_MO, _ROW_MO), (_J_FQ, _ROW_FQ))


def _tile_row(jj):
    row = _TILE_RUNS[0][1] + PROJ_TN * (jj - _TILE_RUNS[0][0])
    for t0, r0 in _TILE_RUNS[1:]:
        row = jnp.where(jj >= t0, r0 + PROJ_TN * (jj - t0), row)
    return row


def _norm_kernel(x_ref, sc_ref, sh_ref, n1w_ref, wsm_m_ref, wsm_f_ref, bsm_ref, h_ref, sm_ref, smt_ref):
    x = x_ref[...]
    rstd = lax.rsqrt(jnp.mean(x * x, axis=-1, keepdims=True) + NORM_EPS)
    h = (x * rstd * n1w_ref[...]) * (1.0 + sc_ref[...]) + sh_ref[...]
    hb = h.astype(BF16)
    h_ref[...] = hb
    wsm = jnp.concatenate([wsm_m_ref[...], wsm_f_ref[...]], axis=0).astype(BF16)
    smt = _dot_nt(wsm, hb) + bsm_ref[...]
    row = lax.broadcasted_iota(jnp.int32, smt.shape, 0)
    smt = jnp.where(row < LFM0, smt, _log_sigmoid(smt))
    smt_ref[...] = smt
    pad = jnp.zeros((LANES - SMALL_ROWS, smt.shape[1]), F32)
    sm_ref[...] = jnp.concatenate([smt, pad], axis=0).T


def _proj_kernel(h_ref, hs_ref, w_ref, b_ref, *rest, kind, scale, emit_f32):
    *rest, wb_ref = rest
    nw_ref = None
    if kind == "headnorm":
        nw_ref, *rest = rest
    n_out = len(rest) // 2
    i = pl.program_id(1)

    @pl.when(i == 0)
    def _():
        wb_ref[...] = w_ref[...].astype(BF16)

    def project(rows_ref, outs):
        acc = _dot_nt(rows_ref[...], wb_ref[...]) + b_ref[...]
        if kind == "mix":
            lane = lax.broadcasted_iota(jnp.int32, (1, acc.shape[1]), 1)
            is_mk = jnp.logical_and(pl.program_id(0) == 0, lane >= M_QK)
            outs[0][...] = (acc * jnp.where(is_mk, M_DK ** -0.5, 1.0)).astype(BF16)
        elif kind == "sigmoid":
            outs[0][...] = jax.nn.sigmoid(acc).astype(BF16)
        elif kind == "headnorm":
            for h in range(acc.shape[1] // F_DH):
                sl = acc[:, h * F_DH:(h + 1) * F_DH]
                ms = jnp.mean(sl * sl, axis=-1, keepdims=True)
                y = sl * lax.rsqrt(ms + NORM_EPS) * nw_ref[...]
                if emit_f32:
                    outs[0][:, h * F_DH:(h + 1) * F_DH] = y
                outs[-1][:, h * F_DH:(h + 1) * F_DH] = (y * scale).astype(BF16)
        else:
            outs[0][...] = acc
            outs[1][...] = acc.astype(BF16)

    @pl.when(i == 0)
    def _():
        project(hs_ref, rest[n_out:])

    @pl.when(i > 0)
    def _():
        project(h_ref, rest[:n_out])


def _feature_rows_spec(rows, d, first_row):
    def index_map(*g):
        return pl.multiple_of(jnp.asarray(first_row(*g), jnp.int32), SUBLANES), 0

    return pl.BlockSpec((pl.Element(rows), pl.Element(d)), index_map)


def _mod_spec(mod, d, chunk, group_of):
    return pl.BlockSpec((None, mod.shape[1], d), lambda *g: (group_of(*g), 0, chunk))


def _norm(x2, mod, n1w, wts, tm):
    m, d = x2.shape
    tiles_per_group = (m // tm) // mod.shape[0]
    group_of = lambda i: i // tiles_per_group
    return pl.pallas_call(
        _norm_kernel,
        grid=(m // tm,),
        in_specs=[pl.BlockSpec((tm, d), lambda i: (i, 0)),
                  _mod_spec(mod, d, MOD_SC1, group_of),
                  _mod_spec(mod, d, MOD_SH1, group_of),
                  pl.BlockSpec((1, d), lambda i: (0, 0)),
                  _feature_rows_spec(SUBLANES, d, lambda i: _ROW_SMALL_M),
                  _feature_rows_spec(SUBLANES, d, lambda i: _ROW_SMALL_F),
                  pl.BlockSpec((SMALL_ROWS, 1), lambda i: (0, 0))],
        out_specs=[pl.BlockSpec((tm, d), lambda i: (i, 0)),
                   pl.BlockSpec((tm, LANES), lambda i: (i, 0)),
                   pl.BlockSpec((SMALL_ROWS, tm), lambda i: (0, i))],
        out_shape=[jax.ShapeDtypeStruct((m, d), BF16),
                   jax.ShapeDtypeStruct((m, LANES), F32),
                   jax.ShapeDtypeStruct((SMALL_ROWS, m), F32)],
        compiler_params=_params("arbitrary"),
        name="norm",
    )(x2, mod, mod, n1w, wts["w_in_t"], wts["w_in_t"], wts["bsm"])


def _proj(h, hs, wts, j0, j1, kind, tm, nw=None, scale=1.0, emit_f32=False):
    m, d = h.shape
    ms = hs.shape[0]
    n_row_tiles = m // tm
    tn = PROJ_TN
    width = (j1 - j0) * tn
    prompt_row = lambda i: jnp.maximum(i - 1, 0)
    in_specs = [pl.BlockSpec((tm, d), lambda j, i: (prompt_row(i), 0)),
                pl.BlockSpec((ms, d), lambda j, i: (0, 0)),
                _feature_rows_spec(tn, d, lambda j, i: _tile_row(j0 + j)),
                pl.BlockSpec((1, tn), lambda j, i: (0, j0 + j))]
    args = [h, hs, wts["w_in_t"], wts["bbig"]]
    if kind == "headnorm":
        in_specs.append(pl.BlockSpec((1, F_DH), lambda j, i: (0, 0)))
        args.append(nw)
    dtypes = ([F32] if emit_f32 or kind == "copy" else []) + [BF16]
    n_out = len(dtypes)
    out_shape = ([jax.ShapeDtypeStruct((m, width), t) for t in dtypes]
                 + [jax.ShapeDtypeStruct((ms, width), t) for t in dtypes])
    out_specs = ([pl.BlockSpec((tm, tn), lambda j, i: (prompt_row(i), j))] * n_out
                 + [pl.BlockSpec((ms, tn), lambda j, i: (0, j))] * n_out)
    outs = pl.pallas_call(
        functools.partial(_proj_kernel, kind=kind, scale=scale, emit_f32=emit_f32),
        grid=(j1 - j0, n_row_tiles + 1),
        in_specs=in_specs,
        out_specs=out_specs,
        out_shape=out_shape,
        scratch_shapes=[pltpu.VMEM((tn, d), BF16)],
        compiler_params=_params("arbitrary", "arbitrary"),
        name="proj_" + kind,
    )(*args)
    return outs[:n_out], outs[n_out:]


def _cumsum_kernel(sm_ref, fcol_ref, carry_ref):
    c = pl.program_id(1)
    tc = sm_ref.shape[0]

    @pl.when(c == 0)
    def _():
        carry_ref[...] = jnp.zeros_like(carry_ref)

    fc = _dot_f32_rhs(_tri01(tc, "lower"), sm_ref[...]) + carry_ref[0:1, :]
    fcol_ref[...] = fc
    carry_ref[...] = jnp.broadcast_to(fc[tc - 1:tc, :], carry_ref.shape)


def _cumsum(sm, batch, tc=512):
    m = sm.shape[0]
    nt = (m // batch) // tc
    return pl.pallas_call(
        _cumsum_kernel,
        grid=(batch, nt),
        in_specs=[pl.BlockSpec((tc, LANES), lambda b, c: (b * nt + c, 0))],
        out_specs=pl.BlockSpec((tc, LANES), lambda b, c: (b * nt + c, 0)),
        out_shape=jax.ShapeDtypeStruct((m, LANES), F32),
        scratch_shapes=[pltpu.VMEM((SUBLANES, LANES), F32)],
        compiler_params=_params("arbitrary", "arbitrary"),
        name="cumsum",
    )(sm)


def _mlstm_kernel(*refs):
    _mlstm_chunk(pl.program_id(1), pl.num_programs(1) - 1, range(M_HEADS), *refs)


def _mlstm_chunk(c, last_c, heads, q_ref, k_ref, v_ref, sm_ref, smt_ref, og_ref, hw_ref,
                 ym_ref, cout_ref, nm_ref, c_ref, n_ref, m_ref):
    L = q_ref.shape[0]

    @pl.when(c == 0)
    def _():
        for h in heads:
            c_ref[h] = jnp.zeros((M_DV, M_DK), F32)
            n_ref[h:h + 1, :] = jnp.zeros((1, M_DK), F32)
            m_ref[h:h + 1, :] = jnp.zeros((1, LANES), F32)

    r = lax.broadcasted_iota(jnp.int32, (L, L), 0)
    s_idx = lax.broadcasted_iota(jnp.int32, (L, L), 1)
    causal = s_idx <= r
    sm = sm_ref[...]
    smt = smt_ref[...]
    bcol_all = _dot_f32_rhs(_tri01(L, "lower"), sm)
    brow_all = _dot_f32_lhs(smt, _tri01(L, "upper"))

    for h in heads:
        ig_col = sm[:, IG0 + h:IG0 + h + 1]
        b_col = bcol_all[:, LFM0 + h:LFM0 + h + 1]
        ig_row = smt[IG0 + h:IG0 + h + 1, :]
        b_row = brow_all[LFM0 + h:LFM0 + h + 1, :]
        m0 = m_ref[h:h + 1, 0:1]
        a_row = ig_row - b_row
        cm_col = jnp.max(jnp.where(causal, a_row, -jnp.inf), axis=1, keepdims=True)
        mx_col = jnp.maximum(m0, cm_col)
        m_col = b_col + mx_col
        dmat = jnp.exp(jnp.where(causal, a_row - mx_col, -jnp.inf))
        inter = jnp.exp(m0 - mx_col)
        q = q_ref[:, h * M_DK:(h + 1) * M_DK]
        k = k_ref[:, h * M_DK:(h + 1) * M_DK]
        v = v_ref[:, h * M_DV:(h + 1) * M_DV]
        c0 = c_ref[h]
        n0 = n_ref[h:h + 1, :]
        s = _dot_nt(q, k) * dmat
        num = _dot(s.astype(BF16), v) + inter * _dot_nt(q, c0.astype(BF16))
        den = jnp.sum(s, axis=1, keepdims=True) + inter * jnp.sum(q.astype(F32) * n0, axis=1, keepdims=True)
        hh = num / jnp.maximum(jnp.abs(den), jnp.exp(-m_col))
        hn = hh * lax.rsqrt(jnp.mean(hh * hh, axis=1, keepdims=True) + NORM_EPS) * hw_ref[:, h * M_DV:(h + 1) * M_DV]
        ym_ref[:, h * M_DV:(h + 1) * M_DV] = (og_ref[:, h * M_DV:(h + 1) * M_DV] * hn).astype(BF16)
        m_new = m_col[L - 1:L, :]
        b_last = b_col[L - 1:L, :]
        w_end = jnp.exp(ig_col + b_last - b_col - m_new)
        decay = jnp.exp(b_last + m0 - m_new)
        vw = (w_end * v.astype(F32)).astype(BF16)
        c_ref[h] = decay * c0 + _dot_tn(vw, k)
        n_ref[h:h + 1, :] = decay * n0 + jnp.sum(w_end * k.astype(F32), axis=0, keepdims=True)
        m_ref[h:h + 1, :] = jnp.broadcast_to(m_new, (1, LANES))

    @pl.when(c == last_c)
    def _():
        for h in heads:
            cout_ref[h] = c_ref[h]
            nm_ref[h:h + 1, :] = n_ref[h:h + 1, :]
            nm_ref[M_HEADS + h:M_HEADS + h + 1, :] = m_ref[h:h + 1, :]


def _mlstm_prompt(qkv, sm, smt, sig, hw, batch):
    m = qkv.shape[0]
    og_block = sig.shape[1] // M_V - 1
    nc = (m // batch) // CHUNK
    row = lambda b, c: (b * nc + c, 0)
    return pl.pallas_call(
        _mlstm_kernel,
        grid=(batch, nc),
        in_specs=[pl.BlockSpec((CHUNK, M_QK), row),
                  pl.BlockSpec((CHUNK, M_QK), lambda b, c: (b * nc + c, 1)),
                  pl.BlockSpec((CHUNK, M_V), lambda b, c: (b * nc + c, 2 * M_QK // M_V)),
                  pl.BlockSpec((CHUNK, LANES), row),
                  pl.BlockSpec((SMALL_ROWS, CHUNK), lambda b, c: (0, b * nc + c)),
                  pl.BlockSpec((CHUNK, M_V), lambda b, c: (b * nc + c, og_block)),
                  pl.BlockSpec((1, M_V), lambda b, c: (0, 0))],
        out_specs=[pl.BlockSpec((CHUNK, M_V), row),
                   pl.BlockSpec((None, M_HEADS, M_DV, M_DK), lambda b, c: (b, 0, 0, 0)),
                   pl.BlockSpec((None, 2 * M_HEADS, M_DK), lambda b, c: (b, 0, 0))],
        out_shape=[jax.ShapeDtypeStruct((m, M_V), BF16),
                   jax.ShapeDtypeStruct((batch, M_HEADS, M_DV, M_DK), F32),
                   jax.ShapeDtypeStruct((batch, 2 * M_HEADS, M_DK), F32)],
        scratch_shapes=[pltpu.VMEM((M_HEADS, M_DV, M_DK), F32),
                        pltpu.VMEM((SUBLANES, M_DK), F32),
                        pltpu.VMEM((SUBLANES, LANES), F32)],
        compiler_params=_params("arbitrary", "arbitrary"),
        name="mlstm",
    )(qkv, qkv, qkv, sm, smt, sig, hw)


FOX_HEADS_PER_STEP = 4
FOX_ROW_CHUNK = 32


def _forget_columns(f, first):
    hi, mid, lo = _split3(f)
    lane = lax.broadcasted_iota(jnp.int32, (f.shape[0], LANES), 1)
    ones = jnp.where(jnp.abs(lane - (4 - first)) <= 1, 1.0, 0.0)
    ext = jnp.where(lane == first, hi.astype(F32),
                    jnp.where(lane == first + 1, mid.astype(F32),
                              jnp.where(lane == first + 2, lo.astype(F32), ones)))
    return ext.astype(BF16)


def _fox_kernel(q_ref, k_ref, v_ref, fcol_ref, o_ref, qaug_ref, kaug_ref, s_ref, p_ref, m_ref, l_ref, acc_ref):
    hp = pl.program_id(1)
    qi = pl.program_id(2)
    tq = q_ref.shape[0]
    t = k_ref.shape[0]
    nh = FOX_HEADS_PER_STEP

    def forget_sum(fc, h2):
        lane = lax.broadcasted_iota(jnp.int32, fc.shape, 1)
        sel = jnp.where(lane == LFF0 + nh * hp + h2, fc, 0.0)
        return jnp.sum(sel, axis=1, keepdims=True) * LOG2E

    @pl.when(qi == 0)
    def _():
        fc = fcol_ref[...]
        for h2 in range(nh):
            kaug_ref[h2, :, 0:F_DH] = k_ref[:, h2 * F_DH:(h2 + 1) * F_DH]
            kaug_ref[h2, :, F_DH:2 * F_DH] = _forget_columns(-forget_sum(fc, h2), 3)

    fq = fcol_ref[pl.ds(pl.multiple_of(qi * tq, tq), tq), :]
    for h2 in range(nh):
        qaug_ref[h2, :, 0:F_DH] = q_ref[:, h2 * F_DH:(h2 + 1) * F_DH]
        qaug_ref[h2, :, F_DH:2 * F_DH] = _forget_columns(forget_sum(fq, h2), 0)
    m_ref[...] = jnp.full_like(m_ref, -jnp.inf)
    l_ref[...] = jnp.zeros_like(l_ref)
    acc_ref[...] = jnp.zeros_like(acc_ref)
    rc = FOX_ROW_CHUNK
    n_lane_tiles = tq // LANES

    def softmax_rows(h2, c, masked):
        rows = slice(c * rc, (c + 1) * rc)
        s = s_ref[h2, rows, :]
        if masked:
            r = lax.broadcasted_iota(jnp.int32, (rc, tq), 0) + c * rc
            col = lax.broadcasted_iota(jnp.int32, (rc, tq), 1)
            s = jnp.where(r >= col, s, -jnp.inf)
        tiles = [s[:, k * LANES:(k + 1) * LANES] for k in range(n_lane_tiles)]
        m_old = m_ref[h2, rows, :]
        mx = functools.reduce(jnp.maximum, tiles)
        m_new = jnp.maximum(m_old, jnp.max(mx, axis=1, keepdims=True))
        alpha = jnp.exp2(m_old - m_new)
        ps = [jnp.exp2(tile - m_new) for tile in tiles]
        row_sum = jnp.sum(functools.reduce(jnp.add, ps), axis=1, keepdims=True)
        l_ref[h2, rows, :] = alpha * l_ref[h2, rows, :] + row_sum
        for k in range(n_lane_tiles):
            p_ref[h2, rows, k * LANES:(k + 1) * LANES] = ps[k].astype(BF16)
        acc_ref[h2, rows, :] = alpha * acc_ref[h2, rows, :]
        m_ref[h2, rows, :] = m_new

    def block(kj, masked):
        off = pl.multiple_of(kj * tq, tq)
        for h2 in range(nh):
            s_ref[h2] = _dot_nt(qaug_ref[h2], kaug_ref[h2, pl.ds(off, tq), :])
        for h2 in range(nh):
            for c in range(tq // rc):
                softmax_rows(h2, c, masked)
            acc_ref[h2] += _dot(p_ref[h2], v_ref[pl.ds(off, tq), h2 * F_DH:(h2 + 1) * F_DH])

    def body(kj, carry):
        block(kj, False)
        return carry

    lax.fori_loop(0, qi, body, 0)
    block(qi, True)
    for h2 in range(nh):
        o_ref[:, h2 * F_DH:(h2 + 1) * F_DH] = (acc_ref[h2] / l_ref[h2]).astype(BF16)


def _fox_prompt(qf, kfb, vfb, fcol, batch, tq=512):
    m = qf.shape[0]
    t = m // batch
    nq = t // tq
    nh = FOX_HEADS_PER_STEP
    wide = nh * F_DH
    return pl.pallas_call(
        _fox_kernel,
        grid=(batch, F_HEADS // nh, nq),
        in_specs=[pl.BlockSpec((tq, wide), lambda b, h, i: (b * nq + i, h)),
                  pl.BlockSpec((t, wide), lambda b, h, i: (b, h)),
                  pl.BlockSpec((t, wide), lambda b, h, i: (b, h)),
                  pl.BlockSpec((t, LANES), lambda b, h, i: (b, 0))],
        out_specs=pl.BlockSpec((tq, wide), lambda b, h, i: (b * nq + i, h)),
        out_shape=jax.ShapeDtypeStruct((m, F_W), BF16),
        scratch_shapes=[pltpu.VMEM((nh, tq, 2 * F_DH), BF16),
                        pltpu.VMEM((nh, t, 2 * F_DH), BF16),
                        pltpu.VMEM((nh, tq, tq), F32),
                        pltpu.VMEM((nh, tq, tq), BF16),
                        pltpu.VMEM((nh, tq, LANES), F32),
                        pltpu.VMEM((nh, tq, LANES), F32),
                        pltpu.VMEM((nh, tq, F_DH), F32)],
        compiler_params=_params("arbitrary", "arbitrary", "arbitrary"),
        name="fox",
    )(qf, kfb, vfb, fcol)


MSTEP_ROWS = 8


def _mstep_kernel(q_ref, k_ref, v_ref, sm_ref, og_ref, hw_ref, c0_ref, n0_ref, m0_ref,
                  ym_ref, cout_ref, nout_ref, mout_ref, cq_ref):
    rows = q_ref.shape[0]
    sm = sm_ref[...]
    ig = sm[:, IG0:IG0 + M_HEADS]
    lf = sm[:, LFM0:LFM0 + M_HEADS]
    m0 = m0_ref[...]
    m_new = lf + jnp.maximum(m0, ig - lf)
    w_end = jnp.exp(ig - m_new)
    decay = jnp.exp(lf + m0 - m_new)
    mout_ref[...] = m_new
    row_id = lax.broadcasted_iota(jnp.int32, (SUBLANES, M_DV), 0)

    for h in range(M_HEADS):
        qh = q_ref[:, h * M_DK:(h + 1) * M_DK]
        kh = k_ref[:, h * M_DK:(h + 1) * M_DK]
        vh = v_ref[:, h * M_DV:(h + 1) * M_DV]
        qf = qh.astype(F32)
        kf = kh.astype(F32)
        vf = vh.astype(F32)
        wh = w_end[:, h:h + 1]
        dh = decay[:, h:h + 1]
        for r in range(rows):
            c0 = c0_ref[r, h]
            q8 = jnp.broadcast_to(qh[r:r + 1, :], (SUBLANES, M_DK))
            cq_ref[r:r + 1, :] = _dot_nt(q8, c0.astype(BF16))[0:1, :]
            vw8 = jnp.where(row_id == 0, jnp.broadcast_to(wh[r:r + 1, :] * vf[r:r + 1, :], (SUBLANES, M_DV)), 0.0)
            k8 = jnp.broadcast_to(kh[r:r + 1, :], (SUBLANES, M_DK))
            cout_ref[r, h] = dh[r:r + 1, :] * c0 + _dot_tn(vw8.astype(BF16), k8)
        n0 = n0_ref[:, h, :]
        s = jnp.sum(qf * kf, axis=1, keepdims=True) * wh
        num = s * vf + dh * cq_ref[...]
        den = s + dh * jnp.sum(n0 * qf, axis=1, keepdims=True)
        hh = num / jnp.maximum(jnp.abs(den), jnp.exp(-m_new[:, h:h + 1]))
        hn = hh * lax.rsqrt(jnp.mean(hh * hh, axis=1, keepdims=True) + NORM_EPS) * hw_ref[:, h * M_DV:(h + 1) * M_DV]
        ym_ref[:, h * M_DV:(h + 1) * M_DV] = (og_ref[:, h * M_DV:(h + 1) * M_DV] * hn).astype(BF16)
        nout_ref[:, h, :] = dh * n0 + wh * kf


def _mlstm_step(qkv, sm, sig, hw, c0, n0, m0):
    db = qkv.shape[0]
    og_block = sig.shape[1] // M_V - 1
    rows = MSTEP_ROWS
    row = lambda i: (i, 0)
    return pl.pallas_call(
        _mstep_kernel,
        grid=(db // rows,),
        in_specs=[pl.BlockSpec((rows, M_QK), row),
                  pl.BlockSpec((rows, M_QK), lambda i: (i, 1)),
                  pl.BlockSpec((rows, M_V), lambda i: (i, 2 * M_QK // M_V)),
                  pl.BlockSpec((rows, LANES), row),
                  pl.BlockSpec((rows, M_V), lambda i: (i, og_block)),
                  pl.BlockSpec((1, M_V), lambda i: (0, 0)),
                  pl.BlockSpec((rows, M_HEADS, M_DV, M_DK), lambda i: (i, 0, 0, 0)),
                  pl.BlockSpec((rows, M_HEADS, M_DK), lambda i: (i, 0, 0)),
                  pl.BlockSpec((rows, M_HEADS), row)],
        out_specs=[pl.BlockSpec((rows, M_V), row),
                   pl.BlockSpec((rows, M_HEADS, M_DV, M_DK), lambda i: (i, 0, 0, 0)),
                   pl.BlockSpec((rows, M_HEADS, M_DK), lambda i: (i, 0, 0)),
                   pl.BlockSpec((rows, M_HEADS), row)],
        out_shape=[jax.ShapeDtypeStruct((db, M_V), BF16),
                   jax.ShapeDtypeStruct(c0.shape, F32),
                   jax.ShapeDtypeStruct(n0.shape, F32),
                   jax.ShapeDtypeStruct(m0.shape, F32)],
        scratch_shapes=[pltpu.VMEM((rows, M_DV), F32)],
        compiler_params=_params("arbitrary"),
        name="mstep",
    )(qkv, qkv, qkv, sm, sig, hw, c0, n0, m0)


N_MLSTM_IN = 7
N_MLSTM_OUT = 3
N_DECODE_SCRATCH = 4


def _decode_kernel(*refs, n_pages, mlstm_steps_per_chunk, mlstm_chunks):
    n_dec_in = 3 * n_pages + 4
    n_ml = N_MLSTM_IN if mlstm_steps_per_chunk else 0
    dec_in = refs[1:1 + n_dec_in]
    ml_in = refs[1 + n_dec_in:1 + n_dec_in + n_ml]
    outs = refs[1 + n_dec_in + n_ml:]
    o_ref, outs = outs[0], outs[1:]
    if mlstm_steps_per_chunk:
        ml_out, outs = outs[:N_MLSTM_OUT], outs[N_MLSTM_OUT:]
        ml_scratch = outs[N_DECODE_SCRATCH:]
        step = pl.program_id(0)
        chunk = (step // mlstm_steps_per_chunk) % mlstm_chunks
        heads_per_step = M_HEADS // mlstm_steps_per_chunk
        for g in range(mlstm_steps_per_chunk):
            @pl.when(step % mlstm_steps_per_chunk == g)
            def _(g=g):
                _mlstm_chunk(chunk, mlstm_chunks - 1, range(g * heads_per_step, (g + 1) * heads_per_step),
                             *ml_in, *ml_out, *ml_scratch)
    _decode_sequence(dec_in[:n_pages], dec_in[n_pages:2 * n_pages], dec_in[2 * n_pages:3 * n_pages],
                     *dec_in[3 * n_pages:], o_ref, *outs[:N_DECODE_SCRATCH])


def _decode_sequence(k_refs, v_refs, lf_refs, q_ref, kn_ref, vn_ref, sm_ref, o_ref,
                     s_ref, xn_ref, widen_ref, spread_ref):
    n_pages = len(k_refs)
    page = xn_ref.shape[1]
    pw = page * F_HEADS

    for p in range(n_pages):
        xn_ref[p * F_HEADS:(p + 1) * F_HEADS, :] = lf_refs[p][...]
    xn = xn_ref[...] * LOG2E

    @pl.when(pl.program_id(0) == 0)
    def _():
        t_id = lax.broadcasted_iota(jnp.int32, (page, pw), 0)
        l_id = lax.broadcasted_iota(jnp.int32, (page, pw), 1)
        widen_ref[...] = jnp.where((l_id // F_HEADS) == t_id, 1.0, 0.0).astype(BF16)
        r = lax.broadcasted_iota(jnp.int32, (LANES, pw), 0)
        c = lax.broadcasted_iota(jnp.int32, (LANES, pw), 1)
        spread_ref[...] = jnp.where(r == LFF0 + (c & (F_HEADS - 1)), 1.0, 0.0).astype(BF16)

    wide = _dot_f32_lhs(xn, widen_ref[...])
    r_id = lax.broadcasted_iota(jnp.int32, wide.shape, 0)
    c_id = lax.broadcasted_iota(jnp.int32, wide.shape, 1)
    own = jnp.where((r_id & (F_HEADS - 1)) == (c_id & (F_HEADS - 1)), wide, 0.0)
    g_p = lax.broadcasted_iota(jnp.int32, (n_pages, n_pages * F_HEADS), 0)
    g_r = lax.broadcasted_iota(jnp.int32, (n_pages, n_pages * F_HEADS), 1)
    gather = jnp.where((g_r // F_HEADS) == g_p, 1.0, 0.0).astype(BF16)
    x = _dot_f32_rhs(gather, own)
    lane = lax.broadcasted_iota(jnp.int32, x.shape, 1)
    y, z = x, x
    step = F_HEADS
    while step < pw:
        y = y + jnp.where(lane < pw - step, pltpu.roll(y, pw - step, axis=1), 0.0)
        z = z + pltpu.roll(z, step, axis=1)
        step *= 2
    after = y - x

    run = _dot_f32_lhs(jnp.broadcast_to(sm_ref[...] * LOG2E, (SUBLANES, LANES)), spread_ref[...])[0:1, :]

    q = q_ref[...]
    sub = lax.broadcasted_iota(jnp.int32, (F_HEADS, pw), 0)
    lane_h = lax.broadcasted_iota(jnp.int32, (F_HEADS, pw), 1) & (F_HEADS - 1)
    valid = lane_h == sub
    s_self = jnp.sum(q.astype(F32) * kn_ref[...], axis=1, keepdims=True)
    m_run = s_self
    for p in reversed(range(n_pages)):
        kb = k_refs[p][...].astype(BF16)
        st = _dot_nt(q, kb) + (run + after[p:p + 1, :])
        st = jnp.where(valid, st, -jnp.inf)
        s_ref[p] = st
        m_run = jnp.maximum(m_run, jnp.max(st, axis=1, keepdims=True))
        run = run + z[p:p + 1, :]
    w_self = jnp.exp2(s_self - m_run)
    l = w_self
    acc = w_self * vn_ref[...]
    for p in range(n_pages):
        pe = jnp.exp2(s_ref[p] - m_run)
        l = l + jnp.sum(pe, axis=1, keepdims=True)
        acc = acc + _dot(pe.astype(BF16), v_refs[p][...].astype(BF16))
    o_ref[...] = (acc / l).astype(BF16)


def _mlstm_fits_decode(db, batch, seq_len):
    chunk_steps = batch * (seq_len // CHUNK)
    return db % chunk_steps == 0 and db // chunk_steps in (1, 2, 4)


def _fox_decode(page_table, ck, cv, clf, layer, q3, k3, v3, sm3, mlstm=None):
    db, n_pages = page_table.shape
    pw = ck.shape[2]
    page = clf.shape[3]
    seq = lambda b, pt: (b, 0, 0)

    def pg(p):
        return lambda b, pt: (layer, pt[b, p], 0, 0)

    in_specs = ([pl.BlockSpec((None, None, pw, F_DH), pg(p)) for p in range(n_pages)]
                + [pl.BlockSpec((None, None, pw, F_DH), pg(p)) for p in range(n_pages)]
                + [pl.BlockSpec((None, None, F_HEADS, page), pg(p)) for p in range(n_pages)]
                + [pl.BlockSpec((None, F_HEADS, F_DH), seq),
                   pl.BlockSpec((None, F_HEADS, F_DH), seq),
                   pl.BlockSpec((None, F_HEADS, F_DH), seq),
                   pl.BlockSpec((None, 1, LANES), seq)])
    args = [page_table, *([ck] * n_pages), *([cv] * n_pages), *([clf] * n_pages), q3, k3, v3, sm3]
    out_specs = [pl.BlockSpec((None, F_HEADS, F_DH), seq)]
    out_shape = [jax.ShapeDtypeStruct((db, F_HEADS, F_DH), BF16)]
    scratch = [pltpu.VMEM((n_pages, F_HEADS, pw), F32), pltpu.VMEM((n_pages * F_HEADS, page), F32),
               pltpu.VMEM((page, pw), BF16), pltpu.VMEM((LANES, pw), BF16)]
    per_chunk, nc = 0, 0
    if mlstm is not None:
        qkv, sm, smt, sig, hw, batch = mlstm
        m = qkv.shape[0]
        nc = (m // batch) // CHUNK
        per_chunk = db // (batch * nc)
        og_block = sig.shape[1] // M_V - 1
        row = lambda s, pt: (s // per_chunk, 0)
        in_specs += [pl.BlockSpec((CHUNK, M_QK), row),
                     pl.BlockSpec((CHUNK, M_QK), lambda s, pt: (s // per_chunk, 1)),
                     pl.BlockSpec((CHUNK, M_V), lambda s, pt: (s // per_chunk, 2 * M_QK // M_V)),
                     pl.BlockSpec((CHUNK, LANES), row),
                     pl.BlockSpec((SMALL_ROWS, CHUNK), lambda s, pt: (0, s // per_chunk)),
                     pl.BlockSpec((CHUNK, M_V), lambda s, pt: (s // per_chunk, og_block)),
                     pl.BlockSpec((1, M_V), lambda s, pt: (0, 0))]
        args += [qkv, qkv, qkv, sm, smt, sig, hw]
        seq_of = lambda s, pt: s // (per_chunk * nc)
        out_specs += [pl.BlockSpec((CHUNK, M_V), row),
                      pl.BlockSpec((None, M_HEADS, M_DV, M_DK), lambda s, pt: (seq_of(s, pt), 0, 0, 0)),
                      pl.BlockSpec((None, 2 * M_HEADS, M_DK), lambda s, pt: (seq_of(s, pt), 0, 0))]
        out_shape += [jax.ShapeDtypeStruct((m, M_V), BF16),
                      jax.ShapeDtypeStruct((batch, M_HEADS, M_DV, M_DK), F32),
                      jax.ShapeDtypeStruct((batch, 2 * M_HEADS, M_DK), F32)]
        scratch += [pltpu.VMEM((M_HEADS, M_DV, M_DK), F32),
                    pltpu.VMEM((SUBLANES, M_DK), F32),
                    pltpu.VMEM((SUBLANES, LANES), F32)]
    grid_spec = pltpu.PrefetchScalarGridSpec(
        num_scalar_prefetch=1,
        grid=(db,),
        in_specs=in_specs,
        out_specs=out_specs,
        scratch_shapes=scratch,
    )
    outs = pl.pallas_call(
        functools.partial(_decode_kernel, n_pages=n_pages, mlstm_steps_per_chunk=per_chunk, mlstm_chunks=nc),
        grid_spec=grid_spec,
        out_shape=out_shape,
        compiler_params=_params("arbitrary"),
        name="decode",
    )(*args)
    return outs if mlstm is not None else outs[0]


def _merge_kernel(ym_ref, yf_ref, sga_ref, sgb_ref, x_ref, g1_ref, sc_ref, sh_ref, n2w_ref,
                  wa_ref, wb_ref, wo_ref, x1_ref, h2_ref):
    merged = sga_ref[...] * _dot(ym_ref[...], wa_ref[...]) + sgb_ref[...] * _dot(yf_ref[...], wb_ref[...])
    x1 = x_ref[...] + g1_ref[...] * _dot(merged.astype(BF16), wo_ref[...])
    x1_ref[...] = x1
    rstd = lax.rsqrt(jnp.mean(x1 * x1, axis=-1, keepdims=True) + NORM_EPS)
    h2_ref[...] = ((x1 * rstd * n2w_ref[...]) * (1.0 + sc_ref[...]) + sh_ref[...]).astype(BF16)


def _merge(ym, yf, sig, x2, mod, n2w, wts, tm):
    m, d = x2.shape
    tiles_per_group = (m // tm) // mod.shape[0]
    group_of = lambda i: i // tiles_per_group
    row = lambda i: (i, 0)
    const = lambda i: (0, 0)
    resident = dict(pipeline_mode=pl.Buffered(1))
    return pl.pallas_call(
        _merge_kernel,
        grid=(m // tm,),
        in_specs=[pl.BlockSpec((tm, M_V), row),
                  pl.BlockSpec((tm, F_W), row),
                  pl.BlockSpec((tm, d), row),
                  pl.BlockSpec((tm, d), lambda i: (i, 1)),
                  pl.BlockSpec((tm, d), row),
                  _mod_spec(mod, d, MOD_G1, group_of),
                  _mod_spec(mod, d, MOD_SC2, group_of),
                  _mod_spec(mod, d, MOD_SH2, group_of),
                  pl.BlockSpec((1, d), const),
                  pl.BlockSpec((M_V, d), const, **resident),
                  pl.BlockSpec((F_W, d), const, **resident),
                  pl.BlockSpec((d, d), const, **resident)],
        out_specs=[pl.BlockSpec((tm, d), row), pl.BlockSpec((tm, d), row)],
        out_shape=[jax.ShapeDtypeStruct((m, d), F32), jax.ShapeDtypeStruct((m, d), BF16)],
        compiler_params=_params("arbitrary"),
        name="merge",
    )(ym, yf, sig, sig, x2, mod, mod, mod, n2w, wts["wa"], wts["wb"], wts["wo"])


FFN_TF = 512


def _ffn_kernel(h2_ref, x1_ref, g2_ref, p_ref, wa_ref, wg_ref, cw_ref, cb_ref, w2_ref,
                y_ref, a_out_ref, *rest, seq):
    f = pl.program_id(2)
    tm = h2_ref.shape[0]
    carry_ref = rest[-1]

    @pl.when(f == 0)
    def _():
        y_ref[...] = jnp.zeros_like(y_ref)

    if seq:
        @pl.when(pl.program_id(1) == 0)
        def _():
            carry_ref[f] = p_ref[...]
        wa, wg, w2 = wa_ref[...], wg_ref[...], w2_ref[...]
    else:
        wa, wg, w2 = wa_ref[...].astype(BF16), wg_ref[...].astype(BF16), w2_ref[...].astype(BF16)
        wa_out_ref, wg_out_ref, w2_out_ref = rest[:3]
        wa_out_ref[...] = wa
        wg_out_ref[...] = wg
        w2_out_ref[...] = w2

    h2 = h2_ref[...]
    a = _dot(h2, wa)
    g = _dot(h2, wg)
    if seq:
        prev = carry_ref[f]
        p2, p1 = prev[SUBLANES - 2:SUBLANES - 1, :], prev[SUBLANES - 1:SUBLANES, :]
        r = lax.broadcasted_iota(jnp.int32, a.shape, 0)
        a1 = jnp.where(r == 0, p1, pltpu.roll(a, 1, axis=0))
        a2 = jnp.where(r == 0, p2, jnp.where(r == 1, p1, pltpu.roll(a, 2, axis=0)))
        carry_ref[f] = a[tm - SUBLANES:tm, :]
        a_out_ref[...] = a[tm - SUBLANES:tm, :]
    else:
        a2, a1 = p_ref[0], p_ref[1]
        a_out_ref[...] = a
    ac = cw_ref[0:1, :] * a2 + cw_ref[1:2, :] * a1 + cw_ref[2:3, :] * a + cb_ref[...]
    u = (jax.nn.gelu(ac, approximate=True) * g).astype(BF16)
    y_ref[...] += _dot(u, w2)

    @pl.when(f == pl.num_programs(2) - 1)
    def _():
        y_ref[...] = x1_ref[...] + g2_ref[...] * y_ref[...]


def _ffn(h2, x1, mod, prev, wts, tm, seq):
    m, d = x1.shape
    groups = mod.shape[0]
    tiles = (m // tm) // groups
    dff = wts["w2"].shape[0]
    tf = FFN_TF
    nf = dff // tf
    row = lambda b, i, f: (b * tiles + i, 0)
    up_tile = pl.BlockSpec((None, d, tf), lambda b, i, f: (f, 0, 0))
    down_tile = pl.BlockSpec((tf, d), lambda b, i, f: (f, 0))
    out_specs = [pl.BlockSpec((tm, d), row)]
    out_shape = [jax.ShapeDtypeStruct((m, d), F32)]
    if seq:
        prev_spec = pl.BlockSpec((None, SUBLANES, tf), lambda b, i, f: (b, 0, f))
        out_specs.append(pl.BlockSpec((None, None, SUBLANES, tf), lambda b, i, f: (b, i, 0, f)))
        out_shape.append(jax.ShapeDtypeStruct((groups, tiles, SUBLANES, dff), F32))
        w_specs = [up_tile, up_tile]
        w_args = [wts["wa_t"], wts["wg_t"]]
    else:
        assert groups == 1 and tiles == 1, "the bf16 weight tiles are written once"
        prev_spec = pl.BlockSpec((CONV_W - 1, tm, tf), lambda b, i, f: (0, i, f))
        out_specs += [pl.BlockSpec((tm, tf), lambda b, i, f: (i, f)), up_tile, up_tile, down_tile]
        out_shape += [jax.ShapeDtypeStruct((m, dff), F32),
                      jax.ShapeDtypeStruct((nf, d, tf), BF16),
                      jax.ShapeDtypeStruct((nf, d, tf), BF16),
                      jax.ShapeDtypeStruct((dff, d), BF16)]
        w_specs = [pl.BlockSpec((d, tf), lambda b, i, f: (0, f)),
                   pl.BlockSpec((d, tf), lambda b, i, f: (0, f + nf))]
        w_args = [wts["w1"], wts["w1"]]
    return pl.pallas_call(
        functools.partial(_ffn_kernel, seq=seq),
        grid=(groups, tiles, nf),
        in_specs=[pl.BlockSpec((tm, d), row),
                  pl.BlockSpec((tm, d), row),
                  _mod_spec(mod, d, MOD_G2, lambda b, i, f: b),
                  prev_spec,
                  *w_specs,
                  pl.BlockSpec((CONV_W, tf), lambda b, i, f: (0, f)),
                  pl.BlockSpec((1, tf), lambda b, i, f: (0, f)),
                  down_tile],
        out_specs=out_specs,
        out_shape=out_shape,
        scratch_shapes=[pltpu.VMEM((nf, SUBLANES, tf), F32)],
        compiler_params=_params("arbitrary", "arbitrary", "arbitrary"),
        name="ffn_seq" if seq else "ffn_row",
    )(h2, x1, mod, prev, *w_args, wts["cw"], wts["cb"], wts["w2"])


def _prep_weights(w_in, b_in, f_qnorm_w, f_knorm_w, w_proj_a, w_proj_b, w_out, w_ffn_in, conv_w, conv_b, w_ffn_out):
    o = _SPLIT_OFFS
    bbig = jnp.concatenate([b_in[:o[3]], b_in[o[10]:], b_in[o[3]:o[4]], b_in[o[6]:o[9]]])
    bsm = jnp.concatenate([b_in[o[4]:o[6]], b_in[o[9]:o[10]]])
    return {
        "w_in_t": w_in.T,
        "bbig": bbig.reshape(1, -1),
        "bsm": bsm.reshape(SMALL_ROWS, 1),
        "qnw": f_qnorm_w.reshape(1, F_DH),
        "knw": f_knorm_w.reshape(1, F_DH),
        "wa": w_proj_a.astype(BF16),
        "wb": w_proj_b.astype(BF16),
        "wo": w_out.astype(BF16),
        "w1": w_ffn_in,
        "cw": conv_w,
        "cb": conv_b.reshape(1, -1),
        "w2": w_ffn_out,
    }


def kernel(x_prompt, x_sample, c_prompt, c_sample, cache_k, cache_v, cache_logf, page_table, state_C, state_n, state_m, state_conv, w_ada, b_ada, norm1_w, w_in, b_in, m_hnorm_w, f_qnorm_w, f_knorm_w, w_proj_a, w_proj_b, w_out, norm2_w, w_ffn_in, conv_w, conv_b, w_ffn_out):
    depth = w_ada.shape[0]
    assert depth == 1, "single-layer step"
    batch, seq_len, d = x_prompt.shape
    db = x_sample.shape[0]
    assert x_sample.shape[1] == 1
    dff = w_ffn_out.shape[1]
    layer = 0

    wts = _prep_weights(w_in[layer], b_in[layer], f_qnorm_w[layer], f_knorm_w[layer], w_proj_a[layer],
                        w_proj_b[layer], w_out[layer], w_ffn_in[layer], conv_w[layer], conv_b[layer],
                        w_ffn_out[layer])
    n1w = norm1_w[layer].reshape(1, d)
    n2w = norm2_w[layer].reshape(1, d)
    hw = m_hnorm_w[layer].reshape(1, M_V)

    mod_p, mod_s = _ada(c_prompt, c_sample, w_ada[layer], b_ada[layer])
    mod_p = mod_p.reshape(batch, 1, 6 * d)
    mod_s = mod_s.reshape(1, db, 6 * d)

    tm_p = min(1024, seq_len)
    xp = x_prompt.reshape(batch * seq_len, d)
    xs = x_sample.reshape(db, d)
    h, sm, smt = _norm(xp, mod_p, n1w, wts, tm_p)
    h_s, sm_s, _ = _norm(xs, mod_s, n1w, wts, db)
    (qkv,), (qkv_s,) = _proj(h, h_s, wts, _J_MIX, _J_GATE, "mix", tm_p)
    (sig,), (sig_s,) = _proj(h, h_s, wts, _J_GATE, _J_FQ, "sigmoid", tm_p)
    (qf,), (qf_s,) = _proj(h, h_s, wts, _J_FQ, _J_FK, "headnorm", tm_p, nw=wts["qnw"], scale=FOX_Q_SCALE)
    (kf, kfb), (kf_s, _) = _proj(h, h_s, wts, _J_FK, _J_FV, "headnorm", tm_p, nw=wts["knw"], emit_f32=True)
    (vf, vfb), (vf_s, _) = _proj(h, h_s, wts, _J_FV, _J_END, "copy", tm_p)
    fcol = _cumsum(sm, batch)

    n_pool, page = cache_k.shape[1], cache_k.shape[2]
    decode_args = (page_table,
                   cache_k.reshape(depth, n_pool, page * F_HEADS, F_DH),
                   cache_v.reshape(depth, n_pool, page * F_HEADS, F_DH),
                   cache_logf.transpose(0, 1, 3, 2),
                   layer,
                   qf_s.reshape(db, F_HEADS, F_DH), kf_s.reshape(db, F_HEADS, F_DH),
                   vf_s.reshape(db, F_HEADS, F_DH), sm_s.reshape(db, 1, LANES))
    if _mlstm_fits_decode(db, batch, seq_len):
        yf_s, ym, c_p, nm_p = _fox_decode(*decode_args, mlstm=(qkv, sm, smt, sig, hw, batch))
    else:
        yf_s = _fox_decode(*decode_args)
        ym, c_p, nm_p = _mlstm_prompt(qkv, sm, smt, sig, hw, batch)
    yf_s = yf_s.reshape(db, F_W)

    ym_s, c_s, n_s, m_s = _mlstm_step(qkv_s, sm_s, sig_s, hw, state_C[layer], state_n[layer], state_m[layer])
    x1_s, h2_s = _merge(ym_s, yf_s, sig_s, xs, mod_s, n2w, wts, db)
    prev_s = state_conv[layer].transpose(1, 0, 2)
    ys, a_s, wa_t, wg_t, w2_b = _ffn(h2_s, x1_s, mod_s, prev_s, wts, db, seq=False)
    wts_ffn = {"wa_t": wa_t, "wg_t": wg_t, "w2": w2_b, "cw": wts["cw"], "cb": wts["cb"]}

    yf = _fox_prompt(qf, kfb, vfb, fcol, batch)
    x1, h2 = _merge(ym, yf, sig, xp, mod_p, n2w, wts, min(512, seq_len))
    conv0 = jnp.zeros((batch, SUBLANES, dff), F32)
    yp, a_last = _ffn(h2, x1, mod_p, conv0, wts_ffn, tm_p, seq=True)

    y_prompt = yp.reshape(batch, seq_len, d)
    k_prompt = kf.reshape(depth, batch, seq_len, F_HEADS, F_DH)
    v_prompt = vf.reshape(depth, batch, seq_len, F_HEADS, F_DH)
    logf_prompt = sm[:, LFF0:LFF0 + F_HEADS].reshape(depth, batch, seq_len, F_HEADS)
    c_prompt_out = c_p[None]
    n_prompt = nm_p[:, :M_HEADS, :][None]
    m_prompt = nm_p[:, M_HEADS:, 0][None]
    conv_prompt = a_last[:, -1, SUBLANES - (CONV_W - 1):, :][None]

    y_sample = ys.reshape(db, 1, d)
    k_sample = kf_s.reshape(depth, db, 1, F_HEADS, F_DH)
    v_sample = vf_s.reshape(depth, db, 1, F_HEADS, F_DH)
    logf_sample = sm_s[:, LFF0:LFF0 + F_HEADS].reshape(depth, db, 1, F_HEADS)
    conv_sample = jnp.stack([state_conv[layer][:, CONV_W - 2, :], a_s], axis=1)[None]

    return (y_prompt, y_sample, k_prompt, v_prompt, logf_prompt, c_prompt_out, n_prompt, m_prompt, conv_prompt,
            k_sample, v_sample, logf_sample, c_s[None], n_s[None], m_s[None], conv_sample)
```

```python
import functools

import jax
import jax.numpy as jnp
from jax import lax
from jax.experimental import pallas as pl
from jax.experimental.pallas import tpu as pltpu

F32 = jnp.float32
BF16 = jnp.bfloat16

NORM_EPS = 1e-6
M_HEADS, M_DK, M_DV = 4, 128, 256
F_HEADS, F_DH = 8, 128
M_QK = M_HEADS * M_DK
M_V = M_HEADS * M_DV
F_W = F_HEADS * F_DH
CONV_W = 3
CHUNK = 128
LANES = 128
SUBLANES = 8
SMALL_ROWS = 16
VMEM_LIMIT = 60 * 1024 * 1024

IG0, LFM0, LFF0 = 0, M_HEADS, 2 * M_HEADS

LOG2E = 1.4426950408889634
FOX_Q_SCALE = F_DH ** -0.5 * LOG2E


def _dot(a, b):
    return jnp.dot(a, b, preferred_element_type=F32)


def _dot_nt(a, b):
    return lax.dot_general(a, b, (((1,), (1,)), ((), ())), preferred_element_type=F32)


def _dot_tn(a, b):
    return lax.dot_general(a, b, (((0,), (0,)), ((), ())), preferred_element_type=F32)


def _split3(x):
    hi = x.astype(BF16)
    r1 = x - hi.astype(F32)
    mid = r1.astype(BF16)
    lo = (r1 - mid.astype(F32)).astype(BF16)
    return hi, mid, lo


def _dot_f32_lhs(x, mat01):
    hi, mid, lo = _split3(x)
    return _dot(hi, mat01) + _dot(mid, mat01) + _dot(lo, mat01)


def _dot_f32_rhs(mat01, x):
    hi, mid, lo = _split3(x)
    return _dot(mat01, hi) + _dot(mat01, mid) + _dot(mat01, lo)


def _log_sigmoid(x):
    return jnp.minimum(x, 0.0) - jnp.log1p(jnp.exp(-jnp.abs(x)))


def _tri01(n, kind):
    r = lax.broadcasted_iota(jnp.int32, (n, n), 0)
    c = lax.broadcasted_iota(jnp.int32, (n, n), 1)
    m = c <= r if kind == "lower" else r <= c
    return jnp.where(m, 1.0, 0.0).astype(BF16)


def _params(*sem):
    return pltpu.CompilerParams(dimension_semantics=sem, vmem_limit_bytes=VMEM_LIMIT)


def _ada_kernel(cp_ref, cs_ref, w_ref, b_ref, op_ref, os_ref):
    w = w_ref[...].astype(BF16)
    for c_ref, o_ref in ((cp_ref, op_ref), (cs_ref, os_ref)):
        c = c_ref[...]
        s = (c * jax.nn.sigmoid(c)).astype(BF16)
        o_ref[...] = _dot(s, w) + b_ref[...]


def _ada(c_p, c_s, w_ada, b_ada, tn=1024):
    d = c_p.shape[1]
    n = w_ada.shape[1]
    return pl.pallas_call(
        _ada_kernel,
        grid=(n // tn,),
        in_specs=[pl.BlockSpec(c_p.shape, lambda j: (0, 0)),
                  pl.BlockSpec(c_s.shape, lambda j: (0, 0)),
                  pl.BlockSpec((d, tn), lambda j: (0, j)),
                  pl.BlockSpec((1, tn), lambda j: (0, j))],
        out_specs=[pl.BlockSpec((c_p.shape[0], tn), lambda j: (0, j)),
                   pl.BlockSpec((c_s.shape[0], tn), lambda j: (0, j))],
        out_shape=[jax.ShapeDtypeStruct((c_p.shape[0], n), F32), jax.ShapeDtypeStruct((c_s.shape[0], n), F32)],
        compiler_params=_params("arbitrary"),
        name="ada",
    )(c_p, c_s, w_ada, b_ada.reshape(1, n))


MOD_SH1, MOD_SC1, MOD_G1, MOD_SH2, MOD_SC2, MOD_G2 = range(6)


PROJ_TN = 1024
_J_MIX, _J_GATE, _J_FQ, _J_FK, _J_FV, _J_END = 0, 2, 7, 8, 9, 10
_J_MO = _J_GATE + 4
_SPLIT_SIZES = (M_QK, M_QK, M_V, M_V, M_HEADS, M_HEADS, F_W, F_W, F_W, F_HEADS)
_SPLIT_OFFS = tuple(sum(_SPLIT_SIZES[:i]) for i in range(len(_SPLIT_SIZES) + 1))
_ROW_MO, _ROW_SMALL_M, _ROW_FQ, _ROW_SMALL_F, _ROW_GATES = (_SPLIT_OFFS[3], _SPLIT_OFFS[4], _SPLIT_OFFS[6],
                                                            _SPLIT_OFFS[9], _SPLIT_OFFS[10])
_TILE_RUNS = ((_J_MIX, 0), (_J_GATE, _ROW_GATES), (_J_MO, _ROW_MO), (_J_FQ, _ROW_FQ))


def _tile_row(jj):
    row = _TILE_RUNS[0][1] + PROJ_TN * (jj - _TILE_RUNS[0][0])
    for t0, r0 in _TILE_RUNS[1:]:
        row = jnp.where(jj >= t0, r0 + PROJ_TN * (jj - t0), row)
    return row


def _norm_kernel(x_ref, sc_ref, sh_ref, n1w_ref, wsm_m_ref, wsm_f_ref, bsm_ref, h_ref, sm_ref, smt_ref):
    x = x_ref[...]
    rstd = lax.rsqrt(jnp.mean(x * x, axis=-1, keepdims=True) + NORM_EPS)
    h = (x * rstd * n1w_ref[...]) * (1.0 + sc_ref[...]) + sh_ref[...]
    hb = h.astype(BF16)
    h_ref[...] = hb
    wsm = jnp.concatenate([wsm_m_ref[...], wsm_f_ref[...]], axis=0).astype(BF16)
    smt = _dot_nt(wsm, hb) + bsm_ref[...]
    row = lax.broadcasted_iota(jnp.int32, smt.shape, 0)
    smt = jnp.where(row < LFM0, smt, _log_sigmoid(smt))
    smt_ref[...] = smt
    pad = jnp.zeros((LANES - SMALL_ROWS, smt.shape[1]), F32)
    sm_ref[...] = jnp.concatenate([smt, pad], axis=0).T


def _proj_kernel(h_ref, hs_ref, w_ref, b_ref, *rest, kind, scale, emit_f32):
    *rest, wb_ref = rest
    nw_ref = None
    if kind == "headnorm":
        nw_ref, *rest = rest
    n_out = len(rest) // 2
    i = pl.program_id(1)

    @pl.when(i == 0)
    def _():
        wb_ref[...] = w_ref[...].astype(BF16)

    def project(rows_ref, outs):
        acc = _dot_nt(rows_ref[...], wb_ref[...]) + b_ref[...]
        if kind == "mix":
            lane = lax.broadcasted_iota(jnp.int32, (1, acc.shape[1]), 1)
            is_mk = jnp.logical_and(pl.program_id(0) == 0, lane >= M_QK)
            outs[0][...] = (acc * jnp.where(is_mk, M_DK ** -0.5, 1.0)).astype(BF16)
        elif kind == "sigmoid":
            outs[0][...] = jax.nn.sigmoid(acc).astype(BF16)
        elif kind == "headnorm":
            for h in range(acc.shape[1] // F_DH):
                sl = acc[:, h * F_DH:(h + 1) * F_DH]
                ms = jnp.mean(sl * sl, axis=-1, keepdims=True)
                y = sl * lax.rsqrt(ms + NORM_EPS) * nw_ref[...]
                if emit_f32:
                    outs[0][:, h * F_DH:(h + 1) * F_DH] = y
                outs[-1][:, h * F_DH:(h + 1) * F_DH] = (y * scale).astype(BF16)
        else:
            outs[0][...] = acc
            outs[1][...] = acc.astype(BF16)

    @pl.when(i == 0)
    def _():
        project(hs_ref, rest[n_out:])

    @pl.when(i > 0)
    def _():
        project(h_ref, rest[:n_out])


def _feature_rows_spec(rows, d, first_row):
    def index_map(*g):
        return pl.multiple_of(jnp.asarray(first_row(*g), jnp.int32), SUBLANES), 0

    return pl.BlockSpec((pl.Element(rows), pl.Element(d)), index_map)


def _mod_spec(mod, d, chunk, group_of):
    return pl.BlockSpec((None, mod.shape[1], d), lambda *g: (group_of(*g), 0, chunk))


def _norm(x2, mod, n1w, wts, tm):
    m, d = x2.shape
    tiles_per_group = (m // tm) // mod.shape[0]
    group_of = lambda i: i // tiles_per_group
    return pl.pallas_call(
        _norm_kernel,
        grid=(m // tm,),
        in_specs=[pl.BlockSpec((tm, d), lambda i: (i, 0)),
                  _mod_spec(mod, d, MOD_SC1, group_of),
                  _mod_spec(mod, d, MOD_SH1, group_of),
                  pl.BlockSpec((1, d), lambda i: (0, 0)),
                  _feature_rows_spec(SUBLANES, d, lambda i: _ROW_SMALL_M),
                  _feature_rows_spec(SUBLANES, d, lambda i: _ROW_SMALL_F),
                  pl.BlockSpec((SMALL_ROWS, 1), lambda i: (0, 0))],
        out_specs=[pl.BlockSpec((tm, d), lambda i: (i, 0)),
                   pl.BlockSpec((tm, LANES), lambda i: (i, 0)),
                   pl.BlockSpec((SMALL_ROWS, tm), lambda i: (0, i))],
        out_shape=[jax.ShapeDtypeStruct((m, d), BF16),
                   jax.ShapeDtypeStruct((m, LANES), F32),
                   jax.ShapeDtypeStruct((SMALL_ROWS, m), F32)],
        compiler_params=_params("arbitrary"),
        name="norm",
    )(x2, mod, mod, n1w, wts["w_in_t"], wts["w_in_t"], wts["bsm"])


def _proj(h, hs, wts, j0, j1, kind, tm, nw=None, scale=1.0, emit_f32=False):
    m, d = h.shape
    ms = hs.shape[0]
    n_row_tiles = m // tm
    tn = PROJ_TN
    width = (j1 - j0) * tn
    prompt_row = lambda i: jnp.maximum(i - 1, 0)
    in_specs = [pl.BlockSpec((tm, d), lambda j, i: (prompt_row(i), 0)),
                pl.BlockSpec((ms, d), lambda j, i: (0, 0)),
                _feature_rows_spec(tn, d, lambda j, i: _tile_row(j0 + j)),
                pl.BlockSpec((1, tn), lambda j, i: (0, j0 + j))]
    args = [h, hs, wts["w_in_t"], wts["bbig"]]
    if kind == "headnorm":
        in_specs.append(pl.BlockSpec((1, F_DH), lambda j, i: (0, 0)))
        args.append(nw)
    dtypes = ([F32] if emit_f32 or kind == "copy" else []) + [BF16]
    n_out = len(dtypes)
    out_shape = ([jax.ShapeDtypeStruct((m, width), t) for t in dtypes]
                 + [jax.ShapeDtypeStruct((ms, width), t) for t in dtypes])
    out_specs = ([pl.BlockSpec((tm, tn), lambda j, i: (prompt_row(i), j))] * n_out
                 + [pl.BlockSpec((ms, tn), lambda j, i: (0, j))] * n_out)
    outs = pl.pallas_call(
        functools.partial(_proj_kernel, kind=kind, scale=scale, emit_f32=emit_f32),
        grid=(j1 - j0, n_row_tiles + 1),
        in_specs=in_specs,
        out_specs=out_specs,
        out_shape=out_shape,
        scratch_shapes=[pltpu.VMEM((tn, d), BF16)],
        compiler_params=_params("arbitrary", "arbitrary"),
        name="proj_" + kind,
    )(*args)
    return outs[:n_out], outs[n_out:]


def _cumsum_kernel(sm_ref, fcol_ref, carry_ref):
    c = pl.program_id(1)
    tc = sm_ref.shape[0]

    @pl.when(c == 0)
    def _():
        carry_ref[...] = jnp.zeros_like(carry_ref)

    fc = _dot_f32_rhs(_tri01(tc, "lower"), sm_ref[...]) + carry_ref[0:1, :]
    fcol_ref[...] = fc
    carry_ref[...] = jnp.broadcast_to(fc[tc - 1:tc, :], carry_ref.shape)


def _cumsum(sm, batch, tc=512):
    m = sm.shape[0]
    nt = (m // batch) // tc
    return pl.pallas_call(
        _cumsum_kernel,
        grid=(batch, nt),
        in_specs=[pl.BlockSpec((tc, LANES), lambda b, c: (b * nt + c, 0))],
        out_specs=pl.BlockSpec((tc, LANES), lambda b, c: (b * nt + c, 0)),
        out_shape=jax.ShapeDtypeStruct((m, LANES), F32),
        scratch_shapes=[pltpu.VMEM((SUBLANES, LANES), F32)],
        compiler_params=_params("arbitrary", "arbitrary"),
        name="cumsum",
    )(sm)


def _mlstm_kernel(*refs):
    _mlstm_chunk(pl.program_id(1), pl.num_programs(1) - 1, range(M_HEADS), *refs)


def _mlstm_chunk(c, last_c, heads, q_ref, k_ref, v_ref, sm_ref, smt_ref, og_ref, hw_ref,
                 ym_ref, cout_ref, nm_ref, c_ref, n_ref, m_ref):
    L = q_ref.shape[0]

    @pl.when(c == 0)
    def _():
        for h in heads:
            c_ref[h] = jnp.zeros((M_DV, M_DK), F32)
            n_ref[h:h + 1, :] = jnp.zeros((1, M_DK), F32)
            m_ref[h:h + 1, :] = jnp.zeros((1, LANES), F32)

    r = lax.broadcasted_iota(jnp.int32, (L, L), 0)
    s_idx = lax.broadcasted_iota(jnp.int32, (L, L), 1)
    causal = s_idx <= r
    sm = sm_ref[...]
    smt = smt_ref[...]
    bcol_all = _dot_f32_rhs(_tri01(L, "lower"), sm)
    brow_all = _dot_f32_lhs(smt, _tri01(L, "upper"))

    for h in heads:
        ig_col = sm[:, IG0 + h:IG0 + h + 1]
        b_col = bcol_all[:, LFM0 + h:LFM0 + h + 1]
        ig_row = smt[IG0 + h:IG0 + h + 1, :]
        b_row = brow_all[LFM0 + h:LFM0 + h + 1, :]
        m0 = m_ref[h:h + 1, 0:1]
        a_row = ig_row - b_row
        cm_col = jnp.max(jnp.where(causal, a_row, -jnp.inf), axis=1, keepdims=True)
        mx_col = jnp.maximum(m0, cm_col)
        m_col = b_col + mx_col
        dmat = jnp.exp(jnp.where(causal, a_row - mx_col, -jnp.inf))
        inter = jnp.exp(m0 - mx_col)
        q = q_ref[:, h * M_DK:(h + 1) * M_DK]
        k = k_ref[:, h * M_DK:(h + 1) * M_DK]
        v = v_ref[:, h * M_DV:(h + 1) * M_DV]
        c0 = c_ref[h]
        n0 = n_ref[h:h + 1, :]
        s = _dot_nt(q, k) * dmat
        num = _dot(s.astype(BF16), v) + inter * _dot_nt(q, c0.astype(BF16))
        den = jnp.sum(s, axis=1, keepdims=True) + inter * jnp.sum(q.astype(F32) * n0, axis=1, keepdims=True)
        hh = num / jnp.maximum(jnp.abs(den), jnp.exp(-m_col))
        hn = hh * lax.rsqrt(jnp.mean(hh * hh, axis=1, keepdims=True) + NORM_EPS) * hw_ref[:, h * M_DV:(h + 1) * M_DV]
        ym_ref[:, h * M_DV:(h + 1) * M_DV] = (og_ref[:, h * M_DV:(h + 1) * M_DV] * hn).astype(BF16)
        m_new = m_col[L - 1:L, :]
        b_last = b_col[L - 1:L, :]
        w_end = jnp.exp(ig_col + b_last - b_col - m_new)
        decay = jnp.exp(b_last + m0 - m_new)
        vw = (w_end * v.astype(F32)).astype(BF16)
        c_ref[h] = decay * c0 + _dot_tn(vw, k)
        n_ref[h:h + 1, :] = decay * n0 + jnp.sum(w_end * k.astype(F32), axis=0, keepdims=True)
        m_ref[h:h + 1, :] = jnp.broadcast_to(m_new, (1, LANES))

    @pl.when(c == last_c)
    def _():
        for h in heads:
            cout_ref[h] = c_ref[h]
            nm_ref[h:h + 1, :] = n_ref[h:h + 1, :]
            nm_ref[M_HEADS + h:M_HEADS + h + 1, :] = m_ref[h:h + 1, :]


def _mlstm_prompt(qkv, sm, smt, sig, hw, batch):
    m = qkv.shape[0]
    og_block = sig.shape[1] // M_V - 1
    nc = (m // batch) // CHUNK
    row = lambda b, c: (b * nc + c, 0)
    return pl.pallas_call(
        _mlstm_kernel,
        grid=(batch, nc),
        in_specs=[pl.BlockSpec((CHUNK, M_QK), row),
                  pl.BlockSpec((CHUNK, M_QK), lambda b, c: (b * nc + c, 1)),
                  pl.BlockSpec((CHUNK, M_V), lambda b, c: (b * nc + c, 2 * M_QK // M_V)),
                  pl.BlockSpec((CHUNK, LANES), row),
                  pl.BlockSpec((SMALL_ROWS, CHUNK), lambda b, c: (0, b * nc + c)),
                  pl.BlockSpec((CHUNK, M_V), lambda b, c: (b * nc + c, og_block)),
                  pl.BlockSpec((1, M_V), lambda b, c: (0, 0))],
        out_specs=[pl.BlockSpec((CHUNK, M_V), row),
                   pl.BlockSpec((None, M_HEADS, M_DV, M_DK), lambda b, c: (b, 0, 0, 0)),
                   pl.BlockSpec((None, 2 * M_HEADS, M_DK), lambda b, c: (b, 0, 0))],
        out_shape=[jax.ShapeDtypeStruct((m, M_V), BF16),
                   jax.ShapeDtypeStruct((batch, M_HEADS, M_DV, M_DK), F32),
                   jax.ShapeDtypeStruct((batch, 2 * M_HEADS, M_DK), F32)],
        scratch_shapes=[pltpu.VMEM((M_HEADS, M_DV, M_DK), F32),
                        pltpu.VMEM((SUBLANES, M_DK), F32),
                        pltpu.VMEM((SUBLANES, LANES), F32)],
        compiler_params=_params("arbitrary", "arbitrary"),
        name="mlstm",
    )(qkv, qkv, qkv, sm, smt, sig, hw)


FOX_HEADS_PER_STEP = 4
FOX_ROW_CHUNK = 32


def _forget_columns(f, first):
    hi, mid, lo = _split3(f)
    lane = lax.broadcasted_iota(jnp.int32, (f.shape[0], LANES), 1)
    ones = jnp.where(jnp.abs(lane - (4 - first)) <= 1, 1.0, 0.0)
    ext = jnp.where(lane == first, hi.astype(F32),
                    jnp.where(lane == first + 1, mid.astype(F32),
                              jnp.where(lane == first + 2, lo.astype(F32), ones)))
    return ext.astype(BF16)


def _fox_kernel(q_ref, k_ref, v_ref, fcol_ref, o_ref, qaug_ref, kaug_ref, s_ref, p_ref, m_ref, l_ref, acc_ref):
    hp = pl.program_id(1)
    qi = pl.program_id(2)
    tq = q_ref.shape[0]
    t = k_ref.shape[0]
    nh = FOX_HEADS_PER_STEP

    def forget_sum(fc, h2):
        lane = lax.broadcasted_iota(jnp.int32, fc.shape, 1)
        sel = jnp.where(lane == LFF0 + nh * hp + h2, fc, 0.0)
        return jnp.sum(sel, axis=1, keepdims=True) * LOG2E

    @pl.when(qi == 0)
    def _():
        fc = fcol_ref[...]
        for h2 in range(nh):
            kaug_ref[h2, :, 0:F_DH] = k_ref[:, h2 * F_DH:(h2 + 1) * F_DH]
            kaug_ref[h2, :, F_DH:2 * F_DH] = _forget_columns(-forget_sum(fc, h2), 3)

    fq = fcol_ref[pl.ds(pl.multiple_of(qi * tq, tq), tq), :]
    for h2 in range(nh):
        qaug_ref[h2, :, 0:F_DH] = q_ref[:, h2 * F_DH:(h2 + 1) * F_DH]
        qaug_ref[h2, :, F_DH:2 * F_DH] = _forget_columns(forget_sum(fq, h2), 0)
    m_ref[...] = jnp.full_like(m_ref, -jnp.inf)
    l_ref[...] = jnp.zeros_like(l_ref)
    acc_ref[...] = jnp.zeros_like(acc_ref)
    rc = FOX_ROW_CHUNK
    n_lane_tiles = tq // LANES

    def softmax_rows(h2, c, masked):
        rows = slice(c * rc, (c + 1) * rc)
        s = s_ref[h2, rows, :]
        if masked:
            r = lax.broadcasted_iota(jnp.int32, (rc, tq), 0) + c * rc
            col = lax.broadcasted_iota(jnp.int32, (rc, tq), 1)
            s = jnp.where(r >= col, s, -jnp.inf)
        tiles = [s[:, k * LANES:(k + 1) * LANES] for k in range(n_lane_tiles)]
        m_old = m_ref[h2, rows, :]
        mx = functools.reduce(jnp.maximum, tiles)
        m_new = jnp.maximum(m_old, jnp.max(mx, axis=1, keepdims=True))
        alpha = jnp.exp2(m_old - m_new)
        ps = [jnp.exp2(tile - m_new) for tile in tiles]
        row_sum = jnp.sum(functools.reduce(jnp.add, ps), axis=1, keepdims=True)
        l_ref[h2, rows, :] = alpha * l_ref[h2, rows, :] + row_sum
        for k in range(n_lane_tiles):
            p_ref[h2, rows, k * LANES:(k + 1) * LANES] = ps[k].astype(BF16)
        acc_ref[h2, rows, :] = alpha * acc_ref[h2, rows, :]
        m_ref[h2, rows, :] = m_new

    def block(kj, masked):
        off = pl.multiple_of(kj * tq, tq)
        for h2 in range(nh):
            s_ref[h2] = _dot_nt(qaug_ref[h2], kaug_ref[h2, pl.ds(off, tq), :])
        for h2 in range(nh):
            for c in range(tq // rc):
                softmax_rows(h2, c, masked)
            acc_ref[h2] += _dot(p_ref[h2], v_ref[pl.ds(off, tq), h2 * F_DH:(h2 + 1) * F_DH])

    def body(kj, carry):
        block(kj, False)
        return carry

    lax.fori_loop(0, qi, body, 0)
    block(qi, True)
    for h2 in range(nh):
        o_ref[:, h2 * F_DH:(h2 + 1) * F_DH] = (acc_ref[h2] / l_ref[h2]).astype(BF16)


def _fox_prompt(qf, kfb, vfb, fcol, batch, tq=512):
    m = qf.shape[0]
    t = m // batch
    nq = t // tq
    nh = FOX_HEADS_PER_STEP
    wide = nh * F_DH
    return pl.pallas_call(
        _fox_kernel,
        grid=(batch, F_HEADS // nh, nq),
        in_specs=[pl.BlockSpec((tq, wide), lambda b, h, i: (b * nq + i, h)),
                  pl.BlockSpec((t, wide), lambda b, h, i: (b, h)),
                  pl.BlockSpec((t, wide), lambda b, h, i: (b, h)),
                  pl.BlockSpec((t, LANES), lambda b, h, i: (b, 0))],
        out_specs=pl.BlockSpec((tq, wide), lambda b, h, i: (b * nq + i, h)),
        out_shape=jax.ShapeDtypeStruct((m, F_W), BF16),
        scratch_shapes=[pltpu.VMEM((nh, tq, 2 * F_DH), BF16),
                        pltpu.VMEM((nh, t, 2 * F_DH), BF16),
                        pltpu.VMEM((nh, tq, tq), F32),
                        pltpu.VMEM((nh, tq, tq), BF16),
                        pltpu.VMEM((nh, tq, LANES), F32),
                        pltpu.VMEM((nh, tq, LANES), F32),
                        pltpu.VMEM((nh, tq, F_DH), F32)],
        compiler_params=_params("arbitrary", "arbitrary", "arbitrary"),
        name="fox",
    )(qf, kfb, vfb, fcol)


MSTEP_ROWS = 8


def _mstep_kernel(q_ref, k_ref, v_ref, sm_ref, og_ref, hw_ref, c0_ref, n0_ref, m0_ref,
                  ym_ref, cout_ref, nout_ref, mout_ref, cq_ref):
    rows = q_ref.shape[0]
    sm = sm_ref[...]
    ig = sm[:, IG0:IG0 + M_HEADS]
    lf = sm[:, LFM0:LFM0 + M_HEADS]
    m0 = m0_ref[...]
    m_new = lf + jnp.maximum(m0, ig - lf)
    w_end = jnp.exp(ig - m_new)
    decay = jnp.exp(lf + m0 - m_new)
    mout_ref[...] = m_new
    row_id = lax.broadcasted_iota(jnp.int32, (SUBLANES, M_DV), 0)

    for h in range(M_HEADS):
        qh = q_ref[:, h * M_DK:(h + 1) * M_DK]
        kh = k_ref[:, h * M_DK:(h + 1) * M_DK]
        vh = v_ref[:, h * M_DV:(h + 1) * M_DV]
        qf = qh.astype(F32)
        kf = kh.astype(F32)
        vf = vh.astype(F32)
        wh = w_end[:, h:h + 1]
        dh = decay[:, h:h + 1]
        for r in range(rows):
            c0 = c0_ref[r, h]
            q8 = jnp.broadcast_to(qh[r:r + 1, :], (SUBLANES, M_DK))
            cq_ref[r:r + 1, :] = _dot_nt(q8, c0.astype(BF16))[0:1, :]
            vw8 = jnp.where(row_id == 0, jnp.broadcast_to(wh[r:r + 1, :] * vf[r:r + 1, :], (SUBLANES, M_DV)), 0.0)
            k8 = jnp.broadcast_to(kh[r:r + 1, :], (SUBLANES, M_DK))
            cout_ref[r, h] = dh[r:r + 1, :] * c0 + _dot_tn(vw8.astype(BF16), k8)
        n0 = n0_ref[:, h, :]
        s = jnp.sum(qf * kf, axis=1, keepdims=True) * wh
        num = s * vf + dh * cq_ref[...]
        den = s + dh * jnp.sum(n0 * qf, axis=1, keepdims=True)
        hh = num / jnp.maximum(jnp.abs(den), jnp.exp(-m_new[:, h:h + 1]))
        hn = hh * lax.rsqrt(jnp.mean(hh * hh, axis=1, keepdims=True) + NORM_EPS) * hw_ref[:, h * M_DV:(h + 1) * M_DV]
        ym_ref[:, h * M_DV:(h + 1) * M_DV] = (og_ref[:, h * M_DV:(h + 1) * M_DV] * hn).astype(BF16)
        nout_ref[:, h, :] = dh * n0 + wh * kf


def _mlstm_step(qkv, sm, sig, hw, c0, n0, m0):
    db = qkv.shape[0]
    og_block = sig.shape[1] // M_V - 1
    rows = MSTEP_ROWS
    row = lambda i: (i, 0)
    return pl.pallas_call(
        _mstep_kernel,
        grid=(db // rows,),
        in_specs=[pl.BlockSpec((rows, M_QK), row),
                  pl.BlockSpec((rows, M_QK), lambda i: (i, 1)),
                  pl.BlockSpec((rows, M_V), lambda i: (i, 2 * M_QK // M_V)),
                  pl.BlockSpec((rows, LANES), row),
                  pl.BlockSpec((rows, M_V), lambda i: (i, og_block)),
                  pl.BlockSpec((1, M_V), lambda i: (0, 0)),
                  pl.BlockSpec((rows, M_HEADS, M_DV, M_DK), lambda i: (i, 0, 0, 0)),
                  pl.BlockSpec((rows, M_HEADS, M_DK), lambda i: (i, 0, 0)),
                  pl.BlockSpec((rows, M_HEADS), row)],
        out_specs=[pl.BlockSpec((rows, M_V), row),
                   pl.BlockSpec((rows, M_HEADS, M_DV, M_DK), lambda i: (i, 0, 0, 0)),
                   pl.BlockSpec((rows, M_HEADS, M_DK), lambda i: (i, 0, 0)),
                   pl.BlockSpec((rows, M_HEADS), row)],
        out_shape=[jax.ShapeDtypeStruct((db, M_V), BF16),
                   jax.ShapeDtypeStruct(c0.shape, F32),
                   jax.ShapeDtypeStruct(n0.shape, F32),
                   jax.ShapeDtypeStruct(m0.shape, F32)],
        scratch_shapes=[pltpu.VMEM((rows, M_DV), F32)],
        compiler_params=_params("arbitrary"),
        name="mstep",
    )(qkv, qkv, qkv, sm, sig, hw, c0, n0, m0)


N_MLSTM_IN = 7
N_MLSTM_OUT = 3
N_DECODE_SCRATCH = 4


def _decode_kernel(*refs, n_pages, mlstm_steps_per_chunk, mlstm_chunks):
    n_dec_in = 3 * n_pages + 4
    n_ml = N_MLSTM_IN if mlstm_steps_per_chunk else 0
    dec_in = refs[1:1 + n_dec_in]
    ml_in = refs[1 + n_dec_in:1 + n_dec_in + n_ml]
    outs = refs[1 + n_dec_in + n_ml:]
    o_ref, outs = outs[0], outs[1:]
    if mlstm_steps_per_chunk:
        ml_out, outs = outs[:N_MLSTM_OUT], outs[N_MLSTM_OUT:]
        ml_scratch = outs[N_DECODE_SCRATCH:]
        step = pl.program_id(0)
        chunk = (step // mlstm_steps_per_chunk) % mlstm_chunks
        heads_per_step = M_HEADS // mlstm_steps_per_chunk
        for g in range(mlstm_steps_per_chunk):
            @pl.when(step % mlstm_steps_per_chunk == g)
            def _(g=g):
                _mlstm_chunk(chunk, mlstm_chunks - 1, range(g * heads_per_step, (g + 1) * heads_per_step),
                             *ml_in, *ml_out, *ml_scratch)
    _decode_sequence(dec_in[:n_pages], dec_in[n_pages:2 * n_pages], dec_in[2 * n_pages:3 * n_pages],
                     *dec_in[3 * n_pages:], o_ref, *outs[:N_DECODE_SCRATCH])


def _decode_sequence(k_refs, v_refs, lf_refs, q_ref, kn_ref, vn_ref, sm_ref, o_ref,
                     s_ref, xn_ref, widen_ref, spread_ref):
    n_pages = len(k_refs)
    page = xn_ref.shape[1]
    pw = page * F_HEADS

    for p in range(n_pages):
        xn_ref[p * F_HEADS:(p + 1) * F_HEADS, :] = lf_refs[p][...]
    xn = xn_ref[...] * LOG2E

    @pl.when(pl.program_id(0) == 0)
    def _():
        t_id = lax.broadcasted_iota(jnp.int32, (page, pw), 0)
        l_id = lax.broadcasted_iota(jnp.int32, (page, pw), 1)
        widen_ref[...] = jnp.where((l_id // F_HEADS) == t_id, 1.0, 0.0).astype(BF16)
        r = lax.broadcasted_iota(jnp.int32, (LANES, pw), 0)
        c = lax.broadcasted_iota(jnp.int32, (LANES, pw), 1)
        spread_ref[...] = jnp.where(r == LFF0 + (c & (F_HEADS - 1)), 1.0, 0.0).astype(BF16)

    wide = _dot_f32_lhs(xn, widen_ref[...])
    r_id = lax.broadcasted_iota(jnp.int32, wide.shape, 0)
    c_id = lax.broadcasted_iota(jnp.int32, wide.shape, 1)
    own = jnp.where((r_id & (F_HEADS - 1)) == (c_id & (F_HEADS - 1)), wide, 0.0)
    g_p = lax.broadcasted_iota(jnp.int32, (n_pages, n_pages * F_HEADS), 0)
    g_r = lax.broadcasted_iota(jnp.int32, (n_pages, n_pages * F_HEADS), 1)
    gather = jnp.where((g_r // F_HEADS) == g_p, 1.0, 0.0).astype(BF16)
    x = _dot_f32_rhs(gather, own)
    lane = lax.broadcasted_iota(jnp.int32, x.shape, 1)
    y, z = x, x
    step = F_HEADS
    while step < pw:
        y = y + jnp.where(lane < pw - step, pltpu.roll(y, pw - step, axis=1), 0.0)
        z = z + pltpu.roll(z, step, axis=1)
        step *= 2
    after = y - x

    run = _dot_f32_lhs(jnp.broadcast_to(sm_ref[...] * LOG2E, (SUBLANES, LANES)), spread_ref[...])[0:1, :]

    q = q_ref[...]
    sub = lax.broadcasted_iota(jnp.int32, (F_HEADS, pw), 0)
    lane_h = lax.broadcasted_iota(jnp.int32, (F_HEADS, pw), 1) & (F_HEADS - 1)
    valid = lane_h == sub
    s_self = jnp.sum(q.astype(F32) * kn_ref[...], axis=1, keepdims=True)
    m_run = s_self
    for p in reversed(range(n_pages)):
        kb = k_refs[p][...].astype(BF16)
        st = _dot_nt(q, kb) + (run + after[p:p + 1, :])
        st = jnp.where(valid, st, -jnp.inf)
        s_ref[p] = st
        m_run = jnp.maximum(m_run, jnp.max(st, axis=1, keepdims=True))
        run = run + z[p:p + 1, :]
    w_self = jnp.exp2(s_self - m_run)
    l = w_self
    acc = w_self * vn_ref[...]
    for p in range(n_pages):
        pe = jnp.exp2(s_ref[p] - m_run)
        l = l + jnp.sum(pe, axis=1, keepdims=True)
        acc = acc + _dot(pe.astype(BF16), v_refs[p][...].astype(BF16))
    o_ref[...] = (acc / l).astype(BF16)


def _mlstm_fits_decode(db, batch, seq_len):
    chunk_steps = batch * (seq_len // CHUNK)
    return db % chunk_steps == 0 and db // chunk_steps in (1, 2, 4)


def _fox_decode(page_table, ck, cv, clf, layer, q3, k3, v3, sm3, mlstm=None):
    db, n_pages = page_table.shape
    pw = ck.shape[2]
    page = clf.shape[3]
    seq = lambda b, pt: (b, 0, 0)

    def pg(p):
        return lambda b, pt: (layer, pt[b, p], 0, 0)

    in_specs = ([pl.BlockSpec((None, None, pw, F_DH), pg(p)) for p in range(n_pages)]
                + [pl.BlockSpec((None, None, pw, F_DH), pg(p)) for p in range(n_pages)]
                + [pl.BlockSpec((None, None, F_HEADS, page), pg(p)) for p in range(n_pages)]
                + [pl.BlockSpec((None, F_HEADS, F_DH), seq),
                   pl.BlockSpec((None, F_HEADS, F_DH), seq),
                   pl.BlockSpec((None, F_HEADS, F_DH), seq),
                   pl.BlockSpec((None, 1, LANES), seq)])
    args = [page_table, *([ck] * n_pages), *([cv] * n_pages), *([clf] * n_pages), q3, k3, v3, sm3]
    out_specs = [pl.BlockSpec((None, F_HEADS, F_DH), seq)]
    out_shape = [jax.ShapeDtypeStruct((db, F_HEADS, F_DH), BF16)]
    scratch = [pltpu.VMEM((n_pages, F_HEADS, pw), F32), pltpu.VMEM((n_pages * F_HEADS, page), F32),
               pltpu.VMEM((page, pw), BF16), pltpu.VMEM((LANES, pw), BF16)]
    per_chunk, nc = 0, 0
    if mlstm is not None:
        qkv, sm, smt, sig, hw, batch = mlstm
        m = qkv.shape[0]
        nc = (m // batch) // CHUNK
        per_chunk = db // (batch * nc)
        og_block = sig.shape[1] // M_V - 1
        row = lambda s, pt: (s // per_chunk, 0)
        in_specs += [pl.BlockSpec((CHUNK, M_QK), row),
                     pl.BlockSpec((CHUNK, M_QK), lambda s, pt: (s // per_chunk, 1)),
                     pl.BlockSpec((CHUNK, M_V), lambda s, pt: (s // per_chunk, 2 * M_QK // M_V)),
                     pl.BlockSpec((CHUNK, LANES), row),
                     pl.BlockSpec((SMALL_ROWS, CHUNK), lambda s, pt: (0, s // per_chunk)),
                     pl.BlockSpec((CHUNK, M_V), lambda s, pt: (s // per_chunk, og_block)),
                     pl.BlockSpec((1, M_V), lambda s, pt: (0, 0))]
        args += [qkv, qkv, qkv, sm, smt, sig, hw]
        seq_of = lambda s, pt: s // (per_chunk * nc)
        out_specs += [pl.BlockSpec((CHUNK, M_V), row),
                      pl.BlockSpec((None, M_HEADS, M_DV, M_DK), lambda s, pt: (seq_of(s, pt), 0, 0, 0)),
                      pl.BlockSpec((None, 2 * M_HEADS, M_DK), lambda s, pt: (seq_of(s, pt), 0, 0))]
        out_shape += [jax.ShapeDtypeStruct((m, M_V), BF16),
                      jax.ShapeDtypeStruct((batch, M_HEADS, M_DV, M_DK), F32),
                      jax.ShapeDtypeStruct((batch, 2 * M_HEADS, M_DK), F32)]
        scratch += [pltpu.VMEM((M_HEADS, M_DV, M_DK), F32),
                    pltpu.VMEM((SUBLANES, M_DK), F32),
                    pltpu.VMEM((SUBLANES, LANES), F32)]
    grid_spec = pltpu.PrefetchScalarGridSpec(
        num_scalar_prefetch=1,
        grid=(db,),
        in_specs=in_specs,
        out_specs=out_specs,
        scratch_shapes=scratch,
    )
    outs = pl.pallas_call(
        functools.partial(_decode_kernel, n_pages=n_pages, mlstm_steps_per_chunk=per_chunk, mlstm_chunks=nc),
        grid_spec=grid_spec,
        out_shape=out_shape,
        compiler_params=_params("arbitrary"),
        name="decode",
    )(*args)
    return outs if mlstm is not None else outs[0]


def _merge_kernel(ym_ref, yf_ref, sga_ref, sgb_ref, x_ref, g1_ref, sc_ref, sh_ref, n2w_ref,
                  wa_ref, wb_ref, wo_ref, x1_ref, h2_ref):
    merged = sga_ref[...] * _dot(ym_ref[...], wa_ref[...]) + sgb_ref[...] * _dot(yf_ref[...], wb_ref[...])
    x1 = x_ref[...] + g1_ref[...] * _dot(merged.astype(BF16), wo_ref[...])
    x1_ref[...] = x1
    rstd = lax.rsqrt(jnp.mean(x1 * x1, axis=-1, keepdims=True) + NORM_EPS)
    h2_ref[...] = ((x1 * rstd * n2w_ref[...]) * (1.0 + sc_ref[...]) + sh_ref[...]).astype(BF16)


def _merge(ym, yf, sig, x2, mod, n2w, wts, tm):
    m, d = x2.shape
    tiles_per_group = (m // tm) // mod.shape[0]
    group_of = lambda i: i // tiles_per_group
    row = lambda i: (i, 0)
    const = lambda i: (0, 0)
    resident = dict(pipeline_mode=pl.Buffered(1))
    return pl.pallas_call(
        _merge_kernel,
        grid=(m // tm,),
        in_specs=[pl.BlockSpec((tm, M_V), row),
                  pl.BlockSpec((tm, F_W), row),
                  pl.BlockSpec((tm, d), row),
                  pl.BlockSpec((tm, d), lambda i: (i, 1)),
                  pl.BlockSpec((tm, d), row),
                  _mod_spec(mod, d, MOD_G1, group_of),
                  _mod_spec(mod, d, MOD_SC2, group_of),
                  _mod_spec(mod, d, MOD_SH2, group_of),
                  pl.BlockSpec((1, d), const),
                  pl.BlockSpec((M_V, d), const, **resident),
                  pl.BlockSpec((F_W, d), const, **resident),
                  pl.BlockSpec((d, d), const, **resident)],
        out_specs=[pl.BlockSpec((tm, d), row), pl.BlockSpec((tm, d), row)],
        out_shape=[jax.ShapeDtypeStruct((m, d), F32), jax.ShapeDtypeStruct((m, d), BF16)],
        compiler_params=_params("arbitrary"),
        name="merge",
    )(ym, yf, sig, sig, x2, mod, mod, mod, n2w, wts["wa"], wts["wb"], wts["wo"])


FFN_TF = 512


def _ffn_kernel(h2_ref, x1_ref, g2_ref, p_ref, wa_ref, wg_ref, cw_ref, cb_ref, w2_ref,
                y_ref, a_out_ref, *rest, seq):
    f = pl.program_id(2)
    tm = h2_ref.shape[0]
    carry_ref = rest[-1]

    @pl.when(f == 0)
    def _():
        y_ref[...] = jnp.zeros_like(y_ref)

    if seq:
        @pl.when(pl.program_id(1) == 0)
        def _():
            carry_ref[f] = p_ref[...]
        wa, wg, w2 = wa_ref[...], wg_ref[...], w2_ref[...]
    else:
        wa, wg, w2 = wa_ref[...].astype(BF16), wg_ref[...].astype(BF16), w2_ref[...].astype(BF16)
        wa_out_ref, wg_out_ref, w2_out_ref = rest[:3]
        wa_out_ref[...] = wa
        wg_out_ref[...] = wg
        w2_out_ref[...] = w2

    h2 = h2_ref[...]
    a = _dot(h2, wa)
    g = _dot(h2, wg)
    if seq:
        prev = carry_ref[f]
        p2, p1 = prev[SUBLANES - 2:SUBLANES - 1, :], prev[SUBLANES - 1:SUBLANES, :]
        r = lax.broadcasted_iota(jnp.int32, a.shape, 0)
        a1 = jnp.where(r == 0, p1, pltpu.roll(a, 1, axis=0))
        a2 = jnp.where(r == 0, p2, jnp.where(r == 1, p1, pltpu.roll(a, 2, axis=0)))
        carry_ref[f] = a[tm - SUBLANES:tm, :]
        a_out_ref[...] = a[tm - SUBLANES:tm, :]
    else:
        a2, a1 = p_ref[0], p_ref[1]
        a_out_ref[...] = a
    ac = cw_ref[0:1, :] * a2 + cw_ref[1:2, :] * a1 + cw_ref[2:3, :] * a + cb_ref[...]
    u = (jax.nn.gelu(ac, approximate=True) * g).astype(BF16)
    y_ref[...] += _dot(u, w2)

    @pl.when(f == pl.num_programs(2) - 1)
    def _():
        y_ref[...] = x1_ref[...] + g2_ref[...] * y_ref[...]


def _ffn(h2, x1, mod, prev, wts, tm, seq):
    m, d = x1.shape
    groups = mod.shape[0]
    tiles = (m // tm) // groups
    dff = wts["w2"].shape[0]
    tf = FFN_TF
    nf = dff // tf
    row = lambda b, i, f: (b * tiles + i, 0)
    up_tile = pl.BlockSpec((None, d, tf), lambda b, i, f: (f, 0, 0))
    down_tile = pl.BlockSpec((tf, d), lambda b, i, f: (f, 0))
    out_specs = [pl.BlockSpec((tm, d), row)]
    out_shape = [jax.ShapeDtypeStruct((m, d), F32)]
    if seq:
        prev_spec = pl.BlockSpec((None, SUBLANES, tf), lambda b, i, f: (b, 0, f))
        out_specs.append(pl.BlockSpec((None, None, SUBLANES, tf), lambda b, i, f: (b, i, 0, f)))
        out_shape.append(jax.ShapeDtypeStruct((groups, tiles, SUBLANES, dff), F32))
        w_specs = [up_tile, up_tile]
        w_args = [wts["wa_t"], wts["wg_t"]]
    else:
        assert groups == 1 and tiles == 1, "the bf16 weight tiles are written once"
        prev_spec = pl.BlockSpec((CONV_W - 1, tm, tf), lambda b, i, f: (0, i, f))
        out_specs += [pl.BlockSpec((tm, tf), lambda b, i, f: (i, f)), up_tile, up_tile, down_tile]
        out_shape += [jax.ShapeDtypeStruct((m, dff), F32),
                      jax.ShapeDtypeStruct((nf, d, tf), BF16),
                      jax.ShapeDtypeStruct((nf, d, tf), BF16),
                      jax.ShapeDtypeStruct((dff, d), BF16)]
        w_specs = [pl.BlockSpec((d, tf), lambda b, i, f: (0, f)),
                   pl.BlockSpec((d, tf), lambda b, i, f: (0, f + nf))]
        w_args = [wts["w1"], wts["w1"]]
    return pl.pallas_call(
        functools.partial(_ffn_kernel, seq=seq),
        grid=(groups, tiles, nf),
        in_specs=[pl.BlockSpec((tm, d), row),
                  pl.BlockSpec((tm, d), row),
                  _mod_spec(mod, d, MOD_G2, lambda b, i, f: b),
                  prev_spec,
                  *w_specs,
                  pl.BlockSpec((CONV_W, tf), lambda b, i, f: (0, f)),
                  pl.BlockSpec((1, tf), lambda b, i, f: (0, f)),
                  down_tile],
        out_specs=out_specs,
        out_shape=out_shape,
        scratch_shapes=[pltpu.VMEM((nf, SUBLANES, tf), F32)],
        compiler_params=_params("arbitrary", "arbitrary", "arbitrary"),
        name="ffn_seq" if seq else "ffn_row",
    )(h2, x1, mod, prev, *w_args, wts["cw"], wts["cb"], wts["w2"])


def _prep_weights(w_in, b_in, f_qnorm_w, f_knorm_w, w_proj_a, w_proj_b, w_out, w_ffn_in, conv_w, conv_b, w_ffn_out):
    o = _SPLIT_OFFS
    bbig = jnp.concatenate([b_in[:o[3]], b_in[o[10]:], b_in[o[3]:o[4]], b_in[o[6]:o[9]]])
    bsm = jnp.concatenate([b_in[o[4]:o[6]], b_in[o[9]:o[10]]])
    return {
        "w_in_t": w_in.T,
        "bbig": bbig.reshape(1, -1),
        "bsm": bsm.reshape(SMALL_ROWS, 1),
        "qnw": f_qnorm_w.reshape(1, F_DH),
        "knw": f_knorm_w.reshape(1, F_DH),
        "wa": w_proj_a.astype(BF16),
        "wb": w_proj_b.astype(BF16),
        "wo": w_out.astype(BF16),
        "w1": w_ffn_in,
        "cw": conv_w,
        "cb": conv_b.reshape(1, -1),
        "w2": w_ffn_out,
    }


def kernel(x_prompt, x_sample, c_prompt, c_sample, cache_k, cache_v, cache_logf, page_table, state_C, state_n, state_m, state_conv, w_ada, b_ada, norm1_w, w_in, b_in, m_hnorm_w, f_qnorm_w, f_knorm_w, w_proj_a, w_proj_b, w_out, norm2_w, w_ffn_in, conv_w, conv_b, w_ffn_out):
    depth = w_ada.shape[0]
    assert depth == 1, "single-layer step"
    batch, seq_len, d = x_prompt.shape
    db = x_sample.shape[0]
    assert x_sample.shape[1] == 1
    dff = w_ffn_out.shape[1]
    layer = 0

    wts = _prep_weights(w_in[layer], b_in[layer], f_qnorm_w[layer], f_knorm_w[layer], w_proj_a[layer],
                        w_proj_b[layer], w_out[layer], w_ffn_in[layer], conv_w[layer], conv_b[layer],
                        w_ffn_out[layer])
    n1w = norm1_w[layer].reshape(1, d)
    n2w = norm2_w[layer].reshape(1, d)
    hw = m_hnorm_w[layer].reshape(1, M_V)

    mod_p, mod_s = _ada(c_prompt, c_sample, w_ada[layer], b_ada[layer])
    mod_p = mod_p.reshape(batch, 1, 6 * d)
    mod_s = mod_s.reshape(1, db, 6 * d)

    tm_p = min(1024, seq_len)
    xp = x_prompt.reshape(batch * seq_len, d)
    xs = x_sample.reshape(db, d)
    h, sm, smt = _norm(xp, mod_p, n1w, wts, tm_p)
    h_s, sm_s, _ = _norm(xs, mod_s, n1w, wts, db)
    tm_wide = min(2 * tm_p, seq_len)
    (qkv,), (qkv_s,) = _proj(h, h_s, wts, _J_MIX, _J_GATE, "mix", tm_wide)
    (sig,), (sig_s,) = _proj(h, h_s, wts, _J_GATE, _J_FQ, "sigmoid", tm_wide)
    (qf,), (qf_s,) = _proj(h, h_s, wts, _J_FQ, _J_FK, "headnorm", tm_p, nw=wts["qnw"], scale=FOX_Q_SCALE)
    (kf, kfb), (kf_s, _) = _proj(h, h_s, wts, _J_FK, _J_FV, "headnorm", tm_p, nw=wts["knw"], emit_f32=True)
    (vf, vfb), (vf_s, _) = _proj(h, h_s, wts, _J_FV, _J_END, "copy", tm_p)
    fcol = _cumsum(sm, batch)

    n_pool, page = cache_k.shape[1], cache_k.shape[2]
    decode_args = (page_table,
                   cache_k.reshape(depth, n_pool, page * F_HEADS, F_DH),
                   cache_v.reshape(depth, n_pool, page * F_HEADS, F_DH),
                   cache_logf.transpose(0, 1, 3, 2),
                   layer,
                   qf_s.reshape(db, F_HEADS, F_DH), kf_s.reshape(db, F_HEADS, F_DH),
                   vf_s.reshape(db, F_HEADS, F_DH), sm_s.reshape(db, 1, LANES))
    if _mlstm_fits_decode(db, batch, seq_len):
        yf_s, ym, c_p, nm_p = _fox_decode(*decode_args, mlstm=(qkv, sm, smt, sig, hw, batch))
    else:
        yf_s = _fox_decode(*decode_args)
        ym, c_p, nm_p = _mlstm_prompt(qkv, sm, smt, sig, hw, batch)
    yf_s = yf_s.reshape(db, F_W)

    ym_s, c_s, n_s, m_s = _mlstm_step(qkv_s, sm_s, sig_s, hw, state_C[layer], state_n[layer], state_m[layer])
    x1_s, h2_s = _merge(ym_s, yf_s, sig_s, xs, mod_s, n2w, wts, db)
    prev_s = state_conv[layer].transpose(1, 0, 2)
    ys, a_s, wa_t, wg_t, w2_b = _ffn(h2_s, x1_s, mod_s, prev_s, wts, db, seq=False)
    wts_ffn = {"wa_t": wa_t, "wg_t": wg_t, "w2": w2_b, "cw": wts["cw"], "cb": wts["cb"]}

    yf = _fox_prompt(qf, kfb, vfb, fcol, batch)
    x1, h2 = _merge(ym, yf, sig, xp, mod_p, n2w, wts, min(512, seq_len))
    conv0 = jnp.zeros((batch, SUBLANES, dff), F32)
    yp, a_last = _ffn(h2, x1, mod_p, conv0, wts_ffn, tm_p, seq=True)

    y_prompt = yp.reshape(batch, seq_len, d)
    k_prompt = kf.reshape(depth, batch, seq_len, F_HEADS, F_DH)
    v_prompt = vf.reshape(depth, batch, seq_len, F_HEADS, F_DH)
    logf_prompt = sm[:, LFF0:LFF0 + F_HEADS].reshape(depth, batch, seq_len, F_HEADS)
    c_prompt_out = c_p[None]
    n_prompt = nm_p[:, :M_HEADS, :][None]
    m_prompt = nm_p[:, M_HEADS:, 0][None]
    conv_prompt = a_last[:, -1, SUBLANES - (CONV_W - 1):, :][None]

    y_sample = ys.reshape(db, 1, d)
    k_sample = kf_s.reshape(depth, db, 1, F_HEADS, F_DH)
    v_sample = vf_s.reshape(depth, db, 1, F_HEADS, F_DH)
    logf_sample = sm_s[:, LFF0:LFF0 + F_HEADS].reshape(depth, db, 1, F_HEADS)
    conv_sample = jnp.stack([state_conv[layer][:, CONV_W - 2, :], a_s], axis=1)[None]

    return (y_prompt, y_sample, k_prompt, v_prompt, logf_prompt, c_prompt_out, n_prompt, m_prompt, conv_prompt,
            k_sample, v_sample, logf_sample, c_s[None], n_s[None], m_s[None], conv_sample)
```
